```python
import jax
import jax.numpy as jnp
from jax import lax
import numpy as np

D_MODEL = 1024
BATCH = 1
SEQ = 16384
DEPTH = 2

N_MIXERS = 4
GROUP_W = D_MODEL // N_MIXERS
D_MIX = N_MIXERS * GROUP_W
NORM_EPS = 1e-6
POOL_WINDOWS = (2, 4, 8, 16)
POOL_CG = GROUP_W // 4
HEAD_DIM = 64
ATT_HEADS = GROUP_W // HEAD_DIM
ATT_KV_HEADS = 2
ATT_WINDOW = 128
ATT_BLOCK = 128
CONV_W = 3
RWKV_HEAD = 64
RWKV_HEADS = GROUP_W // RWKV_HEAD
LORA_W = 32
LORA_A = 32
LORA_V = 32
LORA_G = 64
RWKV_GN_EPS = 64e-5
N_A = GROUP_W
N_B = (ATT_HEADS + 2 * ATT_KV_HEADS) * HEAD_DIM
N_C = 3 * GROUP_W
N_D = 3 * GROUP_W + LORA_W + LORA_A + LORA_G
N_IN = N_A + N_B + N_C + N_D
N_GROUPS = 4
EXP_PER_GROUP = 8
N_EXPERTS = N_GROUPS * EXP_PER_GROUP
TOP_K_IN_GROUP = 2
D_EXPERT = 128

kernel_name = 'hybrid_parallel_heads_hmoe'


def rms_norm(x, g):
    xf = x.astype(jnp.float32)
    y = xf * lax.rsqrt(jnp.mean(xf * xf, axis=-1, keepdims=True) + NORM_EPS)
    return (y * g.astype(jnp.float32)).astype(x.dtype)


def pool_mixer(u, w_pool, scale):
    B, T, _ = u.shape
    uf = u.astype(jnp.float32).reshape(B, T, len(POOL_WINDOWS), POOL_CG)
    cs = jnp.cumsum(uf, axis=1)
    count = jnp.arange(1, T + 1, dtype=jnp.float32)[None, :, None]
    diffs = []
    for j, w in enumerate(POOL_WINDOWS):
        cj = cs[:, :, j]
        prev = jnp.pad(cj, ((0, 0), (w, 0), (0, 0)))[:, :T]
        diffs.append((cj - prev) / jnp.minimum(count, w) - uf[:, :, j])
    d = jnp.stack(diffs, axis=2)
    y = jnp.einsum('btgc,gce->btge', d, w_pool.astype(jnp.float32)).reshape(B, T, GROUP_W)
    return (y * scale.astype(jnp.float32)).astype(u.dtype)


def swa_sink_attention(q, k, v, sinks):
    B, T, H, hd = q.shape
    KVH = k.shape[2]
    G = H // KVH
    nb = T // ATT_BLOCK
    f32 = jnp.float32
    qb = q.astype(f32).reshape(B, nb, ATT_BLOCK, KVH, G, hd) * (hd ** -0.5)

    def band(z):
        zp = jnp.pad(z.astype(f32), ((0, 0), (ATT_BLOCK, 0), (0, 0), (0, 0)))
        zp = zp.reshape(B, nb + 1, ATT_BLOCK, KVH, hd)
        return jnp.concatenate([zp[:, :-1], zp[:, 1:]], axis=2)

    kb, vb = band(k), band(v)
    s = jnp.einsum('bnqhgd,bnkhd->bnhgqk', qb, kb)
    qi = jnp.arange(ATT_BLOCK)[:, None]
    ki = jnp.arange(2 * ATT_BLOCK)[None, :]
    dist = qi + ATT_BLOCK - ki
    kpos = (jnp.arange(nb)[:, None, None] - 1) * ATT_BLOCK + ki
    valid = (dist >= 0) & (dist < ATT_WINDOW) & (kpos >= 0)
    slopes = (2.0 ** (-8.0 * jnp.arange(1, H + 1, dtype=f32) / H)).reshape(KVH, G)
    s = s - slopes[:, :, None, None] * dist.astype(f32)
    s = jnp.where(valid[None, :, None, None], s, -jnp.inf)
    sink = sinks.astype(f32).reshape(KVH, G)[None, None, :, :, None, None]
    m = jnp.maximum(jnp.max(s, axis=-1, keepdims=True), sink)
    p = jnp.exp(s - m)
    p = p / (jnp.sum(p, axis=-1, keepdims=True) + jnp.exp(sink - m))
    o = jnp.einsum('bnhgqk,bnkhd->bnqhgd', p, vb)
    return o.reshape(B, T, H * hd).astype(q.dtype)


def short_gated_conv(bg, cg, xin, conv_w):
    z = cg * xin
    y = lax.conv_general_dilated(z, conv_w[:, None, :], window_strides=(1,),
                                 padding=[(CONV_W - 1, 0)],
                                 dimension_numbers=('NWC', 'WIO', 'NWC'),
                                 feature_group_count=GROUP_W)
    return bg * y


def rwkv7_time_mix(p, v_first, mu, w0, w2, a0, a2, g2, k_k, k_a, r_k, ln_g, ln_b, v_res):
    f32 = jnp.float32
    B, T, _ = p.shape
    H, N = RWKV_HEADS, RWKV_HEAD
    p = p.astype(f32)
    prev = jnp.pad(p, ((0, 0), (1, 0), (0, 0)))[:, :-1]
    z = p + (prev - p) * mu.astype(f32)
    cuts = [GROUP_W, 2 * GROUP_W, 3 * GROUP_W, 3 * GROUP_W + LORA_W, 3 * GROUP_W + LORA_W + LORA_A]
    r, k, v, wd, ad, gd = jnp.split(z, cuts, axis=-1)
    w = -jax.nn.softplus(-(w0.astype(f32) + jnp.tanh(wd) @ w2.astype(f32))) - 0.5
    decay = jnp.exp(-jnp.exp(w))
    a = jax.nn.sigmoid(a0.astype(f32) + ad @ a2.astype(f32))
    g = jax.nn.sigmoid(gd) @ g2.astype(f32)
    if v_res is None:
        v_first = v
    else:
        v0, v1, v2 = v_res
        v = v + (v_first - v) * jax.nn.sigmoid(v0.astype(f32) + (v @ v1.astype(f32)) @ v2.astype(f32))

    def heads(t):
        return t.reshape(B, T, H, N)

    kk = heads(k * k_k.astype(f32))
    kk = kk / jnp.maximum(jnp.sqrt(jnp.sum(kk * kk, axis=-1, keepdims=True)), 1e-12)
    k = k * (1.0 + (a - 1.0) * k_a.astype(f32))

    def step(S, inp):
        r_t, w_t, k_t, v_t, kk_t, a_t = inp
        sa = jnp.einsum('bhij,bhj->bhi', S, -kk_t)
        S = (S * w_t[:, :, None, :] + sa[..., :, None] * (kk_t * a_t)[:, :, None, :]
             + v_t[..., :, None] * k_t[..., None, :])
        return S, jnp.einsum('bhij,bhj->bhi', S, r_t)

    def tm(t):
        return jnp.moveaxis(heads(t), 1, 0)

    S0 = jnp.zeros((B, H, N, N), f32)
    _, y = lax.scan(step, S0, (tm(r), tm(decay), tm(k), tm(v), jnp.moveaxis(kk, 1, 0), tm(a)))
    y = jnp.moveaxis(y, 0, 1)
    mean = jnp.mean(y, axis=-1, keepdims=True)
    var = jnp.mean(jnp.square(y - mean), axis=-1, keepdims=True)
    y = ((y - mean) * lax.rsqrt(var + RWKV_GN_EPS)).reshape(B, T, GROUP_W)
    y = y * ln_g.astype(f32) + ln_b.astype(f32)
    bonus = (jnp.sum(heads(r * k * r_k.astype(f32)), axis=-1, keepdims=True) * heads(v)).reshape(B, T, GROUP_W)
    return (y + bonus) * g, v_first


def hier_moe(h, w_grp, b_grp, w_exp, b_exp, w_gate, w_up, w_down):
    B, T, D = h.shape
    f32 = jnp.float32
    hf = h.reshape(B * T, D)
    grp_prob = jax.nn.softmax((hf @ w_grp).astype(f32) + b_grp.astype(f32), axis=-1)
    gp, gi = lax.top_k(grp_prob, 1)
    exp_logits = ((hf @ w_exp).astype(f32) + b_exp.astype(f32)).reshape(-1, N_GROUPS, EXP_PER_GROUP)
    in_grp = jnp.take_along_axis(exp_logits, gi[:, :, None], axis=1)[:, 0]
    ev, ei = lax.top_k(in_grp, TOP_K_IN_GROUP)
    wts = jax.nn.softmax(ev, axis=-1) * gp
    eidx = gi * EXP_PER_GROUP + ei
    gates = jnp.einsum('nk,nke->ne', wts, jax.nn.one_hot(eidx, N_EXPERTS, dtype=f32))
    hg = jnp.einsum('nd,edf->nef', hf, w_gate)
    hu = jnp.einsum('nd,edf->nef', hf, w_up)
    act = jax.nn.silu(hg) * hu * gates[:, :, None].astype(h.dtype)
    y = jnp.einsum('nef,efd->nd', act, w_down)
    return y.reshape(B, T, D)


def setup_inputs(seed: int = 0) -> dict:
    key = jax.random.key(seed)
    keys = jax.random.split(key, 48)
    counter = [0]
    f32 = jnp.float32
    L, D = DEPTH, D_MODEL

    def nxt():
        kk = keys[counter[0]]
        counter[0] += 1
        return kk

    def nrm(shape, scale):
        return jax.random.normal(nxt(), shape, f32) * scale

    def unif(shape, lo, hi):
        return jax.random.uniform(nxt(), shape, f32, lo, hi)

    return {
        'x': nrm((BATCH, SEQ, D), 1.0),
        'c': nrm((BATCH, D), 1.0),
        'w_ada': nrm((L, D, 6 * D), 0.5 * D ** -0.5),
        'b_ada': nrm((L, 6 * D), 0.02),
        'norm1_g': 1.0 + nrm((L, D), 0.05),
        'norm2_g': 1.0 + nrm((L, D), 0.05),
        'w_in': nrm((L, D, N_IN), D ** -0.5),
        'w_out': nrm((L, D_MIX, D), D_MIX ** -0.5),
        'pool_w': nrm((L, len(POOL_WINDOWS), POOL_CG, POOL_CG), POOL_CG ** -0.5),
        'pool_scale': unif((L, GROUP_W), 0.5, 1.5),
        'attn_sinks': nrm((L, ATT_HEADS), 0.5),
        'conv_w': nrm((L, CONV_W, GROUP_W), CONV_W ** -0.5),
        'rwkv_mu': unif((L, N_D), 0.0, 1.0),
        'rwkv_w0': unif((L, GROUP_W), -6.0, -0.5),
        'rwkv_w2': nrm((L, LORA_W, GROUP_W), 0.5 * LORA_W ** -0.5),
        'rwkv_a0': nrm((L, GROUP_W), 0.1),
        'rwkv_a2': nrm((L, LORA_A, GROUP_W), 0.5 * LORA_A ** -0.5),
        'rwkv_g2': nrm((L, LORA_G, GROUP_W), LORA_G ** -0.5),
        'rwkv_k_k': 0.85 + nrm((L, GROUP_W), 0.05),
        'rwkv_k_a': 1.0 + nrm((L, GROUP_W), 0.05),
        'rwkv_r_k': nrm((L, GROUP_W), 0.1),
        'rwkv_ln_g': 1.0 + nrm((L, GROUP_W), 0.05),
        'rwkv_ln_b': nrm((L, GROUP_W), 0.02),
        'rwkv_v0': nrm((L - 1, GROUP_W), 0.1),
        'rwkv_v1': nrm((L - 1, GROUP_W, LORA_V), GROUP_W ** -0.5),
        'rwkv_v2': nrm((L - 1, LORA_V, GROUP_W), 0.5 * LORA_V ** -0.5),
        'moe_w_grp': nrm((L, D, N_GROUPS), D ** -0.5),
        'moe_b_grp': nrm((L, N_GROUPS), 0.01),
        'moe_w_exp': nrm((L, D, N_EXPERTS), D ** -0.5),
        'moe_b_exp': nrm((L, N_EXPERTS), 0.01),
        'moe_w_gate': nrm((L, N_EXPERTS, D, D_EXPERT), D ** -0.5),
        'moe_w_up': nrm((L, N_EXPERTS, D, D_EXPERT), D ** -0.5),
        'moe_w_down': nrm((L, N_EXPERTS, D_EXPERT, D), D_EXPERT ** -0.5),
        'final_g': 1.0 + nrm((D,), 0.05),
    }


def reference(x, c, w_ada, b_ada, norm1_g, norm2_g, w_in, w_out, pool_w, pool_scale, attn_sinks,
              conv_w, rwkv_mu, rwkv_w0, rwkv_w2, rwkv_a0, rwkv_a2, rwkv_g2, rwkv_k_k, rwkv_k_a,
              rwkv_r_k, rwkv_ln_g, rwkv_ln_b, rwkv_v0, rwkv_v1, rwkv_v2, moe_w_grp, moe_b_grp,
              moe_w_exp, moe_b_exp, moe_w_gate, moe_w_up, moe_w_down, final_g):
    B, T, _ = x.shape
    cond = jax.nn.silu(c)
    v_first = None
    for l in range(DEPTH):
        mod = cond @ w_ada[l] + b_ada[l]
        sh1, sc1, gt1, sh2, sc2, gt2 = [m[:, None, :] for m in jnp.split(mod, 6, axis=-1)]
        h = rms_norm(x, norm1_g[l]) * (1.0 + sc1) + sh1
        p = h @ w_in[l]
        pa, pb, pc, pd = jnp.split(p, [N_A, N_A + N_B, N_A + N_B + N_C], axis=-1)
        ya = pool_mixer(pa, pool_w[l], pool_scale[l])
        q, k, v = jnp.split(pb, [ATT_HEADS * HEAD_DIM, (ATT_HEADS + ATT_KV_HEADS) * HEAD_DIM], axis=-1)
        yb = swa_sink_attention(q.reshape(B, T, ATT_HEADS, HEAD_DIM),
                                k.reshape(B, T, ATT_KV_HEADS, HEAD_DIM),
                                v.reshape(B, T, ATT_KV_HEADS, HEAD_DIM), attn_sinks[l])
        bg, cg, xin = jnp.split(pc, 3, axis=-1)
        yc = short_gated_conv(bg, cg, xin, conv_w[l])
        v_res = None if l == 0 else (rwkv_v0[l - 1], rwkv_v1[l - 1], rwkv_v2[l - 1])
        yd, v_first = rwkv7_time_mix(pd, v_first, rwkv_mu[l], rwkv_w0[l], rwkv_w2[l], rwkv_a0[l],
                                     rwkv_a2[l], rwkv_g2[l], rwkv_k_k[l], rwkv_k_a[l], rwkv_r_k[l],
                                     rwkv_ln_g[l], rwkv_ln_b[l], v_res)
        y = jnp.concatenate([ya, yb, yc, yd.astype(x.dtype)], axis=-1) @ w_out[l]
        x = x + gt1 * y
        h2 = rms_norm(x, norm2_g[l]) * (1.0 + sc2) + sh2
        x = x + gt2 * hier_moe(h2, moe_w_grp[l], moe_b_grp[l], moe_w_exp[l], moe_b_exp[l],
                               moe_w_gate[l], moe_w_up[l], moe_w_down[l])
    return rms_norm(x, final_g)
```

```python
import functools

import jax
import jax.numpy as jnp
from jax import lax
from jax.experimental import pallas as pl
from jax.experimental.pallas import tpu as pltpu

F32 = jnp.float32
BF16 = jnp.bfloat16

D_MODEL = 1024
GROUP_W = 256
NORM_EPS = 1e-6
POOL_WINDOWS = (2, 4, 8, 16)
POOL_CG = 64
HEAD_DIM = 64
ATT_HEADS = 4
ATT_KV_HEADS = 2
ATT_BLOCK = 128
CONV_W = 3
RWKV_HEAD = 64
RWKV_HEADS = 4
LORA_W = 32
LORA_A = 32
LORA_G = 64
RWKV_GN_EPS = 64e-5
N_A = GROUP_W
N_B = (ATT_HEADS + 2 * ATT_KV_HEADS) * HEAD_DIM
N_C = 3 * GROUP_W
N_D = 3 * GROUP_W + LORA_W + LORA_A + LORA_G
N_IN = N_A + N_B + N_C + N_D
N_GROUPS = 4
EXP_PER_GROUP = 8
N_EXPERTS = 32
D_EXPERT = 128

LANES = 128
SUBLANES = 8
VMEM_LIMIT = 56 * 1024 * 1024

CHUNK = 64
POOL_HALO = 16
CONV_HALO = 8

P_IN = 1
P_OUT = 1
P_POOL = 1
P_ATT = 1
P_MOE = 1
P_RWKV = 3
P_ROUTER = 3


def _split(a):
    hi = a.astype(BF16)
    lo = (a - hi.astype(F32)).astype(BF16)
    return hi, lo


def _mm(a, b, dims):
    return lax.dot_general(a, b, (dims, ((), ())), preferred_element_type=F32)


_NN = ((1,), (0,))
_NT = ((1,), (1,))


def _dot(a, b, passes=1, dims=_NN):
    if passes == 1:
        return _mm(a.astype(BF16), b.astype(BF16), dims)
    a0, a1 = _split(a)
    b0, b1 = _split(b)
    return _mm(a0, b0, dims) + (_mm(a0, b1, dims) + _mm(a1, b0, dims))


def _dot_w(a, w_parts, passes):
    if passes == 1:
        return _mm(a.astype(BF16), w_parts[0], _NN)
    a0, a1 = _split(a)
    return _mm(a0, w_parts[0], _NN) + (_mm(a0, w_parts[1], _NN) + _mm(a1, w_parts[0], _NN))


def _split_w(w, passes):
    hi = w.astype(BF16)
    if passes == 1:
        return hi[None]
    lo = (w - hi.astype(F32)).astype(BF16)
    return jnp.stack([hi, lo])


def _sigmoid(x):
    return 1.0 / (1.0 + jnp.exp(-x))


def _cparams(sem):
    return pltpu.CompilerParams(dimension_semantics=sem, vmem_limit_bytes=VMEM_LIMIT)


def _full(shape):
    return pl.BlockSpec(shape, lambda *_: (0,) * len(shape))


def _tile(n, pref):
    t = min(n, pref)
    assert n % t == 0, (n, t)
    return t


def _mod_kernel(c_ref, w_ref, b_ref, o_ref):
    c = c_ref[...]
    cond = c * _sigmoid(c)
    o_ref[...] = _dot(cond, w_ref[...], 3) + b_ref[...]


def _ada_mod(c, w_ada, b_ada):
    depth = w_ada.shape[0]
    c8 = jnp.broadcast_to(c, (SUBLANES, D_MODEL))
    out = pl.pallas_call(
        _mod_kernel,
        out_shape=jax.ShapeDtypeStruct((depth, SUBLANES, 6 * D_MODEL), F32),
        grid=(depth, 6),
        in_specs=[
            pl.BlockSpec((SUBLANES, D_MODEL), lambda l, j: (0, 0)),
            pl.BlockSpec((None, D_MODEL, D_MODEL), lambda l, j: (l, 0, j)),
            pl.BlockSpec((None, 1, D_MODEL), lambda l, j: (l, 0, j)),
        ],
        out_specs=pl.BlockSpec((None, SUBLANES, D_MODEL), lambda l, j: (l, 0, j)),
        compiler_params=_cparams(("arbitrary", "arbitrary")),
        name="ada_mod",
    )(c8, w_ada, b_ada.reshape(depth, 1, 6 * D_MODEL))
    return out[:, 0:1, :]


def _modulated_norm(x, g, sc, sh):
    y = x * lax.rsqrt(jnp.mean(x * x, axis=-1, keepdims=True) + NORM_EPS) * g
    return y * (1.0 + sc) + sh


def _inproj_kernel(x_ref, g_ref, sc_ref, sh_ref, w_ref, pa_ref, pb_ref, pc_ref, pd_ref, *, passes):
    h = _modulated_norm(x_ref[...], g_ref[...], sc_ref[...], sh_ref[...])
    if passes == 1:
        hs = (h.astype(BF16),)
    else:
        hs = _split(h)
    col = 0
    for o_ref, width in ((pa_ref, N_A), (pb_ref, N_B), (pc_ref, N_C), (pd_ref, N_D)):
        w_hi = w_ref[0, :, col:col + width]
        acc = _mm(hs[0], w_hi, _NN)
        if passes > 1:
            acc = acc + (_mm(hs[0], w_ref[1, :, col:col + width], _NN) + _mm(hs[1], w_hi, _NN))
        o_ref[...] = acc
        col += width


def _in_proj(x, g, sc, sh, w_parts):
    t = x.shape[0]
    tm = _tile(t, 512)
    vec = _full((1, D_MODEL))
    return pl.pallas_call(
        functools.partial(_inproj_kernel, passes=P_IN),
        out_shape=[jax.ShapeDtypeStruct((t, n), F32) for n in (N_A, N_B, N_C, N_D)],
        grid=(t // tm,),
        in_specs=[pl.BlockSpec((tm, D_MODEL), lambda i: (i, 0)), vec, vec, vec, _full(w_parts.shape)],
        out_specs=[pl.BlockSpec((tm, n), lambda i: (i, 0)) for n in (N_A, N_B, N_C, N_D)],
        compiler_params=_cparams(("parallel",)),
        name="in_proj",
    )(x, g, sc, sh, w_parts)


def _shift_rows(ext, n):
    return pltpu.roll(ext, n, axis=0)


def _pool_conv_kernel(pa_ref, pah_ref, pc_ref, pch_ref, wp_ref, scale_ref, cw_ref, ya_ref, yc_ref, *, tt):
    i = pl.program_id(0)
    not_first = (i > 0).astype(F32)
    u = pa_ref[...]
    ext = jnp.concatenate([pah_ref[...] * not_first, u], axis=0)
    s2 = ext + _shift_rows(ext, 1)
    s4 = s2 + _shift_rows(s2, 2)
    s8 = s4 + _shift_rows(s4, 4)
    s16 = s8 + _shift_rows(s8, 8)
    grp = lax.broadcasted_iota(jnp.int32, (tt, GROUP_W), 1) // POOL_CG
    tpos = lax.broadcasted_iota(jnp.int32, (tt, GROUP_W), 0) + (i * tt + 1)
    wsum = jnp.where(grp == 0, s2[POOL_HALO:], jnp.where(grp == 1, s4[POOL_HALO:],
                     jnp.where(grp == 2, s8[POOL_HALO:], s16[POOL_HALO:])))
    win = jnp.where(grp == 0, 2, jnp.where(grp == 1, 4, jnp.where(grp == 2, 8, 16)))
    cnt = jnp.minimum(tpos, win).astype(F32)
    d = wsum / cnt - u
    ya_ref[...] = _dot_w(d, (wp_ref[0], wp_ref[-1]), P_POOL) * scale_ref[...]
    z = pc_ref[:, GROUP_W:2 * GROUP_W] * pc_ref[:, 2 * GROUP_W:3 * GROUP_W]
    zh = pch_ref[:, GROUP_W:2 * GROUP_W] * pch_ref[:, 2 * GROUP_W:3 * GROUP_W] * not_first
    zext = jnp.concatenate([zh, z], axis=0)
    y = (cw_ref[2:3, :] * z + cw_ref[1:2, :] * _shift_rows(zext, 1)[CONV_HALO:]
         + cw_ref[0:1, :] * _shift_rows(zext, 2)[CONV_HALO:])
    yc_ref[...] = pc_ref[:, 0:GROUP_W] * y


def _pool_conv(pa, pc, wp_parts, pool_scale, conv_w):
    t = pa.shape[0]
    tt = _tile(t, 512)
    rp, rc = tt // POOL_HALO, tt // CONV_HALO
    return pl.pallas_call(
        functools.partial(_pool_conv_kernel, tt=tt),
        out_shape=[jax.ShapeDtypeStruct((t, GROUP_W), F32)] * 2,
        grid=(t // tt,),
        in_specs=[
            pl.BlockSpec((tt, N_A), lambda i: (i, 0)),
            pl.BlockSpec((POOL_HALO, N_A), lambda i: (jnp.maximum(i * rp - 1, 0), 0)),
            pl.BlockSpec((tt, N_C), lambda i: (i, 0)),
            pl.BlockSpec((CONV_HALO, N_C), lambda i: (jnp.maximum(i * rc - 1, 0), 0)),
            _full(wp_parts.shape), _full((1, GROUP_W)), _full((CONV_W, GROUP_W)),
        ],
        out_specs=[pl.BlockSpec((tt, GROUP_W), lambda i: (i, 0))] * 2,
        compiler_params=_cparams(("parallel",)),
        name="pool_conv",
    )(pa, pa, pc, pc, wp_parts, pool_scale, conv_w)


def _attn_kernel(cur_ref, prev_ref, sink_ref, o_ref, *, tq):
    i = pl.program_id(0)
    nq = ATT_HEADS * HEAD_DIM
    nkv = ATT_KV_HEADS * HEAD_DIM
    blk = ATT_BLOCK
    qi = lax.broadcasted_iota(jnp.int32, (blk, 2 * blk), 0)
    ki = lax.broadcasted_iota(jnp.int32, (blk, 2 * blk), 1)
    dist = qi + blk - ki
    in_win = (dist >= 0) & (dist < blk)
    distf = dist.astype(F32)
    for b in range(tq // blk):
        rows = slice(b * blk, (b + 1) * blk)
        q = cur_ref[rows, 0:nq] * (HEAD_DIM ** -0.5)
        k_c = cur_ref[rows, nq:nq + nkv]
        v_c = cur_ref[rows, nq + nkv:nq + 2 * nkv]
        if b == 0:
            k_p, v_p = prev_ref[:, 0:nkv], prev_ref[:, nkv:2 * nkv]
            valid = in_win & (ki >= blk * (1 - (i > 0).astype(jnp.int32)))
        else:
            prow = slice((b - 1) * blk, b * blk)
            k_p, v_p = cur_ref[prow, nq:nq + nkv], cur_ref[prow, nq + nkv:nq + 2 * nkv]
            valid = in_win
        outs = []
        for h in range(ATT_HEADS):
            g = h // (ATT_HEADS // ATT_KV_HEADS)
            gs = slice(g * HEAD_DIM, (g + 1) * HEAD_DIM)
            kcat = jnp.concatenate([k_p[:, gs], k_c[:, gs]], axis=0)
            vcat = jnp.concatenate([v_p[:, gs], v_c[:, gs]], axis=0)
            s = _dot(q[:, h * HEAD_DIM:(h + 1) * HEAD_DIM], kcat, P_ATT, _NT)
            slope = 2.0 ** (-8.0 * (h + 1) / ATT_HEADS)
            s = jnp.where(valid, s - slope * distf, -jnp.inf)
            sink = sink_ref[:, h:h + 1]
            m = jnp.maximum(jnp.max(s, axis=-1, keepdims=True), sink)
            p = jnp.exp(s - m)
            den = jnp.sum(p, axis=-1, keepdims=True) + jnp.exp(sink - m)
            outs.append(_dot(p, vcat, P_ATT) / den)
        o_ref[rows, :] = jnp.concatenate(outs, axis=-1)


def _attention(pb, sinks):
    t = pb.shape[0]
    tq = _tile(t, 512)
    nb = tq // ATT_BLOCK
    return pl.pallas_call(
        functools.partial(_attn_kernel, tq=tq),
        out_shape=jax.ShapeDtypeStruct((t, GROUP_W), F32),
        grid=(t // tq,),
        in_specs=[
            pl.BlockSpec((tq, N_B), lambda i: (i, 0)),
            pl.BlockSpec((ATT_BLOCK, 2 * ATT_KV_HEADS * HEAD_DIM), lambda i: (jnp.maximum(i * nb - 1, 0), 1)),
            _full((1, ATT_HEADS)),
        ],
        out_specs=pl.BlockSpec((tq, GROUP_W), lambda i: (i, 0)),
        compiler_params=_cparams(("parallel",)),
        name="swa_attention",
    )(pb, pb, sinks)


def _rwkv_kernel(*refs, tt, has_vres):
    if has_vres:
        (pd_ref, prev_ref, vf_ref, mu_ref, w0_ref, w2_ref, a0_ref, a2_ref, g2_ref, kk_ref, ka_ref, rk_ref,
         lng_ref, lnb_ref, v0_ref, v1_ref, v2_ref, y_ref,
         st_ref, r_s, k_s, v_s, kn_s, b_s, lw_s, m_s, c_s, rq_s, y0_s, yr_s) = refs
    else:
        (pd_ref, prev_ref, mu_ref, w0_ref, w2_ref, a0_ref, a2_ref, g2_ref, kk_ref, ka_ref, rk_ref,
         lng_ref, lnb_ref, y_ref, vout_ref,
         st_ref, r_s, k_s, v_s, kn_s, b_s, lw_s, m_s, c_s, rq_s, y0_s, yr_s) = refs
    i = pl.program_id(0)
    nh, hd, gw = RWKV_HEADS, RWKV_HEAD, GROUP_W
    P = P_RWKV

    @pl.when(i == 0)
    def _():
        st_ref[...] = jnp.zeros_like(st_ref)

    hr = lax.broadcasted_iota(jnp.int32, (gw, gw), 0) // hd
    hc = lax.broadcasted_iota(jnp.int32, (gw, gw), 1) // hd
    hsum = (hr == hc).astype(F32)

    p = pd_ref[...]
    row = lax.broadcasted_iota(jnp.int32, (tt, 1), 0)
    prev_row = prev_ref[SUBLANES - 1:SUBLANES, :] * (i > 0).astype(F32)
    shifted = jnp.where(row == 0, prev_row, pltpu.roll(p, 1, axis=0))
    z = p + (shifted - p) * mu_ref[...]
    r = z[:, 0:gw]
    k = z[:, gw:2 * gw]
    v = z[:, 2 * gw:3 * gw]
    o = 3 * gw
    wd = z[:, o:o + LORA_W]
    ad = z[:, o + LORA_W:o + LORA_W + LORA_A]
    gd = z[:, o + LORA_W + LORA_A:o + LORA_W + LORA_A + LORA_G]
    wpre = -(w0_ref[...] + _dot(jnp.tanh(wd), w2_ref[...], P))
    softplus = jnp.maximum(wpre, 0.0) + jnp.log(1.0 + jnp.exp(-jnp.abs(wpre)))
    lw_s[...] = -jnp.exp(-softplus - 0.5)
    a = _sigmoid(a0_ref[...] + _dot(ad, a2_ref[...], P))
    g = _dot(_sigmoid(gd), g2_ref[...], P)
    if has_vres:
        mix = _sigmoid(v0_ref[...] + _dot(_dot(v, v1_ref[...], P), v2_ref[...], P))
        v = v + (vf_ref[...] - v) * mix
    else:
        vout_ref[...] = v
    kk = k * kk_ref[...]
    kk = kk / jnp.maximum(jnp.sqrt(_dot(kk * kk, hsum, P)), 1e-12)
    k = k * (1.0 + (a - 1.0) * ka_ref[...])
    r_s[...] = r
    k_s[...] = k
    v_s[...] = v
    kn_s[...] = kk
    b_s[...] = kk * a

    ti = lax.broadcasted_iota(jnp.int32, (CHUNK, CHUNK), 0)
    tj = lax.broadcasted_iota(jnp.int32, (CHUNK, CHUNK), 1)
    low_strict = ti > tj
    low_incl = ti >= tj
    tril = low_incl.astype(F32)
    eye = (ti == tj).astype(F32)

    def phase1(c, carry):
        rows = pl.ds(pl.multiple_of(c * CHUNK, CHUNK), CHUNK)
        lw = lw_s[rows, :]
        cl = _dot(tril, lw, P)
        cl_end = cl[CHUNK - 1:CHUNK, :]
        e_neg = jnp.exp(-cl)
        e_end = jnp.exp(cl_end - cl)
        kn, bb, kc, vc = kn_s[rows, :], b_s[rows, :], k_s[rows, :], v_s[rows, :]
        at = -kn * jnp.exp(cl - lw)
        bh = bb * e_neg
        kh = kc * e_neg
        rt = r_s[rows, :] * jnp.exp(cl)
        be = bb * e_end
        ke = kc * e_end
        p_end = jnp.exp(cl_end)
        for h in range(nh):
            hs = slice(h * hd, (h + 1) * hd)
            at_h, bh_h, kh_h, rt_h, v_h = at[:, hs], bh[:, hs], kh[:, hs], rt[:, hs], vc[:, hs]
            a_ab = jnp.where(low_strict, _dot(at_h, bh_h, P, _NT), 0.0)
            a_ak = jnp.where(low_strict, _dot(at_h, kh_h, P, _NT), 0.0)
            b_rb = jnp.where(low_incl, _dot(rt_h, bh_h, P, _NT), 0.0)
            b_rk = jnp.where(low_incl, _dot(rt_h, kh_h, P, _NT), 0.0)
            tinv = eye + a_ab
            apow = a_ab
            for _ in range(5):
                apow = _dot(apow, apow, P)
                tinv = tinv + _dot(tinv, apow, P)
            w1 = _dot(tinv, at_h, P)
            u0 = _dot(tinv, _dot(a_ak, v_h, P), P)
            rq_s[c, h] = rt_h + _dot(b_rb, w1, P)
            y0_s[c, h] = _dot(b_rb, u0, P) + _dot(b_rk, v_h, P)
            be_t = be[:, hs].T
            m_s[c, h] = eye * p_end[:, hs] + _dot(be_t, w1, P)
            c_s[c, h] = _dot(be_t, u0, P) + _dot(ke[:, hs].T, v_h, P)
        return carry

    nch = tt // CHUNK
    lax.fori_loop(0, nch, phase1, 0)

    def phase2(c, carry):
        rows = pl.ds(pl.multiple_of(c * CHUNK, CHUNK), CHUNK)
        ys = []
        for h in range(nh):
            zst = st_ref[h]
            ys.append(_dot(rq_s[c, h], zst, P) + y0_s[c, h])
            st_ref[h] = _dot(m_s[c, h], zst, P) + c_s[c, h]
        yr_s[rows, :] = jnp.concatenate(ys, axis=-1)
        return carry

    lax.fori_loop(0, nch, phase2, 0)

    y = yr_s[...]
    inv_n = 1.0 / hd
    mean = _dot(y, hsum, P) * inv_n
    yc = y - mean
    var = _dot(yc * yc, hsum, P) * inv_n
    yn = yc * lax.rsqrt(var + RWKV_GN_EPS) * lng_ref[...] + lnb_ref[...]
    bonus = _dot(r_s[...] * k_s[...] * rk_ref[...], hsum, P) * v_s[...]
    y_ref[...] = (yn + bonus) * g


def _rwkv(pd, v_first, prm):
    t = pd.shape[0]
    tt = _tile(t, 512)
    nch = tt // CHUNK
    has_vres = v_first is not None
    gw = GROUP_W
    row_blk = lambda n: pl.BlockSpec((tt, n), lambda i: (i, 0))
    vec = _full((1, gw))
    in_specs = [row_blk(N_D),
                pl.BlockSpec((SUBLANES, N_D), lambda i: (jnp.maximum(i * (tt // SUBLANES) - 1, 0), 0))]
    args = [pd, pd]
    if has_vres:
        in_specs.append(row_blk(gw))
        args.append(v_first)
    in_specs += [_full((1, N_D)), vec, _full((LORA_W, gw)), vec, _full((LORA_A, gw)), _full((LORA_G, gw)),
                 vec, vec, vec, vec, vec]
    args += [prm["mu"], prm["w0"], prm["w2"], prm["a0"], prm["a2"], prm["g2"], prm["k_k"], prm["k_a"],
             prm["r_k"], prm["ln_g"], prm["ln_b"]]
    if has_vres:
        in_specs += [vec, _full((gw, 32)), _full((32, gw))]
        args += [prm["v0"], prm["v1"], prm["v2"]]
        out_shape = jax.ShapeDtypeStruct((t, gw), F32)
        out_specs = row_blk(gw)
    else:
        out_shape = [jax.ShapeDtypeStruct((t, gw), F32)] * 2
        out_specs = [row_blk(gw)] * 2
    tok = pltpu.VMEM((tt, gw), F32)
    op = pltpu.VMEM((nch, RWKV_HEADS, RWKV_HEAD, RWKV_HEAD), F32)
    scratch = [pltpu.VMEM((RWKV_HEADS, RWKV_HEAD, RWKV_HEAD), F32),
               tok, tok, tok, tok, tok, tok, op, op, op, op, tok]
    return pl.pallas_call(
        functools.partial(_rwkv_kernel, tt=tt, has_vres=has_vres),
        out_shape=out_shape,
        grid=(t // tt,),
        in_specs=in_specs,
        out_specs=out_specs,
        scratch_shapes=scratch,
        compiler_params=_cparams(("arbitrary",)),
        name="rwkv7",
    )(*args)


def _outproj_kernel(ya_ref, yb_ref, yc_ref, yd_ref, x_ref, wo_ref, gt_ref, g_ref, sc_ref, sh_ref, wr_ref, br_ref,
                    x1_ref, h2_ref, gate_ref, *, tm):
    y = None
    for j, y_ref in enumerate((ya_ref, yb_ref, yc_ref, yd_ref)):
        rows = slice(j * GROUP_W, (j + 1) * GROUP_W)
        part = _dot_w(y_ref[...], (wo_ref[0, rows, :], wo_ref[-1, rows, :]), P_OUT)
        y = part if y is None else y + part
    x1 = x_ref[...] + gt_ref[...] * y
    x1_ref[...] = x1
    h2 = _modulated_norm(x1, g_ref[...], sc_ref[...], sh_ref[...])
    h2_ref[...] = h2.astype(h2_ref.dtype)
    lg = _dot(h2, wr_ref[...], P_ROUTER) + br_ref[...]
    lane = lax.broadcasted_iota(jnp.int32, (tm, LANES), 1)
    lanef = lane.astype(F32)
    gl = jnp.where(lane < N_GROUPS, lg[:, 0:LANES], -jnp.inf)
    gmax = jnp.max(gl, axis=-1, keepdims=True)
    gp = 1.0 / jnp.sum(jnp.exp(gl - gmax), axis=-1, keepdims=True)
    gi = jnp.min(jnp.where(gl == gmax, lanef, float(LANES)), axis=-1, keepdims=True)
    lo = gi * EXP_PER_GROUP
    in_grp = (lanef >= lo) & (lanef < lo + EXP_PER_GROUP)
    el = jnp.where(in_grp, lg[:, LANES:2 * LANES], -jnp.inf)
    m1 = jnp.max(el, axis=-1, keepdims=True)
    i1 = jnp.min(jnp.where(el == m1, lanef, float(LANES)), axis=-1, keepdims=True)
    el2 = jnp.where(lanef == i1, -jnp.inf, el)
    m2 = jnp.max(el2, axis=-1, keepdims=True)
    i2 = jnp.min(jnp.where(el2 == m2, lanef, float(LANES)), axis=-1, keepdims=True)
    e2 = jnp.exp(m2 - m1)
    w_top = gp / (1.0 + e2)
    gate_ref[...] = jnp.where(lanef == i1, w_top, jnp.where(lanef == i2, w_top * e2, 0.0))


def _out_proj(ya, yb, yc, yd, x, wo_parts, gt, g, sc, sh, w_router, b_router):
    t = x.shape[0]
    tm = _tile(t, 512)
    row_blk = lambda n: pl.BlockSpec((tm, n), lambda i: (i, 0))
    vec = _full((1, D_MODEL))
    return pl.pallas_call(
        functools.partial(_outproj_kernel, tm=tm),
        out_shape=[jax.ShapeDtypeStruct((t, D_MODEL), F32), jax.ShapeDtypeStruct((t, D_MODEL), BF16),
                   jax.ShapeDtypeStruct((t, LANES), F32)],
        grid=(t // tm,),
        in_specs=[row_blk(GROUP_W)] * 4 + [row_blk(D_MODEL), _full(wo_parts.shape), vec, vec, vec, vec,
                                           _full((D_MODEL, 2 * LANES)), _full((1, 2 * LANES))],
        out_specs=[row_blk(D_MODEL), row_blk(D_MODEL), row_blk(LANES)],
        compiler_params=_cparams(("parallel",)),
        name="out_proj_router",
    )(ya, yb, yc, yd, x, wo_parts, gt, g, sc, sh, w_router, b_router)


def _moe_kernel(h_ref, gate_ref, ex_ref, wg_ref, wu_ref, wd_ref, x1_ref, gt_ref, fg_ref, o_ref, acc_ref, *, final):
    e = pl.program_id(1)

    @pl.when(e == 0)
    def _():
        acc_ref[...] = jnp.zeros_like(acc_ref)

    h = h_ref[...]
    hg = _mm(h, wg_ref[...], _NN)
    hu = _mm(h, wu_ref[...], _NN)
    gates = _dot(gate_ref[...], ex_ref[...], 3)
    act = hg * _sigmoid(hg) * hu * gates
    acc_ref[...] += _mm(act.astype(BF16), wd_ref[...], _NN)

    @pl.when(e == pl.num_programs(1) - 1)
    def _():
        out = x1_ref[...] + gt_ref[...] * acc_ref[...]
        if final:
            out = out * lax.rsqrt(jnp.mean(out * out, axis=-1, keepdims=True) + NORM_EPS) * fg_ref[...]
        o_ref[...] = out


def _moe(h2, gates, expand, wg, wu, wd, x1, gt, final_g, final):
    t = h2.shape[0]
    tm = _tile(t, 512)
    eb = 8
    nw = eb * D_EXPERT
    vec = _full((1, D_MODEL))
    return pl.pallas_call(
        functools.partial(_moe_kernel, final=final),
        out_shape=jax.ShapeDtypeStruct((t, D_MODEL), F32),
        grid=(t // tm, N_EXPERTS // eb),
        in_specs=[
            pl.BlockSpec((tm, D_MODEL), lambda i, e: (i, 0)),
            pl.BlockSpec((tm, LANES), lambda i, e: (i, 0)),
            pl.BlockSpec((LANES, nw), lambda i, e: (0, e)),
            pl.BlockSpec((D_MODEL, nw), lambda i, e: (0, e)),
            pl.BlockSpec((D_MODEL, nw), lambda i, e: (0, e)),
            pl.BlockSpec((nw, D_MODEL), lambda i, e: (e, 0)),
            pl.BlockSpec((tm, D_MODEL), lambda i, e: (i, 0)),
            vec, vec,
        ],
        out_specs=pl.BlockSpec((tm, D_MODEL), lambda i, e: (i, 0)),
        scratch_shapes=[pltpu.VMEM((tm, D_MODEL), F32)],
        compiler_params=_cparams(("parallel", "arbitrary")),
        name="moe_experts",
    )(h2, gates, expand, wg, wu, wd, x1, gt, final_g)


def kernel(x, c, w_ada, b_ada, norm1_g, norm2_g, w_in, w_out, pool_w, pool_scale, attn_sinks, conv_w, rwkv_mu,
           rwkv_w0, rwkv_w2, rwkv_a0, rwkv_a2, rwkv_g2, rwkv_k_k, rwkv_k_a, rwkv_r_k, rwkv_ln_g, rwkv_ln_b,
           rwkv_v0, rwkv_v1, rwkv_v2, moe_w_grp, moe_b_grp, moe_w_exp, moe_b_exp, moe_w_gate, moe_w_up,
           moe_w_down, final_g):
    bsz, t, d = x.shape
    assert bsz == 1 and d == D_MODEL
    depth = w_ada.shape[0]
    xs = x.reshape(t, d)
    mod = _ada_mod(c, w_ada, b_ada)
    expand = (jnp.arange(LANES)[:, None] == jnp.arange(N_EXPERTS * D_EXPERT)[None, :] // D_EXPERT).astype(F32)
    v_first = None
    row = lambda a: a.reshape(1, -1)
    for l in range(depth):
        sh1, sc1, gt1, sh2, sc2, gt2 = [mod[l, :, j * d:(j + 1) * d] for j in range(6)]
        pa, pb, pc, pd = _in_proj(xs, row(norm1_g[l]), sc1, sh1, _split_w(w_in[l], P_IN))
        wp = jnp.zeros((GROUP_W, GROUP_W), F32)
        for gidx in range(len(POOL_WINDOWS)):
            sl = slice(gidx * POOL_CG, (gidx + 1) * POOL_CG)
            wp = wp.at[sl, sl].set(pool_w[l, gidx])
        ya, yc = _pool_conv(pa, pc, _split_w(wp, P_POOL), row(pool_scale[l]), conv_w[l])
        yb = _attention(pb, row(attn_sinks[l]))
        prm = dict(mu=row(rwkv_mu[l]), w0=row(rwkv_w0[l]), w2=rwkv_w2[l], a0=row(rwkv_a0[l]), a2=rwkv_a2[l],
                   g2=rwkv_g2[l], k_k=row(rwkv_k_k[l]), k_a=row(rwkv_k_a[l]), r_k=row(rwkv_r_k[l]),
                   ln_g=row(rwkv_ln_g[l]), ln_b=row(rwkv_ln_b[l]))
        if l == 0:
            yd, v_first = _rwkv(pd, None, prm)
        else:
            prm.update(v0=row(rwkv_v0[l - 1]), v1=rwkv_v1[l - 1], v2=rwkv_v2[l - 1])
            yd = _rwkv(pd, v_first, prm)
        w_router = jnp.zeros((d, 2 * LANES), F32)
        w_router = w_router.at[:, 0:N_GROUPS].set(moe_w_grp[l]).at[:, LANES:LANES + N_EXPERTS].set(moe_w_exp[l])
        b_router = jnp.zeros((1, 2 * LANES), F32)
        b_router = b_router.at[0, 0:N_GROUPS].set(moe_b_grp[l]).at[0, LANES:LANES + N_EXPERTS].set(moe_b_exp[l])
        x1, h2, gates = _out_proj(ya, yb, yc, yd, xs, _split_w(w_out[l], P_OUT), gt1, row(norm2_g[l]), sc2, sh2,
                                  w_router, b_router)
        wg = jnp.transpose(moe_w_gate[l].astype(BF16), (1, 0, 2)).reshape(d, N_EXPERTS * D_EXPERT)
        wu = jnp.transpose(moe_w_up[l].astype(BF16), (1, 0, 2)).reshape(d, N_EXPERTS * D_EXPERT)
        wd = moe_w_down[l].astype(BF16).reshape(N_EXPERTS * D_EXPERT, d)
        xs = _moe(h2, gates, expand, wg, wu, wd, x1, gt2, row(final_g), final=(l == depth - 1))
    return xs.reshape(bsz, t, d)
```

```python
import functools

import jax
import jax.numpy as jnp
from jax import lax
from jax.experimental import pallas as pl
from jax.experimental.pallas import tpu as pltpu

F32 = jnp.float32
BF16 = jnp.bfloat16

D_MODEL = 1024
GROUP_W = 256
NORM_EPS = 1e-6
POOL_WINDOWS = (2, 4, 8, 16)
POOL_CG = 64
HEAD_DIM = 64
ATT_HEADS = 4
ATT_KV_HEADS = 2
ATT_BLOCK = 128
CONV_W = 3
RWKV_HEAD = 64
RWKV_HEADS = 4
LORA_W = 32
LORA_A = 32
LORA_G = 64
RWKV_GN_EPS = 64e-5
N_A = GROUP_W
N_B = (ATT_HEADS + 2 * ATT_KV_HEADS) * HEAD_DIM
N_C = 3 * GROUP_W
N_D = 3 * GROUP_W + LORA_W + LORA_A + LORA_G
N_IN = N_A + N_B + N_C + N_D
N_GROUPS = 4
EXP_PER_GROUP = 8
N_EXPERTS = 32
D_EXPERT = 128

LANES = 128
SUBLANES = 8
VMEM_LIMIT = 56 * 1024 * 1024

CHUNK = 64
POOL_HALO = 16
CONV_HALO = 8

P_IN = 1
P_OUT = 1
P_POOL = 1
P_ATT = 1
P_MOE = 1
P_RWKV = 3
P_CHUNK = 1
P_STATE = 3
P_ROUTER = 3


def _split(a):
    hi = a.astype(BF16)
    lo = (a - hi.astype(F32)).astype(BF16)
    return hi, lo


def _mm(a, b, dims):
    return lax.dot_general(a, b, (dims, ((), ())), preferred_element_type=F32)


_NN = ((1,), (0,))
_NT = ((1,), (1,))


def _dot(a, b, passes=1, dims=_NN):
    if passes == 1:
        return _mm(a.astype(BF16), b.astype(BF16), dims)
    a0, a1 = _split(a)
    b0, b1 = _split(b)
    return _mm(a0, b0, dims) + (_mm(a0, b1, dims) + _mm(a1, b0, dims))


def _bmm(a, b, nt):
    dims = (((2,), (2 if nt else 1,)), ((0,), (0,)))
    return lax.dot_general(a, b, dims, preferred_element_type=F32)


def _bdot(a, b, passes=1, nt=False):
    if passes == 1:
        return _bmm(a.astype(BF16), b.astype(BF16), nt)
    a0, a1 = _split(a)
    b0, b1 = _split(b)
    return _bmm(a0, b0, nt) + (_bmm(a0, b1, nt) + _bmm(a1, b0, nt))


def _dot_w(a, w_parts, passes):
    if passes == 1:
        return _mm(a.astype(BF16), w_parts[0], _NN)
    a0, a1 = _split(a)
    return _mm(a0, w_parts[0], _NN) + (_mm(a0, w_parts[1], _NN) + _mm(a1, w_parts[0], _NN))


def _split_w(w, passes):
    hi = w.astype(BF16)
    if passes == 1:
        return hi[None]
    lo = (w - hi.astype(F32)).astype(BF16)
    return jnp.stack([hi, lo])


def _sigmoid(x):
    return 1.0 / (1.0 + jnp.exp(-x))


def _cparams(sem):
    return pltpu.CompilerParams(dimension_semantics=sem, vmem_limit_bytes=VMEM_LIMIT)


def _full(shape):
    return pl.BlockSpec(shape, lambda *_: (0,) * len(shape))


def _tile(n, pref):
    t = min(n, pref)
    assert n % t == 0, (n, t)
    return t


def _mod_kernel(c_ref, w_ref, b_ref, o_ref):
    c = c_ref[...]
    cond = c * _sigmoid(c)
    o_ref[...] = _dot(cond, w_ref[...], 3) + b_ref[...]


def _ada_mod(c, w_ada, b_ada):
    depth = w_ada.shape[0]
    c8 = jnp.broadcast_to(c, (SUBLANES, D_MODEL))
    out = pl.pallas_call(
        _mod_kernel,
        out_shape=jax.ShapeDtypeStruct((depth, SUBLANES, 6 * D_MODEL), F32),
        grid=(depth, 6),
        in_specs=[
            pl.BlockSpec((SUBLANES, D_MODEL), lambda l, j: (0, 0)),
            pl.BlockSpec((None, D_MODEL, D_MODEL), lambda l, j: (l, 0, j)),
            pl.BlockSpec((None, 1, D_MODEL), lambda l, j: (l, 0, j)),
        ],
        out_specs=pl.BlockSpec((None, SUBLANES, D_MODEL), lambda l, j: (l, 0, j)),
        compiler_params=_cparams(("arbitrary", "arbitrary")),
        name="ada_mod",
    )(c8, w_ada, b_ada.reshape(depth, 1, 6 * D_MODEL))
    return out[:, 0:1, :]


def _modulated_norm(x, g, sc, sh):
    y = x * lax.rsqrt(jnp.mean(x * x, axis=-1, keepdims=True) + NORM_EPS) * g
    return y * (1.0 + sc) + sh


def _inproj_kernel(x_ref, g_ref, sc_ref, sh_ref, w_ref, pa_ref, pb_ref, pc_ref, pd_ref, *, passes):
    h = _modulated_norm(x_ref[...], g_ref[...], sc_ref[...], sh_ref[...])
    if passes == 1:
        hs = (h.astype(BF16),)
    else:
        hs = _split(h)
    col = 0
    for o_ref, width in ((pa_ref, N_A), (pb_ref, N_B), (pc_ref, N_C), (pd_ref, N_D)):
        w_hi = w_ref[0, :, col:col + width]
        acc = _mm(hs[0], w_hi, _NN)
        if passes > 1:
            acc = acc + (_mm(hs[0], w_ref[1, :, col:col + width], _NN) + _mm(hs[1], w_hi, _NN))
        o_ref[...] = acc
        col += width


def _in_proj(x, g, sc, sh, w_parts):
    t = x.shape[0]
    tm = _tile(t, 512)
    vec = _full((1, D_MODEL))
    return pl.pallas_call(
        functools.partial(_inproj_kernel, passes=P_IN),
        out_shape=[jax.ShapeDtypeStruct((t, n), F32) for n in (N_A, N_B, N_C, N_D)],
        grid=(t // tm,),
        in_specs=[pl.BlockSpec((tm, D_MODEL), lambda i: (i, 0)), vec, vec, vec, _full(w_parts.shape)],
        out_specs=[pl.BlockSpec((tm, n), lambda i: (i, 0)) for n in (N_A, N_B, N_C, N_D)],
        compiler_params=_cparams(("parallel",)),
        name="in_proj",
    )(x, g, sc, sh, w_parts)


def _shift_rows(ext, n):
    return pltpu.roll(ext, n, axis=0)


def _pool_conv_kernel(pa_ref, pah_ref, pc_ref, pch_ref, wp_ref, scale_ref, cw_ref, ya_ref, yc_ref, *, tt):
    i = pl.program_id(0)
    not_first = (i > 0).astype(F32)
    u = pa_ref[...]
    ext = jnp.concatenate([pah_ref[...] * not_first, u], axis=0)
    s2 = ext + _shift_rows(ext, 1)
    s4 = s2 + _shift_rows(s2, 2)
    s8 = s4 + _shift_rows(s4, 4)
    s16 = s8 + _shift_rows(s8, 8)
    grp = lax.broadcasted_iota(jnp.int32, (tt, GROUP_W), 1) // POOL_CG
    tpos = lax.broadcasted_iota(jnp.int32, (tt, GROUP_W), 0) + (i * tt + 1)
    wsum = jnp.where(grp == 0, s2[POOL_HALO:], jnp.where(grp == 1, s4[POOL_HALO:],
                     jnp.where(grp == 2, s8[POOL_HALO:], s16[POOL_HALO:])))
    win = jnp.where(grp == 0, 2, jnp.where(grp == 1, 4, jnp.where(grp == 2, 8, 16)))
    cnt = jnp.minimum(tpos, win).astype(F32)
    d = wsum / cnt - u
    ya_ref[...] = _dot_w(d, (wp_ref[0], wp_ref[-1]), P_POOL) * scale_ref[...]
    z = pc_ref[:, GROUP_W:2 * GROUP_W] * pc_ref[:, 2 * GROUP_W:3 * GROUP_W]
    zh = pch_ref[:, GROUP_W:2 * GROUP_W] * pch_ref[:, 2 * GROUP_W:3 * GROUP_W] * not_first
    zext = jnp.concatenate([zh, z], axis=0)
    y = (cw_ref[2:3, :] * z + cw_ref[1:2, :] * _shift_rows(zext, 1)[CONV_HALO:]
         + cw_ref[0:1, :] * _shift_rows(zext, 2)[CONV_HALO:])
    yc_ref[...] = pc_ref[:, 0:GROUP_W] * y


def _pool_conv(pa, pc, wp_parts, pool_scale, conv_w):
    t = pa.shape[0]
    tt = _tile(t, 512)
    rp, rc = tt // POOL_HALO, tt // CONV_HALO
    return pl.pallas_call(
        functools.partial(_pool_conv_kernel, tt=tt),
        out_shape=[jax.ShapeDtypeStruct((t, GROUP_W), F32)] * 2,
        grid=(t // tt,),
        in_specs=[
            pl.BlockSpec((tt, N_A), lambda i: (i, 0)),
            pl.BlockSpec((POOL_HALO, N_A), lambda i: (jnp.maximum(i * rp - 1, 0), 0)),
            pl.BlockSpec((tt, N_C), lambda i: (i, 0)),
            pl.BlockSpec((CONV_HALO, N_C), lambda i: (jnp.maximum(i * rc - 1, 0), 0)),
            _full(wp_parts.shape), _full((1, GROUP_W)), _full((CONV_W, GROUP_W)),
        ],
        out_specs=[pl.BlockSpec((tt, GROUP_W), lambda i: (i, 0))] * 2,
        compiler_params=_cparams(("parallel",)),
        name="pool_conv",
    )(pa, pa, pc, pc, wp_parts, pool_scale, conv_w)


def _attn_kernel(cur_ref, prev_ref, sink_ref, o_ref, *, tq):
    i = pl.program_id(0)
    nq = ATT_HEADS * HEAD_DIM
    nkv = ATT_KV_HEADS * HEAD_DIM
    blk = ATT_BLOCK
    qi = lax.broadcasted_iota(jnp.int32, (blk, 2 * blk), 0)
    ki = lax.broadcasted_iota(jnp.int32, (blk, 2 * blk), 1)
    dist = qi + blk - ki
    in_win = (dist >= 0) & (dist < blk)
    distf = dist.astype(F32)
    for b in range(tq // blk):
        rows = slice(b * blk, (b + 1) * blk)
        q = cur_ref[rows, 0:nq] * (HEAD_DIM ** -0.5)
        k_c = cur_ref[rows, nq:nq + nkv]
        v_c = cur_ref[rows, nq + nkv:nq + 2 * nkv]
        if b == 0:
            k_p, v_p = prev_ref[:, 0:nkv], prev_ref[:, nkv:2 * nkv]
            valid = in_win & (ki >= blk * (1 - (i > 0).astype(jnp.int32)))
        else:
            prow = slice((b - 1) * blk, b * blk)
            k_p, v_p = cur_ref[prow, nq:nq + nkv], cur_ref[prow, nq + nkv:nq + 2 * nkv]
            valid = in_win
        outs = []
        for h in range(ATT_HEADS):
            g = h // (ATT_HEADS // ATT_KV_HEADS)
            gs = slice(g * HEAD_DIM, (g + 1) * HEAD_DIM)
            kcat = jnp.concatenate([k_p[:, gs], k_c[:, gs]], axis=0)
            vcat = jnp.concatenate([v_p[:, gs], v_c[:, gs]], axis=0)
            s = _dot(q[:, h * HEAD_DIM:(h + 1) * HEAD_DIM], kcat, P_ATT, _NT)
            slope = 2.0 ** (-8.0 * (h + 1) / ATT_HEADS)
            s = jnp.where(valid, s - slope * distf, -jnp.inf)
            sink = sink_ref[:, h:h + 1]
            m = jnp.maximum(jnp.max(s, axis=-1, keepdims=True), sink)
            p = jnp.exp(s - m)
            den = jnp.sum(p, axis=-1, keepdims=True) + jnp.exp(sink - m)
            outs.append(_dot(p, vcat, P_ATT) / den)
        o_ref[rows, :] = jnp.concatenate(outs, axis=-1)


def _attention(pb, sinks):
    t = pb.shape[0]
    tq = _tile(t, 512)
    nb = tq // ATT_BLOCK
    return pl.pallas_call(
        functools.partial(_attn_kernel, tq=tq),
        out_shape=jax.ShapeDtypeStruct((t, GROUP_W), F32),
        grid=(t // tq,),
        in_specs=[
            pl.BlockSpec((tq, N_B), lambda i: (i, 0)),
            pl.BlockSpec((ATT_BLOCK, 2 * ATT_KV_HEADS * HEAD_DIM), lambda i: (jnp.maximum(i * nb - 1, 0), 1)),
            _full((1, ATT_HEADS)),
        ],
        out_specs=pl.BlockSpec((tq, GROUP_W), lambda i: (i, 0)),
        compiler_params=_cparams(("parallel",)),
        name="swa_attention",
    )(pb, pb, sinks)


def _rwkv_kernel(*refs, tt, has_vres):
    if has_vres:
        (pd_ref, prev_ref, vf_ref, tril_ref, mu_ref, w0_ref, w2_ref, a0_ref, a2_ref, g2_ref, kk_ref, ka_ref,
         rk_ref, lng_ref, lnb_ref, v0_ref, v1_ref, v2_ref, y_ref, st_ref, rqm_s, y0c_s, yr_s) = refs
    else:
        (pd_ref, prev_ref, tril_ref, mu_ref, w0_ref, w2_ref, a0_ref, a2_ref, g2_ref, kk_ref, ka_ref,
         rk_ref, lng_ref, lnb_ref, y_ref, vout_ref, st_ref, rqm_s, y0c_s, yr_s) = refs
    i = pl.program_id(0)
    nh, hd, gw = RWKV_HEADS, RWKV_HEAD, GROUP_W
    P = P_RWKV

    @pl.when(i == 0)
    def _():
        st_ref[...] = jnp.zeros_like(st_ref)

    hr = lax.broadcasted_iota(jnp.int32, (gw, gw), 0) // hd
    hc = lax.broadcasted_iota(jnp.int32, (gw, gw), 1) // hd
    hsum = (hr == hc).astype(F32)

    p = pd_ref[...]
    row = lax.broadcasted_iota(jnp.int32, (tt, 1), 0)
    prev_row = prev_ref[SUBLANES - 1:SUBLANES, :] * (i > 0).astype(F32)
    shifted = jnp.where(row == 0, prev_row, pltpu.roll(p, 1, axis=0))
    z = p + (shifted - p) * mu_ref[...]
    r = z[:, 0:gw]
    k = z[:, gw:2 * gw]
    v = z[:, 2 * gw:3 * gw]
    o = 3 * gw
    wd = z[:, o:o + LORA_W]
    ad = z[:, o + LORA_W:o + LORA_W + LORA_A]
    gd = z[:, o + LORA_W + LORA_A:o + LORA_W + LORA_A + LORA_G]
    wpre = -(w0_ref[...] + _dot(jnp.tanh(wd), w2_ref[...], P))
    softplus = jnp.maximum(wpre, 0.0) + jnp.log(1.0 + jnp.exp(-jnp.abs(wpre)))
    lw = -jnp.exp(-softplus - 0.5)
    a = _sigmoid(a0_ref[...] + _dot(ad, a2_ref[...], P))
    g = _dot(_sigmoid(gd), g2_ref[...], P)
    if has_vres:
        mix = _sigmoid(v0_ref[...] + _dot(_dot(v, v1_ref[...], P), v2_ref[...], P))
        v = v + (vf_ref[...] - v) * mix
    else:
        vout_ref[...] = v
    kk = k * kk_ref[...]
    kk = kk / jnp.maximum(jnp.sqrt(_dot(kk * kk, hsum, P)), 1e-12)
    k = k * (1.0 + (a - 1.0) * ka_ref[...])
    nch = tt // CHUNK
    L = CHUNK
    ti = lax.broadcasted_iota(jnp.int32, (L, L), 0)
    tj = lax.broadcasted_iota(jnp.int32, (L, L), 1)
    low_strict = ti > tj
    low_incl = ti >= tj
    eye = (ti == tj).astype(F32)

    def to_batch(t2):
        return jnp.concatenate([t2[:, h * hd:(h + 1) * hd].reshape(nch, L, hd) for h in range(nh)], axis=0)

    lw_hi, lw_lo = _split(lw)
    cl = _mm(tril_ref[...], lw_hi, _NN) + _mm(tril_ref[...], lw_lo, _NN)
    cl3 = cl.reshape(nch, L, gw)
    cl_end = cl3[:, L - 1:L, :]
    e_end = jnp.exp(cl_end - cl3).reshape(tt, gw)
    e_neg = jnp.exp(-cl)
    bb = kk * a
    at = to_batch(-kk * jnp.exp(cl - lw))
    bh = to_batch(bb * e_neg)
    kh = to_batch(k * e_neg)
    rt = to_batch(r * jnp.exp(cl))
    be = to_batch(bb * e_end)
    ke = to_batch(k * e_end)
    vb = to_batch(v)
    p_end = jnp.exp(cl_end)
    pe = jnp.concatenate([p_end[:, :, h * hd:(h + 1) * hd] for h in range(nh)], axis=0)

    P = P_CHUNK
    lhs = jnp.concatenate([at, rt], axis=1)
    g_b = _bdot(lhs, bh, P, nt=True)
    g_k = _bdot(lhs, kh, P, nt=True)
    a_ab = jnp.where(low_strict, g_b[:, :L], 0.0)
    b_rb = jnp.where(low_incl, g_b[:, L:], 0.0)
    a_ak = jnp.where(low_strict, g_k[:, :L], 0.0)
    b_rk = jnp.where(low_incl, g_k[:, L:], 0.0)
    apow = _bdot(a_ab, a_ab, P)
    tinv = eye + a_ab
    for _ in range(4):
        sq = _bdot(jnp.concatenate([apow, tinv], axis=1), apow, P)
        tinv = tinv + sq[:, L:]
        apow = sq[:, :L]
    tinv = tinv + _bdot(tinv, apow, P)
    w1 = _bdot(tinv, at, P)
    u0 = _bdot(tinv, _bdot(a_ak, vb, P), P)
    lhs_b = jnp.concatenate([b_rb, jnp.swapaxes(be, 1, 2)], axis=1)
    lhs_k = jnp.concatenate([b_rk, jnp.swapaxes(ke, 1, 2)], axis=1)
    o_w = _bdot(lhs_b, w1, P)
    o_u = _bdot(lhs_b, u0, P) + _bdot(lhs_k, vb, P)
    rqm_s[...] = jnp.concatenate([rt + o_w[:, :L], eye * pe + o_w[:, L:]], axis=1)
    y0c_s[...] = o_u

    zs = [st_ref[h] for h in range(nh)]
    for c in range(nch):
        ys = []
        for h in range(nh):
            b = h * nch + c
            yz = _dot(rqm_s[b], zs[h], P_STATE) + y0c_s[b]
            ys.append(yz[:L])
            zs[h] = yz[L:]
        yr_s[c * L:(c + 1) * L, :] = jnp.concatenate(ys, axis=-1)
    for h in range(nh):
        st_ref[h] = zs[h]

    P = P_RWKV
    y = yr_s[...]
    inv_n = 1.0 / hd
    mean = _dot(y, hsum, P) * inv_n
    yc = y - mean
    var = _dot(yc * yc, hsum, P) * inv_n
    yn = yc * lax.rsqrt(var + RWKV_GN_EPS) * lng_ref[...] + lnb_ref[...]
    bonus = _dot(r * k * rk_ref[...], hsum, P) * v
    y_ref[...] = (yn + bonus) * g


def _rwkv(pd, v_first, prm):
    t = pd.shape[0]
    tt = _tile(t, 512)
    nch = tt // CHUNK
    has_vres = v_first is not None
    gw = GROUP_W
    row_blk = lambda n: pl.BlockSpec((tt, n), lambda i: (i, 0))
    vec = _full((1, gw))
    in_specs = [row_blk(N_D),
                pl.BlockSpec((SUBLANES, N_D), lambda i: (jnp.maximum(i * (tt // SUBLANES) - 1, 0), 0))]
    args = [pd, pd]
    if has_vres:
        in_specs.append(row_blk(gw))
        args.append(v_first)
    tpos = jnp.arange(tt)
    tril = ((tpos[:, None] // CHUNK == tpos[None, :] // CHUNK) & (tpos[:, None] >= tpos[None, :])).astype(BF16)
    in_specs.append(_full((tt, tt)))
    args.append(tril)
    in_specs += [_full((1, N_D)), vec, _full((LORA_W, gw)), vec, _full((LORA_A, gw)), _full((LORA_G, gw)),
                 vec, vec, vec, vec, vec]
    args += [prm["mu"], prm["w0"], prm["w2"], prm["a0"], prm["a2"], prm["g2"], prm["k_k"], prm["k_a"],
             prm["r_k"], prm["ln_g"], prm["ln_b"]]
    if has_vres:
        in_specs += [vec, _full((gw, 32)), _full((32, gw))]
        args += [prm["v0"], prm["v1"], prm["v2"]]
        out_shape = jax.ShapeDtypeStruct((t, gw), F32)
        out_specs = row_blk(gw)
    else:
        out_shape = [jax.ShapeDtypeStruct((t, gw), F32)] * 2
        out_specs = [row_blk(gw)] * 2
    op = pltpu.VMEM((nch * RWKV_HEADS, 2 * CHUNK, RWKV_HEAD), F32)
    scratch = [pltpu.VMEM((RWKV_HEADS, RWKV_HEAD, RWKV_HEAD), F32), op, op, pltpu.VMEM((tt, gw), F32)]
    return pl.pallas_call(
        functools.partial(_rwkv_kernel, tt=tt, has_vres=has_vres),
        out_shape=out_shape,
        grid=(t // tt,),
        in_specs=in_specs,
        out_specs=out_specs,
        scratch_shapes=scratch,
        compiler_params=_cparams(("arbitrary",)),
        name="rwkv7",
    )(*args)


def _outproj_kernel(ya_ref, yb_ref, yc_ref, yd_ref, x_ref, wo_ref, gt_ref, g_ref, sc_ref, sh_ref, wr_ref, br_ref,
                    x1_ref, h2_ref, gate_ref, *, tm):
    y = None
    for j, y_ref in enumerate((ya_ref, yb_ref, yc_ref, yd_ref)):
        rows = slice(j * GROUP_W, (j + 1) * GROUP_W)
        part = _dot_w(y_ref[...], (wo_ref[0, rows, :], wo_ref[-1, rows, :]), P_OUT)
        y = part if y is None else y + part
    x1 = x_ref[...] + gt_ref[...] * y
    x1_ref[...] = x1
    h2 = _modulated_norm(x1, g_ref[...], sc_ref[...], sh_ref[...])
    h2_ref[...] = h2.astype(h2_ref.dtype)
    lg = _dot(h2, wr_ref[...], P_ROUTER) + br_ref[...]
    lane = lax.broadcasted_iota(jnp.int32, (tm, LANES), 1)
    lanef = lane.astype(F32)
    gl = jnp.where(lane < N_GROUPS, lg[:, 0:LANES], -jnp.inf)
    gmax = jnp.max(gl, axis=-1, keepdims=True)
    gp = 1.0 / jnp.sum(jnp.exp(gl - gmax), axis=-1, keepdims=True)
    gi = jnp.min(jnp.where(gl == gmax, lanef, float(LANES)), axis=-1, keepdims=True)
    lo = gi * EXP_PER_GROUP
    in_grp = (lanef >= lo) & (lanef < lo + EXP_PER_GROUP)
    el = jnp.where(in_grp, lg[:, LANES:2 * LANES], -jnp.inf)
    m1 = jnp.max(el, axis=-1, keepdims=True)
    i1 = jnp.min(jnp.where(el == m1, lanef, float(LANES)), axis=-1, keepdims=True)
    el2 = jnp.where(lanef == i1, -jnp.inf, el)
    m2 = jnp.max(el2, axis=-1, keepdims=True)
    i2 = jnp.min(jnp.where(el2 == m2, lanef, float(LANES)), axis=-1, keepdims=True)
    e2 = jnp.exp(m2 - m1)
    w_top = gp / (1.0 + e2)
    gate_ref[...] = jnp.where(lanef == i1, w_top, jnp.where(lanef == i2, w_top * e2, 0.0))


def _out_proj(ya, yb, yc, yd, x, wo_parts, gt, g, sc, sh, w_router, b_router):
    t = x.shape[0]
    tm = _tile(t, 512)
    row_blk = lambda n: pl.BlockSpec((tm, n), lambda i: (i, 0))
    vec = _full((1, D_MODEL))
    return pl.pallas_call(
        functools.partial(_outproj_kernel, tm=tm),
        out_shape=[jax.ShapeDtypeStruct((t, D_MODEL), F32), jax.ShapeDtypeStruct((t, D_MODEL), BF16),
                   jax.ShapeDtypeStruct((t, LANES), F32)],
        grid=(t // tm,),
        in_specs=[row_blk(GROUP_W)] * 4 + [row_blk(D_MODEL), _full(wo_parts.shape), vec, vec, vec, vec,
                                           _full((D_MODEL, 2 * LANES)), _full((1, 2 * LANES))],
        out_specs=[row_blk(D_MODEL), row_blk(D_MODEL), row_blk(LANES)],
        compiler_params=_cparams(("parallel",)),
        name="out_proj_router",
    )(ya, yb, yc, yd, x, wo_parts, gt, g, sc, sh, w_router, b_router)


def _moe_kernel(h_ref, gate_ref, ex_ref, wg_ref, wu_ref, wd_ref, x1_ref, gt_ref, fg_ref, o_ref, acc_ref, *, final):
    e = pl.program_id(1)

    @pl.when(e == 0)
    def _():
        acc_ref[...] = jnp.zeros_like(acc_ref)

    h = h_ref[...]
    hg = _mm(h, wg_ref[...], _NN)
    hu = _mm(h, wu_ref[...], _NN)
    gates = _mm(gate_ref[...].astype(BF16), ex_ref[...], _NN)
    act = hg * _sigmoid(hg) * hu * gates
    acc_ref[...] += _mm(act.astype(BF16), wd_ref[...], _NN)

    @pl.when(e == pl.num_programs(1) - 1)
    def _():
        out = x1_ref[...] + gt_ref[...] * acc_ref[...]
        if final:
            out = out * lax.rsqrt(jnp.mean(out * out, axis=-1, keepdims=True) + NORM_EPS) * fg_ref[...]
        o_ref[...] = out


def _moe(h2, gates, expand, wg, wu, wd, x1, gt, final_g, final):
    t = h2.shape[0]
    tm = _tile(t, 512)
    eb = 8
    nw = eb * D_EXPERT
    vec = _full((1, D_MODEL))
    return pl.pallas_call(
        functools.partial(_moe_kernel, final=final),
        out_shape=jax.ShapeDtypeStruct((t, D_MODEL), F32),
        grid=(t // tm, N_EXPERTS // eb),
        in_specs=[
            pl.BlockSpec((tm, D_MODEL), lambda i, e: (i, 0)),
            pl.BlockSpec((tm, LANES), lambda i, e: (i, 0)),
            pl.BlockSpec((LANES, nw), lambda i, e: (0, e)),
            pl.BlockSpec((D_MODEL, nw), lambda i, e: (0, e)),
            pl.BlockSpec((D_MODEL, nw), lambda i, e: (0, e)),
            pl.BlockSpec((nw, D_MODEL), lambda i, e: (e, 0)),
            pl.BlockSpec((tm, D_MODEL), lambda i, e: (i, 0)),
            vec, vec,
        ],
        out_specs=pl.BlockSpec((tm, D_MODEL), lambda i, e: (i, 0)),
        scratch_shapes=[pltpu.VMEM((tm, D_MODEL), F32)],
        compiler_params=_cparams(("parallel", "arbitrary")),
        name="moe_experts",
    )(h2, gates, expand, wg, wu, wd, x1, gt, final_g)


def kernel(x, c, w_ada, b_ada, norm1_g, norm2_g, w_in, w_out, pool_w, pool_scale, attn_sinks, conv_w, rwkv_mu,
           rwkv_w0, rwkv_w2, rwkv_a0, rwkv_a2, rwkv_g2, rwkv_k_k, rwkv_k_a, rwkv_r_k, rwkv_ln_g, rwkv_ln_b,
           rwkv_v0, rwkv_v1, rwkv_v2, moe_w_grp, moe_b_grp, moe_w_exp, moe_b_exp, moe_w_gate, moe_w_up,
           moe_w_down, final_g):
    bsz, t, d = x.shape
    assert bsz == 1 and d == D_MODEL
    depth = w_ada.shape[0]
    xs = x.reshape(t, d)
    mod = _ada_mod(c, w_ada, b_ada)
    expand = (jnp.arange(LANES)[:, None] == jnp.arange(N_EXPERTS * D_EXPERT)[None, :] // D_EXPERT).astype(BF16)
    v_first = None
    row = lambda a: a.reshape(1, -1)
    for l in range(depth):
        sh1, sc1, gt1, sh2, sc2, gt2 = [mod[l, :, j * d:(j + 1) * d] for j in range(6)]
        pa, pb, pc, pd = _in_proj(xs, row(norm1_g[l]), sc1, sh1, _split_w(w_in[l], P_IN))
        wp = jnp.zeros((GROUP_W, GROUP_W), F32)
        for gidx in range(len(POOL_WINDOWS)):
            sl = slice(gidx * POOL_CG, (gidx + 1) * POOL_CG)
            wp = wp.at[sl, sl].set(pool_w[l, gidx])
        ya, yc = _pool_conv(pa, pc, _split_w(wp, P_POOL), row(pool_scale[l]), conv_w[l])
        yb = _attention(pb, row(attn_sinks[l]))
        prm = dict(mu=row(rwkv_mu[l]), w0=row(rwkv_w0[l]), w2=rwkv_w2[l], a0=row(rwkv_a0[l]), a2=rwkv_a2[l],
                   g2=rwkv_g2[l], k_k=row(rwkv_k_k[l]), k_a=row(rwkv_k_a[l]), r_k=row(rwkv_r_k[l]),
                   ln_g=row(rwkv_ln_g[l]), ln_b=row(rwkv_ln_b[l]))
        if l == 0:
            yd, v_first = _rwkv(pd, None, prm)
        else:
            prm.update(v0=row(rwkv_v0[l - 1]), v1=rwkv_v1[l - 1], v2=rwkv_v2[l - 1])
            yd = _rwkv(pd, v_first, prm)
        w_router = jnp.zeros((d, 2 * LANES), F32)
        w_router = w_router.at[:, 0:N_GROUPS].set(moe_w_grp[l]).at[:, LANES:LANES + N_EXPERTS].set(moe_w_exp[l])
        b_router = jnp.zeros((1, 2 * LANES), F32)
        b_router = b_router.at[0, 0:N_GROUPS].set(moe_b_grp[l]).at[0, LANES:LANES + N_EXPERTS].set(moe_b_exp[l])
        x1, h2, gates = _out_proj(ya, yb, yc, yd, xs, _split_w(w_out[l], P_OUT), gt1, row(norm2_g[l]), sc2, sh2,
                                  w_router, b_router)
        wg = jnp.transpose(moe_w_gate[l].astype(BF16), (1, 0, 2)).reshape(d, N_EXPERTS * D_EXPERT)
        wu = jnp.transpose(moe_w_up[l].astype(BF16), (1, 0, 2)).reshape(d, N_EXPERTS * D_EXPERT)
        wd = moe_w_down[l].astype(BF16).reshape(N_EXPERTS * D_EXPERT, d)
        xs = _moe(h2, gates, expand, wg, wu, wd, x1, gt2, row(final_g), final=(l == depth - 1))
    return xs.reshape(bsz, t, d)
```

```python
import functools

import jax
import jax.numpy as jnp
from jax import lax
from jax.experimental import pallas as pl
from jax.experimental.pallas import tpu as pltpu

F32 = jnp.float32
BF16 = jnp.bfloat16

D_MODEL = 1024
GROUP_W = 256
NORM_EPS = 1e-6
POOL_WINDOWS = (2, 4, 8, 16)
POOL_CG = 64
HEAD_DIM = 64
ATT_HEADS = 4
ATT_KV_HEADS = 2
ATT_BLOCK = 128
CONV_W = 3
RWKV_HEAD = 64
RWKV_HEADS = 4
LORA_W = 32
LORA_A = 32
LORA_G = 64
RWKV_GN_EPS = 64e-5
N_A = GROUP_W
N_B = (ATT_HEADS + 2 * ATT_KV_HEADS) * HEAD_DIM
N_C = 3 * GROUP_W
N_D = 3 * GROUP_W + LORA_W + LORA_A + LORA_G
N_IN = N_A + N_B + N_C + N_D
N_GROUPS = 4
EXP_PER_GROUP = 8
N_EXPERTS = 32
D_EXPERT = 128

LANES = 128
SUBLANES = 8
VMEM_LIMIT = 56 * 1024 * 1024

CHUNK = 64
POOL_HALO = 16
CONV_HALO = 8

P_IN = 1
P_OUT = 1
P_POOL = 1
P_ATT = 1
P_MOE = 1
P_RWKV = 3
P_CHUNK = 1
P_STATE = 3
P_ROUTER = 3


def _split(a):
    hi = a.astype(BF16)
    lo = (a - hi.astype(F32)).astype(BF16)
    return hi, lo


def _mm(a, b, dims):
    return lax.dot_general(a, b, (dims, ((), ())), preferred_element_type=F32)


_NN = ((1,), (0,))
_NT = ((1,), (1,))


def _dot(a, b, passes=1, dims=_NN):
    if passes == 1:
        return _mm(a.astype(BF16), b.astype(BF16), dims)
    a0, a1 = _split(a)
    b0, b1 = _split(b)
    return _mm(a0, b0, dims) + (_mm(a0, b1, dims) + _mm(a1, b0, dims))


def _bmm(a, b, nt):
    dims = (((2,), (2 if nt else 1,)), ((0,), (0,)))
    return lax.dot_general(a, b, dims, preferred_element_type=F32)


def _bdot(a, b, passes=1, nt=False):
    if passes == 1:
        return _bmm(a.astype(BF16), b.astype(BF16), nt)
    a0, a1 = _split(a)
    b0, b1 = _split(b)
    return _bmm(a0, b0, nt) + (_bmm(a0, b1, nt) + _bmm(a1, b0, nt))


def _lhs_w(a, passes):
    if passes == 1:
        return a.astype(BF16)
    a0, a1 = _split(a)
    return jnp.concatenate([a0, a0, a1], axis=1)


def _dot_w(a, w_cat, passes):
    return _mm(_lhs_w(a, passes), w_cat, _NN)


def _split_w(w, passes):
    hi = w.astype(BF16)
    if passes == 1:
        return hi
    lo = (w - hi.astype(F32)).astype(BF16)
    return jnp.concatenate([hi, lo, hi], axis=0)


def _sigmoid(x):
    return 1.0 / (1.0 + jnp.exp(-x))


def _cparams(sem):
    return pltpu.CompilerParams(dimension_semantics=sem, vmem_limit_bytes=VMEM_LIMIT)


def _full(shape):
    return pl.BlockSpec(shape, lambda *_: (0,) * len(shape))


def _tile(n, pref):
    t = min(n, pref)
    assert n % t == 0, (n, t)
    return t


def _mod_kernel(c_ref, w_ref, b_ref, o_ref):
    c = c_ref[...]
    cond = c * _sigmoid(c)
    o_ref[...] = _dot(cond, w_ref[...], 3) + b_ref[...]


def _ada_mod(c, w_ada, b_ada):
    depth = w_ada.shape[0]
    c8 = jnp.broadcast_to(c, (SUBLANES, D_MODEL))
    out = pl.pallas_call(
        _mod_kernel,
        out_shape=jax.ShapeDtypeStruct((depth, SUBLANES, 6 * D_MODEL), F32),
        grid=(depth, 6),
        in_specs=[
            pl.BlockSpec((SUBLANES, D_MODEL), lambda l, j: (0, 0)),
            pl.BlockSpec((None, D_MODEL, D_MODEL), lambda l, j: (l, 0, j)),
            pl.BlockSpec((None, 1, D_MODEL), lambda l, j: (l, 0, j)),
        ],
        out_specs=pl.BlockSpec((None, SUBLANES, D_MODEL), lambda l, j: (l, 0, j)),
        compiler_params=_cparams(("arbitrary", "arbitrary")),
        name="ada_mod",
    )(c8, w_ada, b_ada.reshape(depth, 1, 6 * D_MODEL))
    return out[:, 0:1, :]


def _modulated_norm(x, g, sc, sh):
    y = x * lax.rsqrt(jnp.mean(x * x, axis=-1, keepdims=True) + NORM_EPS) * g
    return y * (1.0 + sc) + sh


def _inproj_kernel(x_ref, g_ref, sc_ref, sh_ref, w_ref, pa_ref, pb_ref, pc_ref, pd_ref, *, passes):
    h = _modulated_norm(x_ref[...], g_ref[...], sc_ref[...], sh_ref[...])
    lhs = _lhs_w(h, passes)
    col = 0
    for o_ref, width in ((pa_ref, N_A), (pb_ref, N_B), (pc_ref, N_C), (pd_ref, N_D)):
        o_ref[...] = _mm(lhs, w_ref[:, col:col + width], _NN)
        col += width


def _in_proj(x, g, sc, sh, w_parts):
    t = x.shape[0]
    tm = _tile(t, 512)
    vec = _full((1, D_MODEL))
    return pl.pallas_call(
        functools.partial(_inproj_kernel, passes=P_IN),
        out_shape=[jax.ShapeDtypeStruct((t, n), F32) for n in (N_A, N_B, N_C, N_D)],
        grid=(t // tm,),
        in_specs=[pl.BlockSpec((tm, D_MODEL), lambda i: (i, 0)), vec, vec, vec, _full(w_parts.shape)],
        out_specs=[pl.BlockSpec((tm, n), lambda i: (i, 0)) for n in (N_A, N_B, N_C, N_D)],
        compiler_params=_cparams(("parallel",)),
        name="in_proj",
    )(x, g, sc, sh, w_parts)


def _shift_rows(ext, n):
    return pltpu.roll(ext, n, axis=0)


def _pool_conv_kernel(pa_ref, pah_ref, pc_ref, pch_ref, wp_ref, scale_ref, cw_ref, ya_ref, yc_ref, *, tt):
    i = pl.program_id(0)
    not_first = (i > 0).astype(F32)
    u = pa_ref[...]
    ext = jnp.concatenate([pah_ref[...] * not_first, u], axis=0)
    s2 = ext + _shift_rows(ext, 1)
    s4 = s2 + _shift_rows(s2, 2)
    s8 = s4 + _shift_rows(s4, 4)
    s16 = s8 + _shift_rows(s8, 8)
    grp = lax.broadcasted_iota(jnp.int32, (tt, GROUP_W), 1) // POOL_CG
    tpos = lax.broadcasted_iota(jnp.int32, (tt, GROUP_W), 0) + (i * tt + 1)
    wsum = jnp.where(grp == 0, s2[POOL_HALO:], jnp.where(grp == 1, s4[POOL_HALO:],
                     jnp.where(grp == 2, s8[POOL_HALO:], s16[POOL_HALO:])))
    win = jnp.where(grp == 0, 2, jnp.where(grp == 1, 4, jnp.where(grp == 2, 8, 16)))
    cnt = jnp.minimum(tpos, win).astype(F32)
    d = wsum / cnt - u
    ya_ref[...] = _dot_w(d, wp_ref[...], P_POOL) * scale_ref[...]
    z = pc_ref[:, GROUP_W:2 * GROUP_W] * pc_ref[:, 2 * GROUP_W:3 * GROUP_W]
    zh = pch_ref[:, GROUP_W:2 * GROUP_W] * pch_ref[:, 2 * GROUP_W:3 * GROUP_W] * not_first
    zext = jnp.concatenate([zh, z], axis=0)
    y = (cw_ref[2:3, :] * z + cw_ref[1:2, :] * _shift_rows(zext, 1)[CONV_HALO:]
         + cw_ref[0:1, :] * _shift_rows(zext, 2)[CONV_HALO:])
    yc_ref[...] = pc_ref[:, 0:GROUP_W] * y


def _pool_conv(pa, pc, wp_parts, pool_scale, conv_w):
    t = pa.shape[0]
    tt = _tile(t, 512)
    rp, rc = tt // POOL_HALO, tt // CONV_HALO
    return pl.pallas_call(
        functools.partial(_pool_conv_kernel, tt=tt),
        out_shape=[jax.ShapeDtypeStruct((t, GROUP_W), F32)] * 2,
        grid=(t // tt,),
        in_specs=[
            pl.BlockSpec((tt, N_A), lambda i: (i, 0)),
            pl.BlockSpec((POOL_HALO, N_A), lambda i: (jnp.maximum(i * rp - 1, 0), 0)),
            pl.BlockSpec((tt, N_C), lambda i: (i, 0)),
            pl.BlockSpec((CONV_HALO, N_C), lambda i: (jnp.maximum(i * rc - 1, 0), 0)),
            _full(wp_parts.shape), _full((1, GROUP_W)), _full((CONV_W, GROUP_W)),
        ],
        out_specs=[pl.BlockSpec((tt, GROUP_W), lambda i: (i, 0))] * 2,
        compiler_params=_cparams(("parallel",)),
        name="pool_conv",
    )(pa, pa, pc, pc, wp_parts, pool_scale, conv_w)


def _attn_kernel(cur_ref, prev_ref, sink_ref, o_ref, *, tq):
    i = pl.program_id(0)
    nq = ATT_HEADS * HEAD_DIM
    nkv = ATT_KV_HEADS * HEAD_DIM
    blk = ATT_BLOCK
    hd = HEAD_DIM
    rep = ATT_HEADS // ATT_KV_HEADS
    nblk = tq // blk
    ri = lax.broadcasted_iota(jnp.int32, (rep * blk, 2 * blk), 0)
    ki = lax.broadcasted_iota(jnp.int32, (rep * blk, 2 * blk), 1)
    dist = (ri % blk) + blk - ki
    in_win = (dist >= 0) & (dist < blk)
    distf = dist.astype(F32)
    head_col = lax.broadcasted_iota(jnp.int32, (rep * blk, 1), 0) // blk
    bias, sink = [], []
    for g in range(ATT_KV_HEADS):
        slope_g = jnp.zeros((rep * blk, 1), F32)
        sink_g = jnp.zeros((rep * blk, 1), F32)
        for j in range(rep):
            h = g * rep + j
            slope_g = jnp.where(head_col == j, 2.0 ** (-8.0 * (h + 1) / ATT_HEADS), slope_g)
            sink_g = jnp.where(head_col == j, sink_ref[:, h:h + 1], sink_g)
        bias.append(slope_g * distf)
        sink.append(sink_g)
    bias = jnp.stack(bias)[None]
    sink = jnp.stack(sink)[None]

    q = cur_ref[:, 0:nq] * (hd ** -0.5)
    kv = jnp.concatenate([prev_ref[...], cur_ref[:, nq:nq + 2 * nkv]], axis=0)
    qs, ks, vs = [], [], []
    for b in range(nblk):
        for g in range(ATT_KV_HEADS):
            qs.append(jnp.concatenate(
                [q[b * blk:(b + 1) * blk, (g * rep + j) * hd:(g * rep + j + 1) * hd] for j in range(rep)], axis=0))
            ks.append(kv[b * blk:(b + 2) * blk, g * hd:(g + 1) * hd])
            vs.append(kv[b * blk:(b + 2) * blk, nkv + g * hd:nkv + (g + 1) * hd])
    s = _bdot(jnp.stack(qs), jnp.stack(ks), P_ATT, nt=True)
    s = s.reshape(nblk, ATT_KV_HEADS, rep * blk, 2 * blk)
    s = jnp.where(in_win, s - bias, -jnp.inf)
    first_ok = ki >= blk * (1 - (i > 0).astype(jnp.int32))
    s = jnp.concatenate([jnp.where(first_ok, s[0:1], -jnp.inf), s[1:]], axis=0)
    m = jnp.maximum(jnp.max(s, axis=-1, keepdims=True), sink)
    p = jnp.exp(s - m)
    den = jnp.sum(p, axis=-1, keepdims=True) + jnp.exp(sink - m)
    o = _bdot(p.reshape(nblk * ATT_KV_HEADS, rep * blk, 2 * blk), jnp.stack(vs), P_ATT)
    o = o.reshape(nblk, ATT_KV_HEADS, rep * blk, hd) / den
    for b in range(nblk):
        o_ref[b * blk:(b + 1) * blk, :] = jnp.concatenate(
            [o[b, g, j * blk:(j + 1) * blk, :] for g in range(ATT_KV_HEADS) for j in range(rep)], axis=-1)


def _attention(pb, sinks):
    t = pb.shape[0]
    tq = _tile(t, 512)
    nb = tq // ATT_BLOCK
    return pl.pallas_call(
        functools.partial(_attn_kernel, tq=tq),
        out_shape=jax.ShapeDtypeStruct((t, GROUP_W), F32),
        grid=(t // tq,),
        in_specs=[
            pl.BlockSpec((tq, N_B), lambda i: (i, 0)),
            pl.BlockSpec((ATT_BLOCK, 2 * ATT_KV_HEADS * HEAD_DIM), lambda i: (jnp.maximum(i * nb - 1, 0), 1)),
            _full((1, ATT_HEADS)),
        ],
        out_specs=pl.BlockSpec((tq, GROUP_W), lambda i: (i, 0)),
        compiler_params=_cparams(("parallel",)),
        name="swa_attention",
    )(pb, pb, sinks)


def _rwkv_kernel(*refs, tt, has_vres):
    if has_vres:
        (pd_ref, prev_ref, vf_ref, tril_ref, mu_ref, w0_ref, w2_ref, a0_ref, a2_ref, g2_ref, kk_ref, ka_ref,
         rk_ref, lng_ref, lnb_ref, v0_ref, v1_ref, v2_ref, y_ref, st_ref, rqm_s, y0c_s, yr_s) = refs
    else:
        (pd_ref, prev_ref, tril_ref, mu_ref, w0_ref, w2_ref, a0_ref, a2_ref, g2_ref, kk_ref, ka_ref,
         rk_ref, lng_ref, lnb_ref, y_ref, vout_ref, st_ref, rqm_s, y0c_s, yr_s) = refs
    i = pl.program_id(0)
    nh, hd, gw = RWKV_HEADS, RWKV_HEAD, GROUP_W
    P = P_RWKV

    @pl.when(i == 0)
    def _():
        st_ref[...] = jnp.zeros_like(st_ref)

    hr = lax.broadcasted_iota(jnp.int32, (gw, gw), 0) // hd
    hc = lax.broadcasted_iota(jnp.int32, (gw, gw), 1) // hd
    hsum = (hr == hc).astype(F32)

    p = pd_ref[...]
    row = lax.broadcasted_iota(jnp.int32, (tt, 1), 0)
    prev_row = prev_ref[SUBLANES - 1:SUBLANES, :] * (i > 0).astype(F32)
    shifted = jnp.where(row == 0, prev_row, pltpu.roll(p, 1, axis=0))
    z = p + (shifted - p) * mu_ref[...]
    r = z[:, 0:gw]
    k = z[:, gw:2 * gw]
    v = z[:, 2 * gw:3 * gw]
    o = 3 * gw
    wd = z[:, o:o + LORA_W]
    ad = z[:, o + LORA_W:o + LORA_W + LORA_A]
    gd = z[:, o + LORA_W + LORA_A:o + LORA_W + LORA_A + LORA_G]
    wpre = -(w0_ref[...] + _dot(jnp.tanh(wd), w2_ref[...], P))
    softplus = jnp.maximum(wpre, 0.0) + jnp.log(1.0 + jnp.exp(-jnp.abs(wpre)))
    lw = -jnp.exp(-softplus - 0.5)
    a = _sigmoid(a0_ref[...] + _dot(ad, a2_ref[...], P))
    g = _dot(_sigmoid(gd), g2_ref[...], P)
    if has_vres:
        mix = _sigmoid(v0_ref[...] + _dot(_dot(v, v1_ref[...], P), v2_ref[...], P))
        v = v + (vf_ref[...] - v) * mix
    else:
        vout_ref[...] = v
    kk = k * kk_ref[...]
    kk = kk / jnp.maximum(jnp.sqrt(_dot(kk * kk, hsum, P)), 1e-12)
    k = k * (1.0 + (a - 1.0) * ka_ref[...])
    nch = tt // CHUNK
    L = CHUNK
    ti = lax.broadcasted_iota(jnp.int32, (L, L), 0)
    tj = lax.broadcasted_iota(jnp.int32, (L, L), 1)
    low_strict = ti > tj
    low_incl = ti >= tj
    eye = (ti == tj).astype(F32)

    def to_batch(t2):
        return jnp.concatenate([t2[:, h * hd:(h + 1) * hd].reshape(nch, L, hd) for h in range(nh)], axis=0)

    lw_hi, lw_lo = _split(lw)
    cl = _mm(tril_ref[...], lw_hi, _NN) + _mm(tril_ref[...], lw_lo, _NN)
    cl3 = cl.reshape(nch, L, gw)
    cl_end = cl3[:, L - 1:L, :]
    e_end = jnp.exp(cl_end - cl3).reshape(tt, gw)
    e_neg = jnp.exp(-cl)
    bb = kk * a
    at = to_batch(-kk * jnp.exp(cl - lw))
    bh = to_batch(bb * e_neg)
    kh = to_batch(k * e_neg)
    rt = to_batch(r * jnp.exp(cl))
    be = to_batch(bb * e_end)
    ke = to_batch(k * e_end)
    vb = to_batch(v)
    p_end = jnp.exp(cl_end)
    pe = jnp.concatenate([p_end[:, :, h * hd:(h + 1) * hd] for h in range(nh)], axis=0)

    P = P_CHUNK
    lhs = jnp.concatenate([at, rt], axis=1)
    g_b = _bdot(lhs, bh, P, nt=True)
    g_k = _bdot(lhs, kh, P, nt=True)
    a_ab = jnp.where(low_strict, g_b[:, :L], 0.0)
    b_rb = jnp.where(low_incl, g_b[:, L:], 0.0)
    a_ak = jnp.where(low_strict, g_k[:, :L], 0.0)
    b_rk = jnp.where(low_incl, g_k[:, L:], 0.0)
    apow = _bdot(a_ab, a_ab, P)
    tinv = eye + a_ab
    for _ in range(4):
        sq = _bdot(jnp.concatenate([apow, tinv], axis=1), apow, P)
        tinv = tinv + sq[:, L:]
        apow = sq[:, :L]
    tinv = tinv + _bdot(tinv, apow, P)
    w1 = _bdot(tinv, at, P)
    u0 = _bdot(tinv, _bdot(a_ak, vb, P), P)
    lhs_b = jnp.concatenate([b_rb, jnp.swapaxes(be, 1, 2)], axis=1)
    lhs_k = jnp.concatenate([b_rk, jnp.swapaxes(ke, 1, 2)], axis=1)
    o_w = _bdot(lhs_b, w1, P)
    o_u = _bdot(lhs_b, u0, P) + _bdot(lhs_k, vb, P)
    rqm_s[...] = jnp.concatenate([rt + o_w[:, :L], eye * pe + o_w[:, L:]], axis=1)
    y0c_s[...] = o_u

    zs = [st_ref[h] for h in range(nh)]
    for c in range(nch):
        ys = []
        for h in range(nh):
            b = h * nch + c
            yz = _dot(rqm_s[b], zs[h], P_STATE) + y0c_s[b]
            ys.append(yz[:L])
            zs[h] = yz[L:]
        yr_s[c * L:(c + 1) * L, :] = jnp.concatenate(ys, axis=-1)
    for h in range(nh):
        st_ref[h] = zs[h]

    P = P_RWKV
    y = yr_s[...]
    inv_n = 1.0 / hd
    mean = _dot(y, hsum, P) * inv_n
    yc = y - mean
    var = _dot(yc * yc, hsum, P) * inv_n
    yn = yc * lax.rsqrt(var + RWKV_GN_EPS) * lng_ref[...] + lnb_ref[...]
    bonus = _dot(r * k * rk_ref[...], hsum, P) * v
    y_ref[...] = (yn + bonus) * g


def _rwkv(pd, v_first, prm):
    t = pd.shape[0]
    tt = _tile(t, 512)
    nch = tt // CHUNK
    has_vres = v_first is not None
    gw = GROUP_W
    row_blk = lambda n: pl.BlockSpec((tt, n), lambda i: (i, 0))
    vec = _full((1, gw))
    in_specs = [row_blk(N_D),
                pl.BlockSpec((SUBLANES, N_D), lambda i: (jnp.maximum(i * (tt // SUBLANES) - 1, 0), 0))]
    args = [pd, pd]
    if has_vres:
        in_specs.append(row_blk(gw))
        args.append(v_first)
    tpos = jnp.arange(tt)
    tril = ((tpos[:, None] // CHUNK == tpos[None, :] // CHUNK) & (tpos[:, None] >= tpos[None, :])).astype(BF16)
    in_specs.append(_full((tt, tt)))
    args.append(tril)
    in_specs += [_full((1, N_D)), vec, _full((LORA_W, gw)), vec, _full((LORA_A, gw)), _full((LORA_G, gw)),
                 vec, vec, vec, vec, vec]
    args += [prm["mu"], prm["w0"], prm["w2"], prm["a0"], prm["a2"], prm["g2"], prm["k_k"], prm["k_a"],
             prm["r_k"], prm["ln_g"], prm["ln_b"]]
    if has_vres:
        in_specs += [vec, _full((gw, 32)), _full((32, gw))]
        args += [prm["v0"], prm["v1"], prm["v2"]]
        out_shape = jax.ShapeDtypeStruct((t, gw), F32)
        out_specs = row_blk(gw)
    else:
        out_shape = [jax.ShapeDtypeStruct((t, gw), F32)] * 2
        out_specs = [row_blk(gw)] * 2
    op = pltpu.VMEM((nch * RWKV_HEADS, 2 * CHUNK, RWKV_HEAD), F32)
    scratch = [pltpu.VMEM((RWKV_HEADS, RWKV_HEAD, RWKV_HEAD), F32), op, op, pltpu.VMEM((tt, gw), F32)]
    return pl.pallas_call(
        functools.partial(_rwkv_kernel, tt=tt, has_vres=has_vres),
        out_shape=out_shape,
        grid=(t // tt,),
        in_specs=in_specs,
        out_specs=out_specs,
        scratch_shapes=scratch,
        compiler_params=_cparams(("arbitrary",)),
        name="rwkv7",
    )(*args)


def _outproj_kernel(ya_ref, yb_ref, yc_ref, yd_ref, x_ref, wo_ref, gt_ref, g_ref, sc_ref, sh_ref, wr_ref, br_ref,
                    x1_ref, h2_ref, gate_ref, *, tm):
    ycat = jnp.concatenate([ya_ref[...], yb_ref[...], yc_ref[...], yd_ref[...]], axis=1)
    x1 = x_ref[...] + gt_ref[...] * _dot_w(ycat, wo_ref[...], P_OUT)
    x1_ref[...] = x1
    h2 = _modulated_norm(x1, g_ref[...], sc_ref[...], sh_ref[...])
    h2_ref[...] = h2.astype(h2_ref.dtype)
    lg = _dot_w(h2, wr_ref[...], P_ROUTER) + br_ref[...]
    lane = lax.broadcasted_iota(jnp.int32, (tm, LANES), 1)
    lanef = lane.astype(F32)
    gl = jnp.where(lane < N_GROUPS, lg[:, 0:LANES], -jnp.inf)
    gmax = jnp.max(gl, axis=-1, keepdims=True)
    gp = 1.0 / jnp.sum(jnp.exp(gl - gmax), axis=-1, keepdims=True)
    gi = jnp.min(jnp.where(gl == gmax, lanef, float(LANES)), axis=-1, keepdims=True)
    lo = gi * EXP_PER_GROUP
    in_grp = (lanef >= lo) & (lanef < lo + EXP_PER_GROUP)
    el = jnp.where(in_grp, lg[:, LANES:2 * LANES], -jnp.inf)
    m1 = jnp.max(el, axis=-1, keepdims=True)
    i1 = jnp.min(jnp.where(el == m1, lanef, float(LANES)), axis=-1, keepdims=True)
    el2 = jnp.where(lanef == i1, -jnp.inf, el)
    m2 = jnp.max(el2, axis=-1, keepdims=True)
    i2 = jnp.min(jnp.where(el2 == m2, lanef, float(LANES)), axis=-1, keepdims=True)
    e2 = jnp.exp(m2 - m1)
    w_top = gp / (1.0 + e2)
    gate_ref[...] = jnp.where(lanef == i1, w_top, jnp.where(lanef == i2, w_top * e2, 0.0))


def _out_proj(ya, yb, yc, yd, x, wo_parts, gt, g, sc, sh, w_router, b_router):
    t = x.shape[0]
    tm = _tile(t, 512)
    row_blk = lambda n: pl.BlockSpec((tm, n), lambda i: (i, 0))
    vec = _full((1, D_MODEL))
    return pl.pallas_call(
        functools.partial(_outproj_kernel, tm=tm),
        out_shape=[jax.ShapeDtypeStruct((t, D_MODEL), F32), jax.ShapeDtypeStruct((t, D_MODEL), BF16),
                   jax.ShapeDtypeStruct((t, LANES), F32)],
        grid=(t // tm,),
        in_specs=[row_blk(GROUP_W)] * 4 + [row_blk(D_MODEL), _full(wo_parts.shape), vec, vec, vec, vec,
                                           _full(w_router.shape), _full((1, 2 * LANES))],
        out_specs=[row_blk(D_MODEL), row_blk(D_MODEL), row_blk(LANES)],
        compiler_params=_cparams(("parallel",)),
        name="out_proj_router",
    )(ya, yb, yc, yd, x, wo_parts, gt, g, sc, sh, w_router, b_router)


def _moe_kernel(h_ref, gate_ref, ex_ref, wg_ref, wu_ref, wd_ref, x1_ref, gt_ref, fg_ref, o_ref, acc_ref, *, final):
    e = pl.program_id(1)

    @pl.when(e == 0)
    def _():
        acc_ref[...] = jnp.zeros_like(acc_ref)

    h = h_ref[...]
    hg = _mm(h, wg_ref[...], _NN)
    hu = _mm(h, wu_ref[...], _NN)
    gates = _mm(gate_ref[...].astype(BF16), ex_ref[...], _NN)
    act = hg * _sigmoid(hg) * hu * gates
    acc_ref[...] += _mm(act.astype(BF16), wd_ref[...], _NN)

    @pl.when(e == pl.num_programs(1) - 1)
    def _():
        out = x1_ref[...] + gt_ref[...] * acc_ref[...]
        if final:
            out = out * lax.rsqrt(jnp.mean(out * out, axis=-1, keepdims=True) + NORM_EPS) * fg_ref[...]
        o_ref[...] = out


def _moe(h2, gates, expand, wg, wu, wd, x1, gt, final_g, final):
    t = h2.shape[0]
    tm = _tile(t, 512)
    eb = 8
    nw = eb * D_EXPERT
    vec = _full((1, D_MODEL))
    return pl.pallas_call(
        functools.partial(_moe_kernel, final=final),
        out_shape=jax.ShapeDtypeStruct((t, D_MODEL), F32),
        grid=(t // tm, N_EXPERTS // eb),
        in_specs=[
            pl.BlockSpec((tm, D_MODEL), lambda i, e: (i, 0)),
            pl.BlockSpec((tm, LANES), lambda i, e: (i, 0)),
            pl.BlockSpec((LANES, nw), lambda i, e: (0, e)),
            pl.BlockSpec((D_MODEL, nw), lambda i, e: (0, e)),
            pl.BlockSpec((D_MODEL, nw), lambda i, e: (0, e)),
            pl.BlockSpec((nw, D_MODEL), lambda i, e: (e, 0)),
            pl.BlockSpec((tm, D_MODEL), lambda i, e: (i, 0)),
            vec, vec,
        ],
        out_specs=pl.BlockSpec((tm, D_MODEL), lambda i, e: (i, 0)),
        scratch_shapes=[pltpu.VMEM((tm, D_MODEL), F32)],
        compiler_params=_cparams(("parallel", "arbitrary")),
        name="moe_experts",
    )(h2, gates, expand, wg, wu, wd, x1, gt, final_g)


def kernel(x, c, w_ada, b_ada, norm1_g, norm2_g, w_in, w_out, pool_w, pool_scale, attn_sinks, conv_w, rwkv_mu,
           rwkv_w0, rwkv_w2, rwkv_a0, rwkv_a2, rwkv_g2, rwkv_k_k, rwkv_k_a, rwkv_r_k, rwkv_ln_g, rwkv_ln_b,
           rwkv_v0, rwkv_v1, rwkv_v2, moe_w_grp, moe_b_grp, moe_w_exp, moe_b_exp, moe_w_gate, moe_w_up,
           moe_w_down, final_g):
    bsz, t, d = x.shape
    assert bsz == 1 and d == D_MODEL
    depth = w_ada.shape[0]
    xs = x.reshape(t, d)
    mod = _ada_mod(c, w_ada, b_ada)
    expand = (jnp.arange(LANES)[:, None] == jnp.arange(N_EXPERTS * D_EXPERT)[None, :] // D_EXPERT).astype(BF16)
    v_first = None
    row = lambda a: a.reshape(1, -1)
    for l in range(depth):
        sh1, sc1, gt1, sh2, sc2, gt2 = [mod[l, :, j * d:(j + 1) * d] for j in range(6)]
        pa, pb, pc, pd = _in_proj(xs, row(norm1_g[l]), sc1, sh1, _split_w(w_in[l], P_IN))
        wp = jnp.zeros((GROUP_W, GROUP_W), F32)
        for gidx in range(len(POOL_WINDOWS)):
            sl = slice(gidx * POOL_CG, (gidx + 1) * POOL_CG)
            wp = wp.at[sl, sl].set(pool_w[l, gidx])
        ya, yc = _pool_conv(pa, pc, _split_w(wp, P_POOL), row(pool_scale[l]), conv_w[l])
        yb = _attention(pb, row(attn_sinks[l]))
        prm = dict(mu=row(rwkv_mu[l]), w0=row(rwkv_w0[l]), w2=rwkv_w2[l], a0=row(rwkv_a0[l]), a2=rwkv_a2[l],
                   g2=rwkv_g2[l], k_k=row(rwkv_k_k[l]), k_a=row(rwkv_k_a[l]), r_k=row(rwkv_r_k[l]),
                   ln_g=row(rwkv_ln_g[l]), ln_b=row(rwkv_ln_b[l]))
        if l == 0:
            yd, v_first = _rwkv(pd, None, prm)
        else:
            prm.update(v0=row(rwkv_v0[l - 1]), v1=rwkv_v1[l - 1], v2=rwkv_v2[l - 1])
            yd = _rwkv(pd, v_first, prm)
        w_router = jnp.zeros((d, 2 * LANES), F32)
        w_router = w_router.at[:, 0:N_GROUPS].set(moe_w_grp[l]).at[:, LANES:LANES + N_EXPERTS].set(moe_w_exp[l])
        b_router = jnp.zeros((1, 2 * LANES), F32)
        b_router = b_router.at[0, 0:N_GROUPS].set(moe_b_grp[l]).at[0, LANES:LANES + N_EXPERTS].set(moe_b_exp[l])
        x1, h2, gates = _out_proj(ya, yb, yc, yd, xs, _split_w(w_out[l], P_OUT), gt1, row(norm2_g[l]), sc2, sh2,
                                  _split_w(w_router, P_ROUTER), b_router)
        wg =jnp.transpose(moe_w_gate[l].astype(BF16), (1, 0, 2)).reshape(d, N_EXPERTS * D_EXPERT)
        wu = jnp.transpose(moe_w_up[l].astype(BF16), (1, 0, 2)).reshape(d, N_EXPERTS * D_EXPERT)
        wd = moe_w_down[l].astype(BF16).reshape(N_EXPERTS * D_EXPERT, d)
        xs = _moe(h2, gates, expand, wg, wu, wd, x1, gt2, row(final_g), final=(l == depth - 1))
    return xs.reshape(bsz, t, d)
```

```python
import functools

import jax
import jax.numpy as jnp
from jax import lax
from jax.experimental import pallas as pl
from jax.experimental.pallas import tpu as pltpu

F32 = jnp.float32
BF16 = jnp.bfloat16

D_MODEL = 1024
GROUP_W = 256
NORM_EPS = 1e-6
POOL_WINDOWS = (2, 4, 8, 16)
POOL_CG = 64
HEAD_DIM = 64
ATT_HEADS = 4
ATT_KV_HEADS = 2
ATT_BLOCK = 128
CONV_W = 3
RWKV_HEAD = 64
RWKV_HEADS = 4
LORA_W = 32
LORA_A = 32
LORA_G = 64
RWKV_GN_EPS = 64e-5
N_A = GROUP_W
N_B = (ATT_HEADS + 2 * ATT_KV_HEADS) * HEAD_DIM
N_C = 3 * GROUP_W
N_D = 3 * GROUP_W + LORA_W + LORA_A + LORA_G
N_IN = N_A + N_B + N_C + N_D
N_GROUPS = 4
EXP_PER_GROUP = 8
N_EXPERTS = 32
D_EXPERT = 128

LANES = 128
SUBLANES = 8
VMEM_LIMIT = 56 * 1024 * 1024

CHUNK = 64
POOL_HALO = 16
CONV_HALO = 8
ROUTE_GROUP_LANE = 8
MOE_TILE = 1024
MOE_CAP = 320

P_IN = 1
P_OUT = 1
P_POOL = 1
P_ATT = 1
P_MOE = 1
P_RWKV = 3
P_CHUNK = 1
P_STATE = 3
P_ROUTER = 3


def _split(a):
    hi = a.astype(BF16)
    lo = (a - hi.astype(F32)).astype(BF16)
    return hi, lo


def _mm(a, b, dims):
    return lax.dot_general(a, b, (dims, ((), ())), preferred_element_type=F32)


_NN = ((1,), (0,))
_NT = ((1,), (1,))


def _dot(a, b, passes=1, dims=_NN):
    if passes == 1:
        return _mm(a.astype(BF16), b.astype(BF16), dims)
    a0, a1 = _split(a)
    b0, b1 = _split(b)
    return _mm(a0, b0, dims) + (_mm(a0, b1, dims) + _mm(a1, b0, dims))


def _bmm(a, b, nt):
    dims = (((2,), (2 if nt else 1,)), ((0,), (0,)))
    return lax.dot_general(a, b, dims, preferred_element_type=F32)


def _bdot(a, b, passes=1, nt=False):
    if passes == 1:
        return _bmm(a.astype(BF16), b.astype(BF16), nt)
    a0, a1 = _split(a)
    b0, b1 = _split(b)
    return _bmm(a0, b0, nt) + (_bmm(a0, b1, nt) + _bmm(a1, b0, nt))


def _lhs_w(a, passes):
    if passes == 1:
        return a.astype(BF16)
    a0, a1 = _split(a)
    return jnp.concatenate([a0, a0, a1], axis=1)


def _dot_w(a, w_cat, passes):
    return _mm(_lhs_w(a, passes), w_cat, _NN)


def _split_w(w, passes):
    hi = w.astype(BF16)
    if passes == 1:
        return hi
    lo = (w - hi.astype(F32)).astype(BF16)
    return jnp.concatenate([hi, lo, hi], axis=0)


def _sigmoid(x):
    return 1.0 / (1.0 + jnp.exp(-x))


def _cparams(sem):
    return pltpu.CompilerParams(dimension_semantics=sem, vmem_limit_bytes=VMEM_LIMIT)


def _full(shape):
    return pl.BlockSpec(shape, lambda *_: (0,) * len(shape))


def _tile(n, pref):
    t = min(n, pref)
    assert n % t == 0, (n, t)
    return t


def _mod_kernel(c_ref, w_ref, b_ref, o_ref):
    c = c_ref[...]
    cond = c * _sigmoid(c)
    o_ref[...] = _dot(cond, w_ref[...], 3) + b_ref[...]


def _ada_mod(c, w_ada, b_ada):
    depth = w_ada.shape[0]
    c8 = jnp.broadcast_to(c, (SUBLANES, D_MODEL))
    out = pl.pallas_call(
        _mod_kernel,
        out_shape=jax.ShapeDtypeStruct((depth, SUBLANES, 6 * D_MODEL), F32),
        grid=(depth, 6),
        in_specs=[
            pl.BlockSpec((SUBLANES, D_MODEL), lambda l, j: (0, 0)),
            pl.BlockSpec((None, D_MODEL, D_MODEL), lambda l, j: (l, 0, j)),
            pl.BlockSpec((None, 1, D_MODEL), lambda l, j: (l, 0, j)),
        ],
        out_specs=pl.BlockSpec((None, SUBLANES, D_MODEL), lambda l, j: (l, 0, j)),
        compiler_params=_cparams(("arbitrary", "arbitrary")),
        name="ada_mod",
    )(c8, w_ada, b_ada.reshape(depth, 1, 6 * D_MODEL))
    return out[:, 0:1, :]


def _modulated_norm(x, g, sc, sh):
    y = x * lax.rsqrt(jnp.mean(x * x, axis=-1, keepdims=True) + NORM_EPS) * g
    return y * (1.0 + sc) + sh


def _inproj_kernel(x_ref, g_ref, sc_ref, sh_ref, w_ref, pa_ref, pb_ref, pc_ref, pd_ref, *, passes):
    h = _modulated_norm(x_ref[...], g_ref[...], sc_ref[...], sh_ref[...])
    lhs = _lhs_w(h, passes)
    col = 0
    for o_ref, width in ((pa_ref, N_A), (pb_ref, N_B), (pc_ref, N_C), (pd_ref, N_D)):
        o_ref[...] = _mm(lhs, w_ref[:, col:col + width], _NN)
        col += width


def _in_proj(x, g, sc, sh, w_parts):
    t = x.shape[0]
    tm = _tile(t, 512)
    vec = _full((1, D_MODEL))
    return pl.pallas_call(
        functools.partial(_inproj_kernel, passes=P_IN),
        out_shape=[jax.ShapeDtypeStruct((t, n), F32) for n in (N_A, N_B, N_C, N_D)],
        grid=(t // tm,),
        in_specs=[pl.BlockSpec((tm, D_MODEL), lambda i: (i, 0)), vec, vec, vec, _full(w_parts.shape)],
        out_specs=[pl.BlockSpec((tm, n), lambda i: (i, 0)) for n in (N_A, N_B, N_C, N_D)],
        compiler_params=_cparams(("parallel",)),
        name="in_proj",
    )(x, g, sc, sh, w_parts)


def _shift_rows(ext, n):
    return pltpu.roll(ext, n, axis=0)


def _pool_conv_kernel(pa_ref, pah_ref, pc_ref, pch_ref, wp_ref, scale_ref, cw_ref, ya_ref, yc_ref, *, tt):
    i = pl.program_id(0)
    not_first = (i > 0).astype(F32)
    u = pa_ref[...]
    ext = jnp.concatenate([pah_ref[...] * not_first, u], axis=0)
    s2 = ext + _shift_rows(ext, 1)
    s4 = s2 + _shift_rows(s2, 2)
    s8 = s4 + _shift_rows(s4, 4)
    s16 = s8 + _shift_rows(s8, 8)
    grp = lax.broadcasted_iota(jnp.int32, (tt, GROUP_W), 1) // POOL_CG
    tpos = lax.broadcasted_iota(jnp.int32, (tt, GROUP_W), 0) + (i * tt + 1)
    wsum = jnp.where(grp == 0, s2[POOL_HALO:], jnp.where(grp == 1, s4[POOL_HALO:],
                     jnp.where(grp == 2, s8[POOL_HALO:], s16[POOL_HALO:])))
    win = jnp.where(grp == 0, 2, jnp.where(grp == 1, 4, jnp.where(grp == 2, 8, 16)))
    cnt = jnp.minimum(tpos, win).astype(F32)
    d = wsum / cnt - u
    ya_ref[...] = _dot_w(d, wp_ref[...], P_POOL) * scale_ref[...]
    z = pc_ref[:, GROUP_W:2 * GROUP_W] * pc_ref[:, 2 * GROUP_W:3 * GROUP_W]
    zh = pch_ref[:, GROUP_W:2 * GROUP_W] * pch_ref[:, 2 * GROUP_W:3 * GROUP_W] * not_first
    zext = jnp.concatenate([zh, z], axis=0)
    y = (cw_ref[2:3, :] * z + cw_ref[1:2, :] * _shift_rows(zext, 1)[CONV_HALO:]
         + cw_ref[0:1, :] * _shift_rows(zext, 2)[CONV_HALO:])
    yc_ref[...] = pc_ref[:, 0:GROUP_W] * y


def _pool_conv(pa, pc, wp_parts, pool_scale, conv_w):
    t = pa.shape[0]
    tt = _tile(t, 512)
    rp, rc = tt // POOL_HALO, tt // CONV_HALO
    return pl.pallas_call(
        functools.partial(_pool_conv_kernel, tt=tt),
        out_shape=[jax.ShapeDtypeStruct((t, GROUP_W), F32)] * 2,
        grid=(t // tt,),
        in_specs=[
            pl.BlockSpec((tt, N_A), lambda i: (i, 0)),
            pl.BlockSpec((POOL_HALO, N_A), lambda i: (jnp.maximum(i * rp - 1, 0), 0)),
            pl.BlockSpec((tt, N_C), lambda i: (i, 0)),
            pl.BlockSpec((CONV_HALO, N_C), lambda i: (jnp.maximum(i * rc - 1, 0), 0)),
            _full(wp_parts.shape), _full((1, GROUP_W)), _full((CONV_W, GROUP_W)),
        ],
        out_specs=[pl.BlockSpec((tt, GROUP_W), lambda i: (i, 0))] * 2,
        compiler_params=_cparams(("parallel",)),
        name="pool_conv",
    )(pa, pa, pc, pc, wp_parts, pool_scale, conv_w)


def _attn_kernel(cur_ref, prev_ref, sink_ref, o_ref, *, tq):
    i = pl.program_id(0)
    nq = ATT_HEADS * HEAD_DIM
    nkv = ATT_KV_HEADS * HEAD_DIM
    blk = ATT_BLOCK
    hd = HEAD_DIM
    rep = ATT_HEADS // ATT_KV_HEADS
    nblk = tq // blk
    ri = lax.broadcasted_iota(jnp.int32, (rep * blk, 2 * blk), 0)
    ki = lax.broadcasted_iota(jnp.int32, (rep * blk, 2 * blk), 1)
    dist = (ri % blk) + blk - ki
    in_win = (dist >= 0) & (dist < blk)
    distf = dist.astype(F32)
    head_col = lax.broadcasted_iota(jnp.int32, (rep * blk, 1), 0) // blk
    bias, sink = [], []
    for g in range(ATT_KV_HEADS):
        slope_g = jnp.zeros((rep * blk, 1), F32)
        sink_g = jnp.zeros((rep * blk, 1), F32)
        for j in range(rep):
            h = g * rep + j
            slope_g = jnp.where(head_col == j, 2.0 ** (-8.0 * (h + 1) / ATT_HEADS), slope_g)
            sink_g = jnp.where(head_col == j, sink_ref[:, h:h + 1], sink_g)
        bias.append(slope_g * distf)
        sink.append(sink_g)
    bias = jnp.stack(bias)[None]
    sink = jnp.stack(sink)[None]

    q = cur_ref[:, 0:nq] * (hd ** -0.5)
    kv = jnp.concatenate([prev_ref[...], cur_ref[:, nq:nq + 2 * nkv]], axis=0)
    qs, ks, vs = [], [], []
    for b in range(nblk):
        for g in range(ATT_KV_HEADS):
            qs.append(jnp.concatenate(
                [q[b * blk:(b + 1) * blk, (g * rep + j) * hd:(g * rep + j + 1) * hd] for j in range(rep)], axis=0))
            ks.append(kv[b * blk:(b + 2) * blk, g * hd:(g + 1) * hd])
            vs.append(kv[b * blk:(b + 2) * blk, nkv + g * hd:nkv + (g + 1) * hd])
    s = _bdot(jnp.stack(qs), jnp.stack(ks), P_ATT, nt=True)
    s = s.reshape(nblk, ATT_KV_HEADS, rep * blk, 2 * blk)
    s = jnp.where(in_win, s - bias, -jnp.inf)
    first_ok = ki >= blk * (1 - (i > 0).astype(jnp.int32))
    s = jnp.concatenate([jnp.where(first_ok, s[0:1], -jnp.inf), s[1:]], axis=0)
    m = jnp.maximum(jnp.max(s, axis=-1, keepdims=True), sink)
    p = jnp.exp(s - m)
    den = jnp.sum(p, axis=-1, keepdims=True) + jnp.exp(sink - m)
    o = _bdot(p.reshape(nblk * ATT_KV_HEADS, rep * blk, 2 * blk), jnp.stack(vs), P_ATT)
    o = o.reshape(nblk, ATT_KV_HEADS, rep * blk, hd) / den
    for b in range(nblk):
        o_ref[b * blk:(b + 1) * blk, :] = jnp.concatenate(
            [o[b, g, j * blk:(j + 1) * blk, :] for g in range(ATT_KV_HEADS) for j in range(rep)], axis=-1)


def _attention(pb, sinks):
    t = pb.shape[0]
    tq = _tile(t, 512)
    nb = tq // ATT_BLOCK
    return pl.pallas_call(
        functools.partial(_attn_kernel, tq=tq),
        out_shape=jax.ShapeDtypeStruct((t, GROUP_W), F32),
        grid=(t // tq,),
        in_specs=[
            pl.BlockSpec((tq, N_B), lambda i: (i, 0)),
            pl.BlockSpec((ATT_BLOCK, 2 * ATT_KV_HEADS * HEAD_DIM), lambda i: (jnp.maximum(i * nb - 1, 0), 1)),
            _full((1, ATT_HEADS)),
        ],
        out_specs=pl.BlockSpec((tq, GROUP_W), lambda i: (i, 0)),
        compiler_params=_cparams(("parallel",)),
        name="swa_attention",
    )(pb, pb, sinks)


def _rwkv_kernel(*refs, tt, has_vres):
    if has_vres:
        (pd_ref, prev_ref, vf_ref, tril_ref, mu_ref, w0_ref, w2_ref, a0_ref, a2_ref, g2_ref, kk_ref, ka_ref,
         rk_ref, lng_ref, lnb_ref, v0_ref, v1_ref, v2_ref, y_ref, st_ref, rqm_s, y0c_s, yr_s) = refs
    else:
        (pd_ref, prev_ref, tril_ref, mu_ref, w0_ref, w2_ref, a0_ref, a2_ref, g2_ref, kk_ref, ka_ref,
         rk_ref, lng_ref, lnb_ref, y_ref, vout_ref, st_ref, rqm_s, y0c_s, yr_s) = refs
    i = pl.program_id(0)
    nh, hd, gw = RWKV_HEADS, RWKV_HEAD, GROUP_W
    P = P_RWKV

    @pl.when(i == 0)
    def _():
        st_ref[...] = jnp.zeros_like(st_ref)

    hr = lax.broadcasted_iota(jnp.int32, (gw, gw), 0) // hd
    hc = lax.broadcasted_iota(jnp.int32, (gw, gw), 1) // hd
    hsum = (hr == hc).astype(F32)

    p = pd_ref[...]
    row = lax.broadcasted_iota(jnp.int32, (tt, 1), 0)
    prev_row = prev_ref[SUBLANES - 1:SUBLANES, :] * (i > 0).astype(F32)
    shifted = jnp.where(row == 0, prev_row, pltpu.roll(p, 1, axis=0))
    z = p + (shifted - p) * mu_ref[...]
    r = z[:, 0:gw]
    k = z[:, gw:2 * gw]
    v = z[:, 2 * gw:3 * gw]
    o = 3 * gw
    wd = z[:, o:o + LORA_W]
    ad = z[:, o + LORA_W:o + LORA_W + LORA_A]
    gd = z[:, o + LORA_W + LORA_A:o + LORA_W + LORA_A + LORA_G]
    wpre = -(w0_ref[...] + _dot(jnp.tanh(wd), w2_ref[...], P))
    softplus = jnp.maximum(wpre, 0.0) + jnp.log(1.0 + jnp.exp(-jnp.abs(wpre)))
    lw = -jnp.exp(-softplus - 0.5)
    a = _sigmoid(a0_ref[...] + _dot(ad, a2_ref[...], P))
    g = _dot(_sigmoid(gd), g2_ref[...], P)
    if has_vres:
        mix = _sigmoid(v0_ref[...] + _dot(_dot(v, v1_ref[...], P), v2_ref[...], P))
        v = v + (vf_ref[...] - v) * mix
    else:
        vout_ref[...] = v
    kk = k * kk_ref[...]
    kk = kk / jnp.maximum(jnp.sqrt(_dot(kk * kk, hsum, P)), 1e-12)
    k = k * (1.0 + (a - 1.0) * ka_ref[...])
    nch = tt // CHUNK
    L = CHUNK
    ti = lax.broadcasted_iota(jnp.int32, (L, L), 0)
    tj = lax.broadcasted_iota(jnp.int32, (L, L), 1)
    low_strict = ti > tj
    low_incl = ti >= tj
    eye = (ti == tj).astype(F32)

    def to_batch(t2):
        return jnp.concatenate([t2[:, h * hd:(h + 1) * hd].reshape(nch, L, hd) for h in range(nh)], axis=0)

    lw_hi, lw_lo = _split(lw)
    cl = _mm(tril_ref[...], lw_hi, _NN) + _mm(tril_ref[...], lw_lo, _NN)
    cl3 = cl.reshape(nch, L, gw)
    cl_end = cl3[:, L - 1:L, :]
    e_end = jnp.exp(cl_end - cl3).reshape(tt, gw)
    e_neg = jnp.exp(-cl)
    bb = kk * a
    at = to_batch(-kk * jnp.exp(cl - lw))
    bh = to_batch(bb * e_neg)
    kh = to_batch(k * e_neg)
    rt = to_batch(r * jnp.exp(cl))
    be = to_batch(bb * e_end)
    ke = to_batch(k * e_end)
    vb = to_batch(v)
    p_end = jnp.exp(cl_end)
    pe = jnp.concatenate([p_end[:, :, h * hd:(h + 1) * hd] for h in range(nh)], axis=0)

    P = P_CHUNK
    lhs = jnp.concatenate([at, rt], axis=1)
    g_b = _bdot(lhs, bh, P, nt=True)
    g_k = _bdot(lhs, kh, P, nt=True)
    a_ab = jnp.where(low_strict, g_b[:, :L], 0.0)
    b_rb = jnp.where(low_incl, g_b[:, L:], 0.0)
    a_ak = jnp.where(low_strict, g_k[:, :L], 0.0)
    b_rk = jnp.where(low_incl, g_k[:, L:], 0.0)
    apow = _bdot(a_ab, a_ab, P)
    tinv = eye + a_ab
    for _ in range(4):
        sq = _bdot(jnp.concatenate([apow, tinv], axis=1), apow, P)
        tinv = tinv + sq[:, L:]
        apow = sq[:, :L]
    tinv = tinv + _bdot(tinv, apow, P)
    w1 = _bdot(tinv, at, P)
    u0 = _bdot(tinv, _bdot(a_ak, vb, P), P)
    lhs_b = jnp.concatenate([b_rb, jnp.swapaxes(be, 1, 2)], axis=1)
    lhs_k = jnp.concatenate([b_rk, jnp.swapaxes(ke, 1, 2)], axis=1)
    o_w = _bdot(lhs_b, w1, P)
    o_u = _bdot(lhs_b, u0, P) + _bdot(lhs_k, vb, P)
    rqm_s[...] = jnp.concatenate([rt + o_w[:, :L], eye * pe + o_w[:, L:]], axis=1)
    y0c_s[...] = o_u

    zs = [st_ref[h] for h in range(nh)]
    for c in range(nch):
        ys = []
        for h in range(nh):
            b = h * nch + c
            yz = _dot(rqm_s[b], zs[h], P_STATE) + y0c_s[b]
            ys.append(yz[:L])
            zs[h] = yz[L:]
        yr_s[c * L:(c + 1) * L, :] = jnp.concatenate(ys, axis=-1)
    for h in range(nh):
        st_ref[h] = zs[h]

    P = P_RWKV
    y = yr_s[...]
    inv_n = 1.0 / hd
    mean = _dot(y, hsum, P) * inv_n
    yc = y - mean
    var = _dot(yc * yc, hsum, P) * inv_n
    yn = yc * lax.rsqrt(var + RWKV_GN_EPS) * lng_ref[...] + lnb_ref[...]
    bonus = _dot(r * k * rk_ref[...], hsum, P) * v
    y_ref[...] = (yn + bonus) * g


def _rwkv(pd, v_first, prm):
    t = pd.shape[0]
    tt = _tile(t, 512)
    nch = tt // CHUNK
    has_vres = v_first is not None
    gw = GROUP_W
    row_blk = lambda n: pl.BlockSpec((tt, n), lambda i: (i, 0))
    vec = _full((1, gw))
    in_specs = [row_blk(N_D),
                pl.BlockSpec((SUBLANES, N_D), lambda i: (jnp.maximum(i * (tt // SUBLANES) - 1, 0), 0))]
    args = [pd, pd]
    if has_vres:
        in_specs.append(row_blk(gw))
        args.append(v_first)
    tpos = jnp.arange(tt)
    tril = ((tpos[:, None] // CHUNK == tpos[None, :] // CHUNK) & (tpos[:, None] >= tpos[None, :])).astype(BF16)
    in_specs.append(_full((tt, tt)))
    args.append(tril)
    in_specs += [_full((1, N_D)), vec, _full((LORA_W, gw)), vec, _full((LORA_A, gw)), _full((LORA_G, gw)),
                 vec, vec, vec, vec, vec]
    args += [prm["mu"], prm["w0"], prm["w2"], prm["a0"], prm["a2"], prm["g2"], prm["k_k"], prm["k_a"],
             prm["r_k"], prm["ln_g"], prm["ln_b"]]
    if has_vres:
        in_specs += [vec, _full((gw, 32)), _full((32, gw))]
        args += [prm["v0"], prm["v1"], prm["v2"]]
        out_shape = jax.ShapeDtypeStruct((t, gw), F32)
        out_specs = row_blk(gw)
    else:
        out_shape = [jax.ShapeDtypeStruct((t, gw), F32)] * 2
        out_specs = [row_blk(gw)] * 2
    op = pltpu.VMEM((nch * RWKV_HEADS, 2 * CHUNK, RWKV_HEAD), F32)
    scratch = [pltpu.VMEM((RWKV_HEADS, RWKV_HEAD, RWKV_HEAD), F32), op, op, pltpu.VMEM((tt, gw), F32)]
    return pl.pallas_call(
        functools.partial(_rwkv_kernel, tt=tt, has_vres=has_vres),
        out_shape=out_shape,
        grid=(t // tt,),
        in_specs=in_specs,
        out_specs=out_specs,
        scratch_shapes=scratch,
        compiler_params=_cparams(("arbitrary",)),
        name="rwkv7",
    )(*args)


def _outproj_kernel(ya_ref, yb_ref, yc_ref, yd_ref, x_ref, wo_ref, gt_ref, g_ref, sc_ref, sh_ref, wr_ref, br_ref,
                    x1_ref, h2_ref, gate_ref, *, tm):
    ycat = jnp.concatenate([ya_ref[...], yb_ref[...], yc_ref[...], yd_ref[...]], axis=1)
    x1 = x_ref[...] + gt_ref[...] * _dot_w(ycat, wo_ref[...], P_OUT)
    x1_ref[...] = x1
    h2 = _modulated_norm(x1, g_ref[...], sc_ref[...], sh_ref[...])
    h2_ref[...] = h2.astype(h2_ref.dtype)
    lg = _dot_w(h2, wr_ref[...], P_ROUTER) + br_ref[...]
    lane = lax.broadcasted_iota(jnp.int32, (tm, LANES), 1)
    lanef = lane.astype(F32)
    gl = jnp.where(lane < N_GROUPS, lg[:, 0:LANES], -jnp.inf)
    gmax = jnp.max(gl, axis=-1, keepdims=True)
    gp = 1.0 / jnp.sum(jnp.exp(gl - gmax), axis=-1, keepdims=True)
    gi = jnp.min(jnp.where(gl == gmax, lanef, float(LANES)), axis=-1, keepdims=True)
    lo = gi * EXP_PER_GROUP
    in_grp = (lanef >= lo) & (lanef < lo + EXP_PER_GROUP)
    el = jnp.where(in_grp, lg[:, LANES:2 * LANES], -jnp.inf)
    m1 = jnp.max(el, axis=-1, keepdims=True)
    i1 = jnp.min(jnp.where(el == m1, lanef, float(LANES)), axis=-1, keepdims=True)
    el2 = jnp.where(lanef == i1, -jnp.inf, el)
    m2 = jnp.max(el2, axis=-1, keepdims=True)
    i2 = jnp.min(jnp.where(el2 == m2, lanef, float(LANES)), axis=-1, keepdims=True)
    e2 = jnp.exp(m2 - m1)
    w_top = gp / (1.0 + e2)
    gate_ref[...] = jnp.where(lanef == i1 - lo, w_top, jnp.where(lanef == i2 - lo, w_top * e2,
                              jnp.where(lane == ROUTE_GROUP_LANE, gi, 0.0)))


def _out_proj(ya, yb, yc, yd, x, wo_parts, gt, g, sc, sh, w_router, b_router):
    t = x.shape[0]
    tm = _tile(t, 512)
    row_blk = lambda n: pl.BlockSpec((tm, n), lambda i: (i, 0))
    vec = _full((1, D_MODEL))
    return pl.pallas_call(
        functools.partial(_outproj_kernel, tm=tm),
        out_shape=[jax.ShapeDtypeStruct((t, D_MODEL), F32), jax.ShapeDtypeStruct((t, D_MODEL), BF16),
                   jax.ShapeDtypeStruct((t, LANES), F32)],
        grid=(t // tm,),
        in_specs=[row_blk(GROUP_W)] * 4 + [row_blk(D_MODEL), _full(wo_parts.shape), vec, vec, vec, vec,
                                           _full(w_router.shape), _full((1, 2 * LANES))],
        out_specs=[row_blk(D_MODEL), row_blk(D_MODEL), row_blk(LANES)],
        compiler_params=_cparams(("parallel",)),
        name="out_proj_router",
    )(ya, yb, yc, yd, x, wo_parts, gt, g, sc, sh, w_router, b_router)


def _moe_kernel(h_ref, route_ref, tril_ref, ex_ref, wg_ref, wu_ref, wd_ref, x1_ref, gt_ref, fg_ref, o_ref,
                acc_ref, col_ref, row_ref, cnt_ref, *, final, tm, cap):
    g = pl.program_id(1)
    lane = lax.broadcasted_iota(jnp.int32, (tm, LANES), 1)

    @pl.when(g == 0)
    def _():
        acc_ref[...] = jnp.zeros_like(acc_ref)
        gi = route_ref[:, ROUTE_GROUP_LANE:ROUTE_GROUP_LANE + 1]
        onehot = lane.astype(F32) == gi
        cum = _mm(tril_ref[...], jnp.where(onehot, 1.0, 0.0).astype(BF16), _NN)
        rank = jnp.sum(jnp.where(onehot, cum, 0.0), axis=-1, keepdims=True) - 1.0
        cnt_ref[...] = cum[tm - 1:tm, :]
        rec = jnp.where(lane == 0, rank, jnp.where(lane == 1, gi, 0.0))
        col_ref[...] = rec
        row_ref[...] = rec.T[0:SUBLANES, :]

    gf = g.astype(F32)
    count = jnp.sum(jnp.where(lane[0:1, :] == g, cnt_ref[...], 0.0))
    n_rounds = (count.astype(jnp.int32) + (cap - 1)) // cap
    sel_col = jnp.where(col_ref[:, 1:2] == gf, col_ref[:, 0:1], -1.0)
    sel_row = jnp.where(row_ref[1:2, :] == gf, row_ref[0:1, :], -1.0)
    hx = jnp.concatenate([h_ref[...], route_ref[...].astype(BF16)], axis=1)

    def one_round(r, carry):
        base = (r * cap).astype(F32)
        slot_r = lax.broadcasted_iota(jnp.int32, (cap, tm), 0).astype(F32) + base
        gather = jnp.where(slot_r == sel_row, 1.0, 0.0).astype(BF16)
        hs = _mm(gather, hx, _NN)
        h = hs[:, 0:D_MODEL].astype(BF16)
        gates = _mm(hs[:, D_MODEL:D_MODEL + LANES].astype(BF16), ex_ref[...], _NN)
        hg = _mm(h, wg_ref[...], _NN)
        hu = _mm(h, wu_ref[...], _NN)
        act = hg * _sigmoid(hg) * hu * gates
        ys = _mm(act.astype(BF16), wd_ref[...], _NN)
        slot_c = lax.broadcasted_iota(jnp.int32, (tm, cap), 1).astype(F32) + base
        scatter = jnp.where(slot_c == sel_col, 1.0, 0.0).astype(BF16)
        acc_ref[...] += _mm(scatter, ys.astype(BF16), _NN)
        return carry

    lax.fori_loop(0, n_rounds, one_round, 0)

    @pl.when(g == pl.num_programs(1) - 1)
    def _():
        out = x1_ref[...] + gt_ref[...] * acc_ref[...]
        if final:
            out = out * lax.rsqrt(jnp.mean(out * out, axis=-1, keepdims=True) + NORM_EPS) * fg_ref[...]
        o_ref[...] = out


def _moe(h2, route, wg, wu, wd, x1, gt, final_g, final):
    t = h2.shape[0]
    tm = _tile(t, MOE_TILE)
    cap = min(MOE_CAP, tm)
    nw = EXP_PER_GROUP * D_EXPERT
    vec = _full((1, D_MODEL))
    tpos = jnp.arange(tm)
    tril = (tpos[:, None] >= tpos[None, :]).astype(BF16)
    expand = (jnp.arange(LANES)[:, None] == jnp.arange(nw)[None, :] // D_EXPERT).astype(BF16)
    return pl.pallas_call(
        functools.partial(_moe_kernel, final=final, tm=tm, cap=cap),
        out_shape=jax.ShapeDtypeStruct((t, D_MODEL), F32),
        grid=(t // tm, N_GROUPS),
        in_specs=[
            pl.BlockSpec((tm, D_MODEL), lambda i, g: (i, 0)),
            pl.BlockSpec((tm, LANES), lambda i, g: (i, 0)),
            _full((tm, tm)), _full((LANES, nw)),
            pl.BlockSpec((D_MODEL, nw), lambda i, g: (0, g)),
            pl.BlockSpec((D_MODEL, nw), lambda i, g: (0, g)),
            pl.BlockSpec((nw, D_MODEL), lambda i, g: (g, 0)),
            pl.BlockSpec((tm, D_MODEL), lambda i, g: (i, 0)),
            vec, vec,
        ],
        out_specs=pl.BlockSpec((tm, D_MODEL), lambda i, g: (i, 0)),
        scratch_shapes=[pltpu.VMEM((tm, D_MODEL), F32), pltpu.VMEM((tm, LANES), F32),
                        pltpu.VMEM((SUBLANES, tm), F32), pltpu.VMEM((1, LANES), F32)],
        compiler_params=_cparams(("parallel", "arbitrary")),
        name="moe_experts",
    )(h2, route, tril, expand, wg, wu, wd, x1, gt, final_g)


def kernel(x, c, w_ada, b_ada, norm1_g, norm2_g, w_in, w_out, pool_w, pool_scale, attn_sinks, conv_w, rwkv_mu,
           rwkv_w0, rwkv_w2, rwkv_a0, rwkv_a2, rwkv_g2, rwkv_k_k, rwkv_k_a, rwkv_r_k, rwkv_ln_g, rwkv_ln_b,
           rwkv_v0, rwkv_v1, rwkv_v2, moe_w_grp, moe_b_grp, moe_w_exp, moe_b_exp, moe_w_gate, moe_w_up,
           moe_w_down, final_g):
    bsz, t, d = x.shape
    assert bsz == 1 and d == D_MODEL
    depth = w_ada.shape[0]
    xs = x.reshape(t, d)
    mod = _ada_mod(c, w_ada, b_ada)
    v_first = None
    row = lambda a: a.reshape(1, -1)
    for l in range(depth):
        sh1, sc1, gt1, sh2, sc2, gt2 = [mod[l, :, j * d:(j + 1) * d] for j in range(6)]
        pa, pb, pc, pd = _in_proj(xs, row(norm1_g[l]), sc1, sh1, _split_w(w_in[l], P_IN))
        wp = jnp.zeros((GROUP_W, GROUP_W), F32)
        for gidx in range(len(POOL_WINDOWS)):
            sl = slice(gidx * POOL_CG, (gidx + 1) * POOL_CG)
            wp = wp.at[sl, sl].set(pool_w[l, gidx])
        ya, yc = _pool_conv(pa, pc, _split_w(wp, P_POOL), row(pool_scale[l]), conv_w[l])
        yb = _attention(pb, row(attn_sinks[l]))
        prm = dict(mu=row(rwkv_mu[l]), w0=row(rwkv_w0[l]), w2=rwkv_w2[l], a0=row(rwkv_a0[l]), a2=rwkv_a2[l],
                   g2=rwkv_g2[l], k_k=row(rwkv_k_k[l]), k_a=row(rwkv_k_a[l]), r_k=row(rwkv_r_k[l]),
                   ln_g=row(rwkv_ln_g[l]), ln_b=row(rwkv_ln_b[l]))
        if l == 0:
            yd, v_first = _rwkv(pd, None, prm)
        else:
            prm.update(v0=row(rwkv_v0[l - 1]), v1=rwkv_v1[l - 1], v2=rwkv_v2[l - 1])
            yd = _rwkv(pd, v_first, prm)
        w_router = jnp.zeros((d, 2 * LANES), F32)
        w_router = w_router.at[:, 0:N_GROUPS].set(moe_w_grp[l]).at[:, LANES:LANES + N_EXPERTS].set(moe_w_exp[l])
        b_router = jnp.zeros((1, 2 * LANES), F32)
        b_router = b_router.at[0, 0:N_GROUPS].set(moe_b_grp[l]).at[0, LANES:LANES + N_EXPERTS].set(moe_b_exp[l])
        x1, h2, gates = _out_proj(ya, yb, yc, yd, xs, _split_w(w_out[l], P_OUT), gt1, row(norm2_g[l]), sc2, sh2,
                                  _split_w(w_router, P_ROUTER), b_router)
        wg =jnp.transpose(moe_w_gate[l].astype(BF16), (1, 0, 2)).reshape(d, N_EXPERTS * D_EXPERT)
        wu = jnp.transpose(moe_w_up[l].astype(BF16), (1, 0, 2)).reshape(d, N_EXPERTS * D_EXPERT)
        wd = moe_w_down[l].astype(BF16).reshape(N_EXPERTS * D_EXPERT, d)
        xs = _moe(h2, gates, wg, wu, wd, x1, gt2, row(final_g), final=(l == depth - 1))
    return xs.reshape(bsz, t, d)
```

```python
import functools

import jax
import jax.numpy as jnp
from jax import lax
from jax.experimental import pallas as pl
from jax.experimental.pallas import tpu as pltpu

F32 = jnp.float32
BF16 = jnp.bfloat16

D_MODEL = 1024
GROUP_W = 256
NORM_EPS = 1e-6
POOL_WINDOWS = (2, 4, 8, 16)
POOL_CG = 64
HEAD_DIM = 64
ATT_HEADS = 4
ATT_KV_HEADS = 2
ATT_BLOCK = 128
CONV_W = 3
RWKV_HEAD = 64
RWKV_HEADS = 4
LORA_W = 32
LORA_A = 32
LORA_G = 64
RWKV_GN_EPS = 64e-5
N_A = GROUP_W
N_B = (ATT_HEADS + 2 * ATT_KV_HEADS) * HEAD_DIM
N_C = 3 * GROUP_W
N_D = 3 * GROUP_W + LORA_W + LORA_A + LORA_G
N_IN = N_A + N_B + N_C + N_D
N_GROUPS = 4
EXP_PER_GROUP = 8
N_EXPERTS = 32
D_EXPERT = 128

LANES = 128
SUBLANES = 8
VMEM_LIMIT = 56 * 1024 * 1024

CHUNK = 64
POOL_HALO = 16
CONV_HALO = 8
ROUTE_GROUP_LANE = 8
MOE_TILE = 1024

P_IN = 1
P_OUT = 1
P_POOL = 1
P_ATT = 1
P_MOE = 1
P_DECAY = 3
P_LORA = 1
P_STATE = 1
P_ROUTER = 3


def _split(a):
    hi = a.astype(BF16)
    lo = (a - hi.astype(F32)).astype(BF16)
    return hi, lo


def _mm(a, b, dims):
    return lax.dot_general(a, b, (dims, ((), ())), preferred_element_type=F32)


_NN = ((1,), (0,))
_NT = ((1,), (1,))


def _dot(a, b, passes=1, dims=_NN):
    if passes == 1:
        return _mm(a.astype(BF16), b.astype(BF16), dims)
    a0, a1 = _split(a)
    b0, b1 = _split(b)
    return _mm(a0, b0, dims) + (_mm(a0, b1, dims) + _mm(a1, b0, dims))


def _bmm(a, b, nt):
    dims = (((2,), (2 if nt else 1,)), ((0,), (0,)))
    return lax.dot_general(a, b, dims, preferred_element_type=F32)


def _bdot(a, b, passes=1, nt=False):
    if passes == 1:
        return _bmm(a.astype(BF16), b.astype(BF16), nt)
    a0, a1 = _split(a)
    b0, b1 = _split(b)
    return _bmm(a0, b0, nt) + (_bmm(a0, b1, nt) + _bmm(a1, b0, nt))


def _lhs_w(a, passes):
    if passes == 1:
        return a.astype(BF16)
    a0, a1 = _split(a)
    return jnp.concatenate([a0, a0, a1], axis=1)


def _dot_w(a, w_cat, passes):
    return _mm(_lhs_w(a, passes), w_cat, _NN)


def _split_w(w, passes):
    hi = w.astype(BF16)
    if passes == 1:
        return hi
    lo = (w - hi.astype(F32)).astype(BF16)
    return jnp.concatenate([hi, lo, hi], axis=0)


def _sigmoid(x):
    return 1.0 / (1.0 + jnp.exp(-x))


def _cparams(sem):
    return pltpu.CompilerParams(dimension_semantics=sem, vmem_limit_bytes=VMEM_LIMIT)


def _full(shape):
    return pl.BlockSpec(shape, lambda *_: (0,) * len(shape))


def _tile(n, pref):
    t = min(n, pref)
    assert n % t == 0, (n, t)
    return t


def _mod_kernel(c_ref, w_ref, b_ref, o_ref):
    c = c_ref[...]
    cond = c * _sigmoid(c)
    o_ref[...] = _dot(cond, w_ref[...], 3) + b_ref[...]


def _ada_mod(c, w_ada, b_ada):
    depth = w_ada.shape[0]
    c8 = jnp.broadcast_to(c, (SUBLANES, D_MODEL))
    out = pl.pallas_call(
        _mod_kernel,
        out_shape=jax.ShapeDtypeStruct((depth, SUBLANES, 6 * D_MODEL), F32),
        grid=(depth, 6),
        in_specs=[
            pl.BlockSpec((SUBLANES, D_MODEL), lambda l, j: (0, 0)),
            pl.BlockSpec((None, D_MODEL, D_MODEL), lambda l, j: (l, 0, j)),
            pl.BlockSpec((None, 1, D_MODEL), lambda l, j: (l, 0, j)),
        ],
        out_specs=pl.BlockSpec((None, SUBLANES, D_MODEL), lambda l, j: (l, 0, j)),
        compiler_params=_cparams(("arbitrary", "arbitrary")),
        name="ada_mod",
    )(c8, w_ada, b_ada.reshape(depth, 1, 6 * D_MODEL))
    return out[:, 0:1, :]


def _modulated_norm(x, g, sc, sh):
    y = x * lax.rsqrt(jnp.mean(x * x, axis=-1, keepdims=True) + NORM_EPS) * g
    return y * (1.0 + sc) + sh


def _inproj_kernel(x_ref, g_ref, sc_ref, sh_ref, w_ref, pa_ref, pb_ref, pc_ref, pd_ref, *, passes):
    h = _modulated_norm(x_ref[...], g_ref[...], sc_ref[...], sh_ref[...])
    lhs = _lhs_w(h, passes)
    col = 0
    for o_ref, width in ((pa_ref, N_A), (pb_ref, N_B), (pc_ref, N_C), (pd_ref, N_D)):
        o_ref[...] = _mm(lhs, w_ref[:, col:col + width], _NN)
        col += width


def _in_proj(x, g, sc, sh, w_parts):
    t = x.shape[0]
    tm = _tile(t, 512)
    vec = _full((1, D_MODEL))
    return pl.pallas_call(
        functools.partial(_inproj_kernel, passes=P_IN),
        out_shape=[jax.ShapeDtypeStruct((t, n), F32) for n in (N_A, N_B, N_C, N_D)],
        grid=(t // tm,),
        in_specs=[pl.BlockSpec((tm, D_MODEL), lambda i: (i, 0)), vec, vec, vec, _full(w_parts.shape)],
        out_specs=[pl.BlockSpec((tm, n), lambda i: (i, 0)) for n in (N_A, N_B, N_C, N_D)],
        compiler_params=_cparams(("parallel",)),
        name="in_proj",
    )(x, g, sc, sh, w_parts)


def _shift_rows(ext, n):
    return pltpu.roll(ext, n, axis=0)


def _pool_conv_kernel(pa_ref, pah_ref, pc_ref, pch_ref, wp_ref, scale_ref, cw_ref, ya_ref, yc_ref, *, tt):
    i = pl.program_id(0)
    not_first = (i > 0).astype(F32)
    u = pa_ref[...]
    ext = jnp.concatenate([pah_ref[...] * not_first, u], axis=0)
    s2 = ext + _shift_rows(ext, 1)
    s4 = s2 + _shift_rows(s2, 2)
    s8 = s4 + _shift_rows(s4, 4)
    s16 = s8 + _shift_rows(s8, 8)
    grp = lax.broadcasted_iota(jnp.int32, (tt, GROUP_W), 1) // POOL_CG
    tpos = lax.broadcasted_iota(jnp.int32, (tt, GROUP_W), 0) + (i * tt + 1)
    wsum = jnp.where(grp == 0, s2[POOL_HALO:], jnp.where(grp == 1, s4[POOL_HALO:],
                     jnp.where(grp == 2, s8[POOL_HALO:], s16[POOL_HALO:])))
    win = jnp.where(grp == 0, 2, jnp.where(grp == 1, 4, jnp.where(grp == 2, 8, 16)))
    cnt = jnp.minimum(tpos, win).astype(F32)
    d = wsum / cnt - u
    ya_ref[...] = _dot_w(d, wp_ref[...], P_POOL) * scale_ref[...]
    z = pc_ref[:, GROUP_W:2 * GROUP_W] * pc_ref[:, 2 * GROUP_W:3 * GROUP_W]
    zh = pch_ref[:, GROUP_W:2 * GROUP_W] * pch_ref[:, 2 * GROUP_W:3 * GROUP_W] * not_first
    zext = jnp.concatenate([zh, z], axis=0)
    y = (cw_ref[2:3, :] * z + cw_ref[1:2, :] * _shift_rows(zext, 1)[CONV_HALO:]
         + cw_ref[0:1, :] * _shift_rows(zext, 2)[CONV_HALO:])
    yc_ref[...] = pc_ref[:, 0:GROUP_W] * y


def _pool_conv(pa, pc, wp_parts, pool_scale, conv_w):
    t = pa.shape[0]
    tt = _tile(t, 512)
    rp, rc = tt // POOL_HALO, tt // CONV_HALO
    return pl.pallas_call(
        functools.partial(_pool_conv_kernel, tt=tt),
        out_shape=[jax.ShapeDtypeStruct((t, GROUP_W), F32)] * 2,
        grid=(t // tt,),
        in_specs=[
            pl.BlockSpec((tt, N_A), lambda i: (i, 0)),
            pl.BlockSpec((POOL_HALO, N_A), lambda i: (jnp.maximum(i * rp - 1, 0), 0)),
            pl.BlockSpec((tt, N_C), lambda i: (i, 0)),
            pl.BlockSpec((CONV_HALO, N_C), lambda i: (jnp.maximum(i * rc - 1, 0), 0)),
            _full(wp_parts.shape), _full((1, GROUP_W)), _full((CONV_W, GROUP_W)),
        ],
        out_specs=[pl.BlockSpec((tt, GROUP_W), lambda i: (i, 0))] * 2,
        compiler_params=_cparams(("parallel",)),
        name="pool_conv",
    )(pa, pa, pc, pc, wp_parts, pool_scale, conv_w)


def _attn_kernel(cur_ref, prev_ref, sink_ref, o_ref, *, tq):
    i = pl.program_id(0)
    nq = ATT_HEADS * HEAD_DIM
    nkv = ATT_KV_HEADS * HEAD_DIM
    blk = ATT_BLOCK
    hd = HEAD_DIM
    rep = ATT_HEADS // ATT_KV_HEADS
    nblk = tq // blk
    ri = lax.broadcasted_iota(jnp.int32, (rep * blk, 2 * blk), 0)
    ki = lax.broadcasted_iota(jnp.int32, (rep * blk, 2 * blk), 1)
    dist = (ri % blk) + blk - ki
    in_win = (dist >= 0) & (dist < blk)
    distf = dist.astype(F32)
    head_col = lax.broadcasted_iota(jnp.int32, (rep * blk, 1), 0) // blk
    bias, sink = [], []
    for g in range(ATT_KV_HEADS):
        slope_g = jnp.zeros((rep * blk, 1), F32)
        sink_g = jnp.zeros((rep * blk, 1), F32)
        for j in range(rep):
            h = g * rep + j
            slope_g = jnp.where(head_col == j, 2.0 ** (-8.0 * (h + 1) / ATT_HEADS), slope_g)
            sink_g = jnp.where(head_col == j, sink_ref[:, h:h + 1], sink_g)
        bias.append(slope_g * distf)
        sink.append(sink_g)
    bias = jnp.stack(bias)[None]
    sink = jnp.stack(sink)[None]

    q = cur_ref[:, 0:nq] * (hd ** -0.5)
    kv = jnp.concatenate([prev_ref[...], cur_ref[:, nq:nq + 2 * nkv]], axis=0)
    qs, ks, vs = [], [], []
    for b in range(nblk):
        for g in range(ATT_KV_HEADS):
            qs.append(jnp.concatenate(
                [q[b * blk:(b + 1) * blk, (g * rep + j) * hd:(g * rep + j + 1) * hd] for j in range(rep)], axis=0))
            ks.append(kv[b * blk:(b + 2) * blk, g * hd:(g + 1) * hd])
            vs.append(kv[b * blk:(b + 2) * blk, nkv + g * hd:nkv + (g + 1) * hd])
    s = _bdot(jnp.stack(qs), jnp.stack(ks), P_ATT, nt=True)
    s = s.reshape(nblk, ATT_KV_HEADS, rep * blk, 2 * blk)
    s = jnp.where(in_win, s - bias, -jnp.inf)
    first_ok = ki >= blk * (1 - (i > 0).astype(jnp.int32))
    s = jnp.concatenate([jnp.where(first_ok, s[0:1], -jnp.inf), s[1:]], axis=0)
    m = jnp.maximum(jnp.max(s, axis=-1, keepdims=True), sink)
    p = jnp.exp(s - m)
    den = jnp.sum(p, axis=-1, keepdims=True) + jnp.exp(sink - m)
    o = _bdot(p.reshape(nblk * ATT_KV_HEADS, rep * blk, 2 * blk), jnp.stack(vs), P_ATT)
    o = o.reshape(nblk, ATT_KV_HEADS, rep * blk, hd) / den
    for b in range(nblk):
        o_ref[b * blk:(b + 1) * blk, :] = jnp.concatenate(
            [o[b, g, j * blk:(j + 1) * blk, :] for g in range(ATT_KV_HEADS) for j in range(rep)], axis=-1)


def _attention(pb, sinks):
    t = pb.shape[0]
    tq = _tile(t, 512)
    nb = tq // ATT_BLOCK
    return pl.pallas_call(
        functools.partial(_attn_kernel, tq=tq),
        out_shape=jax.ShapeDtypeStruct((t, GROUP_W), F32),
        grid=(t // tq,),
        in_specs=[
            pl.BlockSpec((tq, N_B), lambda i: (i, 0)),
            pl.BlockSpec((ATT_BLOCK, 2 * ATT_KV_HEADS * HEAD_DIM), lambda i: (jnp.maximum(i * nb - 1, 0), 1)),
            _full((1, ATT_HEADS)),
        ],
        out_specs=pl.BlockSpec((tq, GROUP_W), lambda i: (i, 0)),
        compiler_params=_cparams(("parallel",)),
        name="swa_attention",
    )(pb, pb, sinks)


def _rwkv_kernel(*refs, tt, has_vres):
    if has_vres:
        (pd_ref, prev_ref, vf_ref, tril_ref, mu_ref, w0_ref, w2_ref, a0_ref, a2_ref, g2_ref, kk_ref, ka_ref,
         rk_ref, lng_ref, lnb_ref, v0_ref, v1_ref, v2_ref, y_ref, st_ref, rqm_s, y0c_s, yr_s) = refs
    else:
        (pd_ref, prev_ref, tril_ref, mu_ref, w0_ref, w2_ref, a0_ref, a2_ref, g2_ref, kk_ref, ka_ref,
         rk_ref, lng_ref, lnb_ref, y_ref, vout_ref, st_ref, rqm_s, y0c_s, yr_s) = refs
    i = pl.program_id(0)
    nh, hd, gw = RWKV_HEADS, RWKV_HEAD, GROUP_W

    @pl.when(i == 0)
    def _():
        st_ref[...] = jnp.zeros_like(st_ref)

    hr = lax.broadcasted_iota(jnp.int32, (2 * gw, gw), 0) % gw // hd
    hc = lax.broadcasted_iota(jnp.int32, (2 * gw, gw), 1) // hd
    hsum2 = jnp.where(hr == hc, 1.0, 0.0).astype(BF16)

    def head_sum(t2):
        return _mm(jnp.concatenate(_split(t2), axis=1), hsum2, _NN)

    p = pd_ref[...]
    row = lax.broadcasted_iota(jnp.int32, (tt, 1), 0)
    prev_row = prev_ref[SUBLANES - 1:SUBLANES, :] * (i > 0).astype(F32)
    shifted = jnp.where(row == 0, prev_row, pltpu.roll(p, 1, axis=0))
    z = p + (shifted - p) * mu_ref[...]
    r = z[:, 0:gw]
    k = z[:, gw:2 * gw]
    v = z[:, 2 * gw:3 * gw]
    o = 3 * gw
    wd = z[:, o:o + LORA_W]
    ad = z[:, o + LORA_W:o + LORA_W + LORA_A]
    gd = z[:, o + LORA_W + LORA_A:o + LORA_W + LORA_A + LORA_G]
    wpre = -(w0_ref[...] + _dot(jnp.tanh(wd), w2_ref[...], P_DECAY))
    softplus = jnp.maximum(wpre, 0.0) + jnp.log(1.0 + jnp.exp(-jnp.abs(wpre)))
    lw = -jnp.exp(-softplus - 0.5)
    a = _sigmoid(a0_ref[...] + _dot(ad, a2_ref[...], P_LORA))
    g = _dot(_sigmoid(gd), g2_ref[...], P_LORA)
    if has_vres:
        mix = _sigmoid(v0_ref[...] + _dot(_dot(v, v1_ref[...], P_LORA), v2_ref[...], P_LORA))
        v = v + (vf_ref[...] - v) * mix
    else:
        vout_ref[...] = v
    kk = k * kk_ref[...]
    kk = kk / jnp.maximum(jnp.sqrt(head_sum(kk * kk)), 1e-12)
    k = k * (1.0 + (a - 1.0) * ka_ref[...])
    nch = tt // CHUNK
    L = CHUNK
    npair = nh // 2
    pw = 2 * hd
    ti = lax.broadcasted_iota(jnp.int32, (L, pw), 0)
    tl = lax.broadcasted_iota(jnp.int32, (L, pw), 1) % hd
    low_strict = ti > tl
    low_incl = ti >= tl
    eye = (ti == tl).astype(F32)
    first_head = lax.broadcasted_iota(jnp.int32, (1, pw), 1) < hd

    def to_batch(t2):
        return jnp.concatenate([t2[:, q * pw:(q + 1) * pw].reshape(nch, L, pw) for q in range(npair)], axis=0)

    def bd(x):
        zero = jnp.zeros_like(x)
        return jnp.concatenate([jnp.where(first_head, x, zero), jnp.where(first_head, zero, x)], axis=-2)

    def fold(x):
        return x[:, :hd] + x[:, hd:]

    def c16(x):
        return x.astype(BF16)

    lw_hi, lw_lo = _split(lw)
    cl = _mm(tril_ref[...], lw_hi, _NN) + _mm(tril_ref[...], lw_lo, _NN)
    cl3 = cl.reshape(nch, L, gw)
    cl_end = cl3[:, L - 1:L, :]
    e_end = jnp.exp(cl_end - cl3).reshape(tt, gw)
    e_neg = jnp.exp(-cl)
    bb = kk * a
    at = c16(to_batch(-kk * jnp.exp(cl - lw)))
    bh = c16(to_batch(bb * e_neg))
    kh = c16(to_batch(k * e_neg))
    rt = to_batch(r * jnp.exp(cl))
    be = to_batch(bb * e_end)
    ke = to_batch(k * e_end)
    vb = bd(c16(to_batch(v)))
    p_end = jnp.exp(cl_end)
    pe = jnp.concatenate([p_end[:, :, q * pw:(q + 1) * pw] for q in range(npair)], axis=0)

    lhs = jnp.concatenate([at, c16(rt)], axis=1)
    g_b = _bmm(lhs, bd(bh), True)
    g_k = _bmm(lhs, bd(kh), True)
    a_ab = jnp.where(low_strict, g_b[:, :L], 0.0)
    b_rb = jnp.where(low_incl, g_b[:, L:], 0.0)
    a_ak = jnp.where(low_strict, g_k[:, :L], 0.0)
    b_rk = jnp.where(low_incl, g_k[:, L:], 0.0)
    a16 = c16(a_ab)
    apow = _bmm(a16, bd(a16), False)
    tinv = eye + a_ab
    for _ in range(4):
        p16 = c16(apow)
        sq = _bmm(jnp.concatenate([p16, c16(tinv)], axis=1), bd(p16), False)
        tinv = tinv + sq[:, L:]
        apow = sq[:, :L]
    tinv = tinv + _bmm(c16(tinv), bd(c16(apow)), False)
    av = _bmm(c16(a_ak), vb, False)
    wu = _bmm(c16(tinv), jnp.concatenate([bd(at), bd(c16(av))], axis=2), False)
    rhs = jnp.concatenate([bd(c16(wu[:, :, :pw])), bd(c16(wu[:, :, pw:]))], axis=2)
    lhs_b = jnp.concatenate([c16(b_rb), c16(jnp.swapaxes(bd(be), 1, 2))], axis=1)
    lhs_k = jnp.concatenate([c16(b_rk), c16(jnp.swapaxes(bd(ke), 1, 2))], axis=1)
    o_wu = _bmm(lhs_b, rhs, False)
    o_v = _bmm(lhs_k, vb, False)
    rq = rt + o_wu[:, :L, :pw]
    y0 = o_wu[:, :L, pw:] + o_v[:, :L]
    m2 = eye * pe + fold(o_wu[:, L:, :pw])
    c2 = fold(o_wu[:, L:, pw:] + o_v[:, L:])
    rqm_s[...] = jnp.concatenate([rq, m2], axis=1)
    y0c_s[...] = jnp.concatenate([y0, c2], axis=1)

    zs = [st_ref[q] for q in range(npair)]
    for c in range(nch):
        for q in range(npair):
            b = q * nch + c
            yz = _dot(rqm_s[b], bd(zs[q]), P_STATE) + y0c_s[b]
            yr_s[c * L:(c + 1) * L, q * pw:(q + 1) * pw] = yz[:L]
            zs[q] = yz[L:]
    for q in range(npair):
        st_ref[q] = zs[q]

    y = yr_s[...]
    inv_n = 1.0 / hd
    mean = head_sum(y) * inv_n
    yc = y - mean
    var = head_sum(yc * yc) * inv_n
    yn = yc * lax.rsqrt(var + RWKV_GN_EPS) * lng_ref[...] + lnb_ref[...]
    bonus = head_sum(r * k * rk_ref[...]) * v
    y_ref[...] = (yn + bonus) * g


def _rwkv(pd, v_first, prm):
    t = pd.shape[0]
    tt = _tile(t, 512)
    nch = tt // CHUNK
    has_vres = v_first is not None
    gw = GROUP_W
    row_blk = lambda n: pl.BlockSpec((tt, n), lambda i: (i, 0))
    vec = _full((1, gw))
    in_specs = [row_blk(N_D),
                pl.BlockSpec((SUBLANES, N_D), lambda i: (jnp.maximum(i * (tt // SUBLANES) - 1, 0), 0))]
    args = [pd, pd]
    if has_vres:
        in_specs.append(row_blk(gw))
        args.append(v_first)
    tpos = jnp.arange(tt)
    tril = ((tpos[:, None] // CHUNK == tpos[None, :] // CHUNK) & (tpos[:, None] >= tpos[None, :])).astype(BF16)
    in_specs.append(_full((tt, tt)))
    args.append(tril)
    in_specs += [_full((1, N_D)), vec, _full((LORA_W, gw)), vec, _full((LORA_A, gw)), _full((LORA_G, gw)),
                 vec, vec, vec, vec, vec]
    args += [prm["mu"], prm["w0"], prm["w2"], prm["a0"], prm["a2"], prm["g2"], prm["k_k"], prm["k_a"],
             prm["r_k"], prm["ln_g"], prm["ln_b"]]
    if has_vres:
        in_specs += [vec, _full((gw, 32)), _full((32, gw))]
        args += [prm["v0"], prm["v1"], prm["v2"]]
        out_shape = jax.ShapeDtypeStruct((t, gw), F32)
        out_specs = row_blk(gw)
    else:
        out_shape = [jax.ShapeDtypeStruct((t, gw), F32)] * 2
        out_specs = [row_blk(gw)] * 2
    npair, pw = RWKV_HEADS // 2, 2 * RWKV_HEAD
    op = pltpu.VMEM((nch * npair, CHUNK + RWKV_HEAD, pw), F32)
    scratch = [pltpu.VMEM((npair, RWKV_HEAD, pw), F32), op, op, pltpu.VMEM((tt, gw), F32)]
    return pl.pallas_call(
        functools.partial(_rwkv_kernel, tt=tt, has_vres=has_vres),
        out_shape=out_shape,
        grid=(t // tt,),
        in_specs=in_specs,
        out_specs=out_specs,
        scratch_shapes=scratch,
        compiler_params=_cparams(("arbitrary",)),
        name="rwkv7",
    )(*args)


def _outproj_kernel(ya_ref, yb_ref, yc_ref, yd_ref, x_ref, wo_ref, gt_ref, g_ref, sc_ref, sh_ref, wr_ref, br_ref,
                    x1_ref, h2_ref, gate_ref, *, tm):
    ycat = jnp.concatenate([ya_ref[...], yb_ref[...], yc_ref[...], yd_ref[...]], axis=1)
    x1 = x_ref[...] + gt_ref[...] * _dot_w(ycat, wo_ref[...], P_OUT)
    x1_ref[...] = x1
    h2 = _modulated_norm(x1, g_ref[...], sc_ref[...], sh_ref[...])
    h2_ref[...] = h2.astype(h2_ref.dtype)
    lg = _dot_w(h2, wr_ref[...], P_ROUTER) + br_ref[...]
    lane = lax.broadcasted_iota(jnp.int32, (tm, LANES), 1)
    lanef = lane.astype(F32)
    gl = jnp.where(lane < N_GROUPS, lg[:, 0:LANES], -jnp.inf)
    gmax = jnp.max(gl, axis=-1, keepdims=True)
    gp = 1.0 / jnp.sum(jnp.exp(gl - gmax), axis=-1, keepdims=True)
    gi = jnp.min(jnp.where(gl == gmax, lanef, float(LANES)), axis=-1, keepdims=True)
    lo = gi * EXP_PER_GROUP
    in_grp = (lanef >= lo) & (lanef < lo + EXP_PER_GROUP)
    el = jnp.where(in_grp, lg[:, LANES:2 * LANES], -jnp.inf)
    m1 = jnp.max(el, axis=-1, keepdims=True)
    i1 = jnp.min(jnp.where(el == m1, lanef, float(LANES)), axis=-1, keepdims=True)
    el2 = jnp.where(lanef == i1, -jnp.inf, el)
    m2 = jnp.max(el2, axis=-1, keepdims=True)
    i2 = jnp.min(jnp.where(el2 == m2, lanef, float(LANES)), axis=-1, keepdims=True)
    e2 = jnp.exp(m2 - m1)
    w_top = gp / (1.0 + e2)
    gate_ref[...] = jnp.where(lanef == i1 - lo, w_top, jnp.where(lanef == i2 - lo, w_top * e2,
                              jnp.where(lane == ROUTE_GROUP_LANE, gi, 0.0)))


def _out_proj(ya, yb, yc, yd, x, wo_parts, gt, g, sc, sh, w_router, b_router):
    t = x.shape[0]
    tm = _tile(t, 512)
    row_blk = lambda n: pl.BlockSpec((tm, n), lambda i: (i, 0))
    vec = _full((1, D_MODEL))
    return pl.pallas_call(
        functools.partial(_outproj_kernel, tm=tm),
        out_shape=[jax.ShapeDtypeStruct((t, D_MODEL), F32), jax.ShapeDtypeStruct((t, D_MODEL), BF16),
                   jax.ShapeDtypeStruct((t, LANES), F32)],
        grid=(t // tm,),
        in_specs=[row_blk(GROUP_W)] * 4 + [row_blk(D_MODEL), _full(wo_parts.shape), vec, vec, vec, vec,
                                           _full(w_router.shape), _full((1, 2 * LANES))],
        out_specs=[row_blk(D_MODEL), row_blk(D_MODEL), row_blk(LANES)],
        compiler_params=_cparams(("parallel",)),
        name="out_proj_router",
    )(ya, yb, yc, yd, x, wo_parts, gt, g, sc, sh, w_router, b_router)


def _moe_kernel(h_ref, route_ref, tril_ref, ex_ref, wg_ref, wu_ref, wd_ref, x1_ref, gt_ref, fg_ref, o_ref,
                acc_ref, col_ref, row_ref, cnt_ref, *, final, tm, cap, cap_extra):
    g = pl.program_id(1)
    lane = lax.broadcasted_iota(jnp.int32, (tm, LANES), 1)

    @pl.when(g == 0)
    def _():
        acc_ref[...] = jnp.zeros_like(acc_ref)
        gi = route_ref[:, ROUTE_GROUP_LANE:ROUTE_GROUP_LANE + 1]
        onehot = lane.astype(F32) == gi
        cum = _mm(tril_ref[...], jnp.where(onehot, 1.0, 0.0).astype(BF16), _NN)
        rank = jnp.sum(jnp.where(onehot, cum, 0.0), axis=-1, keepdims=True) - 1.0
        cnt_ref[...] = cum[tm - 1:tm, :]
        rec = jnp.where(lane == 0, rank, jnp.where(lane == 1, gi, 0.0))
        col_ref[...] = rec
        row_ref[...] = rec.T[0:SUBLANES, :]

    gf = g.astype(F32)
    count = jnp.sum(jnp.where(lane[0:1, :] == g, cnt_ref[...], 0.0))
    count = count.astype(jnp.int32)
    sel_col = jnp.where(col_ref[:, 1:2] == gf, col_ref[:, 0:1], -1.0)
    sel_row = jnp.where(row_ref[1:2, :] == gf, row_ref[0:1, :], -1.0)
    hx = jnp.concatenate([h_ref[...], route_ref[...].astype(BF16)], axis=1)

    def block(base, rows):
        base = base.astype(F32)
        slot_r = lax.broadcasted_iota(jnp.int32, (rows, tm), 0).astype(F32) + base
        gather = jnp.where(slot_r == sel_row, 1.0, 0.0).astype(BF16)
        hs = _mm(gather, hx, _NN)
        h = hs[:, 0:D_MODEL].astype(BF16)
        gates = _mm(hs[:, D_MODEL:D_MODEL + LANES].astype(BF16), ex_ref[...], _NN)
        hg = _mm(h, wg_ref[...], _NN)
        hu = _mm(h, wu_ref[...], _NN)
        act = hg * _sigmoid(hg) * hu * gates
        ys = _mm(act.astype(BF16), wd_ref[...], _NN)
        slot_c = lax.broadcasted_iota(jnp.int32, (tm, rows), 1).astype(F32) + base
        scatter = jnp.where(slot_c == sel_col, 1.0, 0.0).astype(BF16)
        acc_ref[...] += _mm(scatter, ys.astype(BF16), _NN)

    @pl.when(count > 0)
    def _():
        block(jnp.int32(0), cap)

    def extra(r, carry):
        block(cap + r * cap_extra, cap_extra)
        return carry

    lax.fori_loop(0, (jnp.maximum(count - cap, 0) + (cap_extra - 1)) // cap_extra, extra, 0)

    @pl.when(g == pl.num_programs(1) - 1)
    def _():
        out = x1_ref[...] + gt_ref[...] * acc_ref[...]
        if final:
            out = out * lax.rsqrt(jnp.mean(out * out, axis=-1, keepdims=True) + NORM_EPS) * fg_ref[...]
        o_ref[...] = out


def _moe(h2, route, wg, wu, wd, x1, gt, final_g, final):
    t = h2.shape[0]
    tm = _tile(t, MOE_TILE)
    cap = tm // N_GROUPS
    cap_extra = cap // 2
    nw = EXP_PER_GROUP * D_EXPERT
    vec = _full((1, D_MODEL))
    tpos = jnp.arange(tm)
    tril = (tpos[:, None] >= tpos[None, :]).astype(BF16)
    expand = (jnp.arange(LANES)[:, None] == jnp.arange(nw)[None, :] // D_EXPERT).astype(BF16)
    return pl.pallas_call(
        functools.partial(_moe_kernel, final=final, tm=tm, cap=cap, cap_extra=cap_extra),
        out_shape=jax.ShapeDtypeStruct((t, D_MODEL), F32),
        grid=(t // tm, N_GROUPS),
        in_specs=[
            pl.BlockSpec((tm, D_MODEL), lambda i, g: (i, 0)),
            pl.BlockSpec((tm, LANES), lambda i, g: (i, 0)),
            _full((tm, tm)), _full((LANES, nw)),
            pl.BlockSpec((D_MODEL, nw), lambda i, g: (0, g)),
            pl.BlockSpec((D_MODEL, nw), lambda i, g: (0, g)),
            pl.BlockSpec((nw, D_MODEL), lambda i, g: (g, 0)),
            pl.BlockSpec((tm, D_MODEL), lambda i, g: (i, 0)),
            vec, vec,
        ],
        out_specs=pl.BlockSpec((tm, D_MODEL), lambda i, g: (i, 0)),
        scratch_shapes=[pltpu.VMEM((tm, D_MODEL), F32), pltpu.VMEM((tm, LANES), F32),
                        pltpu.VMEM((SUBLANES, tm), F32), pltpu.VMEM((1, LANES), F32)],
        compiler_params=_cparams(("parallel", "arbitrary")),
        name="moe_experts",
    )(h2, route, tril, expand, wg, wu, wd, x1, gt, final_g)


def kernel(x, c, w_ada, b_ada, norm1_g, norm2_g, w_in, w_out, pool_w, pool_scale, attn_sinks, conv_w, rwkv_mu,
           rwkv_w0, rwkv_w2, rwkv_a0, rwkv_a2, rwkv_g2, rwkv_k_k, rwkv_k_a, rwkv_r_k, rwkv_ln_g, rwkv_ln_b,
           rwkv_v0, rwkv_v1, rwkv_v2, moe_w_grp, moe_b_grp, moe_w_exp, moe_b_exp, moe_w_gate, moe_w_up,
           moe_w_down, final_g):
    bsz, t, d = x.shape
    assert bsz == 1 and d == D_MODEL
    depth = w_ada.shape[0]
    xs = x.reshape(t, d)
    mod = _ada_mod(c, w_ada, b_ada)
    v_first = None
    row = lambda a: a.reshape(1, -1)
    for l in range(depth):
        sh1, sc1, gt1, sh2, sc2, gt2 = [mod[l, :, j * d:(j + 1) * d] for j in range(6)]
        pa, pb, pc, pd = _in_proj(xs, row(norm1_g[l]), sc1, sh1, _split_w(w_in[l], P_IN))
        wp = jnp.zeros((GROUP_W, GROUP_W), F32)
        for gidx in range(len(POOL_WINDOWS)):
            sl = slice(gidx * POOL_CG, (gidx + 1) * POOL_CG)
            wp = wp.at[sl, sl].set(pool_w[l, gidx])
        ya, yc = _pool_conv(pa, pc, _split_w(wp, P_POOL), row(pool_scale[l]), conv_w[l])
        yb = _attention(pb, row(attn_sinks[l]))
        prm = dict(mu=row(rwkv_mu[l]), w0=row(rwkv_w0[l]), w2=rwkv_w2[l], a0=row(rwkv_a0[l]), a2=rwkv_a2[l],
                   g2=rwkv_g2[l], k_k=row(rwkv_k_k[l]), k_a=row(rwkv_k_a[l]), r_k=row(rwkv_r_k[l]),
                   ln_g=row(rwkv_ln_g[l]), ln_b=row(rwkv_ln_b[l]))
        if l == 0:
            yd, v_first = _rwkv(pd, None, prm)
        else:
            prm.update(v0=row(rwkv_v0[l - 1]), v1=rwkv_v1[l - 1], v2=rwkv_v2[l - 1])
            yd = _rwkv(pd, v_first, prm)
        w_router = jnp.zeros((d, 2 * LANES), F32)
        w_router = w_router.at[:, 0:N_GROUPS].set(moe_w_grp[l]).at[:, LANES:LANES + N_EXPERTS].set(moe_w_exp[l])
        b_router = jnp.zeros((1, 2 * LANES), F32)
        b_router = b_router.at[0, 0:N_GROUPS].set(moe_b_grp[l]).at[0, LANES:LANES + N_EXPERTS].set(moe_b_exp[l])
        x1, h2, gates = _out_proj(ya, yb, yc, yd, xs, _split_w(w_out[l], P_OUT), gt1, row(norm2_g[l]), sc2, sh2,
                                  _split_w(w_router, P_ROUTER), b_router)
        wg =jnp.transpose(moe_w_gate[l].astype(BF16), (1, 0, 2)).reshape(d, N_EXPERTS * D_EXPERT)
        wu = jnp.transpose(moe_w_up[l].astype(BF16), (1, 0, 2)).reshape(d, N_EXPERTS * D_EXPERT)
        wd = moe_w_down[l].astype(BF16).reshape(N_EXPERTS * D_EXPERT, d)
        xs = _moe(h2, gates, wg, wu, wd, x1, gt2, row(final_g), final=(l == depth - 1))
    return xs.reshape(bsz, t, d)
```

```python
import functools

import jax
import jax.numpy as jnp
from jax import lax
from jax.experimental import pallas as pl
from jax.experimental.pallas import tpu as pltpu

F32 = jnp.float32
BF16 = jnp.bfloat16

D_MODEL = 1024
GROUP_W = 256
NORM_EPS = 1e-6
POOL_WINDOWS = (2, 4, 8, 16)
POOL_CG = 64
HEAD_DIM = 64
ATT_HEADS = 4
ATT_KV_HEADS = 2
ATT_BLOCK = 128
CONV_W = 3
RWKV_HEAD = 64
RWKV_HEADS = 4
LORA_W = 32
LORA_A = 32
LORA_G = 64
RWKV_GN_EPS = 64e-5
N_A = GROUP_W
N_B = (ATT_HEADS + 2 * ATT_KV_HEADS) * HEAD_DIM
N_C = 3 * GROUP_W
N_D = 3 * GROUP_W + LORA_W + LORA_A + LORA_G
N_IN = N_A + N_B + N_C + N_D
N_GROUPS = 4
EXP_PER_GROUP = 8
N_EXPERTS = 32
D_EXPERT = 128

LANES = 128
SUBLANES = 8
VMEM_LIMIT = 56 * 1024 * 1024

CHUNK = 64
POOL_HALO = 16
CONV_HALO = 8
OUT_SUB = 4
ROUTE_GROUP_LANE = 8
MOE_TILE = 1024

P_IN = 1
P_OUT = 1
P_POOL = 1
P_ATT = 1
P_MOE = 1
P_DECAY = 3
P_LORA = 1
P_STATE = 1
P_ROUTER = 3


def _split(a):
    hi = a.astype(BF16)
    lo = (a - hi.astype(F32)).astype(BF16)
    return hi, lo


def _mm(a, b, dims):
    return lax.dot_general(a, b, (dims, ((), ())), preferred_element_type=F32)


_NN = ((1,), (0,))
_NT = ((1,), (1,))


def _dot(a, b, passes=1, dims=_NN):
    if passes == 1:
        return _mm(a.astype(BF16), b.astype(BF16), dims)
    a0, a1 = _split(a)
    b0, b1 = _split(b)
    return _mm(a0, b0, dims) + (_mm(a0, b1, dims) + _mm(a1, b0, dims))


def _bmm(a, b, nt):
    dims = (((2,), (2 if nt else 1,)), ((0,), (0,)))
    return lax.dot_general(a, b, dims, preferred_element_type=F32)


def _bdot(a, b, passes=1, nt=False):
    if passes == 1:
        return _bmm(a.astype(BF16), b.astype(BF16), nt)
    a0, a1 = _split(a)
    b0, b1 = _split(b)
    return _bmm(a0, b0, nt) + (_bmm(a0, b1, nt) + _bmm(a1, b0, nt))


def _lhs_w(a, passes):
    if passes == 1:
        return a.astype(BF16)
    a0, a1 = _split(a)
    return jnp.concatenate([a0, a0, a1], axis=1)


def _dot_w(a, w_cat, passes):
    return _mm(_lhs_w(a, passes), w_cat, _NN)


def _split_w(w, passes):
    hi = w.astype(BF16)
    if passes == 1:
        return hi
    lo = (w - hi.astype(F32)).astype(BF16)
    return jnp.concatenate([hi, lo, hi], axis=0)


def _sigmoid(x):
    return 1.0 / (1.0 + jnp.exp(-x))


def _cparams(sem):
    return pltpu.CompilerParams(dimension_semantics=sem, vmem_limit_bytes=VMEM_LIMIT)


def _full(shape):
    return pl.BlockSpec(shape, lambda *_: (0,) * len(shape))


def _tile(n, pref):
    t = min(n, pref)
    assert n % t == 0, (n, t)
    return t


def _mod_kernel(c_ref, w_ref, b_ref, o_ref):
    c = c_ref[...]
    cond = c * _sigmoid(c)
    o_ref[...] = _dot(cond, w_ref[...], 3) + b_ref[...]


def _ada_mod(c, w_ada, b_ada):
    depth = w_ada.shape[0]
    c8 = jnp.broadcast_to(c, (SUBLANES, D_MODEL))
    out = pl.pallas_call(
        _mod_kernel,
        out_shape=jax.ShapeDtypeStruct((depth, SUBLANES, 6 * D_MODEL), F32),
        grid=(depth, 6),
        in_specs=[
            pl.BlockSpec((SUBLANES, D_MODEL), lambda l, j: (0, 0)),
            pl.BlockSpec((None, D_MODEL, D_MODEL), lambda l, j: (l, 0, j)),
            pl.BlockSpec((None, 1, D_MODEL), lambda l, j: (l, 0, j)),
        ],
        out_specs=pl.BlockSpec((None, SUBLANES, D_MODEL), lambda l, j: (l, 0, j)),
        compiler_params=_cparams(("arbitrary", "arbitrary")),
        name="ada_mod",
    )(c8, w_ada, b_ada.reshape(depth, 1, 6 * D_MODEL))
    return out[:, 0:1, :]


def _modulated_norm(x, g, sc, sh):
    y = x * lax.rsqrt(jnp.mean(x * x, axis=-1, keepdims=True) + NORM_EPS) * g
    return y * (1.0 + sc) + sh


def _shift_rows(ext, n):
    return pltpu.roll(ext, n, axis=0)


def _pool_mixer(u, halo, wp, scale, t0):
    tt = u.shape[0]
    ext = jnp.concatenate([halo, u], axis=0)
    s2 = ext + _shift_rows(ext, 1)
    s4 = s2 + _shift_rows(s2, 2)
    s8 = s4 + _shift_rows(s4, 4)
    s16 = s8 + _shift_rows(s8, 8)
    grp = lax.broadcasted_iota(jnp.int32, (tt, GROUP_W), 1) // POOL_CG
    tpos = lax.broadcasted_iota(jnp.int32, (tt, GROUP_W), 0) + (t0 + 1)
    wsum = jnp.where(grp == 0, s2[POOL_HALO:], jnp.where(grp == 1, s4[POOL_HALO:],
                     jnp.where(grp == 2, s8[POOL_HALO:], s16[POOL_HALO:])))
    win = jnp.where(grp == 0, 2, jnp.where(grp == 1, 4, jnp.where(grp == 2, 8, 16)))
    cnt = jnp.minimum(tpos, win).astype(F32)
    return _dot_w(wsum / cnt - u, wp, P_POOL) * scale


def _conv_mixer(bg, z, z_halo, cw):
    zext = jnp.concatenate([z_halo, z], axis=0)
    y = (cw[2:3, :] * z + cw[1:2, :] * _shift_rows(zext, 1)[CONV_HALO:]
         + cw[0:1, :] * _shift_rows(zext, 2)[CONV_HALO:])
    return bg * y


def _inproj_kernel(x_ref, g_ref, sc_ref, sh_ref, w_ref, wp_ref, scale_ref, cw_ref,
                   ya_ref, pb_ref, yc_ref, pd_ref, u_tail, z_tail, *, passes, tm):
    i = pl.program_id(0)

    @pl.when(i == 0)
    def _():
        u_tail[...] = jnp.zeros_like(u_tail)
        z_tail[...] = jnp.zeros_like(z_tail)

    h = _modulated_norm(x_ref[...], g_ref[...], sc_ref[...], sh_ref[...])
    lhs = _lhs_w(h, passes)
    proj = lambda col, width: _mm(lhs, w_ref[:, col:col + width], _NN)
    u = proj(0, N_A)
    pc = proj(N_A + N_B, N_C)
    ya_ref[...] = _pool_mixer(u, u_tail[...], wp_ref[...], scale_ref[...], i * tm)
    u_tail[...] = u[tm - POOL_HALO:, :]
    z = pc[:, GROUP_W:2 * GROUP_W] * pc[:, 2 * GROUP_W:3 * GROUP_W]
    yc_ref[...] = _conv_mixer(pc[:, 0:GROUP_W], z, z_tail[...], cw_ref[...])
    z_tail[...] = z[tm - CONV_HALO:, :]
    pb_ref[...] = proj(N_A, N_B)
    pd_ref[...] = proj(N_A + N_B + N_C, N_D)


def _in_proj(x, g, sc, sh, w_parts, wp_parts, pool_scale, conv_w):
    t = x.shape[0]
    tm = _tile(t, 512)
    vec = _full((1, D_MODEL))
    widths = (GROUP_W, N_B, GROUP_W, N_D)
    return pl.pallas_call(
        functools.partial(_inproj_kernel, passes=P_IN, tm=tm),
        out_shape=[jax.ShapeDtypeStruct((t, n), F32) for n in widths],
        grid=(t // tm,),
        in_specs=[pl.BlockSpec((tm, D_MODEL), lambda i: (i, 0)), vec, vec, vec, _full(w_parts.shape),
                  _full(wp_parts.shape), _full((1, GROUP_W)), _full((CONV_W, GROUP_W))],
        out_specs=[pl.BlockSpec((tm, n), lambda i: (i, 0)) for n in widths],
        scratch_shapes=[pltpu.VMEM((POOL_HALO, GROUP_W), F32), pltpu.VMEM((CONV_HALO, GROUP_W), F32)],
        compiler_params=_cparams(("arbitrary",)),
        name="in_proj_pool_conv",
    )(x, g, sc, sh, w_parts, wp_parts, pool_scale, conv_w)


def _attn_kernel(cur_ref, prev_ref, sink_ref, o_ref, *, tq):
    i = pl.program_id(0)
    nq = ATT_HEADS * HEAD_DIM
    nkv = ATT_KV_HEADS * HEAD_DIM
    blk = ATT_BLOCK
    hd = HEAD_DIM
    rep = ATT_HEADS // ATT_KV_HEADS
    nblk = tq // blk
    ri = lax.broadcasted_iota(jnp.int32, (rep * blk, 2 * blk), 0)
    ki = lax.broadcasted_iota(jnp.int32, (rep * blk, 2 * blk), 1)
    dist = (ri % blk) + blk - ki
    in_win = (dist >= 0) & (dist < blk)
    distf = dist.astype(F32)
    head_col = lax.broadcasted_iota(jnp.int32, (rep * blk, 1), 0) // blk
    bias, sink = [], []
    for g in range(ATT_KV_HEADS):
        slope_g = jnp.zeros((rep * blk, 1), F32)
        sink_g = jnp.zeros((rep * blk, 1), F32)
        for j in range(rep):
            h = g * rep + j
            slope_g = jnp.where(head_col == j, 2.0 ** (-8.0 * (h + 1) / ATT_HEADS), slope_g)
            sink_g = jnp.where(head_col == j, sink_ref[:, h:h + 1], sink_g)
        bias.append(slope_g * distf)
        sink.append(sink_g)
    bias = jnp.stack(bias)[None]
    sink = jnp.stack(sink)[None]

    q = cur_ref[:, 0:nq] * (hd ** -0.5)
    kv = jnp.concatenate([prev_ref[...], cur_ref[:, nq:nq + 2 * nkv]], axis=0)
    qs, ks, vs = [], [], []
    for b in range(nblk):
        for g in range(ATT_KV_HEADS):
            qs.append(jnp.concatenate(
                [q[b * blk:(b + 1) * blk, (g * rep + j) * hd:(g * rep + j + 1) * hd] for j in range(rep)], axis=0))
            ks.append(kv[b * blk:(b + 2) * blk, g * hd:(g + 1) * hd])
            vs.append(kv[b * blk:(b + 2) * blk, nkv + g * hd:nkv + (g + 1) * hd])
    s = _bdot(jnp.stack(qs), jnp.stack(ks), P_ATT, nt=True)
    s = s.reshape(nblk, ATT_KV_HEADS, rep * blk, 2 * blk)
    s = jnp.where(in_win, s - bias, -jnp.inf)
    first_ok = ki >= blk * (1 - (i > 0).astype(jnp.int32))
    s = jnp.concatenate([jnp.where(first_ok, s[0:1], -jnp.inf), s[1:]], axis=0)
    m = jnp.maximum(jnp.max(s, axis=-1, keepdims=True), sink)
    p = jnp.exp(s - m)
    den = jnp.sum(p, axis=-1, keepdims=True) + jnp.exp(sink - m)
    o = _bdot(p.reshape(nblk * ATT_KV_HEADS, rep * blk, 2 * blk), jnp.stack(vs), P_ATT)
    o = o.reshape(nblk, ATT_KV_HEADS, rep * blk, hd) / den
    for b in range(nblk):
        o_ref[b * blk:(b + 1) * blk, :] = jnp.concatenate(
            [o[b, g, j * blk:(j + 1) * blk, :] for g in range(ATT_KV_HEADS) for j in range(rep)], axis=-1)


def _attention(pb, sinks):
    t = pb.shape[0]
    tq = _tile(t, 512)
    nb = tq // ATT_BLOCK
    return pl.pallas_call(
        functools.partial(_attn_kernel, tq=tq),
        out_shape=jax.ShapeDtypeStruct((t, GROUP_W), F32),
        grid=(t // tq,),
        in_specs=[
            pl.BlockSpec((tq, N_B), lambda i: (i, 0)),
            pl.BlockSpec((ATT_BLOCK, 2 * ATT_KV_HEADS * HEAD_DIM), lambda i: (jnp.maximum(i * nb - 1, 0), 1)),
            _full((1, ATT_HEADS)),
        ],
        out_specs=pl.BlockSpec((tq, GROUP_W), lambda i: (i, 0)),
        compiler_params=_cparams(("parallel",)),
        name="swa_attention",
    )(pb, pb, sinks)


def _rwkv_kernel(*refs, tt, has_vres):
    if has_vres:
        (pd_ref, prev_ref, vf_ref, tril_ref, mu_ref, w0_ref, w2_ref, a0_ref, a2_ref, g2_ref, kk_ref, ka_ref,
         rk_ref, lng_ref, lnb_ref, v0_ref, v1_ref, v2_ref, y_ref, st_ref, rqm_s, y0c_s, yr_s) = refs
    else:
        (pd_ref, prev_ref, tril_ref, mu_ref, w0_ref, w2_ref, a0_ref, a2_ref, g2_ref, kk_ref, ka_ref,
         rk_ref, lng_ref, lnb_ref, y_ref, vout_ref, st_ref, rqm_s, y0c_s, yr_s) = refs
    i = pl.program_id(0)
    nh, hd, gw = RWKV_HEADS, RWKV_HEAD, GROUP_W

    @pl.when(i == 0)
    def _():
        st_ref[...] = jnp.zeros_like(st_ref)

    hr = lax.broadcasted_iota(jnp.int32, (2 * gw, gw), 0) % gw // hd
    hc = lax.broadcasted_iota(jnp.int32, (2 * gw, gw), 1) // hd
    hsum2 = jnp.where(hr == hc, 1.0, 0.0).astype(BF16)

    def head_sum(t2):
        return _mm(jnp.concatenate(_split(t2), axis=1), hsum2, _NN)

    p = pd_ref[...]
    row = lax.broadcasted_iota(jnp.int32, (tt, 1), 0)
    prev_row = prev_ref[SUBLANES - 1:SUBLANES, :] * (i > 0).astype(F32)
    shifted = jnp.where(row == 0, prev_row, pltpu.roll(p, 1, axis=0))
    z = p + (shifted - p) * mu_ref[...]
    r = z[:, 0:gw]
    k = z[:, gw:2 * gw]
    v = z[:, 2 * gw:3 * gw]
    o = 3 * gw
    wd = z[:, o:o + LORA_W]
    ad = z[:, o + LORA_W:o + LORA_W + LORA_A]
    gd = z[:, o + LORA_W + LORA_A:o + LORA_W + LORA_A + LORA_G]
    wpre = -(w0_ref[...] + _dot(jnp.tanh(wd), w2_ref[...], P_DECAY))
    softplus = jnp.maximum(wpre, 0.0) + jnp.log(1.0 + jnp.exp(-jnp.abs(wpre)))
    lw = -jnp.exp(-softplus - 0.5)
    a = _sigmoid(a0_ref[...] + _dot(ad, a2_ref[...], P_LORA))
    g = _dot(_sigmoid(gd), g2_ref[...], P_LORA)
    if has_vres:
        mix = _sigmoid(v0_ref[...] + _dot(_dot(v, v1_ref[...], P_LORA), v2_ref[...], P_LORA))
        v = v + (vf_ref[...] - v) * mix
    else:
        vout_ref[...] = v
    kk = k * kk_ref[...]
    kk = kk / jnp.maximum(jnp.sqrt(head_sum(kk * kk)), 1e-12)
    k = k * (1.0 + (a - 1.0) * ka_ref[...])
    nch = tt // CHUNK
    L = CHUNK
    npair = nh // 2
    pw = 2 * hd
    ti = lax.broadcasted_iota(jnp.int32, (L, pw), 0)
    tl = lax.broadcasted_iota(jnp.int32, (L, pw), 1) % hd
    low_strict = ti > tl
    low_incl = ti >= tl
    eye = (ti == tl).astype(F32)
    first_head = lax.broadcasted_iota(jnp.int32, (1, pw), 1) < hd

    def to_batch(t2):
        return jnp.concatenate([t2[:, q * pw:(q + 1) * pw].reshape(nch, L, pw) for q in range(npair)], axis=0)

    def bd(x):
        zero = jnp.zeros_like(x)
        return jnp.concatenate([jnp.where(first_head, x, zero), jnp.where(first_head, zero, x)], axis=-2)

    def fold(x):
        return x[:, :hd] + x[:, hd:]

    def c16(x):
        return x.astype(BF16)

    lw_hi, lw_lo = _split(lw)
    cl = _mm(tril_ref[...], lw_hi, _NN) + _mm(tril_ref[...], lw_lo, _NN)
    cl3 = cl.reshape(nch, L, gw)
    cl_end = cl3[:, L - 1:L, :]
    e_end = jnp.exp(cl_end - cl3).reshape(tt, gw)
    e_neg = jnp.exp(-cl)
    bb = kk * a
    at = c16(to_batch(-kk * jnp.exp(cl - lw)))
    bh = c16(to_batch(bb * e_neg))
    kh = c16(to_batch(k * e_neg))
    rt = to_batch(r * jnp.exp(cl))
    be = to_batch(bb * e_end)
    ke = to_batch(k * e_end)
    vb = bd(c16(to_batch(v)))
    p_end = jnp.exp(cl_end)
    pe = jnp.concatenate([p_end[:, :, q * pw:(q + 1) * pw] for q in range(npair)], axis=0)

    lhs = jnp.concatenate([at, c16(rt)], axis=1)
    g_b = _bmm(lhs, bd(bh), True)
    g_k = _bmm(lhs, bd(kh), True)
    a_ab = jnp.where(low_strict, g_b[:, :L], 0.0)
    b_rb = jnp.where(low_incl, g_b[:, L:], 0.0)
    a_ak = jnp.where(low_strict, g_k[:, :L], 0.0)
    b_rk = jnp.where(low_incl, g_k[:, L:], 0.0)
    a16 = c16(a_ab)
    apow = _bmm(a16, bd(a16), False)
    tinv = eye + a_ab
    for _ in range(4):
        p16 = c16(apow)
        sq = _bmm(jnp.concatenate([p16, c16(tinv)], axis=1), bd(p16), False)
        tinv = tinv + sq[:, L:]
        apow = sq[:, :L]
    tinv = tinv + _bmm(c16(tinv), bd(c16(apow)), False)
    av = _bmm(c16(a_ak), vb, False)
    wu = _bmm(c16(tinv), jnp.concatenate([bd(at), bd(c16(av))], axis=2), False)
    rhs = jnp.concatenate([bd(c16(wu[:, :, :pw])), bd(c16(wu[:, :, pw:]))], axis=2)
    lhs_b = jnp.concatenate([c16(b_rb), c16(jnp.swapaxes(bd(be), 1, 2))], axis=1)
    lhs_k = jnp.concatenate([c16(b_rk), c16(jnp.swapaxes(bd(ke), 1, 2))], axis=1)
    o_wu = _bmm(lhs_b, rhs, False)
    o_v = _bmm(lhs_k, vb, False)
    rq = rt + o_wu[:, :L, :pw]
    y0 = o_wu[:, :L, pw:] + o_v[:, :L]
    m2 = eye * pe + fold(o_wu[:, L:, :pw])
    c2 = fold(o_wu[:, L:, pw:] + o_v[:, L:])
    rqm_s[...] = jnp.concatenate([rq, m2], axis=1)
    y0c_s[...] = jnp.concatenate([y0, c2], axis=1)

    zs = [st_ref[q] for q in range(npair)]
    for c in range(nch):
        for q in range(npair):
            b = q * nch + c
            yz = _dot(rqm_s[b], bd(zs[q]), P_STATE) + y0c_s[b]
            yr_s[c * L:(c + 1) * L, q * pw:(q + 1) * pw] = yz[:L]
            zs[q] = yz[L:]
    for q in range(npair):
        st_ref[q] = zs[q]

    y = yr_s[...]
    inv_n = 1.0 / hd
    mean = head_sum(y) * inv_n
    yc = y - mean
    var = head_sum(yc * yc) * inv_n
    yn = yc * lax.rsqrt(var + RWKV_GN_EPS) * lng_ref[...] + lnb_ref[...]
    bonus = head_sum(r * k * rk_ref[...]) * v
    y_ref[...] = (yn + bonus) * g


def _rwkv(pd, v_first, prm):
    t = pd.shape[0]
    tt = _tile(t, 512)
    nch = tt // CHUNK
    has_vres = v_first is not None
    gw = GROUP_W
    row_blk = lambda n: pl.BlockSpec((tt, n), lambda i: (i, 0))
    vec = _full((1, gw))
    in_specs = [row_blk(N_D),
                pl.BlockSpec((SUBLANES, N_D), lambda i: (jnp.maximum(i * (tt // SUBLANES) - 1, 0), 0))]
    args = [pd, pd]
    if has_vres:
        in_specs.append(row_blk(gw))
        args.append(v_first)
    tpos = jnp.arange(tt)
    tril = ((tpos[:, None] // CHUNK == tpos[None, :] // CHUNK) & (tpos[:, None] >= tpos[None, :])).astype(BF16)
    in_specs.append(_full((tt, tt)))
    args.append(tril)
    in_specs += [_full((1, N_D)), vec, _full((LORA_W, gw)), vec, _full((LORA_A, gw)), _full((LORA_G, gw)),
                 vec, vec, vec, vec, vec]
    args += [prm["mu"], prm["w0"], prm["w2"], prm["a0"], prm["a2"], prm["g2"], prm["k_k"], prm["k_a"],
             prm["r_k"], prm["ln_g"], prm["ln_b"]]
    if has_vres:
        in_specs += [vec, _full((gw, 32)), _full((32, gw))]
        args += [prm["v0"], prm["v1"], prm["v2"]]
        out_shape = jax.ShapeDtypeStruct((t, gw), F32)
        out_specs = row_blk(gw)
    else:
        out_shape = [jax.ShapeDtypeStruct((t, gw), F32)] * 2
        out_specs = [row_blk(gw)] * 2
    npair, pw = RWKV_HEADS // 2, 2 * RWKV_HEAD
    op = pltpu.VMEM((nch * npair, CHUNK + RWKV_HEAD, pw), F32)
    scratch = [pltpu.VMEM((npair, RWKV_HEAD, pw), F32), op, op, pltpu.VMEM((tt, gw), F32)]
    return pl.pallas_call(
        functools.partial(_rwkv_kernel, tt=tt, has_vres=has_vres),
        out_shape=out_shape,
        grid=(t // tt,),
        in_specs=in_specs,
        out_specs=out_specs,
        scratch_shapes=scratch,
        compiler_params=_cparams(("arbitrary",)),
        name="rwkv7",
    )(*args)


def _route_record(lg):
    lane = lax.broadcasted_iota(jnp.int32, (lg.shape[0], LANES), 1)
    lanef = lane.astype(F32)
    gl = jnp.where(lane < N_GROUPS, lg[:, 0:LANES], -jnp.inf)
    gmax = jnp.max(gl, axis=-1, keepdims=True)
    gp = 1.0 / jnp.sum(jnp.exp(gl - gmax), axis=-1, keepdims=True)
    gi = jnp.min(jnp.where(gl == gmax, lanef, float(LANES)), axis=-1, keepdims=True)
    lo = gi * EXP_PER_GROUP
    in_grp = (lanef >= lo) & (lanef < lo + EXP_PER_GROUP)
    el = jnp.where(in_grp, lg[:, LANES:2 * LANES], -jnp.inf)
    m1 = jnp.max(el, axis=-1, keepdims=True)
    i1 = jnp.min(jnp.where(el == m1, lanef, float(LANES)), axis=-1, keepdims=True)
    el2 = jnp.where(lanef == i1, -jnp.inf, el)
    m2 = jnp.max(el2, axis=-1, keepdims=True)
    i2 = jnp.min(jnp.where(el2 == m2, lanef, float(LANES)), axis=-1, keepdims=True)
    e2 = jnp.exp(m2 - m1)
    w_top = gp / (1.0 + e2)
    return jnp.where(lanef == i1 - lo, w_top, jnp.where(lanef == i2 - lo, w_top * e2,
                     jnp.where(lane == ROUTE_GROUP_LANE, gi, 0.0)))


def _outproj_kernel(ya_ref, yb_ref, yc_ref, yd_ref, x_ref, wo_ref, gt_ref, g_ref, sc_ref, sh_ref, wr_ref, br_ref,
                    x1_ref, h2_ref, gate_ref, *, tm):
    sub = tm // OUT_SUB
    for blk in range(OUT_SUB):
        rows = slice(blk * sub, (blk + 1) * sub)
        ycat = jnp.concatenate([ya_ref[rows, :], yb_ref[rows, :], yc_ref[rows, :], yd_ref[rows, :]], axis=1)
        x1 = x_ref[rows, :] + gt_ref[...] * _dot_w(ycat, wo_ref[...], P_OUT)
        x1_ref[rows, :] = x1
        h2 = _modulated_norm(x1, g_ref[...], sc_ref[...], sh_ref[...])
        h2_ref[rows, :] = h2.astype(h2_ref.dtype)
        gate_ref[rows, :] = _route_record(_dot_w(h2, wr_ref[...], P_ROUTER) + br_ref[...])


def _out_proj(ya, yb, yc, yd, x, wo_parts, gt, g, sc, sh, w_router, b_router):
    t = x.shape[0]
    tm = _tile(t, 1024)
    row_blk = lambda n: pl.BlockSpec((tm, n), lambda i: (i, 0))
    vec = _full((1, D_MODEL))
    return pl.pallas_call(
        functools.partial(_outproj_kernel, tm=tm),
        out_shape=[jax.ShapeDtypeStruct((t, D_MODEL), F32), jax.ShapeDtypeStruct((t, D_MODEL), BF16),
                   jax.ShapeDtypeStruct((t, LANES), F32)],
        grid=(t // tm,),
        in_specs=[row_blk(GROUP_W)] * 4 + [row_blk(D_MODEL), _full(wo_parts.shape), vec, vec, vec, vec,
                                           _full(w_router.shape), _full((1, 2 * LANES))],
        out_specs=[row_blk(D_MODEL), row_blk(D_MODEL), row_blk(LANES)],
        compiler_params=_cparams(("parallel",)),
        name="out_proj_router",
    )(ya, yb, yc, yd, x, wo_parts, gt, g, sc, sh, w_router, b_router)


def _moe_kernel(h_ref, route_ref, tril_ref, ex_ref, wg_ref, wu_ref, wd_ref, x1_ref, gt_ref, fg_ref, o_ref,
                acc_ref, col_ref, row_ref, cnt_ref, *, final, tm, cap, cap_extra):
    g = pl.program_id(1)
    lane = lax.broadcasted_iota(jnp.int32, (tm, LANES), 1)

    @pl.when(g == 0)
    def _():
        acc_ref[...] = jnp.zeros_like(acc_ref)
        gi = route_ref[:, ROUTE_GROUP_LANE:ROUTE_GROUP_LANE + 1]
        onehot = lane.astype(F32) == gi
        cum = _mm(tril_ref[...], jnp.where(onehot, 1.0, 0.0).astype(BF16), _NN)
        rank = jnp.sum(jnp.where(onehot, cum, 0.0), axis=-1, keepdims=True) - 1.0
        cnt_ref[...] = cum[tm - 1:tm, :]
        rec = jnp.where(lane == 0, rank, jnp.where(lane == 1, gi, 0.0))
        col_ref[...] = rec
        row_ref[...] = rec.T[0:SUBLANES, :]

    gf = g.astype(F32)
    count = jnp.sum(jnp.where(lane[0:1, :] == g, cnt_ref[...], 0.0))
    count = count.astype(jnp.int32)
    sel_col = jnp.where(col_ref[:, 1:2] == gf, col_ref[:, 0:1], -1.0)
    sel_row = jnp.where(row_ref[1:2, :] == gf, row_ref[0:1, :], -1.0)
    hx = jnp.concatenate([h_ref[...], route_ref[...].astype(BF16)], axis=1)

    def block(base, rows):
        base = base.astype(F32)
        slot_r = lax.broadcasted_iota(jnp.int32, (rows, tm), 0).astype(F32) + base
        gather = jnp.where(slot_r == sel_row, 1.0, 0.0).astype(BF16)
        hs = _mm(gather, hx, _NN)
        h = hs[:, 0:D_MODEL].astype(BF16)
        gates = _mm(hs[:, D_MODEL:D_MODEL + LANES].astype(BF16), ex_ref[...], _NN)
        hg = _mm(h, wg_ref[...], _NN)
        hu = _mm(h, wu_ref[...], _NN)
        act = hg * _sigmoid(hg) * hu * gates
        ys = _mm(act.astype(BF16), wd_ref[...], _NN)
        slot_c = lax.broadcasted_iota(jnp.int32, (tm, rows), 1).astype(F32) + base
        scatter = jnp.where(slot_c == sel_col, 1.0, 0.0).astype(BF16)
        acc_ref[...] += _mm(scatter, ys.astype(BF16), _NN)

    @pl.when(count > 0)
    def _():
        block(jnp.int32(0), cap)

    def extra(r, carry):
        block(cap + r * cap_extra, cap_extra)
        return carry

    lax.fori_loop(0, (jnp.maximum(count - cap, 0) + (cap_extra - 1)) // cap_extra, extra, 0)

    @pl.when(g == pl.num_programs(1) - 1)
    def _():
        out = x1_ref[...] + gt_ref[...] * acc_ref[...]
        if final:
            out = out * lax.rsqrt(jnp.mean(out * out, axis=-1, keepdims=True) + NORM_EPS) * fg_ref[...]
        o_ref[...] = out


def _moe(h2, route, wg, wu, wd, x1, gt, final_g, final):
    t = h2.shape[0]
    tm = _tile(t, MOE_TILE)
    cap = tm // N_GROUPS
    cap_extra = cap // 2
    nw = EXP_PER_GROUP * D_EXPERT
    vec = _full((1, D_MODEL))
    tpos = jnp.arange(tm)
    tril = (tpos[:, None] >= tpos[None, :]).astype(BF16)
    expand = (jnp.arange(LANES)[:, None] == jnp.arange(nw)[None, :] // D_EXPERT).astype(BF16)
    return pl.pallas_call(
        functools.partial(_moe_kernel, final=final, tm=tm, cap=cap, cap_extra=cap_extra),
        out_shape=jax.ShapeDtypeStruct((t, D_MODEL), F32),
        grid=(t // tm, N_GROUPS),
        in_specs=[
            pl.BlockSpec((tm, D_MODEL), lambda i, g: (i, 0)),
            pl.BlockSpec((tm, LANES), lambda i, g: (i, 0)),
            _full((tm, tm)), _full((LANES, nw)),
            pl.BlockSpec((D_MODEL, nw), lambda i, g: (0, g)),
            pl.BlockSpec((D_MODEL, nw), lambda i, g: (0, g)),
            pl.BlockSpec((nw, D_MODEL), lambda i, g: (g, 0)),
            pl.BlockSpec((tm, D_MODEL), lambda i, g: (i, 0)),
            vec, vec,
        ],
        out_specs=pl.BlockSpec((tm, D_MODEL), lambda i, g: (i, 0)),
        scratch_shapes=[pltpu.VMEM((tm, D_MODEL), F32), pltpu.VMEM((tm, LANES), F32),
                        pltpu.VMEM((SUBLANES, tm), F32), pltpu.VMEM((1, LANES), F32)],
        compiler_params=_cparams(("parallel", "arbitrary")),
        name="moe_experts",
    )(h2, route, tril, expand, wg, wu, wd, x1, gt, final_g)


def kernel(x, c, w_ada, b_ada, norm1_g, norm2_g, w_in, w_out, pool_w, pool_scale, attn_sinks, conv_w, rwkv_mu,
           rwkv_w0, rwkv_w2, rwkv_a0, rwkv_a2, rwkv_g2, rwkv_k_k, rwkv_k_a, rwkv_r_k, rwkv_ln_g, rwkv_ln_b,
           rwkv_v0, rwkv_v1, rwkv_v2, moe_w_grp, moe_b_grp, moe_w_exp, moe_b_exp, moe_w_gate, moe_w_up,
           moe_w_down, final_g):
    bsz, t, d = x.shape
    assert bsz == 1 and d == D_MODEL
    depth = w_ada.shape[0]
    xs = x.reshape(t, d)
    mod = _ada_mod(c, w_ada, b_ada)
    v_first = None
    row = lambda a: a.reshape(1, -1)
    for l in range(depth):
        sh1, sc1, gt1, sh2, sc2, gt2 = [mod[l, :, j * d:(j + 1) * d] for j in range(6)]
        wp = jax.scipy.linalg.block_diag(*[pool_w[l, gidx] for gidx in range(len(POOL_WINDOWS))])
        ya, pb, yc, pd = _in_proj(xs, row(norm1_g[l]), sc1, sh1, _split_w(w_in[l], P_IN), _split_w(wp, P_POOL),
                                  row(pool_scale[l]), conv_w[l])
        yb = _attention(pb, row(attn_sinks[l]))
        prm = dict(mu=row(rwkv_mu[l]), w0=row(rwkv_w0[l]), w2=rwkv_w2[l], a0=row(rwkv_a0[l]), a2=rwkv_a2[l],
                   g2=rwkv_g2[l], k_k=row(rwkv_k_k[l]), k_a=row(rwkv_k_a[l]), r_k=row(rwkv_r_k[l]),
                   ln_g=row(rwkv_ln_g[l]), ln_b=row(rwkv_ln_b[l]))
        if l == 0:
            yd, v_first = _rwkv(pd, None, prm)
        else:
            prm.update(v0=row(rwkv_v0[l - 1]), v1=rwkv_v1[l - 1], v2=rwkv_v2[l - 1])
            yd = _rwkv(pd, v_first, prm)
        lane_pad = lambda a: jnp.pad(a, ((0, 0), (0, LANES - a.shape[1])))
        w_router = jnp.concatenate([lane_pad(moe_w_grp[l]), lane_pad(moe_w_exp[l])], axis=1)
        b_router = jnp.concatenate([lane_pad(row(moe_b_grp[l])), lane_pad(row(moe_b_exp[l]))], axis=1)
        x1, h2, gates = _out_proj(ya, yb, yc, yd, xs, _split_w(w_out[l], P_OUT), gt1, row(norm2_g[l]), sc2, sh2,
                                  _split_w(w_router, P_ROUTER), b_router)
        wg =jnp.transpose(moe_w_gate[l].astype(BF16), (1, 0, 2)).reshape(d, N_EXPERTS * D_EXPERT)
        wu = jnp.transpose(moe_w_up[l].astype(BF16), (1, 0, 2)).reshape(d, N_EXPERTS * D_EXPERT)
        wd = moe_w_down[l].astype(BF16).reshape(N_EXPERTS * D_EXPERT, d)
        xs = _moe(h2, gates, wg, wu, wd, x1, gt2, row(final_g), final=(l == depth - 1))
    return xs.reshape(bsz, t, d)
```

```python
import functools

import jax
import jax.numpy as jnp
from jax import lax
from jax.experimental import pallas as pl
from jax.experimental.pallas import tpu as pltpu

F32 = jnp.float32
BF16 = jnp.bfloat16

D_MODEL = 1024
GROUP_W = 256
NORM_EPS = 1e-6
POOL_WINDOWS = (2, 4, 8, 16)
POOL_CG = 64
HEAD_DIM = 64
ATT_HEADS = 4
ATT_KV_HEADS = 2
ATT_BLOCK = 128
CONV_W = 3
RWKV_HEAD = 64
RWKV_HEADS = 4
LORA_W = 32
LORA_A = 32
LORA_G = 64
RWKV_GN_EPS = 64e-5
N_A = GROUP_W
N_B = (ATT_HEADS + 2 * ATT_KV_HEADS) * HEAD_DIM
N_C = 3 * GROUP_W
N_D = 3 * GROUP_W + LORA_W + LORA_A + LORA_G
N_IN = N_A + N_B + N_C + N_D
N_GROUPS = 4
EXP_PER_GROUP = 8
N_EXPERTS = 32
D_EXPERT = 128

LANES = 128
SUBLANES = 8
VMEM_LIMIT = 56 * 1024 * 1024

CHUNK = 64
POOL_HALO = 16
CONV_HALO = 8
OUT_SUB = 4
ROUTE_GROUP_LANE = 8
MOE_TILE = 1024

P_IN = 1
P_OUT = 1
P_POOL = 1
P_ATT = 1
P_MOE = 1
P_DECAY = 3
P_LORA = 1
P_STATE = 1
P_ROUTER = 3


def _split(a):
    hi = a.astype(BF16)
    lo = (a - hi.astype(F32)).astype(BF16)
    return hi, lo


def _mm(a, b, dims):
    return lax.dot_general(a, b, (dims, ((), ())), preferred_element_type=F32)


_NN = ((1,), (0,))
_NT = ((1,), (1,))


def _dot(a, b, passes=1, dims=_NN):
    if passes == 1:
        return _mm(a.astype(BF16), b.astype(BF16), dims)
    a0, a1 = _split(a)
    b0, b1 = _split(b)
    return _mm(a0, b0, dims) + (_mm(a0, b1, dims) + _mm(a1, b0, dims))


def _bmm(a, b, nt):
    dims = (((2,), (2 if nt else 1,)), ((0,), (0,)))
    return lax.dot_general(a, b, dims, preferred_element_type=F32)


def _bdot(a, b, passes=1, nt=False):
    if passes == 1:
        return _bmm(a.astype(BF16), b.astype(BF16), nt)
    a0, a1 = _split(a)
    b0, b1 = _split(b)
    return _bmm(a0, b0, nt) + (_bmm(a0, b1, nt) + _bmm(a1, b0, nt))


def _lhs_w(a, passes):
    if passes == 1:
        return a.astype(BF16)
    a0, a1 = _split(a)
    return jnp.concatenate([a0, a0, a1], axis=1)


def _dot_w(a, w_cat, passes):
    return _mm(_lhs_w(a, passes), w_cat, _NN)


def _split_w(w, passes):
    hi = w.astype(BF16)
    if passes == 1:
        return hi
    lo = (w - hi.astype(F32)).astype(BF16)
    return jnp.concatenate([hi, lo, hi], axis=0)


def _sigmoid(x):
    return 1.0 / (1.0 + jnp.exp(-x))


def _cparams(sem):
    return pltpu.CompilerParams(dimension_semantics=sem, vmem_limit_bytes=VMEM_LIMIT)


def _full(shape):
    return pl.BlockSpec(shape, lambda *_: (0,) * len(shape))


def _tile(n, pref):
    t = min(n, pref)
    assert n % t == 0, (n, t)
    return t


def _mod_kernel(c_ref, w_ref, b_ref, o_ref):
    c = c_ref[...]
    cond = c * _sigmoid(c)
    o_ref[...] = _dot(cond, w_ref[...], 3) + b_ref[...]


def _ada_mod(c, w_ada, b_ada):
    depth = w_ada.shape[0]
    c8 = jnp.broadcast_to(c, (SUBLANES, D_MODEL))
    out = pl.pallas_call(
        _mod_kernel,
        out_shape=jax.ShapeDtypeStruct((depth, SUBLANES, 6 * D_MODEL), F32),
        grid=(depth, 6),
        in_specs=[
            pl.BlockSpec((SUBLANES, D_MODEL), lambda l, j: (0, 0)),
            pl.BlockSpec((None, D_MODEL, D_MODEL), lambda l, j: (l, 0, j)),
            pl.BlockSpec((None, 1, D_MODEL), lambda l, j: (l, 0, j)),
        ],
        out_specs=pl.BlockSpec((None, SUBLANES, D_MODEL), lambda l, j: (l, 0, j)),
        compiler_params=_cparams(("arbitrary", "arbitrary")),
        name="ada_mod",
    )(c8, w_ada, b_ada.reshape(depth, 1, 6 * D_MODEL))
    return out[:, 0:1, :]


def _modulated_norm(x, g, sc, sh):
    y = x * lax.rsqrt(jnp.mean(x * x, axis=-1, keepdims=True) + NORM_EPS) * g
    return y * (1.0 + sc) + sh


def _shift_rows(ext, n):
    return pltpu.roll(ext, n, axis=0)


def _pool_mixer(u, halo, wp, scale, t0):
    tt = u.shape[0]
    ext = jnp.concatenate([halo, u], axis=0)
    s2 = ext + _shift_rows(ext, 1)
    s4 = s2 + _shift_rows(s2, 2)
    s8 = s4 + _shift_rows(s4, 4)
    s16 = s8 + _shift_rows(s8, 8)
    grp = lax.broadcasted_iota(jnp.int32, (tt, GROUP_W), 1) // POOL_CG
    tpos = lax.broadcasted_iota(jnp.int32, (tt, GROUP_W), 0) + (t0 + 1)
    wsum = jnp.where(grp == 0, s2[POOL_HALO:], jnp.where(grp == 1, s4[POOL_HALO:],
                     jnp.where(grp == 2, s8[POOL_HALO:], s16[POOL_HALO:])))
    win = jnp.where(grp == 0, 2, jnp.where(grp == 1, 4, jnp.where(grp == 2, 8, 16)))
    cnt = jnp.minimum(tpos, win).astype(F32)
    return _dot_w(wsum / cnt - u, wp, P_POOL) * scale


def _conv_mixer(bg, z, z_halo, cw):
    zext = jnp.concatenate([z_halo, z], axis=0)
    y = (cw[2:3, :] * z + cw[1:2, :] * _shift_rows(zext, 1)[CONV_HALO:]
         + cw[0:1, :] * _shift_rows(zext, 2)[CONV_HALO:])
    return bg * y


def _inproj_kernel(x_ref, g_ref, sc_ref, sh_ref, w_ref, wp_ref, scale_ref, cw_ref,
                   ya_ref, pb_ref, yc_ref, pd_ref, u_tail, z_tail, *, passes, tm):
    i = pl.program_id(0)

    @pl.when(i == 0)
    def _():
        u_tail[...] = jnp.zeros_like(u_tail)
        z_tail[...] = jnp.zeros_like(z_tail)

    h = _modulated_norm(x_ref[...], g_ref[...], sc_ref[...], sh_ref[...])
    lhs = _lhs_w(h, passes)
    proj = lambda col, width: _mm(lhs, w_ref[:, col:col + width], _NN)
    u = proj(0, N_A)
    pc = proj(N_A + N_B, N_C)
    ya_ref[...] = _pool_mixer(u, u_tail[...], wp_ref[...], scale_ref[...], i * tm)
    u_tail[...] = u[tm - POOL_HALO:, :]
    z = pc[:, GROUP_W:2 * GROUP_W] * pc[:, 2 * GROUP_W:3 * GROUP_W]
    yc_ref[...] = _conv_mixer(pc[:, 0:GROUP_W], z, z_tail[...], cw_ref[...])
    z_tail[...] = z[tm - CONV_HALO:, :]
    pb_ref[...] = proj(N_A, N_B)
    pd_ref[...] = proj(N_A + N_B + N_C, N_D)


def _in_proj(x, g, sc, sh, w_parts, wp_parts, pool_scale, conv_w):
    t = x.shape[0]
    tm = _tile(t, 1024)
    vec = _full((1, D_MODEL))
    widths = (GROUP_W, N_B, GROUP_W, N_D)
    return pl.pallas_call(
        functools.partial(_inproj_kernel, passes=P_IN, tm=tm),
        out_shape=[jax.ShapeDtypeStruct((t, n), F32) for n in widths],
        grid=(t // tm,),
        in_specs=[pl.BlockSpec((tm, D_MODEL), lambda i: (i, 0)), vec, vec, vec, _full(w_parts.shape),
                  _full(wp_parts.shape), _full((1, GROUP_W)), _full((CONV_W, GROUP_W))],
        out_specs=[pl.BlockSpec((tm, n), lambda i: (i, 0)) for n in widths],
        scratch_shapes=[pltpu.VMEM((POOL_HALO, GROUP_W), F32), pltpu.VMEM((CONV_HALO, GROUP_W), F32)],
        compiler_params=_cparams(("arbitrary",)),
        name="in_proj_pool_conv",
    )(x, g, sc, sh, w_parts, wp_parts, pool_scale, conv_w)


def _attn_kernel(cur_ref, prev_ref, sink_ref, o_ref, *, tq):
    i = pl.program_id(0)
    nq = ATT_HEADS * HEAD_DIM
    nkv = ATT_KV_HEADS * HEAD_DIM
    blk = ATT_BLOCK
    hd = HEAD_DIM
    rep = ATT_HEADS // ATT_KV_HEADS
    nblk = tq // blk
    ri = lax.broadcasted_iota(jnp.int32, (rep * blk, 2 * blk), 0)
    ki = lax.broadcasted_iota(jnp.int32, (rep * blk, 2 * blk), 1)
    dist = (ri % blk) + blk - ki
    in_win = (dist >= 0) & (dist < blk)
    distf = dist.astype(F32)
    head_col = lax.broadcasted_iota(jnp.int32, (rep * blk, 1), 0) // blk
    bias, sink = [], []
    for g in range(ATT_KV_HEADS):
        slope_g = jnp.zeros((rep * blk, 1), F32)
        sink_g = jnp.zeros((rep * blk, 1), F32)
        for j in range(rep):
            h = g * rep + j
            slope_g = jnp.where(head_col == j, 2.0 ** (-8.0 * (h + 1) / ATT_HEADS), slope_g)
            sink_g = jnp.where(head_col == j, sink_ref[:, h:h + 1], sink_g)
        bias.append(slope_g * distf)
        sink.append(sink_g)
    bias = jnp.stack(bias)[None]
    sink = jnp.stack(sink)[None]

    q = cur_ref[:, 0:nq] * (hd ** -0.5)
    kv = jnp.concatenate([prev_ref[...], cur_ref[:, nq:nq + 2 * nkv]], axis=0)
    qs, ks, vs = [], [], []
    for b in range(nblk):
        for g in range(ATT_KV_HEADS):
            qs.append(jnp.concatenate(
                [q[b * blk:(b + 1) * blk, (g * rep + j) * hd:(g * rep + j + 1) * hd] for j in range(rep)], axis=0))
            ks.append(kv[b * blk:(b + 2) * blk, g * hd:(g + 1) * hd])
            vs.append(kv[b * blk:(b + 2) * blk, nkv + g * hd:nkv + (g + 1) * hd])
    s = _bdot(jnp.stack(qs), jnp.stack(ks), P_ATT, nt=True)
    s = s.reshape(nblk, ATT_KV_HEADS, rep * blk, 2 * blk)
    s = jnp.where(in_win, s - bias, -jnp.inf)
    first_ok = ki >= blk * (1 - (i > 0).astype(jnp.int32))
    s = jnp.concatenate([jnp.where(first_ok, s[0:1], -jnp.inf), s[1:]], axis=0)
    m = jnp.maximum(jnp.max(s, axis=-1, keepdims=True), sink)
    p = jnp.exp(s - m)
    den = jnp.sum(p, axis=-1, keepdims=True) + jnp.exp(sink - m)
    o = _bdot(p.reshape(nblk * ATT_KV_HEADS, rep * blk, 2 * blk), jnp.stack(vs), P_ATT)
    o = o.reshape(nblk, ATT_KV_HEADS, rep * blk, hd) / den
    for b in range(nblk):
        o_ref[b * blk:(b + 1) * blk, :] = jnp.concatenate(
            [o[b, g, j * blk:(j + 1) * blk, :] for g in range(ATT_KV_HEADS) for j in range(rep)], axis=-1)


def _attention(pb, sinks):
    t = pb.shape[0]
    tq = _tile(t, 1024)
    nb = tq // ATT_BLOCK
    return pl.pallas_call(
        functools.partial(_attn_kernel, tq=tq),
        out_shape=jax.ShapeDtypeStruct((t, GROUP_W), F32),
        grid=(t // tq,),
        in_specs=[
            pl.BlockSpec((tq, N_B), lambda i: (i, 0)),
            pl.BlockSpec((ATT_BLOCK, 2 * ATT_KV_HEADS * HEAD_DIM), lambda i: (jnp.maximum(i * nb - 1, 0), 1)),
            _full((1, ATT_HEADS)),
        ],
        out_specs=pl.BlockSpec((tq, GROUP_W), lambda i: (i, 0)),
        compiler_params=_cparams(("parallel",)),
        name="swa_attention",
    )(pb, pb, sinks)


def _rwkv_kernel(*refs, tt, has_vres):
    if has_vres:
        (pd_ref, prev_ref, vf_ref, tril_ref, mu_ref, w0_ref, w2_ref, a0_ref, a2_ref, g2_ref, kk_ref, ka_ref,
         rk_ref, lng_ref, lnb_ref, v0_ref, v1_ref, v2_ref, y_ref, st_ref, yr_s) = refs
    else:
        (pd_ref, prev_ref, tril_ref, mu_ref, w0_ref, w2_ref, a0_ref, a2_ref, g2_ref, kk_ref, ka_ref,
         rk_ref, lng_ref, lnb_ref, y_ref, vout_ref, st_ref, yr_s) = refs
    i = pl.program_id(0)
    nh, hd, gw = RWKV_HEADS, RWKV_HEAD, GROUP_W

    @pl.when(i == 0)
    def _():
        st_ref[...] = jnp.zeros_like(st_ref)

    hr = lax.broadcasted_iota(jnp.int32, (2 * gw, gw), 0) % gw // hd
    hc = lax.broadcasted_iota(jnp.int32, (2 * gw, gw), 1) // hd
    hsum2 = jnp.where(hr == hc, 1.0, 0.0).astype(BF16)

    def head_sum(t2):
        return _mm(jnp.concatenate(_split(t2), axis=1), hsum2, _NN)

    p = pd_ref[...]
    row = lax.broadcasted_iota(jnp.int32, (tt, 1), 0)
    prev_row = prev_ref[SUBLANES - 1:SUBLANES, :] * (i > 0).astype(F32)
    shifted = jnp.where(row == 0, prev_row, pltpu.roll(p, 1, axis=0))
    z = p + (shifted - p) * mu_ref[...]
    r = z[:, 0:gw]
    k = z[:, gw:2 * gw]
    v = z[:, 2 * gw:3 * gw]
    o = 3 * gw
    wd = z[:, o:o + LORA_W]
    ad = z[:, o + LORA_W:o + LORA_W + LORA_A]
    gd = z[:, o + LORA_W + LORA_A:o + LORA_W + LORA_A + LORA_G]
    wpre = -(w0_ref[...] + _dot(jnp.tanh(wd), w2_ref[...], P_DECAY))
    softplus = jnp.maximum(wpre, 0.0) + jnp.log(1.0 + jnp.exp(-jnp.abs(wpre)))
    lw = -jnp.exp(-softplus - 0.5)
    a = _sigmoid(a0_ref[...] + _dot(ad, a2_ref[...], P_LORA))
    g = _dot(_sigmoid(gd), g2_ref[...], P_LORA)
    if has_vres:
        mix = _sigmoid(v0_ref[...] + _dot(_dot(v, v1_ref[...], P_LORA), v2_ref[...], P_LORA))
        v = v + (vf_ref[...] - v) * mix
    else:
        vout_ref[...] = v
    kk = k * kk_ref[...]
    kk = kk / jnp.maximum(jnp.sqrt(head_sum(kk * kk)), 1e-12)
    k = k * (1.0 + (a - 1.0) * ka_ref[...])
    nch = tt // CHUNK
    L = CHUNK
    npair = nh // 2
    pw = 2 * hd
    ti = lax.broadcasted_iota(jnp.int32, (L, pw), 0)
    tl = lax.broadcasted_iota(jnp.int32, (L, pw), 1) % hd
    low_strict = ti > tl
    low_incl = ti >= tl
    eye = (ti == tl).astype(F32)
    first_head = lax.broadcasted_iota(jnp.int32, (1, pw), 1) < hd

    def to_batch(t2, c0, nc):
        return jnp.concatenate([t2[c0 * L:(c0 + nc) * L, q * pw:(q + 1) * pw].reshape(nc, L, pw)
                                for q in range(npair)], axis=0)

    def bd(x):
        zero = jnp.zeros_like(x)
        return jnp.concatenate([jnp.where(first_head, x, zero), jnp.where(first_head, zero, x)], axis=-2)

    def fold(x):
        return x[:, :hd] + x[:, hd:]

    def c16(x):
        return x.astype(BF16)

    lw_hi, lw_lo = _split(lw)
    cl = _mm(tril_ref[...], lw_hi, _NN) + _mm(tril_ref[...], lw_lo, _NN)
    cl3 = cl.reshape(nch, L, gw)
    cl_end = cl3[:, L - 1:L, :]
    e_end = jnp.exp(cl_end - cl3).reshape(tt, gw)
    e_neg = jnp.exp(-cl)
    bb = kk * a
    at_t = -kk * jnp.exp(cl - lw)
    bh_t = bb * e_neg
    kh_t = k * e_neg
    rt_t = r * jnp.exp(cl)
    be_t = bb * e_end
    ke_t = k * e_end
    p_end = jnp.exp(cl_end)

    def chunk_operators(c0, nc):
        at = c16(to_batch(at_t, c0, nc))
        bh = c16(to_batch(bh_t, c0, nc))
        kh = c16(to_batch(kh_t, c0, nc))
        rt = to_batch(rt_t, c0, nc)
        be = to_batch(be_t, c0, nc)
        ke = to_batch(ke_t, c0, nc)
        vb = bd(c16(to_batch(v, c0, nc)))
        pe = jnp.concatenate([p_end[c0:c0 + nc, :, q * pw:(q + 1) * pw] for q in range(npair)], axis=0)
        lhs = jnp.concatenate([at, c16(rt)], axis=1)
        g_b = _bmm(lhs, bd(bh), True)
        g_k = _bmm(lhs, bd(kh), True)
        a_ab = jnp.where(low_strict, g_b[:, :L], 0.0)
        b_rb = jnp.where(low_incl, g_b[:, L:], 0.0)
        a_ak = jnp.where(low_strict, g_k[:, :L], 0.0)
        b_rk = jnp.where(low_incl, g_k[:, L:], 0.0)
        a16 = c16(a_ab)
        apow = _bmm(a16, bd(a16), False)
        tinv = eye + a_ab
        for _ in range(4):
            p16 = c16(apow)
            sq = _bmm(jnp.concatenate([p16, c16(tinv)], axis=1), bd(p16), False)
            tinv = tinv + sq[:, L:]
            apow = sq[:, :L]
        tinv = tinv + _bmm(c16(tinv), bd(c16(apow)), False)
        av = _bmm(c16(a_ak), vb, False)
        wu = _bmm(c16(tinv), jnp.concatenate([bd(at), bd(c16(av))], axis=2), False)
        rhs = jnp.concatenate([bd(c16(wu[:, :, :pw])), bd(c16(wu[:, :, pw:]))], axis=2)
        lhs_b = jnp.concatenate([c16(b_rb), c16(jnp.swapaxes(bd(be), 1, 2))], axis=1)
        lhs_k = jnp.concatenate([c16(b_rk), c16(jnp.swapaxes(bd(ke), 1, 2))], axis=1)
        o_wu = _bmm(lhs_b, rhs, False)
        o_v = _bmm(lhs_k, vb, False)
        rq = rt + o_wu[:, :L, :pw]
        y0 = o_wu[:, :L, pw:] + o_v[:, :L]
        m2 = eye * pe + fold(o_wu[:, L:, :pw])
        c2 = fold(o_wu[:, L:, pw:] + o_v[:, L:])
        return jnp.concatenate([rq, m2], axis=1), jnp.concatenate([y0, c2], axis=1)

    def state_walk(zs, ops, c0, nc):
        rqm, y0c = ops
        for c in range(nc):
            for q in range(npair):
                b = q * nc + c
                yz = _dot(rqm[b], bd(zs[q]), P_STATE) + y0c[b]
                yr_s[(c0 + c) * L:(c0 + c + 1) * L, q * pw:(q + 1) * pw] = yz[:L]
                zs[q] = yz[L:]
        return zs

    zs = state_walk([st_ref[q] for q in range(npair)], chunk_operators(0, nch), 0, nch)
    for q in range(npair):
        st_ref[q] = zs[q]

    y = yr_s[...]
    inv_n = 1.0 / hd
    mean = head_sum(y) * inv_n
    yc = y - mean
    var = head_sum(yc * yc) * inv_n
    yn = yc * lax.rsqrt(var + RWKV_GN_EPS) * lng_ref[...] + lnb_ref[...]
    bonus = head_sum(r * k * rk_ref[...]) * v
    y_ref[...] = (yn + bonus) * g


def _rwkv(pd, v_first, prm):
    t = pd.shape[0]
    tt = _tile(t, 512)
    nch = tt // CHUNK
    has_vres = v_first is not None
    gw = GROUP_W
    row_blk = lambda n: pl.BlockSpec((tt, n), lambda i: (i, 0))
    vec = _full((1, gw))
    in_specs = [row_blk(N_D),
                pl.BlockSpec((SUBLANES, N_D), lambda i: (jnp.maximum(i * (tt // SUBLANES) - 1, 0), 0))]
    args = [pd, pd]
    if has_vres:
        in_specs.append(row_blk(gw))
        args.append(v_first)
    tpos = jnp.arange(tt)
    tril = ((tpos[:, None] // CHUNK == tpos[None, :] // CHUNK) & (tpos[:, None] >= tpos[None, :])).astype(BF16)
    in_specs.append(_full((tt, tt)))
    args.append(tril)
    in_specs += [_full((1, N_D)), vec, _full((LORA_W, gw)), vec, _full((LORA_A, gw)), _full((LORA_G, gw)),
                 vec, vec, vec, vec, vec]
    args += [prm["mu"], prm["w0"], prm["w2"], prm["a0"], prm["a2"], prm["g2"], prm["k_k"], prm["k_a"],
             prm["r_k"], prm["ln_g"], prm["ln_b"]]
    if has_vres:
        in_specs += [vec, _full((gw, 32)), _full((32, gw))]
        args += [prm["v0"], prm["v1"], prm["v2"]]
        out_shape = jax.ShapeDtypeStruct((t, gw), F32)
        out_specs = row_blk(gw)
    else:
        out_shape = [jax.ShapeDtypeStruct((t, gw), F32)] * 2
        out_specs = [row_blk(gw)] * 2
    npair, pw = RWKV_HEADS // 2, 2 * RWKV_HEAD
    scratch = [pltpu.VMEM((npair, RWKV_HEAD, pw), F32), pltpu.VMEM((tt, gw), F32)]
    return pl.pallas_call(
        functools.partial(_rwkv_kernel, tt=tt, has_vres=has_vres),
        out_shape=out_shape,
        grid=(t // tt,),
        in_specs=in_specs,
        out_specs=out_specs,
        scratch_shapes=scratch,
        compiler_params=_cparams(("arbitrary",)),
        name="rwkv7",
    )(*args)


def _route_record(lg):
    lane = lax.broadcasted_iota(jnp.int32, (lg.shape[0], LANES), 1)
    lanef = lane.astype(F32)
    gl = jnp.where(lane < N_GROUPS, lg[:, 0:LANES], -jnp.inf)
    gmax = jnp.max(gl, axis=-1, keepdims=True)
    gp = 1.0 / jnp.sum(jnp.exp(gl - gmax), axis=-1, keepdims=True)
    gi = jnp.min(jnp.where(gl == gmax, lanef, float(LANES)), axis=-1, keepdims=True)
    lo = gi * EXP_PER_GROUP
    in_grp = (lanef >= lo) & (lanef < lo + EXP_PER_GROUP)
    el = jnp.where(in_grp, lg[:, LANES:2 * LANES], -jnp.inf)
    m1 = jnp.max(el, axis=-1, keepdims=True)
    i1 = jnp.min(jnp.where(el == m1, lanef, float(LANES)), axis=-1, keepdims=True)
    el2 = jnp.where(lanef == i1, -jnp.inf, el)
    m2 = jnp.max(el2, axis=-1, keepdims=True)
    i2 = jnp.min(jnp.where(el2 == m2, lanef, float(LANES)), axis=-1, keepdims=True)
    e2 = jnp.exp(m2 - m1)
    w_top = gp / (1.0 + e2)
    return jnp.where(lanef == i1 - lo, w_top, jnp.where(lanef == i2 - lo, w_top * e2,
                     jnp.where(lane == ROUTE_GROUP_LANE, gi, 0.0)))


def _outproj_kernel(ya_ref, yb_ref, yc_ref, yd_ref, x_ref, wo_ref, gt_ref, g_ref, sc_ref, sh_ref, wr_ref, br_ref,
                    x1_ref, h2_ref, gate_ref, *, tm):
    sub = tm // OUT_SUB
    for blk in range(OUT_SUB):
        rows = slice(blk * sub, (blk + 1) * sub)
        ycat = jnp.concatenate([ya_ref[rows, :], yb_ref[rows, :], yc_ref[rows, :], yd_ref[rows, :]], axis=1)
        x1 = x_ref[rows, :] + gt_ref[...] * _dot_w(ycat, wo_ref[...], P_OUT)
        x1_ref[rows, :] = x1
        h2 = _modulated_norm(x1, g_ref[...], sc_ref[...], sh_ref[...])
        h2_ref[rows, :] = h2.astype(h2_ref.dtype)
        gate_ref[rows, :] = _route_record(_dot_w(h2, wr_ref[...], P_ROUTER) + br_ref[...])


def _out_proj(ya, yb, yc, yd, x, wo_parts, gt, g, sc, sh, w_router, b_router):
    t = x.shape[0]
    tm = _tile(t, 1024)
    row_blk = lambda n: pl.BlockSpec((tm, n), lambda i: (i, 0))
    vec = _full((1, D_MODEL))
    return pl.pallas_call(
        functools.partial(_outproj_kernel, tm=tm),
        out_shape=[jax.ShapeDtypeStruct((t, D_MODEL), F32), jax.ShapeDtypeStruct((t, D_MODEL), BF16),
                   jax.ShapeDtypeStruct((t, LANES), F32)],
        grid=(t // tm,),
        in_specs=[row_blk(GROUP_W)] * 4 + [row_blk(D_MODEL), _full(wo_parts.shape), vec, vec, vec, vec,
                                           _full(w_router.shape), _full((1, 2 * LANES))],
        out_specs=[row_blk(D_MODEL), row_blk(D_MODEL), row_blk(LANES)],
        compiler_params=_cparams(("parallel",)),
        name="out_proj_router",
    )(ya, yb, yc, yd, x, wo_parts, gt, g, sc, sh, w_router, b_router)


def _moe_kernel(h_ref, route_ref, tril_ref, ex_ref, wg_ref, wu_ref, wd_ref, x1_ref, gt_ref, fg_ref, o_ref,
                acc_ref, col_ref, row_ref, cnt_ref, *, final, tm, cap, cap_extra):
    g = pl.program_id(1)
    lane = lax.broadcasted_iota(jnp.int32, (tm, LANES), 1)

    @pl.when(g == 0)
    def _():
        acc_ref[...] = jnp.zeros_like(acc_ref)
        gi = route_ref[:, ROUTE_GROUP_LANE:ROUTE_GROUP_LANE + 1]
        onehot = lane.astype(F32) == gi
        cum = _mm(tril_ref[...], jnp.where(onehot, 1.0, 0.0).astype(BF16), _NN)
        rank = jnp.sum(jnp.where(onehot, cum, 0.0), axis=-1, keepdims=True) - 1.0
        cnt_ref[...] = cum[tm - 1:tm, :]
        rec = jnp.where(lane == 0, rank, jnp.where(lane == 1, gi, 0.0))
        col_ref[...] = rec
        row_ref[...] = rec.T[0:SUBLANES, :]

    gf = g.astype(F32)
    count = jnp.sum(jnp.where(lane[0:1, :] == g, cnt_ref[...], 0.0))
    count = count.astype(jnp.int32)
    sel_col = jnp.where(col_ref[:, 1:2] == gf, col_ref[:, 0:1], -1.0)
    sel_row = jnp.where(row_ref[1:2, :] == gf, row_ref[0:1, :], -1.0)
    hx = jnp.concatenate([h_ref[...], route_ref[...].astype(BF16)], axis=1)

    def block(base, rows):
        base = base.astype(F32)
        slot_r = lax.broadcasted_iota(jnp.int32, (rows, tm), 0).astype(F32) + base
        gather = jnp.where(slot_r == sel_row, 1.0, 0.0).astype(BF16)
        hs = _mm(gather, hx, _NN)
        h = hs[:, 0:D_MODEL].astype(BF16)
        gates = _mm(hs[:, D_MODEL:D_MODEL + LANES].astype(BF16), ex_ref[...], _NN)
        hg = _mm(h, wg_ref[...], _NN)
        hu = _mm(h, wu_ref[...], _NN)
        act = hg * _sigmoid(hg) * hu * gates
        ys = _mm(act.astype(BF16), wd_ref[...], _NN)
        slot_c = lax.broadcasted_iota(jnp.int32, (tm, rows), 1).astype(F32) + base
        scatter = jnp.where(slot_c == sel_col, 1.0, 0.0).astype(BF16)
        acc_ref[...] += _mm(scatter, ys.astype(BF16), _NN)

    @pl.when(count > 0)
    def _():
        block(jnp.int32(0), cap)

    def extra(r, carry):
        block(cap + r * cap_extra, cap_extra)
        return carry

    lax.fori_loop(0, (jnp.maximum(count - cap, 0) + (cap_extra - 1)) // cap_extra, extra, 0)

    @pl.when(g == pl.num_programs(1) - 1)
    def _():
        out = x1_ref[...] + gt_ref[...] * acc_ref[...]
        if final:
            out = out * lax.rsqrt(jnp.mean(out * out, axis=-1, keepdims=True) + NORM_EPS) * fg_ref[...]
        o_ref[...] = out


def _moe(h2, route, wg, wu, wd, x1, gt, final_g, final):
    t = h2.shape[0]
    tm = _tile(t, MOE_TILE)
    cap = tm // N_GROUPS
    cap_extra = cap // 2
    nw = EXP_PER_GROUP * D_EXPERT
    vec = _full((1, D_MODEL))
    tpos = jnp.arange(tm)
    tril = (tpos[:, None] >= tpos[None, :]).astype(BF16)
    expand = (jnp.arange(LANES)[:, None] == jnp.arange(nw)[None, :] // D_EXPERT).astype(BF16)
    return pl.pallas_call(
        functools.partial(_moe_kernel, final=final, tm=tm, cap=cap, cap_extra=cap_extra),
        out_shape=jax.ShapeDtypeStruct((t, D_MODEL), F32),
        grid=(t // tm, N_GROUPS),
        in_specs=[
            pl.BlockSpec((tm, D_MODEL), lambda i, g: (i, 0)),
            pl.BlockSpec((tm, LANES), lambda i, g: (i, 0)),
            _full((tm, tm)), _full((LANES, nw)),
            pl.BlockSpec((D_MODEL, nw), lambda i, g: (0, g)),
            pl.BlockSpec((D_MODEL, nw), lambda i, g: (0, g)),
            pl.BlockSpec((nw, D_MODEL), lambda i, g: (g, 0)),
            pl.BlockSpec((tm, D_MODEL), lambda i, g: (i, 0)),
            vec, vec,
        ],
        out_specs=pl.BlockSpec((tm, D_MODEL), lambda i, g: (i, 0)),
        scratch_shapes=[pltpu.VMEM((tm, D_MODEL), F32), pltpu.VMEM((tm, LANES), F32),
                        pltpu.VMEM((SUBLANES, tm), F32), pltpu.VMEM((1, LANES), F32)],
        compiler_params=_cparams(("parallel", "arbitrary")),
        name="moe_experts",
    )(h2, route, tril, expand, wg, wu, wd, x1, gt, final_g)


def kernel(x, c, w_ada, b_ada, norm1_g, norm2_g, w_in, w_out, pool_w, pool_scale, attn_sinks, conv_w, rwkv_mu,
           rwkv_w0, rwkv_w2, rwkv_a0, rwkv_a2, rwkv_g2, rwkv_k_k, rwkv_k_a, rwkv_r_k, rwkv_ln_g, rwkv_ln_b,
           rwkv_v0, rwkv_v1, rwkv_v2, moe_w_grp, moe_b_grp, moe_w_exp, moe_b_exp, moe_w_gate, moe_w_up,
           moe_w_down, final_g):
    bsz, t, d = x.shape
    assert bsz == 1 and d == D_MODEL
    depth = w_ada.shape[0]
    xs = x.reshape(t, d)
    mod = _ada_mod(c, w_ada, b_ada)
    v_first = None
    row = lambda a: a.reshape(1, -1)
    for l in range(depth):
        sh1, sc1, gt1, sh2, sc2, gt2 = [mod[l, :, j * d:(j + 1) * d] for j in range(6)]
        wp = jax.scipy.linalg.block_diag(*[pool_w[l, gidx] for gidx in range(len(POOL_WINDOWS))])
        ya, pb, yc, pd = _in_proj(xs, row(norm1_g[l]), sc1, sh1, _split_w(w_in[l], P_IN), _split_w(wp, P_POOL),
                                  row(pool_scale[l]), conv_w[l])
        yb = _attention(pb, row(attn_sinks[l]))
        prm = dict(mu=row(rwkv_mu[l]), w0=row(rwkv_w0[l]), w2=rwkv_w2[l], a0=row(rwkv_a0[l]), a2=rwkv_a2[l],
                   g2=rwkv_g2[l], k_k=row(rwkv_k_k[l]), k_a=row(rwkv_k_a[l]), r_k=row(rwkv_r_k[l]),
                   ln_g=row(rwkv_ln_g[l]), ln_b=row(rwkv_ln_b[l]))
        if l == 0:
            yd, v_first = _rwkv(pd, None, prm)
        else:
            prm.update(v0=row(rwkv_v0[l - 1]), v1=rwkv_v1[l - 1], v2=rwkv_v2[l - 1])
            yd = _rwkv(pd, v_first, prm)
        lane_pad = lambda a: jnp.pad(a, ((0, 0), (0, LANES - a.shape[1])))
        w_router = jnp.concatenate([lane_pad(moe_w_grp[l]), lane_pad(moe_w_exp[l])], axis=1)
        b_router = jnp.concatenate([lane_pad(row(moe_b_grp[l])), lane_pad(row(moe_b_exp[l]))], axis=1)
        x1, h2, gates = _out_proj(ya, yb, yc, yd, xs, _split_w(w_out[l], P_OUT), gt1, row(norm2_g[l]), sc2, sh2,
                                  _split_w(w_router, P_ROUTER), b_router)
        wg = jnp.transpose(moe_w_gate[l], (1, 0, 2)).reshape(d, N_EXPERTS * D_EXPERT).astype(BF16)
        wu = jnp.transpose(moe_w_up[l], (1, 0, 2)).reshape(d, N_EXPERTS * D_EXPERT).astype(BF16)
        wd = moe_w_down[l].astype(BF16).reshape(N_EXPERTS * D_EXPERT, d)
        xs = _moe(h2, gates, wg, wu, wd, x1, gt2, row(final_g), final=(l == depth - 1))
    return xs.reshape(bsz, t, d)
```

```python
import functools

import jax
import jax.numpy as jnp
from jax import lax
from jax.experimental import pallas as pl
from jax.experimental.pallas import tpu as pltpu

F32 = jnp.float32
BF16 = jnp.bfloat16

D_MODEL = 1024
GROUP_W = 256
NORM_EPS = 1e-6
POOL_WINDOWS = (2, 4, 8, 16)
POOL_CG = 64
HEAD_DIM = 64
ATT_HEADS = 4
ATT_KV_HEADS = 2
ATT_BLOCK = 128
CONV_W = 3
RWKV_HEAD = 64
RWKV_HEADS = 4
LORA_W = 32
LORA_A = 32
LORA_G = 64
RWKV_GN_EPS = 64e-5
N_A = GROUP_W
N_B = (ATT_HEADS + 2 * ATT_KV_HEADS) * HEAD_DIM
N_C = 3 * GROUP_W
N_D = 3 * GROUP_W + LORA_W + LORA_A + LORA_G
N_IN = N_A + N_B + N_C + N_D
N_GROUPS = 4
EXP_PER_GROUP = 8
N_EXPERTS = 32
D_EXPERT = 128

LANES = 128
SUBLANES = 8
VMEM_LIMIT = 56 * 1024 * 1024

CHUNK = 64
POOL_HALO = 16
CONV_HALO = 8
OUT_SUB = 4
ROUTE_GROUP_LANE = 8
MOE_TILE = 512

P_IN = 1
P_OUT = 1
P_POOL = 1
P_ATT = 1
P_MOE = 1
P_DECAY = 3
P_LORA = 1
P_STATE = 1
P_ROUTER = 3


def _split(a):
    hi = a.astype(BF16)
    lo = (a - hi.astype(F32)).astype(BF16)
    return hi, lo


def _mm(a, b, dims):
    return lax.dot_general(a, b, (dims, ((), ())), preferred_element_type=F32)


_NN = ((1,), (0,))
_NT = ((1,), (1,))


def _dot(a, b, passes=1, dims=_NN):
    if passes == 1:
        return _mm(a.astype(BF16), b.astype(BF16), dims)
    a0, a1 = _split(a)
    b0, b1 = _split(b)
    return _mm(a0, b0, dims) + (_mm(a0, b1, dims) + _mm(a1, b0, dims))


def _bmm(a, b, nt):
    dims = (((2,), (2 if nt else 1,)), ((0,), (0,)))
    return lax.dot_general(a, b, dims, preferred_element_type=F32)


def _bdot(a, b, passes=1, nt=False):
    if passes == 1:
        return _bmm(a.astype(BF16), b.astype(BF16), nt)
    a0, a1 = _split(a)
    b0, b1 = _split(b)
    return _bmm(a0, b0, nt) + (_bmm(a0, b1, nt) + _bmm(a1, b0, nt))


def _lhs_w(a, passes):
    if passes == 1:
        return a.astype(BF16)
    a0, a1 = _split(a)
    return jnp.concatenate([a0, a0, a1], axis=1)


def _dot_w(a, w_cat, passes):
    return _mm(_lhs_w(a, passes), w_cat, _NN)


def _split_w(w, passes):
    hi = w.astype(BF16)
    if passes == 1:
        return hi
    lo = (w - hi.astype(F32)).astype(BF16)
    return jnp.concatenate([hi, lo, hi], axis=0)


def _sigmoid(x):
    return 1.0 / (1.0 + jnp.exp(-x))


def _cparams(sem):
    return pltpu.CompilerParams(dimension_semantics=sem, vmem_limit_bytes=VMEM_LIMIT)


def _full(shape):
    return pl.BlockSpec(shape, lambda *_: (0,) * len(shape))


def _tile(n, pref):
    t = min(n, pref)
    assert n % t == 0, (n, t)
    return t


def _mod_kernel(c_ref, w_ref, b_ref, o_ref):
    c = c_ref[...]
    cond = c * _sigmoid(c)
    o_ref[...] = _dot(cond, w_ref[...], 3) + b_ref[...]


def _ada_mod(c, w_ada, b_ada):
    depth = w_ada.shape[0]
    c8 = jnp.broadcast_to(c, (SUBLANES, D_MODEL))
    out = pl.pallas_call(
        _mod_kernel,
        out_shape=jax.ShapeDtypeStruct((depth, SUBLANES, 6 * D_MODEL), F32),
        grid=(depth, 6),
        in_specs=[
            pl.BlockSpec((SUBLANES, D_MODEL), lambda l, j: (0, 0)),
            pl.BlockSpec((None, D_MODEL, D_MODEL), lambda l, j: (l, 0, j)),
            pl.BlockSpec((None, 1, D_MODEL), lambda l, j: (l, 0, j)),
        ],
        out_specs=pl.BlockSpec((None, SUBLANES, D_MODEL), lambda l, j: (l, 0, j)),
        compiler_params=_cparams(("arbitrary", "arbitrary")),
        name="ada_mod",
    )(c8, w_ada, b_ada.reshape(depth, 1, 6 * D_MODEL))
    return out[:, 0:1, :]


def _modulated_norm(x, g, sc, sh):
    y = x * lax.rsqrt(jnp.mean(x * x, axis=-1, keepdims=True) + NORM_EPS) * g
    return y * (1.0 + sc) + sh


def _shift_rows(ext, n):
    return pltpu.roll(ext, n, axis=0)


def _pool_mixer(u, halo, wp, scale, t0):
    tt = u.shape[0]
    ext = jnp.concatenate([halo, u], axis=0)
    s2 = ext + _shift_rows(ext, 1)
    s4 = s2 + _shift_rows(s2, 2)
    s8 = s4 + _shift_rows(s4, 4)
    s16 = s8 + _shift_rows(s8, 8)
    grp = lax.broadcasted_iota(jnp.int32, (tt, GROUP_W), 1) // POOL_CG
    tpos = lax.broadcasted_iota(jnp.int32, (tt, GROUP_W), 0) + (t0 + 1)
    wsum = jnp.where(grp == 0, s2[POOL_HALO:], jnp.where(grp == 1, s4[POOL_HALO:],
                     jnp.where(grp == 2, s8[POOL_HALO:], s16[POOL_HALO:])))
    win = jnp.where(grp == 0, 2, jnp.where(grp == 1, 4, jnp.where(grp == 2, 8, 16)))
    cnt = jnp.minimum(tpos, win).astype(F32)
    return _dot_w(wsum / cnt - u, wp, P_POOL) * scale


def _conv_mixer(bg, z, z_halo, cw):
    zext = jnp.concatenate([z_halo, z], axis=0)
    y = (cw[2:3, :] * z + cw[1:2, :] * _shift_rows(zext, 1)[CONV_HALO:]
         + cw[0:1, :] * _shift_rows(zext, 2)[CONV_HALO:])
    return bg * y


def _inproj_kernel(x_ref, g_ref, sc_ref, sh_ref, w_ref, wp_ref, scale_ref, cw_ref,
                   ya_ref, pb_ref, yc_ref, pd_ref, u_tail, z_tail, *, passes, tm):
    i = pl.program_id(0)

    @pl.when(i == 0)
    def _():
        u_tail[...] = jnp.zeros_like(u_tail)
        z_tail[...] = jnp.zeros_like(z_tail)

    h = _modulated_norm(x_ref[...], g_ref[...], sc_ref[...], sh_ref[...])
    lhs = _lhs_w(h, passes)
    proj = lambda col, width: _mm(lhs, w_ref[:, col:col + width], _NN)
    u = proj(0, N_A)
    pc = proj(N_A + N_B, N_C)
    ya_ref[...] = _pool_mixer(u, u_tail[...], wp_ref[...], scale_ref[...], i * tm)
    u_tail[...] = u[tm - POOL_HALO:, :]
    z = pc[:, GROUP_W:2 * GROUP_W] * pc[:, 2 * GROUP_W:3 * GROUP_W]
    yc_ref[...] = _conv_mixer(pc[:, 0:GROUP_W], z, z_tail[...], cw_ref[...])
    z_tail[...] = z[tm - CONV_HALO:, :]
    pb_ref[...] = proj(N_A, N_B)
    pd_ref[...] = proj(N_A + N_B + N_C, N_D)


def _in_proj(x, g, sc, sh, w_parts, wp_parts, pool_scale, conv_w):
    t = x.shape[0]
    tm = _tile(t, 1024)
    vec = _full((1, D_MODEL))
    widths = (GROUP_W, N_B, GROUP_W, N_D)
    return pl.pallas_call(
        functools.partial(_inproj_kernel, passes=P_IN, tm=tm),
        out_shape=[jax.ShapeDtypeStruct((t, n), F32) for n in widths],
        grid=(t // tm,),
        in_specs=[pl.BlockSpec((tm, D_MODEL), lambda i: (i, 0)), vec, vec, vec, _full(w_parts.shape),
                  _full(wp_parts.shape), _full((1, GROUP_W)), _full((CONV_W, GROUP_W))],
        out_specs=[pl.BlockSpec((tm, n), lambda i: (i, 0)) for n in widths],
        scratch_shapes=[pltpu.VMEM((POOL_HALO, GROUP_W), F32), pltpu.VMEM((CONV_HALO, GROUP_W), F32)],
        compiler_params=_cparams(("arbitrary",)),
        name="in_proj_pool_conv",
    )(x, g, sc, sh, w_parts, wp_parts, pool_scale, conv_w)


def _attn_kernel(cur_ref, prev_ref, sink_ref, o_ref, *, tq):
    i = pl.program_id(0)
    nq = ATT_HEADS * HEAD_DIM
    nkv = ATT_KV_HEADS * HEAD_DIM
    blk = ATT_BLOCK
    hd = HEAD_DIM
    rep = ATT_HEADS // ATT_KV_HEADS
    nblk = tq // blk
    ri = lax.broadcasted_iota(jnp.int32, (rep * blk, 2 * blk), 0)
    ki = lax.broadcasted_iota(jnp.int32, (rep * blk, 2 * blk), 1)
    dist = (ri % blk) + blk - ki
    in_win = (dist >= 0) & (dist < blk)
    distf = dist.astype(F32)
    head_col = lax.broadcasted_iota(jnp.int32, (rep * blk, 1), 0) // blk
    bias, sink = [], []
    for g in range(ATT_KV_HEADS):
        slope_g = jnp.zeros((rep * blk, 1), F32)
        sink_g = jnp.zeros((rep * blk, 1), F32)
        for j in range(rep):
            h = g * rep + j
            slope_g = jnp.where(head_col == j, 2.0 ** (-8.0 * (h + 1) / ATT_HEADS), slope_g)
            sink_g = jnp.where(head_col == j, sink_ref[:, h:h + 1], sink_g)
        bias.append(slope_g * distf)
        sink.append(sink_g)
    bias = jnp.stack(bias)[None]
    sink = jnp.stack(sink)[None]

    q = cur_ref[:, 0:nq] * (hd ** -0.5)
    kv = jnp.concatenate([prev_ref[...], cur_ref[:, nq:nq + 2 * nkv]], axis=0)
    qs, ks, vs = [], [], []
    for b in range(nblk):
        for g in range(ATT_KV_HEADS):
            qs.append(jnp.concatenate(
                [q[b * blk:(b + 1) * blk, (g * rep + j) * hd:(g * rep + j + 1) * hd] for j in range(rep)], axis=0))
            ks.append(kv[b * blk:(b + 2) * blk, g * hd:(g + 1) * hd])
            vs.append(kv[b * blk:(b + 2) * blk, nkv + g * hd:nkv + (g + 1) * hd])
    s = _bdot(jnp.stack(qs), jnp.stack(ks), P_ATT, nt=True)
    s = s.reshape(nblk, ATT_KV_HEADS, rep * blk, 2 * blk)
    s = jnp.where(in_win, s - bias, -jnp.inf)
    first_ok = ki >= blk * (1 - (i > 0).astype(jnp.int32))
    s = jnp.concatenate([jnp.where(first_ok, s[0:1], -jnp.inf), s[1:]], axis=0)
    m = jnp.maximum(jnp.max(s, axis=-1, keepdims=True), sink)
    p = jnp.exp(s - m)
    den = jnp.sum(p, axis=-1, keepdims=True) + jnp.exp(sink - m)
    o = _bdot(p.reshape(nblk * ATT_KV_HEADS, rep * blk, 2 * blk), jnp.stack(vs), P_ATT)
    o = o.reshape(nblk, ATT_KV_HEADS, rep * blk, hd) / den
    for b in range(nblk):
        o_ref[b * blk:(b + 1) * blk, :] = jnp.concatenate(
            [o[b, g, j * blk:(j + 1) * blk, :] for g in range(ATT_KV_HEADS) for j in range(rep)], axis=-1)


def _attention(pb, sinks):
    t = pb.shape[0]
    tq = _tile(t, 1024)
    nb = tq // ATT_BLOCK
    return pl.pallas_call(
        functools.partial(_attn_kernel, tq=tq),
        out_shape=jax.ShapeDtypeStruct((t, GROUP_W), F32),
        grid=(t // tq,),
        in_specs=[
            pl.BlockSpec((tq, N_B), lambda i: (i, 0)),
            pl.BlockSpec((ATT_BLOCK, 2 * ATT_KV_HEADS * HEAD_DIM), lambda i: (jnp.maximum(i * nb - 1, 0), 1)),
            _full((1, ATT_HEADS)),
        ],
        out_specs=pl.BlockSpec((tq, GROUP_W), lambda i: (i, 0)),
        compiler_params=_cparams(("parallel",)),
        name="swa_attention",
    )(pb, pb, sinks)


def _rwkv_kernel(*refs, tt, has_vres):
    if has_vres:
        (pd_ref, prev_ref, vf_ref, tril_ref, mu_ref, w0_ref, w2_ref, a0_ref, a2_ref, g2_ref, kk_ref, ka_ref,
         rk_ref, lng_ref, lnb_ref, v0_ref, v1_ref, v2_ref, y_ref, st_ref, yr_s) = refs
    else:
        (pd_ref, prev_ref, tril_ref, mu_ref, w0_ref, w2_ref, a0_ref, a2_ref, g2_ref, kk_ref, ka_ref,
         rk_ref, lng_ref, lnb_ref, y_ref, vout_ref, st_ref, yr_s) = refs
    i = pl.program_id(0)
    nh, hd, gw = RWKV_HEADS, RWKV_HEAD, GROUP_W

    @pl.when(i == 0)
    def _():
        st_ref[...] = jnp.zeros_like(st_ref)

    hr = lax.broadcasted_iota(jnp.int32, (2 * gw, gw), 0) % gw // hd
    hc = lax.broadcasted_iota(jnp.int32, (2 * gw, gw), 1) // hd
    hsum2 = jnp.where(hr == hc, 1.0, 0.0).astype(BF16)

    def head_sum(t2):
        return _mm(jnp.concatenate(_split(t2), axis=1), hsum2, _NN)

    p = pd_ref[...]
    row = lax.broadcasted_iota(jnp.int32, (tt, 1), 0)
    prev_row = prev_ref[SUBLANES - 1:SUBLANES, :] * (i > 0).astype(F32)
    shifted = jnp.where(row == 0, prev_row, pltpu.roll(p, 1, axis=0))
    z = p + (shifted - p) * mu_ref[...]
    r = z[:, 0:gw]
    k = z[:, gw:2 * gw]
    v = z[:, 2 * gw:3 * gw]
    o = 3 * gw
    wd = z[:, o:o + LORA_W]
    ad = z[:, o + LORA_W:o + LORA_W + LORA_A]
    gd = z[:, o + LORA_W + LORA_A:o + LORA_W + LORA_A + LORA_G]
    wpre = -(w0_ref[...] + _dot(jnp.tanh(wd), w2_ref[...], P_DECAY))
    softplus = jnp.maximum(wpre, 0.0) + jnp.log(1.0 + jnp.exp(-jnp.abs(wpre)))
    lw = -jnp.exp(-softplus - 0.5)
    a = _sigmoid(a0_ref[...] + _dot(ad, a2_ref[...], P_LORA))
    g = _dot(_sigmoid(gd), g2_ref[...], P_LORA)
    if has_vres:
        mix = _sigmoid(v0_ref[...] + _dot(_dot(v, v1_ref[...], P_LORA), v2_ref[...], P_LORA))
        v = v + (vf_ref[...] - v) * mix
    else:
        vout_ref[...] = v
    kk = k * kk_ref[...]
    kk = kk / jnp.maximum(jnp.sqrt(head_sum(kk * kk)), 1e-12)
    k = k * (1.0 + (a - 1.0) * ka_ref[...])
    nch = tt // CHUNK
    L = CHUNK
    npair = nh // 2
    pw = 2 * hd
    ti = lax.broadcasted_iota(jnp.int32, (L, pw), 0)
    tl = lax.broadcasted_iota(jnp.int32, (L, pw), 1) % hd
    low_strict = ti > tl
    low_incl = ti >= tl
    eye = (ti == tl).astype(F32)
    first_head = lax.broadcasted_iota(jnp.int32, (1, pw), 1) < hd

    def to_batch(t2, c0, nc):
        return jnp.concatenate([t2[c0 * L:(c0 + nc) * L, q * pw:(q + 1) * pw].reshape(nc, L, pw)
                                for q in range(npair)], axis=0)

    def bd(x):
        zero = jnp.zeros_like(x)
        return jnp.concatenate([jnp.where(first_head, x, zero), jnp.where(first_head, zero, x)], axis=-2)

    def fold(x):
        return x[:, :hd] + x[:, hd:]

    def c16(x):
        return x.astype(BF16)

    lw_hi, lw_lo = _split(lw)
    cl = _mm(tril_ref[...], lw_hi, _NN) + _mm(tril_ref[...], lw_lo, _NN)
    cl3 = cl.reshape(nch, L, gw)
    cl_end = cl3[:, L - 1:L, :]
    e_end = jnp.exp(cl_end - cl3).reshape(tt, gw)
    e_neg = jnp.exp(-cl)
    bb = kk * a
    at_t = -kk * jnp.exp(cl - lw)
    bh_t = bb * e_neg
    kh_t = k * e_neg
    rt_t = r * jnp.exp(cl)
    be_t = bb * e_end
    ke_t = k * e_end
    p_end = jnp.exp(cl_end)

    def chunk_operators(c0, nc):
        at = c16(to_batch(at_t, c0, nc))
        bh = c16(to_batch(bh_t, c0, nc))
        kh = c16(to_batch(kh_t, c0, nc))
        rt = to_batch(rt_t, c0, nc)
        be = to_batch(be_t, c0, nc)
        ke = to_batch(ke_t, c0, nc)
        vb = bd(c16(to_batch(v, c0, nc)))
        pe = jnp.concatenate([p_end[c0:c0 + nc, :, q * pw:(q + 1) * pw] for q in range(npair)], axis=0)
        lhs = jnp.concatenate([at, c16(rt)], axis=1)
        g_b = _bmm(lhs, bd(bh), True)
        g_k = _bmm(lhs, bd(kh), True)
        a_ab = jnp.where(low_strict, g_b[:, :L], 0.0)
        b_rb = jnp.where(low_incl, g_b[:, L:], 0.0)
        a_ak = jnp.where(low_strict, g_k[:, :L], 0.0)
        b_rk = jnp.where(low_incl, g_k[:, L:], 0.0)
        a16 = c16(a_ab)
        apow = _bmm(a16, bd(a16), False)
        tinv = eye + a_ab
        for _ in range(4):
            p16 = c16(apow)
            sq = _bmm(jnp.concatenate([p16, c16(tinv)], axis=1), bd(p16), False)
            tinv = tinv + sq[:, L:]
            apow = sq[:, :L]
        tinv = tinv + _bmm(c16(tinv), bd(c16(apow)), False)
        av = _bmm(c16(a_ak), vb, False)
        wu = _bmm(c16(tinv), jnp.concatenate([bd(at), bd(c16(av))], axis=2), False)
        rhs = jnp.concatenate([bd(c16(wu[:, :, :pw])), bd(c16(wu[:, :, pw:]))], axis=2)
        lhs_b = jnp.concatenate([c16(b_rb), c16(jnp.swapaxes(bd(be), 1, 2))], axis=1)
        lhs_k = jnp.concatenate([c16(b_rk), c16(jnp.swapaxes(bd(ke), 1, 2))], axis=1)
        o_wu = _bmm(lhs_b, rhs, False)
        o_v = _bmm(lhs_k, vb, False)
        rq = rt + o_wu[:, :L, :pw]
        y0 = o_wu[:, :L, pw:] + o_v[:, :L]
        m2 = eye * pe + fold(o_wu[:, L:, :pw])
        c2 = fold(o_wu[:, L:, pw:] + o_v[:, L:])
        return jnp.concatenate([rq, m2], axis=1), jnp.concatenate([y0, c2], axis=1)

    def state_walk(zs, ops, c0, nc):
        rqm, y0c = ops
        for c in range(nc):
            for q in range(npair):
                b = q * nc + c
                yz = _dot(rqm[b], bd(zs[q]), P_STATE) + y0c[b]
                yr_s[(c0 + c) * L:(c0 + c + 1) * L, q * pw:(q + 1) * pw] = yz[:L]
                zs[q] = yz[L:]
        return zs

    zs = state_walk([st_ref[q] for q in range(npair)], chunk_operators(0, nch), 0, nch)
    for q in range(npair):
        st_ref[q] = zs[q]

    y = yr_s[...]
    inv_n = 1.0 / hd
    mean = head_sum(y) * inv_n
    yc = y - mean
    var = head_sum(yc * yc) * inv_n
    yn = yc * lax.rsqrt(var + RWKV_GN_EPS) * lng_ref[...] + lnb_ref[...]
    bonus = head_sum(r * k * rk_ref[...]) * v
    y_ref[...] = (yn + bonus) * g


def _rwkv(pd, v_first, prm):
    t = pd.shape[0]
    tt = _tile(t, 512)
    nch = tt // CHUNK
    has_vres = v_first is not None
    gw = GROUP_W
    row_blk = lambda n: pl.BlockSpec((tt, n), lambda i: (i, 0))
    vec = _full((1, gw))
    in_specs = [row_blk(N_D),
                pl.BlockSpec((SUBLANES, N_D), lambda i: (jnp.maximum(i * (tt // SUBLANES) - 1, 0), 0))]
    args = [pd, pd]
    if has_vres:
        in_specs.append(row_blk(gw))
        args.append(v_first)
    tpos = jnp.arange(tt)
    tril = ((tpos[:, None] // CHUNK == tpos[None, :] // CHUNK) & (tpos[:, None] >= tpos[None, :])).astype(BF16)
    in_specs.append(_full((tt, tt)))
    args.append(tril)
    in_specs += [_full((1, N_D)), vec, _full((LORA_W, gw)), vec, _full((LORA_A, gw)), _full((LORA_G, gw)),
                 vec, vec, vec, vec, vec]
    args += [prm["mu"], prm["w0"], prm["w2"], prm["a0"], prm["a2"], prm["g2"], prm["k_k"], prm["k_a"],
             prm["r_k"], prm["ln_g"], prm["ln_b"]]
    if has_vres:
        in_specs += [vec, _full((gw, 32)), _full((32, gw))]
        args += [prm["v0"], prm["v1"], prm["v2"]]
        out_shape = jax.ShapeDtypeStruct((t, gw), F32)
        out_specs = row_blk(gw)
    else:
        out_shape = [jax.ShapeDtypeStruct((t, gw), F32)] * 2
        out_specs = [row_blk(gw)] * 2
    npair, pw = RWKV_HEADS // 2, 2 * RWKV_HEAD
    scratch = [pltpu.VMEM((npair, RWKV_HEAD, pw), F32), pltpu.VMEM((tt, gw), F32)]
    return pl.pallas_call(
        functools.partial(_rwkv_kernel, tt=tt, has_vres=has_vres),
        out_shape=out_shape,
        grid=(t // tt,),
        in_specs=in_specs,
        out_specs=out_specs,
        scratch_shapes=scratch,
        compiler_params=_cparams(("arbitrary",)),
        name="rwkv7",
    )(*args)


def _route_record(lg):
    lane = lax.broadcasted_iota(jnp.int32, (lg.shape[0], LANES), 1)
    lanef = lane.astype(F32)
    gl = jnp.where(lane < N_GROUPS, lg[:, 0:LANES], -jnp.inf)
    gmax = jnp.max(gl, axis=-1, keepdims=True)
    gp = 1.0 / jnp.sum(jnp.exp(gl - gmax), axis=-1, keepdims=True)
    gi = jnp.min(jnp.where(gl == gmax, lanef, float(LANES)), axis=-1, keepdims=True)
    lo = gi * EXP_PER_GROUP
    in_grp = (lanef >= lo) & (lanef < lo + EXP_PER_GROUP)
    el = jnp.where(in_grp, lg[:, LANES:2 * LANES], -jnp.inf)
    m1 = jnp.max(el, axis=-1, keepdims=True)
    i1 = jnp.min(jnp.where(el == m1, lanef, float(LANES)), axis=-1, keepdims=True)
    el2 = jnp.where(lanef == i1, -jnp.inf, el)
    m2 = jnp.max(el2, axis=-1, keepdims=True)
    i2 = jnp.min(jnp.where(el2 == m2, lanef, float(LANES)), axis=-1, keepdims=True)
    e2 = jnp.exp(m2 - m1)
    w_top = gp / (1.0 + e2)
    return jnp.where(lanef == i1 - lo, w_top, jnp.where(lanef == i2 - lo, w_top * e2,
                     jnp.where(lane == ROUTE_GROUP_LANE, gi, 0.0)))


def _outproj_kernel(ya_ref, yb_ref, yc_ref, yd_ref, x_ref, wo_ref, gt_ref, g_ref, sc_ref, sh_ref, wr_ref, br_ref,
                    x1_ref, h2_ref, gate_ref, *, tm):
    sub = tm // OUT_SUB
    for blk in range(OUT_SUB):
        rows = slice(blk * sub, (blk + 1) * sub)
        ycat = jnp.concatenate([ya_ref[rows, :], yb_ref[rows, :], yc_ref[rows, :], yd_ref[rows, :]], axis=1)
        x1 = x_ref[rows, :] + gt_ref[...] * _dot_w(ycat, wo_ref[...], P_OUT)
        x1_ref[rows, :] = x1
        h2 = _modulated_norm(x1, g_ref[...], sc_ref[...], sh_ref[...])
        h2_ref[rows, :] = h2.astype(h2_ref.dtype)
        gate_ref[rows, :] = _route_record(_dot_w(h2, wr_ref[...], P_ROUTER) + br_ref[...])


def _out_proj(ya, yb, yc, yd, x, wo_parts, gt, g, sc, sh, w_router, b_router):
    t = x.shape[0]
    tm = _tile(t, 1024)
    row_blk = lambda n: pl.BlockSpec((tm, n), lambda i: (i, 0))
    vec = _full((1, D_MODEL))
    return pl.pallas_call(
        functools.partial(_outproj_kernel, tm=tm),
        out_shape=[jax.ShapeDtypeStruct((t, D_MODEL), F32), jax.ShapeDtypeStruct((t, D_MODEL), BF16),
                   jax.ShapeDtypeStruct((t, LANES), F32)],
        grid=(t // tm,),
        in_specs=[row_blk(GROUP_W)] * 4 + [row_blk(D_MODEL), _full(wo_parts.shape), vec, vec, vec, vec,
                                           _full(w_router.shape), _full((1, 2 * LANES))],
        out_specs=[row_blk(D_MODEL), row_blk(D_MODEL), row_blk(LANES)],
        compiler_params=_cparams(("parallel",)),
        name="out_proj_router",
    )(ya, yb, yc, yd, x, wo_parts, gt, g, sc, sh, w_router, b_router)


def _moe_kernel(h_ref, route_ref, tril_ref, ex_ref, wg_ref, wu_ref, wd_ref, x1_ref, gt_ref, fg_ref, o_ref,
                acc_ref, *, final, tm, cap, cap_extra):
    nw = EXP_PER_GROUP * D_EXPERT
    lane = lax.broadcasted_iota(jnp.int32, (tm, LANES), 1)
    route = route_ref[...]
    gi = route[:, ROUTE_GROUP_LANE:ROUTE_GROUP_LANE + 1]
    onehot = lane.astype(F32) == gi
    cum = _mm(tril_ref[...], jnp.where(onehot, 1.0, 0.0).astype(BF16), _NN)
    rank = jnp.sum(jnp.where(onehot, cum, 0.0), axis=-1, keepdims=True) - 1.0
    counts = cum[tm - 1:tm, :]
    rec = jnp.where(lane == 0, rank, jnp.where(lane == 1, gi, 0.0))
    rec_rows = rec.T[0:SUBLANES, :]
    hx = jnp.concatenate([h_ref[...], route.astype(BF16)], axis=1)
    acc_ref[...] = jnp.zeros_like(acc_ref)

    for g in range(N_GROUPS):
        cols = slice(g * nw, (g + 1) * nw)
        count = jnp.sum(jnp.where(lane[0:1, :] == g, counts, 0.0)).astype(jnp.int32)
        sel_col = jnp.where(gi == float(g), rank, -1.0)
        sel_row = jnp.where(rec_rows[1:2, :] == float(g), rec_rows[0:1, :], -1.0)

        def block(base, rows, cols=cols, sel_col=sel_col, sel_row=sel_row):
            base = base.astype(F32)
            slot_r = lax.broadcasted_iota(jnp.int32, (rows, tm), 0).astype(F32) + base
            gather = jnp.where(slot_r == sel_row, 1.0, 0.0).astype(BF16)
            hs = _mm(gather, hx, _NN)
            h = hs[:, 0:D_MODEL].astype(BF16)
            gates = _mm(hs[:, D_MODEL:D_MODEL + LANES].astype(BF16), ex_ref[...], _NN)
            hg = _mm(h, wg_ref[:, cols], _NN)
            hu = _mm(h, wu_ref[:, cols], _NN)
            act = hg * _sigmoid(hg) * hu * gates
            ys = _mm(act.astype(BF16), wd_ref[cols, :], _NN)
            slot_c = lax.broadcasted_iota(jnp.int32, (tm, rows), 1).astype(F32) + base
            scatter = jnp.where(slot_c == sel_col, 1.0, 0.0).astype(BF16)
            acc_ref[...] += _mm(scatter, ys.astype(BF16), _NN)

        @pl.when(count > 0)
        def _(block=block):
            block(jnp.int32(0), cap)

        def extra(r, carry, block=block):
            block(cap + r * cap_extra, cap_extra)
            return carry

        lax.fori_loop(0, (jnp.maximum(count - cap, 0) + (cap_extra - 1)) // cap_extra, extra, 0)

    out = x1_ref[...] + gt_ref[...] * acc_ref[...]
    if final:
        out = out * lax.rsqrt(jnp.mean(out * out, axis=-1, keepdims=True) + NORM_EPS) * fg_ref[...]
    o_ref[...] = out


def _moe(h2, route, wg, wu, wd, x1, gt, final_g, final):
    t = h2.shape[0]
    tm = _tile(t, MOE_TILE)
    cap = tm // N_GROUPS
    cap_extra = cap // 2
    nw = EXP_PER_GROUP * D_EXPERT
    vec = _full((1, D_MODEL))
    tpos = jnp.arange(tm)
    tril = (tpos[:, None] >= tpos[None, :]).astype(BF16)
    expand = (jnp.arange(LANES)[:, None] == jnp.arange(nw)[None, :] // D_EXPERT).astype(BF16)
    resident = lambda shape: pl.BlockSpec(shape, lambda i: (0,) * len(shape), pipeline_mode=pl.Buffered(1))
    row_blk = lambda n: pl.BlockSpec((tm, n), lambda i: (i, 0))
    return pl.pallas_call(
        functools.partial(_moe_kernel, final=final, tm=tm, cap=cap, cap_extra=cap_extra),
        out_shape=jax.ShapeDtypeStruct((t, D_MODEL), F32),
        grid=(t // tm,),
        in_specs=[row_blk(D_MODEL), row_blk(LANES), resident((tm, tm)), resident((LANES, nw)),
                  resident(wg.shape), resident(wu.shape), resident(wd.shape), row_blk(D_MODEL), vec, vec],
        out_specs=row_blk(D_MODEL),
        scratch_shapes=[pltpu.VMEM((tm, D_MODEL), F32)],
        compiler_params=_cparams(("parallel",)),
        name="moe_experts",
    )(h2, route, tril, expand, wg, wu, wd, x1, gt, final_g)


def kernel(x, c, w_ada, b_ada, norm1_g, norm2_g, w_in, w_out, pool_w, pool_scale, attn_sinks, conv_w, rwkv_mu,
           rwkv_w0, rwkv_w2, rwkv_a0, rwkv_a2, rwkv_g2, rwkv_k_k, rwkv_k_a, rwkv_r_k, rwkv_ln_g, rwkv_ln_b,
           rwkv_v0, rwkv_v1, rwkv_v2, moe_w_grp, moe_b_grp, moe_w_exp, moe_b_exp, moe_w_gate, moe_w_up,
           moe_w_down, final_g):
    bsz, t, d = x.shape
    assert bsz == 1 and d == D_MODEL
    depth = w_ada.shape[0]
    xs = x.reshape(t, d)
    mod = _ada_mod(c, w_ada, b_ada)
    v_first = None
    row = lambda a: a.reshape(1, -1)
    for l in range(depth):
        sh1, sc1, gt1, sh2, sc2, gt2 = [mod[l, :, j * d:(j + 1) * d] for j in range(6)]
        wp = jax.scipy.linalg.block_diag(*[pool_w[l, gidx] for gidx in range(len(POOL_WINDOWS))])
        ya, pb, yc, pd = _in_proj(xs, row(norm1_g[l]), sc1, sh1, _split_w(w_in[l], P_IN), _split_w(wp, P_POOL),
                                  row(pool_scale[l]), conv_w[l])
        yb = _attention(pb, row(attn_sinks[l]))
        prm = dict(mu=row(rwkv_mu[l]), w0=row(rwkv_w0[l]), w2=rwkv_w2[l], a0=row(rwkv_a0[l]), a2=rwkv_a2[l],
                   g2=rwkv_g2[l], k_k=row(rwkv_k_k[l]), k_a=row(rwkv_k_a[l]), r_k=row(rwkv_r_k[l]),
                   ln_g=row(rwkv_ln_g[l]), ln_b=row(rwkv_ln_b[l]))
        if l == 0:
            yd, v_first = _rwkv(pd, None, prm)
        else:
            prm.update(v0=row(rwkv_v0[l - 1]), v1=rwkv_v1[l - 1], v2=rwkv_v2[l - 1])
            yd = _rwkv(pd, v_first, prm)
        lane_pad = lambda a: jnp.pad(a, ((0, 0), (0, LANES - a.shape[1])))
        w_router = jnp.concatenate([lane_pad(moe_w_grp[l]), lane_pad(moe_w_exp[l])], axis=1)
        b_router = jnp.concatenate([lane_pad(row(moe_b_grp[l])), lane_pad(row(moe_b_exp[l]))], axis=1)
        x1, h2, gates = _out_proj(ya, yb, yc, yd, xs, _split_w(w_out[l], P_OUT), gt1, row(norm2_g[l]), sc2, sh2,
                                  _split_w(w_router, P_ROUTER), b_router)
        wg = jnp.transpose(moe_w_gate[l], (1, 0, 2)).reshape(d, N_EXPERTS * D_EXPERT).astype(BF16)
        wu = jnp.transpose(moe_w_up[l], (1, 0, 2)).reshape(d, N_EXPERTS * D_EXPERT).astype(BF16)
        wd = moe_w_down[l].astype(BF16).reshape(N_EXPERTS * D_EXPERT, d)
        xs = _moe(h2, gates, wg, wu, wd, x1, gt2, row(final_g), final=(l == depth - 1))
    return xs.reshape(bsz, t, d)
```

```python
import functools

import jax
import jax.numpy as jnp
from jax import lax
from jax.experimental import pallas as pl
from jax.experimental.pallas import tpu as pltpu

F32 = jnp.float32
BF16 = jnp.bfloat16

D_MODEL = 1024
GROUP_W = 256
NORM_EPS = 1e-6
POOL_WINDOWS = (2, 4, 8, 16)
POOL_CG = 64
HEAD_DIM = 64
ATT_HEADS = 4
ATT_KV_HEADS = 2
ATT_BLOCK = 128
CONV_W = 3
RWKV_HEAD = 64
RWKV_HEADS = 4
LORA_W = 32
LORA_A = 32
LORA_G = 64
RWKV_GN_EPS = 64e-5
N_A = GROUP_W
N_B = (ATT_HEADS + 2 * ATT_KV_HEADS) * HEAD_DIM
N_C = 3 * GROUP_W
N_D = 3 * GROUP_W + LORA_W + LORA_A + LORA_G
N_IN = N_A + N_B + N_C + N_D
N_GROUPS = 4
EXP_PER_GROUP = 8
N_EXPERTS = 32
D_EXPERT = 128

LANES = 128
SUBLANES = 8
VMEM_LIMIT = 56 * 1024 * 1024

CHUNK = 64
POOL_HALO = 16
CONV_HALO = 8
OUT_SUB = 4
ROUTE_GROUP_LANE = 8
MOE_TILE = 1024

P_IN = 1
P_OUT = 1
P_POOL = 1
P_ATT = 1
P_MOE = 1
P_DECAY = 3
P_LORA = 1
P_STATE = 1
P_ROUTER = 3


def _split(a):
    hi = a.astype(BF16)
    lo = (a - hi.astype(F32)).astype(BF16)
    return hi, lo


def _mm(a, b, dims):
    return lax.dot_general(a, b, (dims, ((), ())), preferred_element_type=F32)


_NN = ((1,), (0,))
_NT = ((1,), (1,))


def _dot(a, b, passes=1, dims=_NN):
    if passes == 1:
        return _mm(a.astype(BF16), b.astype(BF16), dims)
    a0, a1 = _split(a)
    b0, b1 = _split(b)
    return _mm(a0, b0, dims) + (_mm(a0, b1, dims) + _mm(a1, b0, dims))


def _bmm(a, b, nt):
    dims = (((2,), (2 if nt else 1,)), ((0,), (0,)))
    return lax.dot_general(a, b, dims, preferred_element_type=F32)


def _bdot(a, b, passes=1, nt=False):
    if passes == 1:
        return _bmm(a.astype(BF16), b.astype(BF16), nt)
    a0, a1 = _split(a)
    b0, b1 = _split(b)
    return _bmm(a0, b0, nt) + (_bmm(a0, b1, nt) + _bmm(a1, b0, nt))


def _lhs_w(a, passes):
    if passes == 1:
        return a.astype(BF16)
    a0, a1 = _split(a)
    return jnp.concatenate([a0, a0, a1], axis=1)


def _dot_w(a, w_cat, passes):
    return _mm(_lhs_w(a, passes), w_cat, _NN)


def _split_w(w, passes):
    hi = w.astype(BF16)
    if passes == 1:
        return hi
    lo = (w - hi.astype(F32)).astype(BF16)
    return jnp.concatenate([hi, lo, hi], axis=0)


def _sigmoid(x):
    return 1.0 / (1.0 + jnp.exp(-x))


def _cparams(sem):
    return pltpu.CompilerParams(dimension_semantics=sem, vmem_limit_bytes=VMEM_LIMIT)


def _full(shape):
    return pl.BlockSpec(shape, lambda *_: (0,) * len(shape))


def _tile(n, pref):
    t = min(n, pref)
    assert n % t == 0, (n, t)
    return t


def _mod_kernel(c_ref, w_ref, b_ref, o_ref):
    c = c_ref[...]
    cond = c * _sigmoid(c)
    o_ref[...] = _dot(cond, w_ref[...], 3) + b_ref[...]


def _ada_mod(c, w_ada, b_ada):
    depth = w_ada.shape[0]
    c8 = jnp.broadcast_to(c, (SUBLANES, D_MODEL))
    out = pl.pallas_call(
        _mod_kernel,
        out_shape=jax.ShapeDtypeStruct((depth, SUBLANES, 6 * D_MODEL), F32),
        grid=(depth, 6),
        in_specs=[
            pl.BlockSpec((SUBLANES, D_MODEL), lambda l, j: (0, 0)),
            pl.BlockSpec((None, D_MODEL, D_MODEL), lambda l, j: (l, 0, j)),
            pl.BlockSpec((None, 1, D_MODEL), lambda l, j: (l, 0, j)),
        ],
        out_specs=pl.BlockSpec((None, SUBLANES, D_MODEL), lambda l, j: (l, 0, j)),
        compiler_params=_cparams(("arbitrary", "arbitrary")),
        name="ada_mod",
    )(c8, w_ada, b_ada.reshape(depth, 1, 6 * D_MODEL))
    return out[:, 0:1, :]


def _modulated_norm(x, g, sc, sh):
    y = x * lax.rsqrt(jnp.mean(x * x, axis=-1, keepdims=True) + NORM_EPS) * g
    return y * (1.0 + sc) + sh


def _shift_rows(ext, n):
    return pltpu.roll(ext, n, axis=0)


def _pool_mixer(u, halo, wp, scale, t0):
    tt = u.shape[0]
    ext = jnp.concatenate([halo, u], axis=0)
    s2 = ext + _shift_rows(ext, 1)
    s4 = s2 + _shift_rows(s2, 2)
    s8 = s4 + _shift_rows(s4, 4)
    s16 = s8 + _shift_rows(s8, 8)
    grp = lax.broadcasted_iota(jnp.int32, (tt, GROUP_W), 1) // POOL_CG
    tpos = lax.broadcasted_iota(jnp.int32, (tt, GROUP_W), 0) + (t0 + 1)
    wsum = jnp.where(grp == 0, s2[POOL_HALO:], jnp.where(grp == 1, s4[POOL_HALO:],
                     jnp.where(grp == 2, s8[POOL_HALO:], s16[POOL_HALO:])))
    win = jnp.where(grp == 0, 2, jnp.where(grp == 1, 4, jnp.where(grp == 2, 8, 16)))
    cnt = jnp.minimum(tpos, win).astype(F32)
    return _dot_w(wsum / cnt - u, wp, P_POOL) * scale


def _conv_mixer(bg, z, z_halo, cw):
    zext = jnp.concatenate([z_halo, z], axis=0)
    y = (cw[2:3, :] * z + cw[1:2, :] * _shift_rows(zext, 1)[CONV_HALO:]
         + cw[0:1, :] * _shift_rows(zext, 2)[CONV_HALO:])
    return bg * y


def _inproj_kernel(x_ref, g_ref, sc_ref, sh_ref, w_ref, wp_ref, scale_ref, cw_ref,
                   ya_ref, pb_ref, yc_ref, pd_ref, u_tail, z_tail, *, passes, tm):
    i = pl.program_id(0)

    @pl.when(i == 0)
    def _():
        u_tail[...] = jnp.zeros_like(u_tail)
        z_tail[...] = jnp.zeros_like(z_tail)

    h = _modulated_norm(x_ref[...], g_ref[...], sc_ref[...], sh_ref[...])
    lhs = _lhs_w(h, passes)
    proj = lambda col, width: _mm(lhs, w_ref[:, col:col + width], _NN)
    u = proj(0, N_A)
    pc = proj(N_A + N_B, N_C)
    ya_ref[...] = _pool_mixer(u, u_tail[...], wp_ref[...], scale_ref[...], i * tm)
    u_tail[...] = u[tm - POOL_HALO:, :]
    z = pc[:, GROUP_W:2 * GROUP_W] * pc[:, 2 * GROUP_W:3 * GROUP_W]
    yc_ref[...] = _conv_mixer(pc[:, 0:GROUP_W], z, z_tail[...], cw_ref[...])
    z_tail[...] = z[tm - CONV_HALO:, :]
    pb_ref[...] = proj(N_A, N_B)
    pd_ref[...] = proj(N_A + N_B + N_C, N_D)


def _in_proj(x, g, sc, sh, w_parts, wp_parts, pool_scale, conv_w):
    t = x.shape[0]
    tm = _tile(t, 1024)
    vec = _full((1, D_MODEL))
    widths = (GROUP_W, N_B, GROUP_W, N_D)
    return pl.pallas_call(
        functools.partial(_inproj_kernel, passes=P_IN, tm=tm),
        out_shape=[jax.ShapeDtypeStruct((t, n), F32) for n in widths],
        grid=(t // tm,),
        in_specs=[pl.BlockSpec((tm, D_MODEL), lambda i: (i, 0)), vec, vec, vec, _full(w_parts.shape),
                  _full(wp_parts.shape), _full((1, GROUP_W)), _full((CONV_W, GROUP_W))],
        out_specs=[pl.BlockSpec((tm, n), lambda i: (i, 0)) for n in widths],
        scratch_shapes=[pltpu.VMEM((POOL_HALO, GROUP_W), F32), pltpu.VMEM((CONV_HALO, GROUP_W), F32)],
        compiler_params=_cparams(("arbitrary",)),
        name="in_proj_pool_conv",
    )(x, g, sc, sh, w_parts, wp_parts, pool_scale, conv_w)


def _attn_kernel(cur_ref, prev_ref, sink_ref, o_ref, *, tq):
    i = pl.program_id(0)
    nq = ATT_HEADS * HEAD_DIM
    nkv = ATT_KV_HEADS * HEAD_DIM
    blk = ATT_BLOCK
    hd = HEAD_DIM
    rep = ATT_HEADS // ATT_KV_HEADS
    nblk = tq // blk
    ri = lax.broadcasted_iota(jnp.int32, (rep * blk, 2 * blk), 0)
    ki = lax.broadcasted_iota(jnp.int32, (rep * blk, 2 * blk), 1)
    dist = (ri % blk) + blk - ki
    in_win = (dist >= 0) & (dist < blk)
    distf = dist.astype(F32)
    head_col = lax.broadcasted_iota(jnp.int32, (rep * blk, 1), 0) // blk
    bias, sink = [], []
    for g in range(ATT_KV_HEADS):
        slope_g = jnp.zeros((rep * blk, 1), F32)
        sink_g = jnp.zeros((rep * blk, 1), F32)
        for j in range(rep):
            h = g * rep + j
            slope_g = jnp.where(head_col == j, 2.0 ** (-8.0 * (h + 1) / ATT_HEADS), slope_g)
            sink_g = jnp.where(head_col == j, sink_ref[:, h:h + 1], sink_g)
        bias.append(slope_g * distf)
        sink.append(sink_g)
    bias = jnp.stack(bias)[None]
    sink = jnp.stack(sink)[None]

    q = cur_ref[:, 0:nq] * (hd ** -0.5)
    kv = jnp.concatenate([prev_ref[...], cur_ref[:, nq:nq + 2 * nkv]], axis=0)
    qs, ks, vs = [], [], []
    for b in range(nblk):
        for g in range(ATT_KV_HEADS):
            qs.append(jnp.concatenate(
                [q[b * blk:(b + 1) * blk, (g * rep + j) * hd:(g * rep + j + 1) * hd] for j in range(rep)], axis=0))
            ks.append(kv[b * blk:(b + 2) * blk, g * hd:(g + 1) * hd])
            vs.append(kv[b * blk:(b + 2) * blk, nkv + g * hd:nkv + (g + 1) * hd])
    s = _bdot(jnp.stack(qs), jnp.stack(ks), P_ATT, nt=True)
    s = s.reshape(nblk, ATT_KV_HEADS, rep * blk, 2 * blk)
    s = jnp.where(in_win, s - bias, -jnp.inf)
    first_ok = ki >= blk * (1 - (i > 0).astype(jnp.int32))
    s = jnp.concatenate([jnp.where(first_ok, s[0:1], -jnp.inf), s[1:]], axis=0)
    m = jnp.maximum(jnp.max(s, axis=-1, keepdims=True), sink)
    p = jnp.exp(s - m)
    den = jnp.sum(p, axis=-1, keepdims=True) + jnp.exp(sink - m)
    o = _bdot(p.reshape(nblk * ATT_KV_HEADS, rep * blk, 2 * blk), jnp.stack(vs), P_ATT)
    o = o.reshape(nblk, ATT_KV_HEADS, rep * blk, hd) / den
    for b in range(nblk):
        o_ref[b * blk:(b + 1) * blk, :] = jnp.concatenate(
            [o[b, g, j * blk:(j + 1) * blk, :] for g in range(ATT_KV_HEADS) for j in range(rep)], axis=-1)


def _attention(pb, sinks):
    t = pb.shape[0]
    tq = _tile(t, 2048)
    nb = tq // ATT_BLOCK
    return pl.pallas_call(
        functools.partial(_attn_kernel, tq=tq),
        out_shape=jax.ShapeDtypeStruct((t, GROUP_W), F32),
        grid=(t // tq,),
        in_specs=[
            pl.BlockSpec((tq, N_B), lambda i: (i, 0)),
            pl.BlockSpec((ATT_BLOCK, 2 * ATT_KV_HEADS * HEAD_DIM), lambda i: (jnp.maximum(i * nb - 1, 0), 1)),
            _full((1, ATT_HEADS)),
        ],
        out_specs=pl.BlockSpec((tq, GROUP_W), lambda i: (i, 0)),
        compiler_params=_cparams(("parallel",)),
        name="swa_attention",
    )(pb, pb, sinks)


def _rwkv_kernel(*refs, tt, has_vres):
    if has_vres:
        (pd_ref, prev_ref, vf_ref, tril_ref, mu_ref, w0_ref, w2_ref, a0_ref, a2_ref, g2_ref, kk_ref, ka_ref,
         rk_ref, lng_ref, lnb_ref, v0_ref, v1_ref, v2_ref, y_ref, st_ref, yr_s) = refs
    else:
        (pd_ref, prev_ref, tril_ref, mu_ref, w0_ref, w2_ref, a0_ref, a2_ref, g2_ref, kk_ref, ka_ref,
         rk_ref, lng_ref, lnb_ref, y_ref, vout_ref, st_ref, yr_s) = refs
    i = pl.program_id(0)
    nh, hd, gw = RWKV_HEADS, RWKV_HEAD, GROUP_W
    @pl.when(i == 0)
    def _():
        st_ref[...] = jnp.zeros_like(st_ref)

    hr = lax.broadcasted_iota(jnp.int32, (2 * gw, gw), 0) % gw // hd
    hc = lax.broadcasted_iota(jnp.int32, (2 * gw, gw), 1) // hd
    hsum2 = jnp.where(hr == hc, 1.0, 0.0).astype(BF16)

    def head_sum(t2):
        return _mm(jnp.concatenate(_split(t2), axis=1), hsum2, _NN)

    p = pd_ref[...]
    row = lax.broadcasted_iota(jnp.int32, (tt, 1), 0)
    prev_row = prev_ref[SUBLANES - 1:SUBLANES, :] * (i > 0).astype(F32)
    shifted = jnp.where(row == 0, prev_row, pltpu.roll(p, 1, axis=0))
    z = p + (shifted - p) * mu_ref[...]
    r = z[:, 0:gw]
    k = z[:, gw:2 * gw]
    v = z[:, 2 * gw:3 * gw]
    o = 3 * gw
    wd = z[:, o:o + LORA_W]
    ad = z[:, o + LORA_W:o + LORA_W + LORA_A]
    gd = z[:, o + LORA_W + LORA_A:o + LORA_W + LORA_A + LORA_G]
    wpre = -(w0_ref[...] + _dot(jnp.tanh(wd), w2_ref[...], P_DECAY))
    softplus = jnp.maximum(wpre, 0.0) + jnp.log(1.0 + jnp.exp(-jnp.abs(wpre)))
    lw = -jnp.exp(-softplus - 0.5)
    a = _sigmoid(a0_ref[...] + _dot(ad, a2_ref[...], P_LORA))
    g = _dot(_sigmoid(gd), g2_ref[...], P_LORA)
    if has_vres:
        mix = _sigmoid(v0_ref[...] + _dot(_dot(v, v1_ref[...], P_LORA), v2_ref[...], P_LORA))
        v = v + (vf_ref[...] - v) * mix
    else:
        vout_ref[...] = v
    kk = k * kk_ref[...]
    kk = kk / jnp.maximum(jnp.sqrt(head_sum(kk * kk)), 1e-12)
    k = k * (1.0 + (a - 1.0) * ka_ref[...])
    nch = tt // CHUNK
    L = CHUNK
    npair = nh // 2
    pw = 2 * hd
    ti = lax.broadcasted_iota(jnp.int32, (L, pw), 0)
    tl = lax.broadcasted_iota(jnp.int32, (L, pw), 1) % hd
    low_strict = ti > tl
    low_incl = ti >= tl
    eye = (ti == tl).astype(F32)
    first_head = lax.broadcasted_iota(jnp.int32, (1, pw), 1) < hd

    def same_block(b):
        return (ti // b) == (tl // b)

    def to_batch(t2, c0, nc):
        return jnp.concatenate([t2[c0 * L:(c0 + nc) * L, q * pw:(q + 1) * pw].reshape(nc, L, pw)
                                for q in range(npair)], axis=0)

    def bd(x):
        zero = jnp.zeros_like(x)
        return jnp.concatenate([jnp.where(first_head, x, zero), jnp.where(first_head, zero, x)], axis=-2)

    def fold(x):
        return x[:, :hd] + x[:, hd:]

    def c16(x):
        return x.astype(BF16)

    lw_hi, lw_lo = _split(lw)
    cl = _mm(tril_ref[...], lw_hi, _NN) + _mm(tril_ref[...], lw_lo, _NN)
    cl3 = cl.reshape(nch, L, gw)
    cl_end = cl3[:, L - 1:L, :]
    e_end = jnp.exp(cl_end - cl3).reshape(tt, gw)
    e_neg = jnp.exp(-cl)
    bb = kk * a
    at_t = -kk * jnp.exp(cl - lw)
    bh_t = bb * e_neg
    kh_t = k * e_neg
    rt_t = r * jnp.exp(cl)
    be_t = bb * e_end
    ke_t = k * e_end
    p_end = jnp.exp(cl_end)

    def chunk_operators(c0, nc):
        at = c16(to_batch(at_t, c0, nc))
        bh = c16(to_batch(bh_t, c0, nc))
        kh = c16(to_batch(kh_t, c0, nc))
        rt = to_batch(rt_t, c0, nc)
        be = to_batch(be_t, c0, nc)
        ke = to_batch(ke_t, c0, nc)
        vb = bd(c16(to_batch(v, c0, nc)))
        pe = jnp.concatenate([p_end[c0:c0 + nc, :, q * pw:(q + 1) * pw] for q in range(npair)], axis=0)
        lhs = jnp.concatenate([at, c16(rt)], axis=1)
        g_b = _bmm(lhs, bd(bh), True)
        g_k = _bmm(lhs, bd(kh), True)
        a_ab = jnp.where(low_strict, g_b[:, :L], 0.0)
        b_rb = jnp.where(low_incl, g_b[:, L:], 0.0)
        a_ak = jnp.where(low_strict, g_k[:, :L], 0.0)
        b_rk = jnp.where(low_incl, g_k[:, L:], 0.0)
        tinv = eye + jnp.where(same_block(2), a_ab, 0.0)
        b = 2
        while b < L:
            e = jnp.where(same_block(2 * b) & ~same_block(b), a_ab, 0.0)
            xe = _bmm(c16(tinv), bd(c16(e)), False)
            tinv = tinv + _bmm(c16(xe), bd(c16(tinv)), False)
            b *= 2
        av = _bmm(c16(a_ak), vb, False)
        wu = _bmm(c16(tinv), jnp.concatenate([bd(at), bd(c16(av))], axis=2), False)
        rhs = jnp.concatenate([bd(c16(wu[:, :, :pw])), bd(c16(wu[:, :, pw:]))], axis=2)
        lhs_b = jnp.concatenate([c16(b_rb), c16(jnp.swapaxes(bd(be), 1, 2))], axis=1)
        lhs_k = jnp.concatenate([c16(b_rk), c16(jnp.swapaxes(bd(ke), 1, 2))], axis=1)
        o_wu = _bmm(lhs_b, rhs, False)
        o_v = _bmm(lhs_k, vb, False)
        rq = rt + o_wu[:, :L, :pw]
        y0 = o_wu[:, :L, pw:] + o_v[:, :L]
        m2 = eye * pe + fold(o_wu[:, L:, :pw])
        c2 = fold(o_wu[:, L:, pw:] + o_v[:, L:])
        return jnp.concatenate([rq, m2], axis=1), jnp.concatenate([y0, c2], axis=1)

    def state_walk(zs, ops, c0, nc):
        rqm, y0c = ops
        for c in range(nc):
            for q in range(npair):
                b = q * nc + c
                yz = _dot(rqm[b], bd(zs[q]), P_STATE) + y0c[b]
                yr_s[(c0 + c) * L:(c0 + c + 1) * L, q * pw:(q + 1) * pw] = yz[:L]
                zs[q] = yz[L:]
        return zs

    zs = state_walk([st_ref[q] for q in range(npair)], chunk_operators(0, nch), 0, nch)
    for q in range(npair):
        st_ref[q] = zs[q]

    y = yr_s[...]
    inv_n = 1.0 / hd
    mean = head_sum(y) * inv_n
    yc = y - mean
    var = head_sum(yc * yc) * inv_n
    yn = yc * lax.rsqrt(var + RWKV_GN_EPS) * lng_ref[...] + lnb_ref[...]
    bonus = head_sum(r * k * rk_ref[...]) * v
    y_ref[...] = (yn + bonus) * g


def _rwkv(pd, v_first, prm):
    t = pd.shape[0]
    tt = _tile(t, 512)
    nch = tt // CHUNK
    has_vres = v_first is not None
    gw = GROUP_W
    row_blk = lambda n: pl.BlockSpec((tt, n), lambda i: (i, 0))
    vec = _full((1, gw))
    in_specs = [row_blk(N_D),
                pl.BlockSpec((SUBLANES, N_D), lambda i: (jnp.maximum(i * (tt // SUBLANES) - 1, 0), 0))]
    args = [pd, pd]
    if has_vres:
        in_specs.append(row_blk(gw))
        args.append(v_first)
    tpos = jnp.arange(tt)
    tril = ((tpos[:, None] // CHUNK == tpos[None, :] // CHUNK) & (tpos[:, None] >= tpos[None, :])).astype(BF16)
    in_specs.append(_full((tt, tt)))
    args.append(tril)
    in_specs += [_full((1, N_D)), vec, _full((LORA_W, gw)), vec, _full((LORA_A, gw)), _full((LORA_G, gw)),
                 vec, vec, vec, vec, vec]
    args += [prm["mu"], prm["w0"], prm["w2"], prm["a0"], prm["a2"], prm["g2"], prm["k_k"], prm["k_a"],
             prm["r_k"], prm["ln_g"], prm["ln_b"]]
    if has_vres:
        in_specs += [vec, _full((gw, 32)), _full((32, gw))]
        args += [prm["v0"], prm["v1"], prm["v2"]]
        out_shape = jax.ShapeDtypeStruct((t, gw), F32)
        out_specs = row_blk(gw)
    else:
        out_shape = [jax.ShapeDtypeStruct((t, gw), F32)] * 2
        out_specs = [row_blk(gw)] * 2
    npair, pw = RWKV_HEADS // 2, 2 * RWKV_HEAD
    scratch = [pltpu.VMEM((npair, RWKV_HEAD, pw), F32), pltpu.VMEM((tt, gw), F32)]
    return pl.pallas_call(
        functools.partial(_rwkv_kernel, tt=tt, has_vres=has_vres),
        out_shape=out_shape,
        grid=(t // tt,),
        in_specs=in_specs,
        out_specs=out_specs,
        scratch_shapes=scratch,
        compiler_params=_cparams(("arbitrary",)),
        name="rwkv7",
    )(*args)


def _route_record(lg):
    lane = lax.broadcasted_iota(jnp.int32, (lg.shape[0], LANES), 1)
    lanef = lane.astype(F32)
    gl = jnp.where(lane < N_GROUPS, lg[:, 0:LANES], -jnp.inf)
    gmax = jnp.max(gl, axis=-1, keepdims=True)
    gp = 1.0 / jnp.sum(jnp.exp(gl - gmax), axis=-1, keepdims=True)
    gi = jnp.min(jnp.where(gl == gmax, lanef, float(LANES)), axis=-1, keepdims=True)
    lo = gi * EXP_PER_GROUP
    in_grp = (lanef >= lo) & (lanef < lo + EXP_PER_GROUP)
    el = jnp.where(in_grp, lg[:, LANES:2 * LANES], -jnp.inf)
    m1 = jnp.max(el, axis=-1, keepdims=True)
    i1 = jnp.min(jnp.where(el == m1, lanef, float(LANES)), axis=-1, keepdims=True)
    el2 = jnp.where(lanef == i1, -jnp.inf, el)
    m2 = jnp.max(el2, axis=-1, keepdims=True)
    i2 = jnp.min(jnp.where(el2 == m2, lanef, float(LANES)), axis=-1, keepdims=True)
    e2 = jnp.exp(m2 - m1)
    w_top = gp / (1.0 + e2)
    return jnp.where(lanef == i1 - lo, w_top, jnp.where(lanef == i2 - lo, w_top * e2,
                     jnp.where(lane == ROUTE_GROUP_LANE, gi, 0.0)))


def _outproj_kernel(ya_ref, yb_ref, yc_ref, yd_ref, x_ref, wo_ref, gt_ref, g_ref, sc_ref, sh_ref, wr_ref, br_ref,
                    x1_ref, h2_ref, gate_ref, *, tm):
    sub = tm // OUT_SUB
    for blk in range(OUT_SUB):
        rows = slice(blk * sub, (blk + 1) * sub)
        ycat = jnp.concatenate([ya_ref[rows, :], yb_ref[rows, :], yc_ref[rows, :], yd_ref[rows, :]], axis=1)
        x1 = x_ref[rows, :] + gt_ref[...] * _dot_w(ycat, wo_ref[...], P_OUT)
        x1_ref[rows, :] = x1
        h2 = _modulated_norm(x1, g_ref[...], sc_ref[...], sh_ref[...])
        h2_ref[rows, :] = h2.astype(h2_ref.dtype)
        gate_ref[rows, :] = _route_record(_dot_w(h2, wr_ref[...], P_ROUTER) + br_ref[...])


def _out_proj(ya, yb, yc, yd, x, wo_parts, gt, g, sc, sh, w_router, b_router):
    t = x.shape[0]
    tm = _tile(t, 1024)
    row_blk = lambda n: pl.BlockSpec((tm, n), lambda i: (i, 0))
    vec = _full((1, D_MODEL))
    return pl.pallas_call(
        functools.partial(_outproj_kernel, tm=tm),
        out_shape=[jax.ShapeDtypeStruct((t, D_MODEL), F32), jax.ShapeDtypeStruct((t, D_MODEL), BF16),
                   jax.ShapeDtypeStruct((t, LANES), F32)],
        grid=(t // tm,),
        in_specs=[row_blk(GROUP_W)] * 4 + [row_blk(D_MODEL), _full(wo_parts.shape), vec, vec, vec, vec,
                                           _full(w_router.shape), _full((1, 2 * LANES))],
        out_specs=[row_blk(D_MODEL), row_blk(D_MODEL), row_blk(LANES)],
        compiler_params=_cparams(("parallel",)),
        name="out_proj_router",
    )(ya, yb, yc, yd, x, wo_parts, gt, g, sc, sh, w_router, b_router)


def _moe_kernel(h_ref, route_ref, tril_ref, wg_ref, wu_ref, wd_ref, x1_ref, gt_ref, fg_ref, o_ref,
                acc_ref, col_ref, row_ref, cnt_ref, *, final, tm, cap, cap_extra):
    g = pl.program_id(1)
    lane = lax.broadcasted_iota(jnp.int32, (tm, LANES), 1)

    @pl.when(g == 0)
    def _():
        acc_ref[...] = jnp.zeros_like(acc_ref)
        gi = route_ref[:, ROUTE_GROUP_LANE:ROUTE_GROUP_LANE + 1]
        onehot = lane.astype(F32) == gi
        cum = _mm(tril_ref[...], jnp.where(onehot, 1.0, 0.0).astype(BF16), _NN)
        rank = jnp.sum(jnp.where(onehot, cum, 0.0), axis=-1, keepdims=True) - 1.0
        cnt_ref[...] = cum[tm - 1:tm, :]
        rec = jnp.where(lane == 0, rank, jnp.where(lane == 1, gi, 0.0))
        col_ref[...] = rec
        row_ref[...] = rec.T[0:SUBLANES, :]

    gf = g.astype(F32)
    count = jnp.sum(jnp.where(lane[0:1, :] == g, cnt_ref[...], 0.0))
    count = count.astype(jnp.int32)
    sel_col = jnp.where(col_ref[:, 1:2] == gf, col_ref[:, 0:1], -1.0)
    sel_row = jnp.where(row_ref[1:2, :] == gf, row_ref[0:1, :], -1.0)
    hx = jnp.concatenate([h_ref[...], route_ref[...].astype(BF16)], axis=1)

    def block(base, rows):
        base = base.astype(F32)
        slot_r = lax.broadcasted_iota(jnp.int32, (rows, tm), 0).astype(F32) + base
        gather = jnp.where(slot_r == sel_row, 1.0, 0.0).astype(BF16)
        hs = _mm(gather, hx, _NN)
        h = hs[:, 0:D_MODEL].astype(BF16)
        slot_gate = hs[:, D_MODEL:D_MODEL + LANES]
        hg = _mm(h, wg_ref[...], _NN)
        hu = _mm(h, wu_ref[...], _NN)
        hu = jnp.concatenate([hu[:, j * D_EXPERT:(j + 1) * D_EXPERT] * slot_gate[:, j:j + 1]
                              for j in range(EXP_PER_GROUP)], axis=1)
        act = hg * _sigmoid(hg) * hu
        ys = _mm(act.astype(BF16), wd_ref[...], _NN)
        slot_c = lax.broadcasted_iota(jnp.int32, (tm, rows), 1).astype(F32) + base
        scatter = jnp.where(slot_c == sel_col, 1.0, 0.0).astype(BF16)
        acc_ref[...] += _mm(scatter, ys.astype(BF16), _NN)

    @pl.when(count > 0)
    def _():
        block(jnp.int32(0), cap)

    def extra(r, carry):
        block(cap + r * cap_extra, cap_extra)
        return carry

    lax.fori_loop(0, (jnp.maximum(count - cap, 0) + (cap_extra - 1)) // cap_extra, extra, 0)

    @pl.when(g == pl.num_programs(1) - 1)
    def _():
        out = x1_ref[...] + gt_ref[...] * acc_ref[...]
        if final:
            out = out * lax.rsqrt(jnp.mean(out * out, axis=-1, keepdims=True) + NORM_EPS) * fg_ref[...]
        o_ref[...] = out


def _moe(h2, route, wg, wu, wd, x1, gt, final_g, final):
    t = h2.shape[0]
    tm = _tile(t, MOE_TILE)
    cap = tm // N_GROUPS
    cap_extra = cap // 2
    nw = EXP_PER_GROUP * D_EXPERT
    vec = _full((1, D_MODEL))
    tpos = jnp.arange(tm)
    tril = (tpos[:, None] >= tpos[None, :]).astype(BF16)
    return pl.pallas_call(
        functools.partial(_moe_kernel, final=final, tm=tm, cap=cap, cap_extra=cap_extra),
        out_shape=jax.ShapeDtypeStruct((t, D_MODEL), F32),
        grid=(t // tm, N_GROUPS),
        in_specs=[
            pl.BlockSpec((tm, D_MODEL), lambda i, g: (i, 0)),
            pl.BlockSpec((tm, LANES), lambda i, g: (i, 0)),
            _full((tm, tm)),
            pl.BlockSpec((D_MODEL, nw), lambda i, g: (0, g)),
            pl.BlockSpec((D_MODEL, nw), lambda i, g: (0, g)),
            pl.BlockSpec((nw, D_MODEL), lambda i, g: (g, 0)),
            pl.BlockSpec((tm, D_MODEL), lambda i, g: (i, 0)),
            vec, vec,
        ],
        out_specs=pl.BlockSpec((tm, D_MODEL), lambda i, g: (i, 0)),
        scratch_shapes=[pltpu.VMEM((tm, D_MODEL), F32), pltpu.VMEM((tm, LANES), F32),
                        pltpu.VMEM((SUBLANES, tm), F32), pltpu.VMEM((1, LANES), F32)],
        compiler_params=_cparams(("parallel", "arbitrary")),
        name="moe_experts",
    )(h2, route, tril, wg, wu, wd, x1, gt, final_g)


def kernel(x, c, w_ada, b_ada, norm1_g, norm2_g, w_in, w_out, pool_w, pool_scale, attn_sinks, conv_w, rwkv_mu,
           rwkv_w0, rwkv_w2, rwkv_a0, rwkv_a2, rwkv_g2, rwkv_k_k, rwkv_k_a, rwkv_r_k, rwkv_ln_g, rwkv_ln_b,
           rwkv_v0, rwkv_v1, rwkv_v2, moe_w_grp, moe_b_grp, moe_w_exp, moe_b_exp, moe_w_gate, moe_w_up,
           moe_w_down, final_g):
    bsz, t, d = x.shape
    assert bsz == 1 and d == D_MODEL
    depth = w_ada.shape[0]
    xs = x.reshape(t, d)
    mod = _ada_mod(c, w_ada, b_ada)
    v_first = None
    row = lambda a: a.reshape(1, -1)
    for l in range(depth):
        sh1, sc1, gt1, sh2, sc2, gt2 = [mod[l, :, j * d:(j + 1) * d] for j in range(6)]
        wp = jax.scipy.linalg.block_diag(*[pool_w[l, gidx] for gidx in range(len(POOL_WINDOWS))])
        ya, pb, yc, pd = _in_proj(xs, row(norm1_g[l]), sc1, sh1, _split_w(w_in[l], P_IN), _split_w(wp, P_POOL),
                                  row(pool_scale[l]), conv_w[l])
        yb = _attention(pb, row(attn_sinks[l]))
        prm = dict(mu=row(rwkv_mu[l]), w0=row(rwkv_w0[l]), w2=rwkv_w2[l], a0=row(rwkv_a0[l]), a2=rwkv_a2[l],
                   g2=rwkv_g2[l], k_k=row(rwkv_k_k[l]), k_a=row(rwkv_k_a[l]), r_k=row(rwkv_r_k[l]),
                   ln_g=row(rwkv_ln_g[l]), ln_b=row(rwkv_ln_b[l]))
        if l == 0:
            yd, v_first = _rwkv(pd, None, prm)
        else:
            prm.update(v0=row(rwkv_v0[l - 1]), v1=rwkv_v1[l - 1], v2=rwkv_v2[l - 1])
            yd = _rwkv(pd, v_first, prm)
        lane_pad = lambda a: jnp.pad(a, ((0, 0), (0, LANES - a.shape[1])))
        w_router = jnp.concatenate([lane_pad(moe_w_grp[l]), lane_pad(moe_w_exp[l])], axis=1)
        b_router = jnp.concatenate([lane_pad(row(moe_b_grp[l])), lane_pad(row(moe_b_exp[l]))], axis=1)
        x1, h2, gates = _out_proj(ya, yb, yc, yd, xs, _split_w(w_out[l], P_OUT), gt1, row(norm2_g[l]), sc2, sh2,
                                  _split_w(w_router, P_ROUTER), b_router)
        wg = jnp.transpose(moe_w_gate[l], (1, 0, 2)).reshape(d, N_EXPERTS * D_EXPERT).astype(BF16)
        wu = jnp.transpose(moe_w_up[l], (1, 0, 2)).reshape(d, N_EXPERTS * D_EXPERT).astype(BF16)
        wd = moe_w_down[l].astype(BF16).reshape(N_EXPERTS * D_EXPERT, d)
        xs = _moe(h2, gates, wg, wu, wd, x1, gt2, row(final_g), final=(l == depth - 1))
    return xs.reshape(bsz, t, d)
```

```python
import functools

import jax
import jax.numpy as jnp
from jax import lax
from jax.experimental import pallas as pl
from jax.experimental.pallas import tpu as pltpu

F32 = jnp.float32
BF16 = jnp.bfloat16

D_MODEL = 1024
GROUP_W = 256
NORM_EPS = 1e-6
POOL_WINDOWS = (2, 4, 8, 16)
POOL_CG = 64
HEAD_DIM = 64
ATT_HEADS = 4
ATT_KV_HEADS = 2
ATT_BLOCK = 128
CONV_W = 3
RWKV_HEAD = 64
RWKV_HEADS = 4
LORA_W = 32
LORA_A = 32
LORA_G = 64
LORA_V = 32
RWKV_GN_EPS = 64e-5
N_A = GROUP_W
N_B = (ATT_HEADS + 2 * ATT_KV_HEADS) * HEAD_DIM
N_C = 3 * GROUP_W
N_D = 3 * GROUP_W + LORA_W + LORA_A + LORA_G
N_GROUPS = 4
EXP_PER_GROUP = 8
N_EXPERTS = 32
D_EXPERT = 128

LANES = 128
SUBLANES = 8
VMEM_LIMIT = 56 * 1024 * 1024

CHUNK = 64
POOL_HALO = 16
CONV_HALO = 8
OUT_SUB = 4
ROUTE_GROUP_LANE = 8
MOE_TILE = 1024

P_IN = 1
P_OUT = 1
P_POOL = 1
P_ATT = 1
P_DECAY = 3
P_LORA = 1
P_STATE = 1
P_ROUTER = 3


def _split(a):
    hi = a.astype(BF16)
    lo = (a - hi.astype(F32)).astype(BF16)
    return hi, lo


def _mm(a, b, dims):
    return lax.dot_general(a, b, (dims, ((), ())), preferred_element_type=F32)


_NN = ((1,), (0,))
_NT = ((1,), (1,))


def _dot(a, b, passes=1, dims=_NN):
    if passes == 1:
        return _mm(a.astype(BF16), b.astype(BF16), dims)
    a0, a1 = _split(a)
    b0, b1 = _split(b)
    return _mm(a0, b0, dims) + (_mm(a0, b1, dims) + _mm(a1, b0, dims))


def _bmm(a, b, nt):
    dims = (((2,), (2 if nt else 1,)), ((0,), (0,)))
    return lax.dot_general(a, b, dims, preferred_element_type=F32)


def _bdot(a, b, passes=1, nt=False):
    if passes == 1:
        return _bmm(a.astype(BF16), b.astype(BF16), nt)
    a0, a1 = _split(a)
    b0, b1 = _split(b)
    return _bmm(a0, b0, nt) + (_bmm(a0, b1, nt) + _bmm(a1, b0, nt))


def _lhs_w(a, passes):
    if passes == 1:
        return a.astype(BF16)
    a0, a1 = _split(a)
    return jnp.concatenate([a0, a0, a1], axis=1)


def _dot_w(a, w_cat, passes):
    return _mm(_lhs_w(a, passes), w_cat, _NN)


def _split_w(w, passes):
    hi = w.astype(BF16)
    if passes == 1:
        return hi
    lo = (w - hi.astype(F32)).astype(BF16)
    return jnp.concatenate([hi, lo, hi], axis=0)


def _sigmoid(x):
    return 1.0 / (1.0 + jnp.exp(-x))


def _cparams(sem):
    return pltpu.CompilerParams(dimension_semantics=sem, vmem_limit_bytes=VMEM_LIMIT)


def _full(shape):
    return pl.BlockSpec(shape, lambda *_: (0,) * len(shape))


def _tile(n, pref):
    t = min(n, pref)
    assert n % t == 0, (n, t)
    return t


def _mod_kernel(c_ref, w_ref, b_ref, o_ref):
    c = c_ref[...]
    cond = c * _sigmoid(c)
    o_ref[...] = _dot(cond, w_ref[...], 3) + b_ref[...]


def _ada_mod(c, w_ada, b_ada):
    depth = w_ada.shape[0]
    c8 = jnp.broadcast_to(c, (SUBLANES, D_MODEL))
    out = pl.pallas_call(
        _mod_kernel,
        out_shape=jax.ShapeDtypeStruct((depth, SUBLANES, 6 * D_MODEL), F32),
        grid=(depth, 6),
        in_specs=[
            pl.BlockSpec((SUBLANES, D_MODEL), lambda l, j: (0, 0)),
            pl.BlockSpec((None, D_MODEL, D_MODEL), lambda l, j: (l, 0, j)),
            pl.BlockSpec((None, 1, D_MODEL), lambda l, j: (l, 0, j)),
        ],
        out_specs=pl.BlockSpec((None, SUBLANES, D_MODEL), lambda l, j: (l, 0, j)),
        compiler_params=_cparams(("arbitrary", "arbitrary")),
        name="ada_mod",
    )(c8, w_ada, b_ada.reshape(depth, 1, 6 * D_MODEL))
    return out[:, 0:1, :]


def _modulated_norm(x, g, sc, sh):
    y = x * lax.rsqrt(jnp.mean(x * x, axis=-1, keepdims=True) + NORM_EPS) * g
    return y * (1.0 + sc) + sh


def _shift_rows(ext, n):
    return pltpu.roll(ext, n, axis=0)


def _pool_mixer(u, halo, wp, scale, t0):
    tt = u.shape[0]
    ext = jnp.concatenate([halo, u], axis=0)
    s2 = ext + _shift_rows(ext, 1)
    s4 = s2 + _shift_rows(s2, 2)
    s8 = s4 + _shift_rows(s4, 4)
    s16 = s8 + _shift_rows(s8, 8)
    grp = lax.broadcasted_iota(jnp.int32, (tt, GROUP_W), 1) // POOL_CG
    tpos = lax.broadcasted_iota(jnp.int32, (tt, GROUP_W), 0) + (t0 + 1)
    wsum = jnp.where(grp == 0, s2[POOL_HALO:], jnp.where(grp == 1, s4[POOL_HALO:],
                     jnp.where(grp == 2, s8[POOL_HALO:], s16[POOL_HALO:])))
    win = jnp.where(grp == 0, 2, jnp.where(grp == 1, 4, jnp.where(grp == 2, 8, 16)))
    cnt = jnp.minimum(tpos, win).astype(F32)
    return _dot_w(wsum / cnt - u, wp, P_POOL) * scale


def _conv_mixer(bg, z, z_halo, cw):
    zext = jnp.concatenate([z_halo, z], axis=0)
    y = (cw[2:3, :] * z + cw[1:2, :] * _shift_rows(zext, 1)[CONV_HALO:]
         + cw[0:1, :] * _shift_rows(zext, 2)[CONV_HALO:])
    return bg * y


def _inproj_kernel(x_ref, g_ref, sc_ref, sh_ref, w_ref, wp_ref, scale_ref, cw_ref,
                   ya_ref, pb_ref, yc_ref, pd_ref, u_tail, z_tail, *, passes, tm):
    i = pl.program_id(0)

    @pl.when(i == 0)
    def _():
        u_tail[...] = jnp.zeros_like(u_tail)
        z_tail[...] = jnp.zeros_like(z_tail)

    h = _modulated_norm(x_ref[...], g_ref[...], sc_ref[...], sh_ref[...])
    lhs = _lhs_w(h, passes)
    proj = lambda col, width: _mm(lhs, w_ref[:, col:col + width], _NN)
    u = proj(0, N_A)
    pc = proj(N_A + N_B, N_C)
    ya_ref[...] = _pool_mixer(u, u_tail[...], wp_ref[...], scale_ref[...], i * tm)
    u_tail[...] = u[tm - POOL_HALO:, :]
    z = pc[:, GROUP_W:2 * GROUP_W] * pc[:, 2 * GROUP_W:3 * GROUP_W]
    yc_ref[...] = _conv_mixer(pc[:, 0:GROUP_W], z, z_tail[...], cw_ref[...])
    z_tail[...] = z[tm - CONV_HALO:, :]
    pb_ref[...] = proj(N_A, N_B)
    pd_ref[...] = proj(N_A + N_B + N_C, N_D)


def _in_proj(x, g, sc, sh, w_parts, wp_parts, pool_scale, conv_w):
    t = x.shape[0]
    tm = _tile(t, 1024)
    vec = _full((1, D_MODEL))
    widths = (GROUP_W, N_B, GROUP_W, N_D)
    return pl.pallas_call(
        functools.partial(_inproj_kernel, passes=P_IN, tm=tm),
        out_shape=[jax.ShapeDtypeStruct((t, n), F32) for n in widths],
        grid=(t // tm,),
        in_specs=[pl.BlockSpec((tm, D_MODEL), lambda i: (i, 0)), vec, vec, vec, _full(w_parts.shape),
                  _full(wp_parts.shape), _full((1, GROUP_W)), _full((CONV_W, GROUP_W))],
        out_specs=[pl.BlockSpec((tm, n), lambda i: (i, 0)) for n in widths],
        scratch_shapes=[pltpu.VMEM((POOL_HALO, GROUP_W), F32), pltpu.VMEM((CONV_HALO, GROUP_W), F32)],
        compiler_params=_cparams(("arbitrary",)),
        name="in_proj_pool_conv",
    )(x, g, sc, sh, w_parts, wp_parts, pool_scale, conv_w)


def _attn_kernel(cur_ref, prev_ref, sink_ref, o_ref, *, tq):
    i = pl.program_id(0)
    nq = ATT_HEADS * HEAD_DIM
    nkv = ATT_KV_HEADS * HEAD_DIM
    blk = ATT_BLOCK
    hd = HEAD_DIM
    rep = ATT_HEADS // ATT_KV_HEADS
    nblk = tq // blk
    ri = lax.broadcasted_iota(jnp.int32, (rep * blk, 2 * blk), 0)
    ki = lax.broadcasted_iota(jnp.int32, (rep * blk, 2 * blk), 1)
    dist = (ri % blk) + blk - ki
    in_win = (dist >= 0) & (dist < blk)
    distf = dist.astype(F32)
    head_col = lax.broadcasted_iota(jnp.int32, (rep * blk, 1), 0) // blk
    bias, sink = [], []
    for g in range(ATT_KV_HEADS):
        slope_g = jnp.zeros((rep * blk, 1), F32)
        sink_g = jnp.zeros((rep * blk, 1), F32)
        for j in range(rep):
            h = g * rep + j
            slope_g = jnp.where(head_col == j, 2.0 ** (-8.0 * (h + 1) / ATT_HEADS), slope_g)
            sink_g = jnp.where(head_col == j, sink_ref[:, h:h + 1], sink_g)
        bias.append(slope_g * distf)
        sink.append(sink_g)
    bias = jnp.stack(bias)[None]
    sink = jnp.stack(sink)[None]

    q = cur_ref[:, 0:nq] * (hd ** -0.5)
    kv = jnp.concatenate([prev_ref[...], cur_ref[:, nq:nq + 2 * nkv]], axis=0)
    qs, ks, vs = [], [], []
    for b in range(nblk):
        for g in range(ATT_KV_HEADS):
            qs.append(jnp.concatenate(
                [q[b * blk:(b + 1) * blk, (g * rep + j) * hd:(g * rep + j + 1) * hd] for j in range(rep)], axis=0))
            ks.append(kv[b * blk:(b + 2) * blk, g * hd:(g + 1) * hd])
            vs.append(kv[b * blk:(b + 2) * blk, nkv + g * hd:nkv + (g + 1) * hd])
    s = _bdot(jnp.stack(qs), jnp.stack(ks), P_ATT, nt=True)
    s = s.reshape(nblk, ATT_KV_HEADS, rep * blk, 2 * blk)
    s = jnp.where(in_win, s - bias, -jnp.inf)
    first_ok = ki >= blk * (1 - (i > 0).astype(jnp.int32))
    s = jnp.concatenate([jnp.where(first_ok, s[0:1], -jnp.inf), s[1:]], axis=0)
    m = jnp.maximum(jnp.max(s, axis=-1, keepdims=True), sink)
    p = jnp.exp(s - m)
    den = jnp.sum(p, axis=-1, keepdims=True) + jnp.exp(sink - m)
    o = _bdot(p.reshape(nblk * ATT_KV_HEADS, rep * blk, 2 * blk), jnp.stack(vs), P_ATT)
    o = o.reshape(nblk, ATT_KV_HEADS, rep * blk, hd) / den
    for b in range(nblk):
        o_ref[b * blk:(b + 1) * blk, :] = jnp.concatenate(
            [o[b, g, j * blk:(j + 1) * blk, :] for g in range(ATT_KV_HEADS) for j in range(rep)], axis=-1)


def _attention(pb, sinks):
    t = pb.shape[0]
    tq = _tile(t, 2048)
    nb = tq // ATT_BLOCK
    return pl.pallas_call(
        functools.partial(_attn_kernel, tq=tq),
        out_shape=jax.ShapeDtypeStruct((t, GROUP_W), F32),
        grid=(t // tq,),
        in_specs=[
            pl.BlockSpec((tq, N_B), lambda i: (i, 0)),
            pl.BlockSpec((ATT_BLOCK, 2 * ATT_KV_HEADS * HEAD_DIM), lambda i: (jnp.maximum(i * nb - 1, 0), 1)),
            _full((1, ATT_HEADS)),
        ],
        out_specs=pl.BlockSpec((tq, GROUP_W), lambda i: (i, 0)),
        compiler_params=_cparams(("parallel",)),
        name="swa_attention",
    )(pb, pb, sinks)


def _rwkv_kernel(*refs, tt, has_vres):
    if has_vres:
        (pd_ref, prev_ref, vf_ref, tril_ref, mu_ref, w0_ref, w2_ref, a0_ref, a2_ref, g2_ref, kk_ref, ka_ref,
         rk_ref, lng_ref, lnb_ref, v0_ref, v1_ref, v2_ref, y_ref, st_ref, yr_s) = refs
    else:
        (pd_ref, prev_ref, tril_ref, mu_ref, w0_ref, w2_ref, a0_ref, a2_ref, g2_ref, kk_ref, ka_ref,
         rk_ref, lng_ref, lnb_ref, y_ref, vout_ref, st_ref, yr_s) = refs
    i = pl.program_id(0)
    nh, hd, gw = RWKV_HEADS, RWKV_HEAD, GROUP_W
    @pl.when(i == 0)
    def _():
        st_ref[...] = jnp.zeros_like(st_ref)

    hr = lax.broadcasted_iota(jnp.int32, (2 * gw, gw), 0) % gw // hd
    hc = lax.broadcasted_iota(jnp.int32, (2 * gw, gw), 1) // hd
    hsum2 = jnp.where(hr == hc, 1.0, 0.0).astype(BF16)

    def head_sum(t2):
        return _mm(jnp.concatenate(_split(t2), axis=1), hsum2, _NN)

    p = pd_ref[...]
    row = lax.broadcasted_iota(jnp.int32, (tt, 1), 0)
    prev_row = prev_ref[SUBLANES - 1:SUBLANES, :] * (i > 0).astype(F32)
    shifted = jnp.where(row == 0, prev_row, pltpu.roll(p, 1, axis=0))
    z = p + (shifted - p) * mu_ref[...]
    r = z[:, 0:gw]
    k = z[:, gw:2 * gw]
    v = z[:, 2 * gw:3 * gw]
    o = 3 * gw
    wd = z[:, o:o + LORA_W]
    ad = z[:, o + LORA_W:o + LORA_W + LORA_A]
    gd = z[:, o + LORA_W + LORA_A:o + LORA_W + LORA_A + LORA_G]
    wpre = -(w0_ref[...] + _dot(jnp.tanh(wd), w2_ref[...], P_DECAY))
    softplus = jnp.maximum(wpre, 0.0) + jnp.log(1.0 + jnp.exp(-jnp.abs(wpre)))
    lw = -jnp.exp(-softplus - 0.5)
    a = _sigmoid(a0_ref[...] + _dot(ad, a2_ref[...], P_LORA))
    g = _dot(_sigmoid(gd), g2_ref[...], P_LORA)
    if has_vres:
        mix = _sigmoid(v0_ref[...] + _dot(_dot(v, v1_ref[...], P_LORA), v2_ref[...], P_LORA))
        v = v + (vf_ref[...] - v) * mix
    else:
        vout_ref[...] = v
    kk = k * kk_ref[...]
    kk = kk / jnp.maximum(jnp.sqrt(head_sum(kk * kk)), 1e-12)
    k = k * (1.0 + (a - 1.0) * ka_ref[...])
    nch = tt // CHUNK
    L = CHUNK
    npair = nh // 2
    pw = 2 * hd
    ti = lax.broadcasted_iota(jnp.int32, (L, pw), 0)
    tl = lax.broadcasted_iota(jnp.int32, (L, pw), 1) % hd
    low_strict = ti > tl
    low_incl = ti >= tl
    eye = (ti == tl).astype(F32)
    first_head = lax.broadcasted_iota(jnp.int32, (1, pw), 1) < hd

    def same_block(b):
        return (ti // b) == (tl // b)

    def to_batch(t2, c0, nc):
        return jnp.concatenate([t2[c0 * L:(c0 + nc) * L, q * pw:(q + 1) * pw].reshape(nc, L, pw)
                                for q in range(npair)], axis=0)

    def bd(x):
        zero = jnp.zeros_like(x)
        return jnp.concatenate([jnp.where(first_head, x, zero), jnp.where(first_head, zero, x)], axis=-2)

    def fold(x):
        return x[:, :hd] + x[:, hd:]

    def c16(x):
        return x.astype(BF16)

    lw_hi, lw_lo = _split(lw)
    cl = _mm(tril_ref[...], lw_hi, _NN) + _mm(tril_ref[...], lw_lo, _NN)
    cl3 = cl.reshape(nch, L, gw)
    cl_end = cl3[:, L - 1:L, :]
    e_end = jnp.exp(cl_end - cl3).reshape(tt, gw)
    e_neg = jnp.exp(-cl)
    bb = kk * a
    at_t = -kk * jnp.exp(cl - lw)
    bh_t = bb * e_neg
    kh_t = k * e_neg
    rt_t = r * jnp.exp(cl)
    be_t = bb * e_end
    ke_t = k * e_end
    p_end = jnp.exp(cl_end)

    def chunk_operators(c0, nc):
        at = c16(to_batch(at_t, c0, nc))
        bh = c16(to_batch(bh_t, c0, nc))
        kh = c16(to_batch(kh_t, c0, nc))
        rt = to_batch(rt_t, c0, nc)
        be = to_batch(be_t, c0, nc)
        ke = to_batch(ke_t, c0, nc)
        vb = bd(c16(to_batch(v, c0, nc)))
        pe = jnp.concatenate([p_end[c0:c0 + nc, :, q * pw:(q + 1) * pw] for q in range(npair)], axis=0)
        lhs = jnp.concatenate([at, c16(rt)], axis=1)
        g_b = _bmm(lhs, bd(bh), True)
        g_k = _bmm(lhs, bd(kh), True)
        a_ab = jnp.where(low_strict, g_b[:, :L], 0.0)
        b_rb = jnp.where(low_incl, g_b[:, L:], 0.0)
        a_ak = jnp.where(low_strict, g_k[:, :L], 0.0)
        b_rk = jnp.where(low_incl, g_k[:, L:], 0.0)
        tinv = eye + jnp.where(same_block(2), a_ab, 0.0)
        b = 2
        while b < L:
            e = jnp.where(same_block(2 * b) & ~same_block(b), a_ab, 0.0)
            xe = _bmm(c16(tinv), bd(c16(e)), False)
            tinv = tinv + _bmm(c16(xe), bd(c16(tinv)), False)
            b *= 2
        av = _bmm(c16(a_ak), vb, False)
        wu = _bmm(c16(tinv), jnp.concatenate([bd(at), bd(c16(av))], axis=2), False)
        rhs = jnp.concatenate([bd(c16(wu[:, :, :pw])), bd(c16(wu[:, :, pw:]))], axis=2)
        lhs_b = jnp.concatenate([c16(b_rb), c16(jnp.swapaxes(bd(be), 1, 2))], axis=1)
        lhs_k = jnp.concatenate([c16(b_rk), c16(jnp.swapaxes(bd(ke), 1, 2))], axis=1)
        o_wu = _bmm(lhs_b, rhs, False)
        o_v = _bmm(lhs_k, vb, False)
        rq = rt + o_wu[:, :L, :pw]
        y0 = o_wu[:, :L, pw:] + o_v[:, :L]
        m2 = eye * pe + fold(o_wu[:, L:, :pw])
        c2 = fold(o_wu[:, L:, pw:] + o_v[:, L:])
        return jnp.concatenate([rq, m2], axis=1), jnp.concatenate([y0, c2], axis=1)

    def state_walk(zs, ops, c0, nc):
        rqm, y0c = ops
        for c in range(nc):
            for q in range(npair):
                b = q * nc + c
                yz = _dot(rqm[b], bd(zs[q]), P_STATE) + y0c[b]
                yr_s[(c0 + c) * L:(c0 + c + 1) * L, q * pw:(q + 1) * pw] = yz[:L]
                zs[q] = yz[L:]
        return zs

    zs = state_walk([st_ref[q] for q in range(npair)], chunk_operators(0, nch), 0, nch)
    for q in range(npair):
        st_ref[q] = zs[q]

    y = yr_s[...]
    inv_n = 1.0 / hd
    mean = head_sum(y) * inv_n
    yc = y - mean
    var = head_sum(yc * yc) * inv_n
    yn = yc * lax.rsqrt(var + RWKV_GN_EPS) * lng_ref[...] + lnb_ref[...]
    bonus = head_sum(r * k * rk_ref[...]) * v
    y_ref[...] = (yn + bonus) * g


def _rwkv(pd, v_first, prm):
    t = pd.shape[0]
    tt = _tile(t, 512)
    nch = tt // CHUNK
    has_vres = v_first is not None
    gw = GROUP_W
    row_blk = lambda n: pl.BlockSpec((tt, n), lambda i: (i, 0))
    vec = _full((1, gw))
    in_specs = [row_blk(N_D),
                pl.BlockSpec((SUBLANES, N_D), lambda i: (jnp.maximum(i * (tt // SUBLANES) - 1, 0), 0))]
    args = [pd, pd]
    if has_vres:
        in_specs.append(row_blk(gw))
        args.append(v_first)
    tpos = jnp.arange(tt)
    tril = ((tpos[:, None] // CHUNK == tpos[None, :] // CHUNK) & (tpos[:, None] >= tpos[None, :])).astype(BF16)
    in_specs.append(_full((tt, tt)))
    args.append(tril)
    in_specs += [_full((1, N_D)), vec, _full((LORA_W, gw)), vec, _full((LORA_A, gw)), _full((LORA_G, gw)),
                 vec, vec, vec, vec, vec]
    args += [prm["mu"], prm["w0"], prm["w2"], prm["a0"], prm["a2"], prm["g2"], prm["k_k"], prm["k_a"],
             prm["r_k"], prm["ln_g"], prm["ln_b"]]
    if has_vres:
        in_specs += [vec, _full((gw, LORA_V)), _full((LORA_V, gw))]
        args += [prm["v0"], prm["v1"], prm["v2"]]
        out_shape = jax.ShapeDtypeStruct((t, gw), F32)
        out_specs = row_blk(gw)
    else:
        out_shape = [jax.ShapeDtypeStruct((t, gw), F32)] * 2
        out_specs = [row_blk(gw)] * 2
    npair, pw = RWKV_HEADS // 2, 2 * RWKV_HEAD
    scratch = [pltpu.VMEM((npair, RWKV_HEAD, pw), F32), pltpu.VMEM((tt, gw), F32)]
    return pl.pallas_call(
        functools.partial(_rwkv_kernel, tt=tt, has_vres=has_vres),
        out_shape=out_shape,
        grid=(t // tt,),
        in_specs=in_specs,
        out_specs=out_specs,
        scratch_shapes=scratch,
        compiler_params=_cparams(("arbitrary",)),
        name="rwkv7",
    )(*args)


def _route_record(lg):
    lane = lax.broadcasted_iota(jnp.int32, (lg.shape[0], LANES), 1)
    lanef = lane.astype(F32)
    gl = jnp.where(lane < N_GROUPS, lg[:, 0:LANES], -jnp.inf)
    gmax = jnp.max(gl, axis=-1, keepdims=True)
    gp = 1.0 / jnp.sum(jnp.exp(gl - gmax), axis=-1, keepdims=True)
    gi = jnp.min(jnp.where(gl == gmax, lanef, float(LANES)), axis=-1, keepdims=True)
    lo = gi * EXP_PER_GROUP
    in_grp = (lanef >= lo) & (lanef < lo + EXP_PER_GROUP)
    el = jnp.where(in_grp, lg[:, LANES:2 * LANES], -jnp.inf)
    m1 = jnp.max(el, axis=-1, keepdims=True)
    i1 = jnp.min(jnp.where(el == m1, lanef, float(LANES)), axis=-1, keepdims=True)
    el2 = jnp.where(lanef == i1, -jnp.inf, el)
    m2 = jnp.max(el2, axis=-1, keepdims=True)
    i2 = jnp.min(jnp.where(el2 == m2, lanef, float(LANES)), axis=-1, keepdims=True)
    e2 = jnp.exp(m2 - m1)
    w_top = gp / (1.0 + e2)
    return jnp.where(lanef == i1 - lo, w_top, jnp.where(lanef == i2 - lo, w_top * e2,
                     jnp.where(lane == ROUTE_GROUP_LANE, gi, 0.0)))


def _outproj_kernel(ya_ref, yb_ref, yc_ref, yd_ref, x_ref, wo_ref, gt_ref, g_ref, sc_ref, sh_ref, wr_ref, br_ref,
                    x1_ref, h2_ref, gate_ref, *, tm):
    sub = tm // OUT_SUB
    for blk in range(OUT_SUB):
        rows = slice(blk * sub, (blk + 1) * sub)
        ycat = jnp.concatenate([ya_ref[rows, :], yb_ref[rows, :], yc_ref[rows, :], yd_ref[rows, :]], axis=1)
        x1 = x_ref[rows, :] + gt_ref[...] * _dot_w(ycat, wo_ref[...], P_OUT)
        x1_ref[rows, :] = x1
        h2 = _modulated_norm(x1, g_ref[...], sc_ref[...], sh_ref[...])
        h2_ref[rows, :] = h2.astype(h2_ref.dtype)
        gate_ref[rows, :] = _route_record(_dot_w(h2, wr_ref[...], P_ROUTER) + br_ref[...])


def _out_proj(ya, yb, yc, yd, x, wo_parts, gt, g, sc, sh, w_router, b_router):
    t = x.shape[0]
    tm = _tile(t, 1024)
    row_blk = lambda n: pl.BlockSpec((tm, n), lambda i: (i, 0))
    vec = _full((1, D_MODEL))
    return pl.pallas_call(
        functools.partial(_outproj_kernel, tm=tm),
        out_shape=[jax.ShapeDtypeStruct((t, D_MODEL), F32), jax.ShapeDtypeStruct((t, D_MODEL), BF16),
                   jax.ShapeDtypeStruct((t, LANES), F32)],
        grid=(t // tm,),
        in_specs=[row_blk(GROUP_W)] * 4 + [row_blk(D_MODEL), _full(wo_parts.shape), vec, vec, vec, vec,
                                           _full(w_router.shape), _full((1, 2 * LANES))],
        out_specs=[row_blk(D_MODEL), row_blk(D_MODEL), row_blk(LANES)],
        compiler_params=_cparams(("parallel",)),
        name="out_proj_router",
    )(ya, yb, yc, yd, x, wo_parts, gt, g, sc, sh, w_router, b_router)


def _moe_kernel(h_ref, route_ref, tril_ref, wg_ref, wu_ref, wd_ref, x1_ref, gt_ref, fg_ref, o_ref,
                acc_ref, col_ref, row_ref, cnt_ref, *, final, tm, cap, cap_extra):
    g = pl.program_id(1)
    lane = lax.broadcasted_iota(jnp.int32, (tm, LANES), 1)

    @pl.when(g == 0)
    def _():
        acc_ref[...] = jnp.zeros_like(acc_ref)
        gi = route_ref[:, ROUTE_GROUP_LANE:ROUTE_GROUP_LANE + 1]
        onehot = lane.astype(F32) == gi
        cum = _mm(tril_ref[...], jnp.where(onehot, 1.0, 0.0).astype(BF16), _NN)
        rank = jnp.sum(jnp.where(onehot, cum, 0.0), axis=-1, keepdims=True) - 1.0
        cnt_ref[0:1, :] = cum[tm - 1:tm, :]
        cnt_ref[1:2, :] = cum[tm // 2 - 1:tm // 2, :]
        rec = jnp.where(lane == 0, rank, jnp.where(lane == 1, gi, 0.0))
        col_ref[...] = rec
        row_ref[...] = rec.T[0:SUBLANES, :]

    gf = g.astype(F32)
    this_group = lane[0:1, :] == g
    count = jnp.sum(jnp.where(this_group, cnt_ref[0:1, :], 0.0)).astype(jnp.int32)
    count_half = jnp.sum(jnp.where(this_group, cnt_ref[1:2, :], 0.0)).astype(jnp.int32)
    sel_col = jnp.where(col_ref[:, 1:2] == gf, col_ref[:, 0:1], -1.0)
    sel_row = jnp.where(row_ref[1:2, :] == gf, row_ref[0:1, :], -1.0)
    hx = jnp.concatenate([h_ref[...], route_ref[...].astype(BF16)], axis=1)

    def block(base, rows, t0=0):
        base = base.astype(F32)
        nt = tm - t0
        slot_r = lax.broadcasted_iota(jnp.int32, (rows, nt), 0).astype(F32) + base
        gather = jnp.where(slot_r == sel_row[:, t0:], 1.0, 0.0).astype(BF16)
        hs = _mm(gather, hx[t0:, :], _NN)
        h = hs[:, 0:D_MODEL].astype(BF16)
        slot_gate = hs[:, D_MODEL:D_MODEL + LANES]
        hg = _mm(h, wg_ref[...], _NN)
        hu = _mm(h, wu_ref[...], _NN)
        hu = jnp.concatenate([hu[:, j * D_EXPERT:(j + 1) * D_EXPERT] * slot_gate[:, j:j + 1]
                              for j in range(EXP_PER_GROUP)], axis=1)
        act = hg * _sigmoid(hg) * hu
        ys = _mm(act.astype(BF16), wd_ref[...], _NN)
        slot_c = lax.broadcasted_iota(jnp.int32, (nt, rows), 1).astype(F32) + base
        scatter = jnp.where(slot_c == sel_col[t0:, :], 1.0, 0.0).astype(BF16)
        acc_ref[t0:, :] += _mm(scatter, ys.astype(BF16), _NN)

    @pl.when(count > 0)
    def _():
        block(jnp.int32(0), cap)

    n_extra = (jnp.maximum(count - cap, 0) + (cap_extra - 1)) // cap_extra
    late = count_half <= cap

    def extra_late(r, carry):
        block(cap + r * cap_extra, cap_extra, t0=tm // 2)
        return carry

    def extra_any(r, carry):
        block(cap + r * cap_extra, cap_extra)
        return carry

    lax.fori_loop(0, jnp.where(late, n_extra, 0), extra_late, 0)
    lax.fori_loop(0, jnp.where(late, 0, n_extra), extra_any, 0)

    @pl.when(g == pl.num_programs(1) - 1)
    def _():
        out = x1_ref[...] + gt_ref[...] * acc_ref[...]
        if final:
            out = out * lax.rsqrt(jnp.mean(out * out, axis=-1, keepdims=True) + NORM_EPS) * fg_ref[...]
        o_ref[...] = out


def _moe(h2, route, wg, wu, wd, x1, gt, final_g, final):
    t = h2.shape[0]
    tm = _tile(t, MOE_TILE)
    cap = tm // N_GROUPS
    cap_extra = cap // 2
    nw = EXP_PER_GROUP * D_EXPERT
    vec = _full((1, D_MODEL))
    tpos = jnp.arange(tm)
    tril = (tpos[:, None] >= tpos[None, :]).astype(BF16)
    return pl.pallas_call(
        functools.partial(_moe_kernel, final=final, tm=tm, cap=cap, cap_extra=cap_extra),
        out_shape=jax.ShapeDtypeStruct((t, D_MODEL), F32),
        grid=(t // tm, N_GROUPS),
        in_specs=[
            pl.BlockSpec((tm, D_MODEL), lambda i, g: (i, 0)),
            pl.BlockSpec((tm, LANES), lambda i, g: (i, 0)),
            _full((tm, tm)),
            pl.BlockSpec((D_MODEL, nw), lambda i, g: (0, g)),
            pl.BlockSpec((D_MODEL, nw), lambda i, g: (0, g)),
            pl.BlockSpec((nw, D_MODEL), lambda i, g: (g, 0)),
            pl.BlockSpec((tm, D_MODEL), lambda i, g: (i, 0)),
            vec, vec,
        ],
        out_specs=pl.BlockSpec((tm, D_MODEL), lambda i, g: (i, 0)),
        scratch_shapes=[pltpu.VMEM((tm, D_MODEL), F32), pltpu.VMEM((tm, LANES), F32),
                        pltpu.VMEM((SUBLANES, tm), F32), pltpu.VMEM((2, LANES), F32)],
        compiler_params=_cparams(("parallel", "arbitrary")),
        name="moe_experts",
    )(h2, route, tril, wg, wu, wd, x1, gt, final_g)


def kernel(x, c, w_ada, b_ada, norm1_g, norm2_g, w_in, w_out, pool_w, pool_scale, attn_sinks, conv_w, rwkv_mu,
           rwkv_w0, rwkv_w2, rwkv_a0, rwkv_a2, rwkv_g2, rwkv_k_k, rwkv_k_a, rwkv_r_k, rwkv_ln_g, rwkv_ln_b,
           rwkv_v0, rwkv_v1, rwkv_v2, moe_w_grp, moe_b_grp, moe_w_exp, moe_b_exp, moe_w_gate, moe_w_up,
           moe_w_down, final_g):
    bsz, t, d = x.shape
    assert bsz == 1 and d == D_MODEL
    depth = w_ada.shape[0]
    xs = x.reshape(t, d)
    mod = _ada_mod(c, w_ada, b_ada)
    v_first = None
    row = lambda a: a.reshape(1, -1)
    for l in range(depth):
        sh1, sc1, gt1, sh2, sc2, gt2 = [mod[l, :, j * d:(j + 1) * d] for j in range(6)]
        wp = jax.scipy.linalg.block_diag(*[pool_w[l, gidx] for gidx in range(len(POOL_WINDOWS))])
        ya, pb, yc, pd = _in_proj(xs, row(norm1_g[l]), sc1, sh1, _split_w(w_in[l], P_IN), _split_w(wp, P_POOL),
                                  row(pool_scale[l]), conv_w[l])
        yb = _attention(pb, row(attn_sinks[l]))
        prm = dict(mu=row(rwkv_mu[l]), w0=row(rwkv_w0[l]), w2=rwkv_w2[l], a0=row(rwkv_a0[l]), a2=rwkv_a2[l],
                   g2=rwkv_g2[l], k_k=row(rwkv_k_k[l]), k_a=row(rwkv_k_a[l]), r_k=row(rwkv_r_k[l]),
                   ln_g=row(rwkv_ln_g[l]), ln_b=row(rwkv_ln_b[l]))
        if l == 0:
            yd, v_first = _rwkv(pd, None, prm)
        else:
            prm.update(v0=row(rwkv_v0[l - 1]), v1=rwkv_v1[l - 1], v2=rwkv_v2[l - 1])
            yd = _rwkv(pd, v_first, prm)
        lane_pad = lambda a: jnp.pad(a, ((0, 0), (0, LANES - a.shape[1])))
        w_router = jnp.concatenate([lane_pad(moe_w_grp[l]), lane_pad(moe_w_exp[l])], axis=1)
        b_router = jnp.concatenate([lane_pad(row(moe_b_grp[l])), lane_pad(row(moe_b_exp[l]))], axis=1)
        x1, h2, gates = _out_proj(ya, yb, yc, yd, xs, _split_w(w_out[l], P_OUT), gt1, row(norm2_g[l]), sc2, sh2,
                                  _split_w(w_router, P_ROUTER), b_router)
        wg = jnp.transpose(moe_w_gate[l], (1, 0, 2)).reshape(d, N_EXPERTS * D_EXPERT).astype(BF16)
        wu = jnp.transpose(moe_w_up[l], (1, 0, 2)).reshape(d, N_EXPERTS * D_EXPERT).astype(BF16)
        wd = moe_w_down[l].astype(BF16).reshape(N_EXPERTS * D_EXPERT, d)
        xs = _moe(h2, gates, wg, wu, wd, x1, gt2, row(final_g), final=(l == depth - 1))
    return xs.reshape(bsz, t, d)
```

```python
import functools

import jax
import jax.numpy as jnp
from jax import lax
from jax.experimental import pallas as pl
from jax.experimental.pallas import tpu as pltpu

F32 = jnp.float32
BF16 = jnp.bfloat16

D_MODEL = 1024
GROUP_W = 256
NORM_EPS = 1e-6
POOL_WINDOWS = (2, 4, 8, 16)
POOL_CG = 64
HEAD_DIM = 64
ATT_HEADS = 4
ATT_KV_HEADS = 2
ATT_BLOCK = 128
CONV_W = 3
RWKV_HEAD = 64
RWKV_HEADS = 4
LORA_W = 32
LORA_A = 32
LORA_G = 64
LORA_V = 32
RWKV_GN_EPS = 64e-5
N_A = GROUP_W
N_B = (ATT_HEADS + 2 * ATT_KV_HEADS) * HEAD_DIM
N_C = 3 * GROUP_W
N_D = 3 * GROUP_W + LORA_W + LORA_A + LORA_G
N_GROUPS = 4
EXP_PER_GROUP = 8
N_EXPERTS = 32
D_EXPERT = 128

LANES = 128
SUBLANES = 8
VMEM_LIMIT = 56 * 1024 * 1024

CHUNK = 64
POOL_HALO = 16
CONV_HALO = 8
OUT_SUB = 2
ROUTE_GROUP_LANE = 8
MOE_TILE = 1024

P_IN = 1
P_OUT = 1
P_POOL = 1
P_ATT = 1
P_LORA = 1
P_STATE = 1
P_ROUTER = 3


def _split(a):
    hi = a.astype(BF16)
    lo = (a - hi.astype(F32)).astype(BF16)
    return hi, lo


def _mm(a, b, dims):
    return lax.dot_general(a, b, (dims, ((), ())), preferred_element_type=F32)


_NN = ((1,), (0,))
_NT = ((1,), (1,))


def _dot(a, b, passes=1, dims=_NN):
    if passes == 1:
        return _mm(a.astype(BF16), b.astype(BF16), dims)
    a0, a1 = _split(a)
    b0, b1 = _split(b)
    return _mm(a0, b0, dims) + (_mm(a0, b1, dims) + _mm(a1, b0, dims))


def _bmm(a, b, nt):
    dims = (((2,), (2 if nt else 1,)), ((0,), (0,)))
    return lax.dot_general(a, b, dims, preferred_element_type=F32)


def _bdot(a, b, passes=1, nt=False):
    if passes == 1:
        return _bmm(a.astype(BF16), b.astype(BF16), nt)
    a0, a1 = _split(a)
    b0, b1 = _split(b)
    return _bmm(a0, b0, nt) + (_bmm(a0, b1, nt) + _bmm(a1, b0, nt))


def _lhs_w(a, passes):
    if passes == 1:
        return a.astype(BF16)
    a0, a1 = _split(a)
    return jnp.concatenate([a0, a0, a1], axis=1)


def _dot_w(a, w_cat, passes):
    return _mm(_lhs_w(a, passes), w_cat, _NN)


def _split_w(w, passes):
    hi = w.astype(BF16)
    if passes == 1:
        return hi
    lo = (w - hi.astype(F32)).astype(BF16)
    return jnp.concatenate([hi, lo, hi], axis=0)


def _sigmoid(x):
    return 1.0 / (1.0 + jnp.exp(-x))


def _cparams(sem):
    return pltpu.CompilerParams(dimension_semantics=sem, vmem_limit_bytes=VMEM_LIMIT)


def _full(shape):
    return pl.BlockSpec(shape, lambda *_: (0,) * len(shape))


def _tile(n, pref):
    t = min(n, pref)
    assert n % t == 0, (n, t)
    return t


def _mod_kernel(c_ref, w_ref, b_ref, o_ref):
    c = c_ref[...]
    cond = c * _sigmoid(c)
    o_ref[...] = _dot(cond, w_ref[...], 3) + b_ref[...]


def _ada_mod(c, w_ada, b_ada):
    depth = w_ada.shape[0]
    c8 = jnp.broadcast_to(c, (SUBLANES, D_MODEL))
    out = pl.pallas_call(
        _mod_kernel,
        out_shape=jax.ShapeDtypeStruct((depth, SUBLANES, 6 * D_MODEL), F32),
        grid=(depth, 6),
        in_specs=[
            pl.BlockSpec((SUBLANES, D_MODEL), lambda l, j: (0, 0)),
            pl.BlockSpec((None, D_MODEL, D_MODEL), lambda l, j: (l, 0, j)),
            pl.BlockSpec((None, 1, D_MODEL), lambda l, j: (l, 0, j)),
        ],
        out_specs=pl.BlockSpec((None, SUBLANES, D_MODEL), lambda l, j: (l, 0, j)),
        compiler_params=_cparams(("arbitrary", "arbitrary")),
        name="ada_mod",
    )(c8, w_ada, b_ada.reshape(depth, 1, 6 * D_MODEL))
    return out[:, 0:1, :]


def _modulated_norm(x, g, sc, sh):
    y = x * lax.rsqrt(jnp.mean(x * x, axis=-1, keepdims=True) + NORM_EPS) * g
    return y * (1.0 + sc) + sh


def _shift_rows(ext, n):
    return pltpu.roll(ext, n, axis=0)


def _pool_mixer(u, halo, wp, scale, t0):
    tt = u.shape[0]
    ext = jnp.concatenate([halo, u], axis=0)
    s2 = ext + _shift_rows(ext, 1)
    s4 = s2 + _shift_rows(s2, 2)
    s8 = s4 + _shift_rows(s4, 4)
    s16 = s8 + _shift_rows(s8, 8)
    grp = lax.broadcasted_iota(jnp.int32, (tt, GROUP_W), 1) // POOL_CG
    tpos = lax.broadcasted_iota(jnp.int32, (tt, GROUP_W), 0) + (t0 + 1)
    wsum = jnp.where(grp == 0, s2[POOL_HALO:], jnp.where(grp == 1, s4[POOL_HALO:],
                     jnp.where(grp == 2, s8[POOL_HALO:], s16[POOL_HALO:])))
    win = jnp.where(grp == 0, 2, jnp.where(grp == 1, 4, jnp.where(grp == 2, 8, 16)))
    cnt = jnp.minimum(tpos, win).astype(F32)
    return _dot_w(wsum / cnt - u, wp, P_POOL) * scale


def _conv_mixer(bg, z, z_halo, cw):
    zext = jnp.concatenate([z_halo, z], axis=0)
    y = (cw[2:3, :] * z + cw[1:2, :] * _shift_rows(zext, 1)[CONV_HALO:]
         + cw[0:1, :] * _shift_rows(zext, 2)[CONV_HALO:])
    return bg * y


def _inproj_kernel(x_ref, g_ref, sc_ref, sh_ref, w_ref, wp_ref, scale_ref, cw_ref,
                   ya_ref, pb_ref, yc_ref, pd_ref, u_tail, z_tail, *, passes, tm):
    i = pl.program_id(0)

    @pl.when(i == 0)
    def _():
        u_tail[...] = jnp.zeros_like(u_tail)
        z_tail[...] = jnp.zeros_like(z_tail)

    h = _modulated_norm(x_ref[...], g_ref[...], sc_ref[...], sh_ref[...])
    lhs = _lhs_w(h, passes)
    proj = lambda col, width: _mm(lhs, w_ref[:, col:col + width], _NN)
    u = proj(0, N_A)
    pc = proj(N_A + N_B, N_C)
    ya_ref[...] = _pool_mixer(u, u_tail[...], wp_ref[...], scale_ref[...], i * tm)
    u_tail[...] = u[tm - POOL_HALO:, :]
    z = pc[:, GROUP_W:2 * GROUP_W] * pc[:, 2 * GROUP_W:3 * GROUP_W]
    yc_ref[...] = _conv_mixer(pc[:, 0:GROUP_W], z, z_tail[...], cw_ref[...])
    z_tail[...] = z[tm - CONV_HALO:, :]
    pb_ref[...] = proj(N_A, N_B)
    pd_ref[...] = proj(N_A + N_B + N_C, N_D)


def _in_proj(x, g, sc, sh, w_parts, wp_parts, pool_scale, conv_w):
    t = x.shape[0]
    tm = _tile(t, 1024)
    vec = _full((1, D_MODEL))
    widths = (GROUP_W, N_B, GROUP_W, N_D)
    return pl.pallas_call(
        functools.partial(_inproj_kernel, passes=P_IN, tm=tm),
        out_shape=[jax.ShapeDtypeStruct((t, n), F32) for n in widths],
        grid=(t // tm,),
        in_specs=[pl.BlockSpec((tm, D_MODEL), lambda i: (i, 0)), vec, vec, vec, _full(w_parts.shape),
                  _full(wp_parts.shape), _full((1, GROUP_W)), _full((CONV_W, GROUP_W))],
        out_specs=[pl.BlockSpec((tm, n), lambda i: (i, 0)) for n in widths],
        scratch_shapes=[pltpu.VMEM((POOL_HALO, GROUP_W), F32), pltpu.VMEM((CONV_HALO, GROUP_W), F32)],
        compiler_params=_cparams(("arbitrary",)),
        name="in_proj_pool_conv",
    )(x, g, sc, sh, w_parts, wp_parts, pool_scale, conv_w)


def _attn_kernel(cur_ref, prev_ref, sink_ref, o_ref, *, tq):
    i = pl.program_id(0)
    nq = ATT_HEADS * HEAD_DIM
    nkv = ATT_KV_HEADS * HEAD_DIM
    blk = ATT_BLOCK
    hd = HEAD_DIM
    rep = ATT_HEADS // ATT_KV_HEADS
    nblk = tq // blk
    ri = lax.broadcasted_iota(jnp.int32, (rep * blk, 2 * blk), 0)
    ki = lax.broadcasted_iota(jnp.int32, (rep * blk, 2 * blk), 1)
    dist = (ri % blk) + blk - ki
    in_win = (dist >= 0) & (dist < blk)
    distf = dist.astype(F32)
    head_col = lax.broadcasted_iota(jnp.int32, (rep * blk, 1), 0) // blk
    bias, sink = [], []
    for g in range(ATT_KV_HEADS):
        slope_g = jnp.zeros((rep * blk, 1), F32)
        sink_g = jnp.zeros((rep * blk, 1), F32)
        for j in range(rep):
            h = g * rep + j
            slope_g = jnp.where(head_col == j, 2.0 ** (-8.0 * (h + 1) / ATT_HEADS), slope_g)
            sink_g = jnp.where(head_col == j, sink_ref[:, h:h + 1], sink_g)
        bias.append(slope_g * distf)
        sink.append(sink_g)
    bias = jnp.stack(bias)[None]
    sink = jnp.stack(sink)[None]

    q = cur_ref[:, 0:nq] * (hd ** -0.5)
    kv = jnp.concatenate([prev_ref[...], cur_ref[:, nq:nq + 2 * nkv]], axis=0)
    qs, ks, vs = [], [], []
    for b in range(nblk):
        for g in range(ATT_KV_HEADS):
            qs.append(jnp.concatenate(
                [q[b * blk:(b + 1) * blk, (g * rep + j) * hd:(g * rep + j + 1) * hd] for j in range(rep)], axis=0))
            ks.append(kv[b * blk:(b + 2) * blk, g * hd:(g + 1) * hd])
            vs.append(kv[b * blk:(b + 2) * blk, nkv + g * hd:nkv + (g + 1) * hd])
    s = _bdot(jnp.stack(qs), jnp.stack(ks), P_ATT, nt=True)
    s = s.reshape(nblk, ATT_KV_HEADS, rep * blk, 2 * blk)
    s = jnp.where(in_win, s - bias, -jnp.inf)
    first_ok = ki >= blk * (1 - (i > 0).astype(jnp.int32))
    s = jnp.concatenate([jnp.where(first_ok, s[0:1], -jnp.inf), s[1:]], axis=0)
    m = jnp.maximum(jnp.max(s, axis=-1, keepdims=True), sink)
    p = jnp.exp(s - m)
    den = jnp.sum(p, axis=-1, keepdims=True) + jnp.exp(sink - m)
    o = _bdot(p.reshape(nblk * ATT_KV_HEADS, rep * blk, 2 * blk), jnp.stack(vs), P_ATT)
    o = o.reshape(nblk, ATT_KV_HEADS, rep * blk, hd) / den
    for b in range(nblk):
        o_ref[b * blk:(b + 1) * blk, :] = jnp.concatenate(
            [o[b, g, j * blk:(j + 1) * blk, :] for g in range(ATT_KV_HEADS) for j in range(rep)], axis=-1)


def _attention(pb, sinks):
    t = pb.shape[0]
    tq = _tile(t, 2048)
    nb = tq // ATT_BLOCK
    return pl.pallas_call(
        functools.partial(_attn_kernel, tq=tq),
        out_shape=jax.ShapeDtypeStruct((t, GROUP_W), F32),
        grid=(t // tq,),
        in_specs=[
            pl.BlockSpec((tq, N_B), lambda i: (i, 0)),
            pl.BlockSpec((ATT_BLOCK, 2 * ATT_KV_HEADS * HEAD_DIM), lambda i: (jnp.maximum(i * nb - 1, 0), 1)),
            _full((1, ATT_HEADS)),
        ],
        out_specs=pl.BlockSpec((tq, GROUP_W), lambda i: (i, 0)),
        compiler_params=_cparams(("parallel",)),
        name="swa_attention",
    )(pb, pb, sinks)


def _rwkv_kernel(*refs, tt, has_vres):
    if has_vres:
        (pd_ref, prev_ref, vf_ref, tril_ref, mu_ref, w0_ref, w2_ref, a0_ref, a2_ref, g2_ref, kk_ref, ka_ref,
         rk_ref, lng_ref, lnb_ref, v0_ref, v1_ref, v2_ref, y_ref, st_ref, yr_s) = refs
    else:
        (pd_ref, prev_ref, tril_ref, mu_ref, w0_ref, w2_ref, a0_ref, a2_ref, g2_ref, kk_ref, ka_ref,
         rk_ref, lng_ref, lnb_ref, y_ref, vout_ref, st_ref, yr_s) = refs
    i = pl.program_id(0)
    nh, hd, gw = RWKV_HEADS, RWKV_HEAD, GROUP_W
    @pl.when(i == 0)
    def _():
        st_ref[...] = jnp.zeros_like(st_ref)

    hr = lax.broadcasted_iota(jnp.int32, (2 * gw, gw), 0) % gw // hd
    hc = lax.broadcasted_iota(jnp.int32, (2 * gw, gw), 1) // hd
    hsum2 = jnp.where(hr == hc, 1.0, 0.0).astype(BF16)

    def head_sum(t2):
        return _mm(jnp.concatenate(_split(t2), axis=1), hsum2, _NN)

    p = pd_ref[...]
    row = lax.broadcasted_iota(jnp.int32, (tt, 1), 0)
    prev_row = prev_ref[SUBLANES - 1:SUBLANES, :] * (i > 0).astype(F32)
    shifted = jnp.where(row == 0, prev_row, pltpu.roll(p, 1, axis=0))
    z = p + (shifted - p) * mu_ref[...]
    r = z[:, 0:gw]
    k = z[:, gw:2 * gw]
    v = z[:, 2 * gw:3 * gw]
    o = 3 * gw
    wd = z[:, o:o + LORA_W]
    ad = z[:, o + LORA_W:o + LORA_W + LORA_A]
    gd = z[:, o + LORA_W + LORA_A:o + LORA_W + LORA_A + LORA_G]
    wpre = -(w0_ref[...] + _dot(jnp.tanh(wd), w2_ref[...], P_LORA))
    softplus = jnp.maximum(wpre, 0.0) + jnp.log(1.0 + jnp.exp(-jnp.abs(wpre)))
    lw = -jnp.exp(-softplus - 0.5)
    a = _sigmoid(a0_ref[...] + _dot(ad, a2_ref[...], P_LORA))
    g = _dot(_sigmoid(gd), g2_ref[...], P_LORA)
    if has_vres:
        mix = _sigmoid(v0_ref[...] + _dot(_dot(v, v1_ref[...], P_LORA), v2_ref[...], P_LORA))
        v = v + (vf_ref[...] - v) * mix
    else:
        vout_ref[...] = v
    kk = k * kk_ref[...]
    kk = kk / jnp.maximum(jnp.sqrt(head_sum(kk * kk)), 1e-12)
    k = k * (1.0 + (a - 1.0) * ka_ref[...])
    nch = tt // CHUNK
    L = CHUNK
    npair = nh // 2
    pw = 2 * hd
    ti = lax.broadcasted_iota(jnp.int32, (L, pw), 0)
    tl = lax.broadcasted_iota(jnp.int32, (L, pw), 1) % hd
    low_strict = ti > tl
    low_incl = ti >= tl
    eye = (ti == tl).astype(F32)
    first_head = lax.broadcasted_iota(jnp.int32, (1, pw), 1) < hd

    def same_block(b):
        return (ti // b) == (tl // b)

    def to_batch(t2, c0, nc):
        return jnp.concatenate([t2[c0 * L:(c0 + nc) * L, q * pw:(q + 1) * pw].reshape(nc, L, pw)
                                for q in range(npair)], axis=0)

    def bd(x):
        zero = jnp.zeros_like(x)
        return jnp.concatenate([jnp.where(first_head, x, zero), jnp.where(first_head, zero, x)], axis=-2)

    def fold(x):
        return x[:, :hd] + x[:, hd:]

    def c16(x):
        return x.astype(BF16)

    lw_hi, lw_lo = _split(lw)
    cl = _mm(tril_ref[...], lw_hi, _NN) + _mm(tril_ref[...], lw_lo, _NN)
    cl3 = cl.reshape(nch, L, gw)
    cl_end = cl3[:, L - 1:L, :]
    e_end = jnp.exp(cl_end - cl3).reshape(tt, gw)
    e_neg = jnp.exp(-cl)
    bb = kk * a
    at_t = -kk * jnp.exp(cl - lw)
    bh_t = bb * e_neg
    kh_t = k * e_neg
    rt_t = r * jnp.exp(cl)
    be_t = bb * e_end
    ke_t = k * e_end
    p_end = jnp.exp(cl_end)

    def chunk_operators(c0, nc):
        at = c16(to_batch(at_t, c0, nc))
        bh = c16(to_batch(bh_t, c0, nc))
        kh = c16(to_batch(kh_t, c0, nc))
        rt = to_batch(rt_t, c0, nc)
        be = to_batch(be_t, c0, nc)
        ke = to_batch(ke_t, c0, nc)
        vb = bd(c16(to_batch(v, c0, nc)))
        pe = jnp.concatenate([p_end[c0:c0 + nc, :, q * pw:(q + 1) * pw] for q in range(npair)], axis=0)
        lhs = jnp.concatenate([at, c16(rt)], axis=1)
        g_b = _bmm(lhs, bd(bh), True)
        g_k = _bmm(lhs, bd(kh), True)
        a_ab = jnp.where(low_strict, g_b[:, :L], 0.0)
        b_rb = jnp.where(low_incl, g_b[:, L:], 0.0)
        a_ak = jnp.where(low_strict, g_k[:, :L], 0.0)
        b_rk = jnp.where(low_incl, g_k[:, L:], 0.0)
        tinv = eye + jnp.where(same_block(2), a_ab, 0.0)
        b = 2
        while b < L:
            e = jnp.where(same_block(2 * b) & ~same_block(b), a_ab, 0.0)
            xe = _bmm(c16(tinv), bd(c16(e)), False)
            tinv = tinv + _bmm(c16(xe), bd(c16(tinv)), False)
            b *= 2
        av = _bmm(c16(a_ak), vb, False)
        wu = _bmm(c16(tinv), jnp.concatenate([bd(at), bd(c16(av))], axis=2), False)
        rhs = jnp.concatenate([bd(c16(wu[:, :, :pw])), bd(c16(wu[:, :, pw:]))], axis=2)
        lhs_b = jnp.concatenate([c16(b_rb), c16(jnp.swapaxes(bd(be), 1, 2))], axis=1)
        lhs_k = jnp.concatenate([c16(b_rk), c16(jnp.swapaxes(bd(ke), 1, 2))], axis=1)
        o_wu = _bmm(lhs_b, rhs, False)
        o_v = _bmm(lhs_k, vb, False)
        rq = rt + o_wu[:, :L, :pw]
        y0 = o_wu[:, :L, pw:] + o_v[:, :L]
        m2 = eye * pe + fold(o_wu[:, L:, :pw])
        c2 = fold(o_wu[:, L:, pw:] + o_v[:, L:])
        return jnp.concatenate([rq, m2], axis=1), jnp.concatenate([y0, c2], axis=1)

    def state_walk(zs, ops, c0, nc):
        rqm, y0c = ops
        for c in range(nc):
            for q in range(npair):
                b = q * nc + c
                yz = _dot(rqm[b], bd(zs[q]), P_STATE) + y0c[b]
                yr_s[(c0 + c) * L:(c0 + c + 1) * L, q * pw:(q + 1) * pw] = yz[:L]
                zs[q] = yz[L:]
        return zs

    zs = state_walk([st_ref[q] for q in range(npair)], chunk_operators(0, nch), 0, nch)
    for q in range(npair):
        st_ref[q] = zs[q]

    y = yr_s[...]
    inv_n = 1.0 / hd
    mean = head_sum(y) * inv_n
    yc = y - mean
    var = head_sum(yc * yc) * inv_n
    yn = yc * lax.rsqrt(var + RWKV_GN_EPS) * lng_ref[...] + lnb_ref[...]
    bonus = head_sum(r * k * rk_ref[...]) * v
    y_ref[...] = (yn + bonus) * g


def _rwkv(pd, v_first, prm):
    t = pd.shape[0]
    tt = _tile(t, 512)
    nch = tt // CHUNK
    has_vres = v_first is not None
    gw = GROUP_W
    row_blk = lambda n: pl.BlockSpec((tt, n), lambda i: (i, 0))
    vec = _full((1, gw))
    in_specs = [row_blk(N_D),
                pl.BlockSpec((SUBLANES, N_D), lambda i: (jnp.maximum(i * (tt // SUBLANES) - 1, 0), 0))]
    args = [pd, pd]
    if has_vres:
        in_specs.append(row_blk(gw))
        args.append(v_first)
    tpos = jnp.arange(tt)
    tril = ((tpos[:, None] // CHUNK == tpos[None, :] // CHUNK) & (tpos[:, None] >= tpos[None, :])).astype(BF16)
    in_specs.append(_full((tt, tt)))
    args.append(tril)
    in_specs += [_full((1, N_D)), vec, _full((LORA_W, gw)), vec, _full((LORA_A, gw)), _full((LORA_G, gw)),
                 vec, vec, vec, vec, vec]
    args += [prm["mu"], prm["w0"], prm["w2"], prm["a0"], prm["a2"], prm["g2"], prm["k_k"], prm["k_a"],
             prm["r_k"], prm["ln_g"], prm["ln_b"]]
    if has_vres:
        in_specs += [vec, _full((gw, LORA_V)), _full((LORA_V, gw))]
        args += [prm["v0"], prm["v1"], prm["v2"]]
        out_shape = jax.ShapeDtypeStruct((t, gw), F32)
        out_specs = row_blk(gw)
    else:
        out_shape = [jax.ShapeDtypeStruct((t, gw), F32)] * 2
        out_specs = [row_blk(gw)] * 2
    npair, pw = RWKV_HEADS // 2, 2 * RWKV_HEAD
    scratch = [pltpu.VMEM((npair, RWKV_HEAD, pw), F32), pltpu.VMEM((tt, gw), F32)]
    return pl.pallas_call(
        functools.partial(_rwkv_kernel, tt=tt, has_vres=has_vres),
        out_shape=out_shape,
        grid=(t // tt,),
        in_specs=in_specs,
        out_specs=out_specs,
        scratch_shapes=scratch,
        compiler_params=_cparams(("arbitrary",)),
        name="rwkv7",
    )(*args)


def _route_record(lg):
    lane = lax.broadcasted_iota(jnp.int32, (lg.shape[0], LANES), 1)
    lanef = lane.astype(F32)
    gl = jnp.where(lane < N_GROUPS, lg[:, 0:LANES], -jnp.inf)
    gmax = jnp.max(gl, axis=-1, keepdims=True)
    gp = 1.0 / jnp.sum(jnp.exp(gl - gmax), axis=-1, keepdims=True)
    gi = jnp.min(jnp.where(gl == gmax, lanef, float(LANES)), axis=-1, keepdims=True)
    lo = gi * EXP_PER_GROUP
    in_grp = (lanef >= lo) & (lanef < lo + EXP_PER_GROUP)
    el = jnp.where(in_grp, lg[:, LANES:2 * LANES], -jnp.inf)
    m1 = jnp.max(el, axis=-1, keepdims=True)
    i1 = jnp.min(jnp.where(el == m1, lanef, float(LANES)), axis=-1, keepdims=True)
    el2 = jnp.where(lanef == i1, -jnp.inf, el)
    m2 = jnp.max(el2, axis=-1, keepdims=True)
    i2 = jnp.min(jnp.where(el2 == m2, lanef, float(LANES)), axis=-1, keepdims=True)
    e2 = jnp.exp(m2 - m1)
    w_top = gp / (1.0 + e2)
    return jnp.where(lanef == i1 - lo, w_top, jnp.where(lanef == i2 - lo, w_top * e2,
                     jnp.where(lane == ROUTE_GROUP_LANE, gi, 0.0)))


def _outproj_kernel(ya_ref, yb_ref, yc_ref, yd_ref, x_ref, wo_ref, gt_ref, g_ref, sc_ref, sh_ref, wr_ref, br_ref,
                    x1_ref, h2_ref, gate_ref, *, tm):
    sub = tm // OUT_SUB
    for blk in range(OUT_SUB):
        rows = slice(blk * sub, (blk + 1) * sub)
        ycat = jnp.concatenate([ya_ref[rows, :], yb_ref[rows, :], yc_ref[rows, :], yd_ref[rows, :]], axis=1)
        x1 = x_ref[rows, :] + gt_ref[...] * _dot_w(ycat, wo_ref[...], P_OUT)
        x1_ref[rows, :] = x1
        h2 = _modulated_norm(x1, g_ref[...], sc_ref[...], sh_ref[...])
        h2_ref[rows, :] = h2.astype(h2_ref.dtype)
        gate_ref[rows, :] = _route_record(_dot_w(h2, wr_ref[...], P_ROUTER) + br_ref[...])


def _out_proj(ya, yb, yc, yd, x, wo_parts, gt, g, sc, sh, w_router, b_router):
    t = x.shape[0]
    tm = _tile(t, 1024)
    row_blk = lambda n: pl.BlockSpec((tm, n), lambda i: (i, 0))
    vec = _full((1, D_MODEL))
    return pl.pallas_call(
        functools.partial(_outproj_kernel, tm=tm),
        out_shape=[jax.ShapeDtypeStruct((t, D_MODEL), F32), jax.ShapeDtypeStruct((t, D_MODEL), BF16),
                   jax.ShapeDtypeStruct((t, LANES), F32)],
        grid=(t // tm,),
        in_specs=[row_blk(GROUP_W)] * 4 + [row_blk(D_MODEL), _full(wo_parts.shape), vec, vec, vec, vec,
                                           _full(w_router.shape), _full((1, 2 * LANES))],
        out_specs=[row_blk(D_MODEL), row_blk(D_MODEL), row_blk(LANES)],
        compiler_params=_cparams(("parallel",)),
        name="out_proj_router",
    )(ya, yb, yc, yd, x, wo_parts, gt, g, sc, sh, w_router, b_router)


def _moe_kernel(h_ref, route_ref, tril_ref, wg_ref, wu_ref, wd_ref, x1_ref, gt_ref, fg_ref, o_ref,
                acc_ref, col_ref, row_ref, cnt_ref, *, final, tm, cap, cap_extra):
    g = pl.program_id(1)
    lane = lax.broadcasted_iota(jnp.int32, (tm, LANES), 1)

    @pl.when(g == 0)
    def _():
        acc_ref[...] = jnp.zeros_like(acc_ref)
        gi = route_ref[:, ROUTE_GROUP_LANE:ROUTE_GROUP_LANE + 1]
        onehot = lane.astype(F32) == gi
        cum = _mm(tril_ref[...], jnp.where(onehot, 1.0, 0.0).astype(BF16), _NN)
        rank = jnp.sum(jnp.where(onehot, cum, 0.0), axis=-1, keepdims=True) - 1.0
        cnt_ref[0:1, :] = cum[tm - 1:tm, :]
        cnt_ref[1:2, :] = cum[tm // 2 - 1:tm // 2, :]
        rec = jnp.where(lane == 0, rank, jnp.where(lane == 1, gi, 0.0))
        col_ref[...] = rec
        row_ref[...] = rec.T[0:SUBLANES, :]

    gf = g.astype(F32)
    this_group = lane[0:1, :] == g
    count = jnp.sum(jnp.where(this_group, cnt_ref[0:1, :], 0.0)).astype(jnp.int32)
    count_half = jnp.sum(jnp.where(this_group, cnt_ref[1:2, :], 0.0)).astype(jnp.int32)
    sel_col = jnp.where(col_ref[:, 1:2] == gf, col_ref[:, 0:1], -1.0)
    sel_row = jnp.where(row_ref[1:2, :] == gf, row_ref[0:1, :], -1.0)
    hx = jnp.concatenate([h_ref[...], route_ref[...].astype(BF16)], axis=1)

    def block(base, rows, t0=0):
        base = base.astype(F32)
        nt = tm - t0
        slot_r = lax.broadcasted_iota(jnp.int32, (rows, nt), 0).astype(F32) + base
        gather = jnp.where(slot_r == sel_row[:, t0:], 1.0, 0.0).astype(BF16)
        hs = _mm(gather, hx[t0:, :], _NN)
        h = hs[:, 0:D_MODEL].astype(BF16)
        slot_gate = hs[:, D_MODEL:D_MODEL + LANES]
        hg = _mm(h, wg_ref[...], _NN)
        hu = _mm(h, wu_ref[...], _NN)
        hu = jnp.concatenate([hu[:, j * D_EXPERT:(j + 1) * D_EXPERT] * slot_gate[:, j:j + 1]
                              for j in range(EXP_PER_GROUP)], axis=1)
        act = hg * _sigmoid(hg) * hu
        ys = _mm(act.astype(BF16), wd_ref[...], _NN)
        slot_c = lax.broadcasted_iota(jnp.int32, (nt, rows), 1).astype(F32) + base
        scatter = jnp.where(slot_c == sel_col[t0:, :], 1.0, 0.0).astype(BF16)
        acc_ref[t0:, :] += _mm(scatter, ys.astype(BF16), _NN)

    @pl.when(count > 0)
    def _():
        block(jnp.int32(0), cap)

    n_extra = (jnp.maximum(count - cap, 0) + (cap_extra - 1)) // cap_extra
    late = count_half <= cap

    def extra_late(r, carry):
        block(cap + r * cap_extra, cap_extra, t0=tm // 2)
        return carry

    def extra_any(r, carry):
        block(cap + r * cap_extra, cap_extra)
        return carry

    lax.fori_loop(0, jnp.where(late, n_extra, 0), extra_late, 0)
    lax.fori_loop(0, jnp.where(late, 0, n_extra), extra_any, 0)

    @pl.when(g == pl.num_programs(1) - 1)
    def _():
        out = x1_ref[...] + gt_ref[...] * acc_ref[...]
        if final:
            out = out * lax.rsqrt(jnp.mean(out * out, axis=-1, keepdims=True) + NORM_EPS) * fg_ref[...]
        o_ref[...] = out


def _moe(h2, route, wg, wu, wd, x1, gt, final_g, final):
    t = h2.shape[0]
    tm = _tile(t, MOE_TILE)
    cap = tm // N_GROUPS
    cap_extra = cap // 2
    nw = EXP_PER_GROUP * D_EXPERT
    vec = _full((1, D_MODEL))
    tpos = jnp.arange(tm)
    tril = (tpos[:, None] >= tpos[None, :]).astype(BF16)
    return pl.pallas_call(
        functools.partial(_moe_kernel, final=final, tm=tm, cap=cap, cap_extra=cap_extra),
        out_shape=jax.ShapeDtypeStruct((t, D_MODEL), F32),
        grid=(t // tm, N_GROUPS),
        in_specs=[
            pl.BlockSpec((tm, D_MODEL), lambda i, g: (i, 0)),
            pl.BlockSpec((tm, LANES), lambda i, g: (i, 0)),
            _full((tm, tm)),
            pl.BlockSpec((D_MODEL, nw), lambda i, g: (0, g)),
            pl.BlockSpec((D_MODEL, nw), lambda i, g: (0, g)),
            pl.BlockSpec((nw, D_MODEL), lambda i, g: (g, 0)),
            pl.BlockSpec((tm, D_MODEL), lambda i, g: (i, 0)),
            vec, vec,
        ],
        out_specs=pl.BlockSpec((tm, D_MODEL), lambda i, g: (i, 0)),
        scratch_shapes=[pltpu.VMEM((tm, D_MODEL), F32), pltpu.VMEM((tm, LANES), F32),
                        pltpu.VMEM((SUBLANES, tm), F32), pltpu.VMEM((2, LANES), F32)],
        compiler_params=_cparams(("parallel", "arbitrary")),
        name="moe_experts",
    )(h2, route, tril, wg, wu, wd, x1, gt, final_g)


def kernel(x, c, w_ada, b_ada, norm1_g, norm2_g, w_in, w_out, pool_w, pool_scale, attn_sinks, conv_w, rwkv_mu,
           rwkv_w0, rwkv_w2, rwkv_a0, rwkv_a2, rwkv_g2, rwkv_k_k, rwkv_k_a, rwkv_r_k, rwkv_ln_g, rwkv_ln_b,
           rwkv_v0, rwkv_v1, rwkv_v2, moe_w_grp, moe_b_grp, moe_w_exp, moe_b_exp, moe_w_gate, moe_w_up,
           moe_w_down, final_g):
    bsz, t, d = x.shape
    assert bsz == 1 and d == D_MODEL
    depth = w_ada.shape[0]
    xs = x.reshape(t, d)
    mod = _ada_mod(c, w_ada, b_ada)
    v_first = None
    row = lambda a: a.reshape(1, -1)
    for l in range(depth):
        sh1, sc1, gt1, sh2, sc2, gt2 = [mod[l, :, j * d:(j + 1) * d] for j in range(6)]
        wp = jax.scipy.linalg.block_diag(*[pool_w[l, gidx] for gidx in range(len(POOL_WINDOWS))])
        ya, pb, yc, pd = _in_proj(xs, row(norm1_g[l]), sc1, sh1, _split_w(w_in[l], P_IN), _split_w(wp, P_POOL),
                                  row(pool_scale[l]), conv_w[l])
        yb = _attention(pb, row(attn_sinks[l]))
        prm = dict(mu=row(rwkv_mu[l]), w0=row(rwkv_w0[l]), w2=rwkv_w2[l], a0=row(rwkv_a0[l]), a2=rwkv_a2[l],
                   g2=rwkv_g2[l], k_k=row(rwkv_k_k[l]), k_a=row(rwkv_k_a[l]), r_k=row(rwkv_r_k[l]),
                   ln_g=row(rwkv_ln_g[l]), ln_b=row(rwkv_ln_b[l]))
        if l == 0:
            yd, v_first = _rwkv(pd, None, prm)
        else:
            prm.update(v0=row(rwkv_v0[l - 1]), v1=rwkv_v1[l - 1], v2=rwkv_v2[l - 1])
            yd = _rwkv(pd, v_first, prm)
        lane_pad = lambda a: jnp.pad(a, ((0, 0), (0, LANES - a.shape[1])))
        w_router = jnp.concatenate([lane_pad(moe_w_grp[l]), lane_pad(moe_w_exp[l])], axis=1)
        b_router = jnp.concatenate([lane_pad(row(moe_b_grp[l])), lane_pad(row(moe_b_exp[l]))], axis=1)
        x1, h2, gates = _out_proj(ya, yb, yc, yd, xs, _split_w(w_out[l], P_OUT), gt1, row(norm2_g[l]), sc2, sh2,
                                  _split_w(w_router, P_ROUTER), b_router)
        wg = jnp.transpose(moe_w_gate[l], (1, 0, 2)).reshape(d, N_EXPERTS * D_EXPERT).astype(BF16)
        wu = jnp.transpose(moe_w_up[l], (1, 0, 2)).reshape(d, N_EXPERTS * D_EXPERT).astype(BF16)
        wd = moe_w_down[l].astype(BF16).reshape(N_EXPERTS * D_EXPERT, d)
        xs = _moe(h2, gates, wg, wu, wd, x1, gt2, row(final_g), final=(l == depth - 1))
    return xs.reshape(bsz, t, d)
```

```python
import functools

import jax
import jax.numpy as jnp
from jax import lax
from jax.experimental import pallas as pl
from jax.experimental.pallas import tpu as pltpu

F32 = jnp.float32
BF16 = jnp.bfloat16

D_MODEL = 1024
GROUP_W = 256
NORM_EPS = 1e-6
POOL_WINDOWS = (2, 4, 8, 16)
POOL_CG = 64
HEAD_DIM = 64
ATT_HEADS = 4
ATT_KV_HEADS = 2
ATT_BLOCK = 128
CONV_W = 3
RWKV_HEAD = 64
RWKV_HEADS = 4
LORA_W = 32
LORA_A = 32
LORA_G = 64
LORA_V = 32
RWKV_GN_EPS = 64e-5
N_A = GROUP_W
N_B = (ATT_HEADS + 2 * ATT_KV_HEADS) * HEAD_DIM
N_C = 3 * GROUP_W
N_D = 3 * GROUP_W + LORA_W + LORA_A + LORA_G
N_GROUPS = 4
EXP_PER_GROUP = 8
N_EXPERTS = 32
D_EXPERT = 128

LANES = 128
SUBLANES = 8
VMEM_LIMIT = 56 * 1024 * 1024

CHUNK = 64
POOL_HALO = 16
CONV_HALO = 8
OUT_SUB = 2
ROUTE_GROUP_LANE = 8
MOE_TILE = 1024

P_IN = 1
P_OUT = 1
P_POOL = 1
P_ATT = 1
P_LORA = 1
P_STATE = 1
P_ROUTER = 3


def _split(a):
    hi = a.astype(BF16)
    lo = (a - hi.astype(F32)).astype(BF16)
    return hi, lo


def _mm(a, b, dims):
    return lax.dot_general(a, b, (dims, ((), ())), preferred_element_type=F32)


_NN = ((1,), (0,))
_NT = ((1,), (1,))


def _dot(a, b, passes=1, dims=_NN):
    if passes == 1:
        return _mm(a.astype(BF16), b.astype(BF16), dims)
    a0, a1 = _split(a)
    b0, b1 = _split(b)
    return _mm(a0, b0, dims) + (_mm(a0, b1, dims) + _mm(a1, b0, dims))


def _bmm(a, b, nt):
    dims = (((2,), (2 if nt else 1,)), ((0,), (0,)))
    return lax.dot_general(a, b, dims, preferred_element_type=F32)


def _bdot(a, b, passes=1, nt=False):
    if passes == 1:
        return _bmm(a.astype(BF16), b.astype(BF16), nt)
    a0, a1 = _split(a)
    b0, b1 = _split(b)
    return _bmm(a0, b0, nt) + (_bmm(a0, b1, nt) + _bmm(a1, b0, nt))


def _lhs_w(a, passes):
    if passes == 1:
        return a.astype(BF16)
    a0, a1 = _split(a)
    return jnp.concatenate([a0, a0, a1], axis=1)


def _dot_w(a, w_cat, passes):
    return _mm(_lhs_w(a, passes), w_cat, _NN)


def _split_w(w, passes):
    hi = w.astype(BF16)
    if passes == 1:
        return hi
    lo = (w - hi.astype(F32)).astype(BF16)
    return jnp.concatenate([hi, lo, hi], axis=0)


def _sigmoid(x):
    return 1.0 / (1.0 + jnp.exp(-x))


def _cparams(sem):
    return pltpu.CompilerParams(dimension_semantics=sem, vmem_limit_bytes=VMEM_LIMIT)


def _full(shape):
    return pl.BlockSpec(shape, lambda *_: (0,) * len(shape))


def _tile(n, pref):
    t = min(n, pref)
    assert n % t == 0, (n, t)
    return t


def _mod_kernel(c_ref, w_ref, b_ref, o_ref):
    c = c_ref[...]
    cond = c * _sigmoid(c)
    o_ref[...] = _dot(cond, w_ref[...], 3) + b_ref[...]


def _ada_mod(c, w_ada, b_ada):
    depth = w_ada.shape[0]
    c8 = jnp.broadcast_to(c, (SUBLANES, D_MODEL))
    out = pl.pallas_call(
        _mod_kernel,
        out_shape=jax.ShapeDtypeStruct((depth, SUBLANES, 6 * D_MODEL), F32),
        grid=(depth, 6),
        in_specs=[
            pl.BlockSpec((SUBLANES, D_MODEL), lambda l, j: (0, 0)),
            pl.BlockSpec((None, D_MODEL, D_MODEL), lambda l, j: (l, 0, j)),
            pl.BlockSpec((None, 1, D_MODEL), lambda l, j: (l, 0, j)),
        ],
        out_specs=pl.BlockSpec((None, SUBLANES, D_MODEL), lambda l, j: (l, 0, j)),
        compiler_params=_cparams(("arbitrary", "arbitrary")),
        name="ada_mod",
    )(c8, w_ada, b_ada.reshape(depth, 1, 6 * D_MODEL))
    return out[:, 0:1, :]


def _modulated_norm(x, g, sc, sh):
    y = x * lax.rsqrt(jnp.mean(x * x, axis=-1, keepdims=True) + NORM_EPS) * g
    return y * (1.0 + sc) + sh


def _shift_rows(ext, n):
    return pltpu.roll(ext, n, axis=0)


def _pool_mixer(u, halo, wp, scale, t0):
    tt = u.shape[0]
    ext = jnp.concatenate([halo, u], axis=0)
    s2 = ext + _shift_rows(ext, 1)
    s4 = s2 + _shift_rows(s2, 2)
    s8 = s4 + _shift_rows(s4, 4)
    s16 = s8 + _shift_rows(s8, 8)
    grp = lax.broadcasted_iota(jnp.int32, (tt, GROUP_W), 1) // POOL_CG
    tpos = lax.broadcasted_iota(jnp.int32, (tt, GROUP_W), 0) + (t0 + 1)
    wsum = jnp.where(grp == 0, s2[POOL_HALO:], jnp.where(grp == 1, s4[POOL_HALO:],
                     jnp.where(grp == 2, s8[POOL_HALO:], s16[POOL_HALO:])))
    win = jnp.where(grp == 0, 2, jnp.where(grp == 1, 4, jnp.where(grp == 2, 8, 16)))
    cnt = jnp.minimum(tpos, win).astype(F32)
    return _dot_w(wsum / cnt - u, wp, P_POOL) * scale


def _conv_mixer(bg, z, z_halo, cw):
    zext = jnp.concatenate([z_halo, z], axis=0)
    y = (cw[2:3, :] * z + cw[1:2, :] * _shift_rows(zext, 1)[CONV_HALO:]
         + cw[0:1, :] * _shift_rows(zext, 2)[CONV_HALO:])
    return bg * y


def _inproj_kernel(x_ref, g_ref, sc_ref, sh_ref, w_ref, wp_ref, scale_ref, cw_ref,
                   ya_ref, pb_ref, yc_ref, pd_ref, u_tail, z_tail, *, passes, tm):
    i = pl.program_id(0)

    @pl.when(i == 0)
    def _():
        u_tail[...] = jnp.zeros_like(u_tail)
        z_tail[...] = jnp.zeros_like(z_tail)

    h = _modulated_norm(x_ref[...], g_ref[...], sc_ref[...], sh_ref[...])
    lhs = _lhs_w(h, passes)
    proj = lambda col, width: _mm(lhs, w_ref[:, col:col + width], _NN)
    u = proj(0, N_A)
    pc = proj(N_A + N_B, N_C)
    ya_ref[...] = _pool_mixer(u, u_tail[...], wp_ref[...], scale_ref[...], i * tm)
    u_tail[...] = u[tm - POOL_HALO:, :]
    z = pc[:, GROUP_W:2 * GROUP_W] * pc[:, 2 * GROUP_W:3 * GROUP_W]
    yc_ref[...] = _conv_mixer(pc[:, 0:GROUP_W], z, z_tail[...], cw_ref[...])
    z_tail[...] = z[tm - CONV_HALO:, :]
    pb_ref[...] = proj(N_A, N_B)
    pd_ref[...] = proj(N_A + N_B + N_C, N_D)


def _in_proj(x, g, sc, sh, w_parts, wp_parts, pool_scale, conv_w):
    t = x.shape[0]
    tm = _tile(t, 1024)
    vec = _full((1, D_MODEL))
    widths = (GROUP_W, N_B, GROUP_W, N_D)
    return pl.pallas_call(
        functools.partial(_inproj_kernel, passes=P_IN, tm=tm),
        out_shape=[jax.ShapeDtypeStruct((t, n), F32) for n in widths],
        grid=(t // tm,),
        in_specs=[pl.BlockSpec((tm, D_MODEL), lambda i: (i, 0)), vec, vec, vec, _full(w_parts.shape),
                  _full(wp_parts.shape), _full((1, GROUP_W)), _full((CONV_W, GROUP_W))],
        out_specs=[pl.BlockSpec((tm, n), lambda i: (i, 0)) for n in widths],
        scratch_shapes=[pltpu.VMEM((POOL_HALO, GROUP_W), F32), pltpu.VMEM((CONV_HALO, GROUP_W), F32)],
        compiler_params=_cparams(("arbitrary",)),
        name="in_proj_pool_conv",
    )(x, g, sc, sh, w_parts, wp_parts, pool_scale, conv_w)


def _attn_kernel(cur_ref, prev_ref, sink_ref, o_ref, *, tq):
    i = pl.program_id(0)
    nq = ATT_HEADS * HEAD_DIM
    nkv = ATT_KV_HEADS * HEAD_DIM
    blk = ATT_BLOCK
    hd = HEAD_DIM
    rep = ATT_HEADS // ATT_KV_HEADS
    nblk = tq // blk
    ri = lax.broadcasted_iota(jnp.int32, (rep * blk, 2 * blk), 0)
    ki = lax.broadcasted_iota(jnp.int32, (rep * blk, 2 * blk), 1)
    dist = (ri % blk) + blk - ki
    in_win = (dist >= 0) & (dist < blk)
    distf = dist.astype(F32)
    head_col = lax.broadcasted_iota(jnp.int32, (rep * blk, 1), 0) // blk
    bias, sink = [], []
    for g in range(ATT_KV_HEADS):
        slope_g = jnp.zeros((rep * blk, 1), F32)
        sink_g = jnp.zeros((rep * blk, 1), F32)
        for j in range(rep):
            h = g * rep + j
            slope_g = jnp.where(head_col == j, 2.0 ** (-8.0 * (h + 1) / ATT_HEADS), slope_g)
            sink_g = jnp.where(head_col == j, sink_ref[:, h:h + 1], sink_g)
        bias.append(slope_g * distf)
        sink.append(sink_g)
    bias = jnp.stack(bias)[None]
    sink = jnp.stack(sink)[None]

    q = cur_ref[:, 0:nq] * (hd ** -0.5)
    kv = jnp.concatenate([prev_ref[...], cur_ref[:, nq:nq + 2 * nkv]], axis=0)
    qs, ks, vs = [], [], []
    for b in range(nblk):
        for g in range(ATT_KV_HEADS):
            qs.append(jnp.concatenate(
                [q[b * blk:(b + 1) * blk, (g * rep + j) * hd:(g * rep + j + 1) * hd] for j in range(rep)], axis=0))
            ks.append(kv[b * blk:(b + 2) * blk, g * hd:(g + 1) * hd])
            vs.append(kv[b * blk:(b + 2) * blk, nkv + g * hd:nkv + (g + 1) * hd])
    s = _bdot(jnp.stack(qs), jnp.stack(ks), P_ATT, nt=True)
    s = s.reshape(nblk, ATT_KV_HEADS, rep * blk, 2 * blk)
    s = jnp.where(in_win, s - bias, -jnp.inf)
    first_ok = ki >= blk * (1 - (i > 0).astype(jnp.int32))
    s = jnp.concatenate([jnp.where(first_ok, s[0:1], -jnp.inf), s[1:]], axis=0)
    m = jnp.maximum(jnp.max(s, axis=-1, keepdims=True), sink)
    p = jnp.exp(s - m)
    den = jnp.sum(p, axis=-1, keepdims=True) + jnp.exp(sink - m)
    o = _bdot(p.reshape(nblk * ATT_KV_HEADS, rep * blk, 2 * blk), jnp.stack(vs), P_ATT)
    o = o.reshape(nblk, ATT_KV_HEADS, rep * blk, hd) / den
    for b in range(nblk):
        o_ref[b * blk:(b + 1) * blk, :] = jnp.concatenate(
            [o[b, g, j * blk:(j + 1) * blk, :] for g in range(ATT_KV_HEADS) for j in range(rep)], axis=-1)


def _attention(pb, sinks):
    t = pb.shape[0]
    tq = _tile(t, 2048)
    nb = tq // ATT_BLOCK
    return pl.pallas_call(
        functools.partial(_attn_kernel, tq=tq),
        out_shape=jax.ShapeDtypeStruct((t, GROUP_W), F32),
        grid=(t // tq,),
        in_specs=[
            pl.BlockSpec((tq, N_B), lambda i: (i, 0)),
            pl.BlockSpec((ATT_BLOCK, 2 * ATT_KV_HEADS * HEAD_DIM), lambda i: (jnp.maximum(i * nb - 1, 0), 1)),
            _full((1, ATT_HEADS)),
        ],
        out_specs=pl.BlockSpec((tq, GROUP_W), lambda i: (i, 0)),
        compiler_params=_cparams(("parallel",)),
        name="swa_attention",
    )(pb, pb, sinks)


def _rwkv_kernel(*refs, tt, has_vres):
    if has_vres:
        (pd_ref, prev_ref, vf_ref, tril_ref, mu_ref, w0_ref, w2_ref, a0_ref, a2_ref, g2_ref, kk_ref, ka_ref,
         rk_ref, lng_ref, lnb_ref, v0_ref, v1_ref, v2_ref, y_ref, st_ref, yr_s) = refs
    else:
        (pd_ref, prev_ref, tril_ref, mu_ref, w0_ref, w2_ref, a0_ref, a2_ref, g2_ref, kk_ref, ka_ref,
         rk_ref, lng_ref, lnb_ref, y_ref, vout_ref, st_ref, yr_s) = refs
    i = pl.program_id(0)
    nh, hd, gw = RWKV_HEADS, RWKV_HEAD, GROUP_W
    @pl.when(i == 0)
    def _():
        st_ref[...] = jnp.zeros_like(st_ref)

    hr = lax.broadcasted_iota(jnp.int32, (2 * gw, gw), 0) % gw // hd
    hc = lax.broadcasted_iota(jnp.int32, (2 * gw, gw), 1) // hd
    hsum2 = jnp.where(hr == hc, 1.0, 0.0).astype(BF16)

    def head_sum(t2):
        return _mm(jnp.concatenate(_split(t2), axis=1), hsum2, _NN)

    p = pd_ref[...]
    row = lax.broadcasted_iota(jnp.int32, (tt, 1), 0)
    prev_row = prev_ref[SUBLANES - 1:SUBLANES, :] * (i > 0).astype(F32)
    shifted = jnp.where(row == 0, prev_row, pltpu.roll(p, 1, axis=0))
    z = p + (shifted - p) * mu_ref[...]
    r = z[:, 0:gw]
    k = z[:, gw:2 * gw]
    v = z[:, 2 * gw:3 * gw]
    o = 3 * gw
    wd = z[:, o:o + LORA_W]
    ad = z[:, o + LORA_W:o + LORA_W + LORA_A]
    gd = z[:, o + LORA_W + LORA_A:o + LORA_W + LORA_A + LORA_G]
    wpre = -(w0_ref[...] + _dot(jnp.tanh(wd), w2_ref[...], P_LORA))
    softplus = jnp.maximum(wpre, 0.0) + jnp.log(1.0 + jnp.exp(-jnp.abs(wpre)))
    lw = -jnp.exp(-softplus - 0.5)
    a = _sigmoid(a0_ref[...] + _dot(ad, a2_ref[...], P_LORA))
    g = _dot(_sigmoid(gd), g2_ref[...], P_LORA)
    if has_vres:
        mix = _sigmoid(v0_ref[...] + _dot(_dot(v, v1_ref[...], P_LORA), v2_ref[...], P_LORA))
        v = v + (vf_ref[...] - v) * mix
    else:
        vout_ref[...] = v
    kk = k * kk_ref[...]
    kk = kk / jnp.maximum(jnp.sqrt(head_sum(kk * kk)), 1e-12)
    k = k * (1.0 + (a - 1.0) * ka_ref[...])
    nch = tt // CHUNK
    L = CHUNK
    npair = nh // 2
    pw = 2 * hd
    ti = lax.broadcasted_iota(jnp.int32, (L, pw), 0)
    tl = lax.broadcasted_iota(jnp.int32, (L, pw), 1) % hd
    low_strict = ti > tl
    low_incl = ti >= tl
    eye = (ti == tl).astype(F32)
    first_head = lax.broadcasted_iota(jnp.int32, (1, pw), 1) < hd

    def same_block(b):
        return (ti // b) == (tl // b)

    def to_batch(t2, c0, nc):
        return jnp.concatenate([t2[c0 * L:(c0 + nc) * L, q * pw:(q + 1) * pw].reshape(nc, L, pw)
                                for q in range(npair)], axis=0)

    def bd(x):
        zero = jnp.zeros_like(x)
        return jnp.concatenate([jnp.where(first_head, x, zero), jnp.where(first_head, zero, x)], axis=-2)

    def fold(x):
        return x[:, :hd] + x[:, hd:]

    def c16(x):
        return x.astype(BF16)

    lw_hi, lw_lo = _split(lw)
    cl = _mm(tril_ref[...], lw_hi, _NN) + _mm(tril_ref[...], lw_lo, _NN)
    cl3 = cl.reshape(nch, L, gw)
    cl_end = cl3[:, L - 1:L, :]
    e_end = jnp.exp(cl_end - cl3).reshape(tt, gw)
    e_neg = jnp.exp(-cl)
    bb = kk * a
    at_t = -kk * jnp.exp(cl - lw)
    bh_t = bb * e_neg
    kh_t = k * e_neg
    rt_t = r * jnp.exp(cl)
    be_t = bb * e_end
    ke_t = k * e_end
    p_end = jnp.exp(cl_end)

    def chunk_operators(c0, nc):
        at = c16(to_batch(at_t, c0, nc))
        bh = c16(to_batch(bh_t, c0, nc))
        kh = c16(to_batch(kh_t, c0, nc))
        rt = to_batch(rt_t, c0, nc)
        be = to_batch(be_t, c0, nc)
        ke = to_batch(ke_t, c0, nc)
        vb = bd(c16(to_batch(v, c0, nc)))
        pe = jnp.concatenate([p_end[c0:c0 + nc, :, q * pw:(q + 1) * pw] for q in range(npair)], axis=0)
        lhs = jnp.concatenate([at, c16(rt)], axis=1)
        g_b = _bmm(lhs, bd(bh), True)
        g_k = _bmm(lhs, bd(kh), True)
        a_ab = jnp.where(low_strict, g_b[:, :L], 0.0)
        b_rb = jnp.where(low_incl, g_b[:, L:], 0.0)
        a_ak = jnp.where(low_strict, g_k[:, :L], 0.0)
        b_rk = jnp.where(low_incl, g_k[:, L:], 0.0)
        tinv = eye + jnp.where(same_block(2), a_ab, 0.0)
        b = 2
        while b < L:
            e = jnp.where(same_block(2 * b) & ~same_block(b), a_ab, 0.0)
            xe = _bmm(c16(tinv), bd(c16(e)), False)
            tinv = tinv + _bmm(c16(xe), bd(c16(tinv)), False)
            b *= 2
        av = _bmm(c16(a_ak), vb, False)
        wu = _bmm(c16(tinv), jnp.concatenate([bd(at), bd(c16(av))], axis=2), False)
        rhs = jnp.concatenate([bd(c16(wu[:, :, :pw])), bd(c16(wu[:, :, pw:]))], axis=2)
        lhs_b = jnp.concatenate([c16(b_rb), c16(jnp.swapaxes(bd(be), 1, 2))], axis=1)
        lhs_k = jnp.concatenate([c16(b_rk), c16(jnp.swapaxes(bd(ke), 1, 2))], axis=1)
        o_wu = _bmm(lhs_b, rhs, False)
        o_v = _bmm(lhs_k, vb, False)
        rq = rt + o_wu[:, :L, :pw]
        y0 = o_wu[:, :L, pw:] + o_v[:, :L]
        m2 = eye * pe + fold(o_wu[:, L:, :pw])
        c2 = fold(o_wu[:, L:, pw:] + o_v[:, L:])
        return jnp.concatenate([rq, m2], axis=1), jnp.concatenate([y0, c2], axis=1)

    def state_walk(zs, ops, c0, nc):
        rqm, y0c = ops
        for c in range(nc):
            for q in range(npair):
                b = q * nc + c
                yz = _dot(rqm[b], bd(zs[q]), P_STATE) + y0c[b]
                yr_s[(c0 + c) * L:(c0 + c + 1) * L, q * pw:(q + 1) * pw] = yz[:L]
                zs[q] = yz[L:]
        return zs

    zs = state_walk([st_ref[q] for q in range(npair)], chunk_operators(0, nch), 0, nch)
    for q in range(npair):
        st_ref[q] = zs[q]

    y = yr_s[...]
    inv_n = 1.0 / hd
    mean = head_sum(y) * inv_n
    yc = y - mean
    var = head_sum(yc * yc) * inv_n
    yn = yc * lax.rsqrt(var + RWKV_GN_EPS) * lng_ref[...] + lnb_ref[...]
    bonus = head_sum(r * k * rk_ref[...]) * v
    y_ref[...] = (yn + bonus) * g


def _rwkv(pd, v_first, prm):
    t = pd.shape[0]
    tt = _tile(t, 512)
    nch = tt // CHUNK
    has_vres = v_first is not None
    gw = GROUP_W
    row_blk = lambda n: pl.BlockSpec((tt, n), lambda i: (i, 0))
    vec = _full((1, gw))
    in_specs = [row_blk(N_D),
                pl.BlockSpec((SUBLANES, N_D), lambda i: (jnp.maximum(i * (tt // SUBLANES) - 1, 0), 0))]
    args = [pd, pd]
    if has_vres:
        in_specs.append(row_blk(gw))
        args.append(v_first)
    tpos = jnp.arange(tt)
    tril = ((tpos[:, None] // CHUNK == tpos[None, :] // CHUNK) & (tpos[:, None] >= tpos[None, :])).astype(BF16)
    in_specs.append(_full((tt, tt)))
    args.append(tril)
    in_specs += [_full((1, N_D)), vec, _full((LORA_W, gw)), vec, _full((LORA_A, gw)), _full((LORA_G, gw)),
                 vec, vec, vec, vec, vec]
    args += [prm["mu"], prm["w0"], prm["w2"], prm["a0"], prm["a2"], prm["g2"], prm["k_k"], prm["k_a"],
             prm["r_k"], prm["ln_g"], prm["ln_b"]]
    if has_vres:
        in_specs += [vec, _full((gw, LORA_V)), _full((LORA_V, gw))]
        args += [prm["v0"], prm["v1"], prm["v2"]]
        out_shape = jax.ShapeDtypeStruct((t, gw), F32)
        out_specs = row_blk(gw)
    else:
        out_shape = [jax.ShapeDtypeStruct((t, gw), F32)] * 2
        out_specs = [row_blk(gw)] * 2
    npair, pw = RWKV_HEADS // 2, 2 * RWKV_HEAD
    scratch = [pltpu.VMEM((npair, RWKV_HEAD, pw), F32), pltpu.VMEM((tt, gw), F32)]
    return pl.pallas_call(
        functools.partial(_rwkv_kernel, tt=tt, has_vres=has_vres),
        out_shape=out_shape,
        grid=(t // tt,),
        in_specs=in_specs,
        out_specs=out_specs,
        scratch_shapes=scratch,
        compiler_params=_cparams(("arbitrary",)),
        name="rwkv7",
    )(*args)


def _route_record(lg):
    lane = lax.broadcasted_iota(jnp.int32, (lg.shape[0], LANES), 1)
    lanef = lane.astype(F32)
    gl = jnp.where(lane < N_GROUPS, lg[:, 0:LANES], -jnp.inf)
    gmax = jnp.max(gl, axis=-1, keepdims=True)
    gp = 1.0 / jnp.sum(jnp.exp(gl - gmax), axis=-1, keepdims=True)
    gi = jnp.min(jnp.where(gl == gmax, lanef, float(LANES)), axis=-1, keepdims=True)
    lo = gi * EXP_PER_GROUP
    in_grp = (lanef >= lo) & (lanef < lo + EXP_PER_GROUP)
    el = jnp.where(in_grp, lg[:, LANES:2 * LANES], -jnp.inf)
    m1 = jnp.max(el, axis=-1, keepdims=True)
    i1 = jnp.min(jnp.where(el == m1, lanef, float(LANES)), axis=-1, keepdims=True)
    el2 = jnp.where(lanef == i1, -jnp.inf, el)
    m2 = jnp.max(el2, axis=-1, keepdims=True)
    i2 = jnp.min(jnp.where(el2 == m2, lanef, float(LANES)), axis=-1, keepdims=True)
    e2 = jnp.exp(m2 - m1)
    w_top = gp / (1.0 + e2)
    return jnp.where(lanef == i1 - lo, w_top, jnp.where(lanef == i2 - lo, w_top * e2,
                     jnp.where(lane == ROUTE_GROUP_LANE, gi, 0.0)))


def _outproj_kernel(ya_ref, yb_ref, yc_ref, yd_ref, x_ref, wo_ref, gt_ref, g_ref, sc_ref, sh_ref, wr_ref, br_ref,
                    x1_ref, h2_ref, gate_ref, *, tm):
    sub = tm // OUT_SUB
    for blk in range(OUT_SUB):
        rows = slice(blk * sub, (blk + 1) * sub)
        ycat = jnp.concatenate([ya_ref[rows, :], yb_ref[rows, :], yc_ref[rows, :], yd_ref[rows, :]], axis=1)
        x1 = x_ref[rows, :] + gt_ref[...] * _dot_w(ycat, wo_ref[...], P_OUT)
        x1_ref[rows, :] = x1
        h2 = _modulated_norm(x1, g_ref[...], sc_ref[...], sh_ref[...])
        h2_ref[rows, :] = h2.astype(h2_ref.dtype)
        gate_ref[rows, :] = _route_record(_dot_w(h2, wr_ref[...], P_ROUTER) + br_ref[...])


def _out_proj(ya, yb, yc, yd, x, wo_parts, gt, g, sc, sh, w_router, b_router):
    t = x.shape[0]
    tm = _tile(t, 1024)
    row_blk = lambda n: pl.BlockSpec((tm, n), lambda i: (i, 0))
    vec = _full((1, D_MODEL))
    return pl.pallas_call(
        functools.partial(_outproj_kernel, tm=tm),
        out_shape=[jax.ShapeDtypeStruct((t, D_MODEL), F32), jax.ShapeDtypeStruct((t, D_MODEL), BF16),
                   jax.ShapeDtypeStruct((t, LANES), F32)],
        grid=(t // tm,),
        in_specs=[row_blk(GROUP_W)] * 4 + [row_blk(D_MODEL), _full(wo_parts.shape), vec, vec, vec, vec,
                                           _full(w_router.shape), _full((1, 2 * LANES))],
        out_specs=[row_blk(D_MODEL), row_blk(D_MODEL), row_blk(LANES)],
        compiler_params=_cparams(("parallel",)),
        name="out_proj_router",
    )(ya, yb, yc, yd, x, wo_parts, gt, g, sc, sh, w_router, b_router)


def _moe_kernel(h_ref, route_ref, tril_ref, wg_ref, wu_ref, wd_ref, x1_ref, gt_ref, fg_ref, o_ref,
                acc_ref, col_ref, row_ref, cnt_ref, *, final, tm, cap, cap_extra):
    g = pl.program_id(1)
    lane = lax.broadcasted_iota(jnp.int32, (tm, LANES), 1)

    @pl.when(g == 0)
    def _():
        acc_ref[...] = jnp.zeros_like(acc_ref)
        gi = route_ref[:, ROUTE_GROUP_LANE:ROUTE_GROUP_LANE + 1]
        onehot = lane.astype(F32) == gi
        cum = _mm(tril_ref[...], jnp.where(onehot, 1.0, 0.0).astype(BF16), _NN)
        rank = jnp.sum(jnp.where(onehot, cum, 0.0), axis=-1, keepdims=True) - 1.0
        cnt_ref[0:1, :] = cum[tm - 1:tm, :]
        cnt_ref[1:2, :] = cum[tm // 2 - 1:tm // 2, :]
        rec = jnp.where(lane == 0, rank, jnp.where(lane == 1, gi, 0.0))
        col_ref[...] = rec
        row_ref[...] = rec.T[0:SUBLANES, :]

    gf = g.astype(F32)
    this_group = lane[0:1, :] == g
    count = jnp.sum(jnp.where(this_group, cnt_ref[0:1, :], 0.0)).astype(jnp.int32)
    count_half = jnp.sum(jnp.where(this_group, cnt_ref[1:2, :], 0.0)).astype(jnp.int32)
    sel_col = jnp.where(col_ref[:, 1:2] == gf, col_ref[:, 0:1], -1.0)
    sel_row = jnp.where(row_ref[1:2, :] == gf, row_ref[0:1, :], -1.0)
    hx = jnp.concatenate([h_ref[...], route_ref[...].astype(BF16)], axis=1)

    def block(base, rows, t0=0):
        base = base.astype(F32)
        nt = tm - t0
        slot_r = lax.broadcasted_iota(jnp.int32, (rows, nt), 0).astype(F32) + base
        gather = jnp.where(slot_r == sel_row[:, t0:], 1.0, 0.0).astype(BF16)
        hs = _mm(gather, hx[t0:, :], _NN)
        h = hs[:, 0:D_MODEL].astype(BF16)
        slot_gate = hs[:, D_MODEL:D_MODEL + LANES]
        hg = _mm(h, wg_ref[...], _NN)
        hu = _mm(h, wu_ref[...], _NN)
        hu = jnp.concatenate([hu[:, j * D_EXPERT:(j + 1) * D_EXPERT] * slot_gate[:, j:j + 1]
                              for j in range(EXP_PER_GROUP)], axis=1)
        act = hg * _sigmoid(hg) * hu
        ys = _mm(act.astype(BF16), wd_ref[...], _NN)
        slot_c = lax.broadcasted_iota(jnp.int32, (nt, rows), 1).astype(F32) + base
        scatter = jnp.where(slot_c == sel_col[t0:, :], 1.0, 0.0).astype(BF16)
        acc_ref[t0:, :] += _mm(scatter, ys.astype(BF16), _NN)

    cap_light = cap - cap_extra // 2

    @pl.when(count > cap_light)
    def _():
        block(jnp.int32(0), cap)

    @pl.when((count > 0) & (count <= cap_light))
    def _():
        block(jnp.int32(0), cap_light)

    n_extra = (jnp.maximum(count - cap, 0) + (cap_extra - 1)) // cap_extra
    late = count_half <= cap

    def extra_late(r, carry):
        block(cap + r * cap_extra, cap_extra, t0=tm // 2)
        return carry

    def extra_any(r, carry):
        block(cap + r * cap_extra, cap_extra)
        return carry

    lax.fori_loop(0, jnp.where(late, n_extra, 0), extra_late, 0)
    lax.fori_loop(0, jnp.where(late, 0, n_extra), extra_any, 0)

    @pl.when(g == pl.num_programs(1) - 1)
    def _():
        out = x1_ref[...] + gt_ref[...] * acc_ref[...]
        if final:
            out = out * lax.rsqrt(jnp.mean(out * out, axis=-1, keepdims=True) + NORM_EPS) * fg_ref[...]
        o_ref[...] = out


def _moe(h2, route, wg, wu, wd, x1, gt, final_g, final):
    t = h2.shape[0]
    tm = _tile(t, MOE_TILE)
    cap = tm // N_GROUPS
    cap_extra = cap // 2
    nw = EXP_PER_GROUP * D_EXPERT
    vec = _full((1, D_MODEL))
    tpos = jnp.arange(tm)
    tril = (tpos[:, None] >= tpos[None, :]).astype(BF16)
    return pl.pallas_call(
        functools.partial(_moe_kernel, final=final, tm=tm, cap=cap, cap_extra=cap_extra),
        out_shape=jax.ShapeDtypeStruct((t, D_MODEL), F32),
        grid=(t // tm, N_GROUPS),
        in_specs=[
            pl.BlockSpec((tm, D_MODEL), lambda i, g: (i, 0)),
            pl.BlockSpec((tm, LANES), lambda i, g: (i, 0)),
            _full((tm, tm)),
            pl.BlockSpec((D_MODEL, nw), lambda i, g: (0, g)),
            pl.BlockSpec((D_MODEL, nw), lambda i, g: (0, g)),
            pl.BlockSpec((nw, D_MODEL), lambda i, g: (g, 0)),
            pl.BlockSpec((tm, D_MODEL), lambda i, g: (i, 0)),
            vec, vec,
        ],
        out_specs=pl.BlockSpec((tm, D_MODEL), lambda i, g: (i, 0)),
        scratch_shapes=[pltpu.VMEM((tm, D_MODEL), F32), pltpu.VMEM((tm, LANES), F32),
                        pltpu.VMEM((SUBLANES, tm), F32), pltpu.VMEM((2, LANES), F32)],
        compiler_params=_cparams(("parallel", "arbitrary")),
        name="moe_experts",
    )(h2, route, tril, wg, wu, wd, x1, gt, final_g)


def kernel(x, c, w_ada, b_ada, norm1_g, norm2_g, w_in, w_out, pool_w, pool_scale, attn_sinks, conv_w, rwkv_mu,
           rwkv_w0, rwkv_w2, rwkv_a0, rwkv_a2, rwkv_g2, rwkv_k_k, rwkv_k_a, rwkv_r_k, rwkv_ln_g, rwkv_ln_b,
           rwkv_v0, rwkv_v1, rwkv_v2, moe_w_grp, moe_b_grp, moe_w_exp, moe_b_exp, moe_w_gate, moe_w_up,
           moe_w_down, final_g):
    bsz, t, d = x.shape
    assert bsz == 1 and d == D_MODEL
    depth = w_ada.shape[0]
    xs = x.reshape(t, d)
    mod = _ada_mod(c, w_ada, b_ada)
    v_first = None
    row = lambda a: a.reshape(1, -1)
    for l in range(depth):
        sh1, sc1, gt1, sh2, sc2, gt2 = [mod[l, :, j * d:(j + 1) * d] for j in range(6)]
        wp = jax.scipy.linalg.block_diag(*[pool_w[l, gidx] for gidx in range(len(POOL_WINDOWS))])
        ya, pb, yc, pd = _in_proj(xs, row(norm1_g[l]), sc1, sh1, _split_w(w_in[l], P_IN), _split_w(wp, P_POOL),
                                  row(pool_scale[l]), conv_w[l])
        yb = _attention(pb, row(attn_sinks[l]))
        prm = dict(mu=row(rwkv_mu[l]), w0=row(rwkv_w0[l]), w2=rwkv_w2[l], a0=row(rwkv_a0[l]), a2=rwkv_a2[l],
                   g2=rwkv_g2[l], k_k=row(rwkv_k_k[l]), k_a=row(rwkv_k_a[l]), r_k=row(rwkv_r_k[l]),
                   ln_g=row(rwkv_ln_g[l]), ln_b=row(rwkv_ln_b[l]))
        if l == 0:
            yd, v_first = _rwkv(pd, None, prm)
        else:
            prm.update(v0=row(rwkv_v0[l - 1]), v1=rwkv_v1[l - 1], v2=rwkv_v2[l - 1])
            yd = _rwkv(pd, v_first, prm)
        lane_pad = lambda a: jnp.pad(a, ((0, 0), (0, LANES - a.shape[1])))
        w_router = jnp.concatenate([lane_pad(moe_w_grp[l]), lane_pad(moe_w_exp[l])], axis=1)
        b_router = jnp.concatenate([lane_pad(row(moe_b_grp[l])), lane_pad(row(moe_b_exp[l]))], axis=1)
        x1, h2, gates = _out_proj(ya, yb, yc, yd, xs, _split_w(w_out[l], P_OUT), gt1, row(norm2_g[l]), sc2, sh2,
                                  _split_w(w_router, P_ROUTER), b_router)
        wg = jnp.transpose(moe_w_gate[l], (1, 0, 2)).reshape(d, N_EXPERTS * D_EXPERT).astype(BF16)
        wu = jnp.transpose(moe_w_up[l], (1, 0, 2)).reshape(d, N_EXPERTS * D_EXPERT).astype(BF16)
        wd = moe_w_down[l].astype(BF16).reshape(N_EXPERTS * D_EXPERT, d)
        xs = _moe(h2, gates, wg, wu, wd, x1, gt2, row(final_g), final=(l == depth - 1))
    return xs.reshape(bsz, t, d)
```

```python
import functools

import jax
import jax.numpy as jnp
from jax import lax
from jax.experimental import pallas as pl
from jax.experimental.pallas import tpu as pltpu

F32 = jnp.float32
BF16 = jnp.bfloat16

D_MODEL = 1024
GROUP_W = 256
NORM_EPS = 1e-6
POOL_WINDOWS = (2, 4, 8, 16)
POOL_CG = 64
HEAD_DIM = 64
ATT_HEADS = 4
ATT_KV_HEADS = 2
ATT_BLOCK = 128
CONV_W = 3
RWKV_HEAD = 64
RWKV_HEADS = 4
LORA_W = 32
LORA_A = 32
LORA_G = 64
LORA_V = 32
RWKV_GN_EPS = 64e-5
N_A = GROUP_W
N_B = (ATT_HEADS + 2 * ATT_KV_HEADS) * HEAD_DIM
N_C = 3 * GROUP_W
N_D = 3 * GROUP_W + LORA_W + LORA_A + LORA_G
N_GROUPS = 4
EXP_PER_GROUP = 8
N_EXPERTS = 32
D_EXPERT = 128

LANES = 128
SUBLANES = 8
VMEM_LIMIT = 56 * 1024 * 1024

CHUNK = 64
POOL_HALO = 16
CONV_HALO = 8
OUT_SUB = 2
ROUTE_GROUP_LANE = 8
MOE_TILE = 1024

P_IN = 1
P_OUT = 1
P_POOL = 1
P_ATT = 1
P_LORA = 1
P_STATE = 1
P_ROUTER = 3


def _split(a):
    hi = a.astype(BF16)
    lo = (a - hi.astype(F32)).astype(BF16)
    return hi, lo


def _mm(a, b, dims):
    return lax.dot_general(a, b, (dims, ((), ())), preferred_element_type=F32)


_NN = ((1,), (0,))
_NT = ((1,), (1,))


def _dot(a, b, passes=1, dims=_NN):
    if passes == 1:
        return _mm(a.astype(BF16), b.astype(BF16), dims)
    a0, a1 = _split(a)
    b0, b1 = _split(b)
    return _mm(a0, b0, dims) + (_mm(a0, b1, dims) + _mm(a1, b0, dims))


def _bmm(a, b, nt):
    dims = (((2,), (2 if nt else 1,)), ((0,), (0,)))
    return lax.dot_general(a, b, dims, preferred_element_type=F32)


def _bdot(a, b, passes=1, nt=False):
    if passes == 1:
        return _bmm(a.astype(BF16), b.astype(BF16), nt)
    a0, a1 = _split(a)
    b0, b1 = _split(b)
    return _bmm(a0, b0, nt) + (_bmm(a0, b1, nt) + _bmm(a1, b0, nt))


def _lhs_w(a, passes):
    if passes == 1:
        return a.astype(BF16)
    a0, a1 = _split(a)
    return jnp.concatenate([a0, a0, a1], axis=1)


def _dot_w(a, w_cat, passes):
    return _mm(_lhs_w(a, passes), w_cat, _NN)


def _split_w(w, passes):
    hi = w.astype(BF16)
    if passes == 1:
        return hi
    lo = (w - hi.astype(F32)).astype(BF16)
    return jnp.concatenate([hi, lo, hi], axis=0)


def _sigmoid(x):
    return 1.0 / (1.0 + jnp.exp(-x))


def _cparams(sem):
    return pltpu.CompilerParams(dimension_semantics=sem, vmem_limit_bytes=VMEM_LIMIT)


def _full(shape):
    return pl.BlockSpec(shape, lambda *_: (0,) * len(shape))


def _tile(n, pref):
    t = min(n, pref)
    assert n % t == 0, (n, t)
    return t


def _mod_kernel(c_ref, w_ref, b_ref, o_ref):
    c = c_ref[...]
    cond = c * _sigmoid(c)
    o_ref[...] = _dot(cond, w_ref[...], 3) + b_ref[...]


def _ada_mod(c, w_ada, b_ada):
    depth = w_ada.shape[0]
    c8 = jnp.broadcast_to(c, (SUBLANES, D_MODEL))
    out = pl.pallas_call(
        _mod_kernel,
        out_shape=jax.ShapeDtypeStruct((depth, SUBLANES, 6 * D_MODEL), F32),
        grid=(depth, 6),
        in_specs=[
            pl.BlockSpec((SUBLANES, D_MODEL), lambda l, j: (0, 0)),
            pl.BlockSpec((None, D_MODEL, D_MODEL), lambda l, j: (l, 0, j)),
            pl.BlockSpec((None, 1, D_MODEL), lambda l, j: (l, 0, j)),
        ],
        out_specs=pl.BlockSpec((None, SUBLANES, D_MODEL), lambda l, j: (l, 0, j)),
        compiler_params=_cparams(("arbitrary", "arbitrary")),
        name="ada_mod",
    )(c8, w_ada, b_ada.reshape(depth, 1, 6 * D_MODEL))
    return out[:, 0:1, :]


def _modulated_norm(x, g, sc, sh):
    y = x * lax.rsqrt(jnp.mean(x * x, axis=-1, keepdims=True) + NORM_EPS) * g
    return y * (1.0 + sc) + sh


def _shift_rows(ext, n):
    return pltpu.roll(ext, n, axis=0)


def _pool_mixer(u, halo, wp, scale, t0):
    tt = u.shape[0]
    ext = jnp.concatenate([halo, u], axis=0)
    s2 = ext + _shift_rows(ext, 1)
    s4 = s2 + _shift_rows(s2, 2)
    s8 = s4 + _shift_rows(s4, 4)
    s16 = s8 + _shift_rows(s8, 8)
    grp = lax.broadcasted_iota(jnp.int32, (tt, GROUP_W), 1) // POOL_CG
    tpos = lax.broadcasted_iota(jnp.int32, (tt, GROUP_W), 0) + (t0 + 1)
    wsum = jnp.where(grp == 0, s2[POOL_HALO:], jnp.where(grp == 1, s4[POOL_HALO:],
                     jnp.where(grp == 2, s8[POOL_HALO:], s16[POOL_HALO:])))
    win = jnp.where(grp == 0, 2, jnp.where(grp == 1, 4, jnp.where(grp == 2, 8, 16)))
    cnt = jnp.minimum(tpos, win).astype(F32)
    return _dot_w(wsum / cnt - u, wp, P_POOL) * scale


def _conv_mixer(bg, z, z_halo, cw):
    zext = jnp.concatenate([z_halo, z], axis=0)
    y = (cw[2:3, :] * z + cw[1:2, :] * _shift_rows(zext, 1)[CONV_HALO:]
         + cw[0:1, :] * _shift_rows(zext, 2)[CONV_HALO:])
    return bg * y


def _inproj_kernel(x_ref, g_ref, sc_ref, sh_ref, w_ref, wp_ref, scale_ref, cw_ref,
                   ya_ref, pb_ref, yc_ref, pd_ref, u_tail, z_tail, *, passes, tm):
    i = pl.program_id(0)

    @pl.when(i == 0)
    def _():
        u_tail[...] = jnp.zeros_like(u_tail)
        z_tail[...] = jnp.zeros_like(z_tail)

    h = _modulated_norm(x_ref[...], g_ref[...], sc_ref[...], sh_ref[...])
    lhs = _lhs_w(h, passes)
    proj = lambda col, width: _mm(lhs, w_ref[:, col:col + width], _NN)
    u = proj(0, N_A)
    pc = proj(N_A + N_B, N_C)
    ya_ref[...] = _pool_mixer(u, u_tail[...], wp_ref[...], scale_ref[...], i * tm)
    u_tail[...] = u[tm - POOL_HALO:, :]
    z = pc[:, GROUP_W:2 * GROUP_W] * pc[:, 2 * GROUP_W:3 * GROUP_W]
    yc_ref[...] = _conv_mixer(pc[:, 0:GROUP_W], z, z_tail[...], cw_ref[...])
    z_tail[...] = z[tm - CONV_HALO:, :]
    pb_ref[...] = proj(N_A, N_B)
    pd_ref[...] = proj(N_A + N_B + N_C, N_D)


def _in_proj(x, g, sc, sh, w_parts, wp_parts, pool_scale, conv_w):
    t = x.shape[0]
    tm = _tile(t, 1024)
    vec = _full((1, D_MODEL))
    widths = (GROUP_W, N_B, GROUP_W, N_D)
    return pl.pallas_call(
        functools.partial(_inproj_kernel, passes=P_IN, tm=tm),
        out_shape=[jax.ShapeDtypeStruct((t, n), F32) for n in widths],
        grid=(t // tm,),
        in_specs=[pl.BlockSpec((tm, D_MODEL), lambda i: (i, 0)), vec, vec, vec, _full(w_parts.shape),
                  _full(wp_parts.shape), _full((1, GROUP_W)), _full((CONV_W, GROUP_W))],
        out_specs=[pl.BlockSpec((tm, n), lambda i: (i, 0)) for n in widths],
        scratch_shapes=[pltpu.VMEM((POOL_HALO, GROUP_W), F32), pltpu.VMEM((CONV_HALO, GROUP_W), F32)],
        compiler_params=_cparams(("arbitrary",)),
        name="in_proj_pool_conv",
    )(x, g, sc, sh, w_parts, wp_parts, pool_scale, conv_w)


def _attn_kernel(cur_ref, prev_ref, sink_ref, o_ref, *, tq):
    i = pl.program_id(0)
    nq = ATT_HEADS * HEAD_DIM
    nkv = ATT_KV_HEADS * HEAD_DIM
    blk = ATT_BLOCK
    hd = HEAD_DIM
    rep = ATT_HEADS // ATT_KV_HEADS
    nblk = tq // blk
    ri = lax.broadcasted_iota(jnp.int32, (rep * blk, 2 * blk), 0)
    ki = lax.broadcasted_iota(jnp.int32, (rep * blk, 2 * blk), 1)
    dist = (ri % blk) + blk - ki
    in_win = (dist >= 0) & (dist < blk)
    distf = dist.astype(F32)
    head_col = lax.broadcasted_iota(jnp.int32, (rep * blk, 1), 0) // blk
    bias, sink = [], []
    for g in range(ATT_KV_HEADS):
        slope_g = jnp.zeros((rep * blk, 1), F32)
        sink_g = jnp.zeros((rep * blk, 1), F32)
        for j in range(rep):
            h = g * rep + j
            slope_g = jnp.where(head_col == j, 2.0 ** (-8.0 * (h + 1) / ATT_HEADS), slope_g)
            sink_g = jnp.where(head_col == j, sink_ref[:, h:h + 1], sink_g)
        bias.append(slope_g * distf)
        sink.append(sink_g)
    bias = jnp.stack(bias)[None]
    sink = jnp.stack(sink)[None]

    q = cur_ref[:, 0:nq] * (hd ** -0.5)
    kv = jnp.concatenate([prev_ref[...], cur_ref[:, nq:nq + 2 * nkv]], axis=0)
    qs, ks, vs = [], [], []
    for b in range(nblk):
        for g in range(ATT_KV_HEADS):
            qs.append(jnp.concatenate(
                [q[b * blk:(b + 1) * blk, (g * rep + j) * hd:(g * rep + j + 1) * hd] for j in range(rep)], axis=0))
            ks.append(kv[b * blk:(b + 2) * blk, g * hd:(g + 1) * hd])
            vs.append(kv[b * blk:(b + 2) * blk, nkv + g * hd:nkv + (g + 1) * hd])
    s = _bdot(jnp.stack(qs), jnp.stack(ks), P_ATT, nt=True)
    s = s.reshape(nblk, ATT_KV_HEADS, rep * blk, 2 * blk)
    s = jnp.where(in_win, s - bias, -jnp.inf)
    first_ok = ki >= blk * (1 - (i > 0).astype(jnp.int32))
    s = jnp.concatenate([jnp.where(first_ok, s[0:1], -jnp.inf), s[1:]], axis=0)
    m = jnp.maximum(jnp.max(s, axis=-1, keepdims=True), sink)
    p = jnp.exp(s - m)
    den = jnp.sum(p, axis=-1, keepdims=True) + jnp.exp(sink - m)
    o = _bdot(p.reshape(nblk * ATT_KV_HEADS, rep * blk, 2 * blk), jnp.stack(vs), P_ATT)
    o = o.reshape(nblk, ATT_KV_HEADS, rep * blk, hd) / den
    for b in range(nblk):
        o_ref[b * blk:(b + 1) * blk, :] = jnp.concatenate(
            [o[b, g, j * blk:(j + 1) * blk, :] for g in range(ATT_KV_HEADS) for j in range(rep)], axis=-1)


def _attention(pb, sinks):
    t = pb.shape[0]
    tq = _tile(t, 2048)
    nb = tq // ATT_BLOCK
    return pl.pallas_call(
        functools.partial(_attn_kernel, tq=tq),
        out_shape=jax.ShapeDtypeStruct((t, GROUP_W), F32),
        grid=(t // tq,),
        in_specs=[
            pl.BlockSpec((tq, N_B), lambda i: (i, 0)),
            pl.BlockSpec((ATT_BLOCK, 2 * ATT_KV_HEADS * HEAD_DIM), lambda i: (jnp.maximum(i * nb - 1, 0), 1)),
            _full((1, ATT_HEADS)),
        ],
        out_specs=pl.BlockSpec((tq, GROUP_W), lambda i: (i, 0)),
        compiler_params=_cparams(("parallel",)),
        name="swa_attention",
    )(pb, pb, sinks)


def _rwkv_kernel(*refs, tt, has_vres):
    if has_vres:
        (pd_ref, prev_ref, vf_ref, tril_ref, mu_ref, w0_ref, w2_ref, a0_ref, a2_ref, g2_ref, kk_ref, ka_ref,
         rk_ref, lng_ref, lnb_ref, v0_ref, v1_ref, v2_ref, y_ref, st_ref, yr_s) = refs
    else:
        (pd_ref, prev_ref, tril_ref, mu_ref, w0_ref, w2_ref, a0_ref, a2_ref, g2_ref, kk_ref, ka_ref,
         rk_ref, lng_ref, lnb_ref, y_ref, vout_ref, st_ref, yr_s) = refs
    i = pl.program_id(0)
    nh, hd, gw = RWKV_HEADS, RWKV_HEAD, GROUP_W
    @pl.when(i == 0)
    def _():
        st_ref[...] = jnp.zeros_like(st_ref)

    hr = lax.broadcasted_iota(jnp.int32, (2 * gw, gw), 0) % gw // hd
    hc = lax.broadcasted_iota(jnp.int32, (2 * gw, gw), 1) // hd
    hsum2 = jnp.where(hr == hc, 1.0, 0.0).astype(BF16)

    def head_sum(t2):
        return _mm(jnp.concatenate(_split(t2), axis=1), hsum2, _NN)

    p = pd_ref[...]
    row = lax.broadcasted_iota(jnp.int32, (tt, 1), 0)
    prev_row = prev_ref[SUBLANES - 1:SUBLANES, :] * (i > 0).astype(F32)
    shifted = jnp.where(row == 0, prev_row, pltpu.roll(p, 1, axis=0))
    z = p + (shifted - p) * mu_ref[...]
    r = z[:, 0:gw]
    k = z[:, gw:2 * gw]
    v = z[:, 2 * gw:3 * gw]
    o = 3 * gw
    wd = z[:, o:o + LORA_W]
    ad = z[:, o + LORA_W:o + LORA_W + LORA_A]
    gd = z[:, o + LORA_W + LORA_A:o + LORA_W + LORA_A + LORA_G]
    wpre = -(w0_ref[...] + _dot(jnp.tanh(wd), w2_ref[...], P_LORA))
    softplus = jnp.maximum(wpre, 0.0) + jnp.log(1.0 + jnp.exp(-jnp.abs(wpre)))
    lw = -jnp.exp(-softplus - 0.5)
    a = _sigmoid(a0_ref[...] + _dot(ad, a2_ref[...], P_LORA))
    g = _dot(_sigmoid(gd), g2_ref[...], P_LORA)
    if has_vres:
        mix = _sigmoid(v0_ref[...] + _dot(_dot(v, v1_ref[...], P_LORA), v2_ref[...], P_LORA))
        v = v + (vf_ref[...] - v) * mix
    else:
        vout_ref[...] = v
    kk = k * kk_ref[...]
    kk = kk / jnp.maximum(jnp.sqrt(head_sum(kk * kk)), 1e-12)
    k = k * (1.0 + (a - 1.0) * ka_ref[...])
    nch = tt // CHUNK
    L = CHUNK
    npair = nh // 2
    pw = 2 * hd
    ti = lax.broadcasted_iota(jnp.int32, (L, pw), 0)
    tl = lax.broadcasted_iota(jnp.int32, (L, pw), 1) % hd
    low_strict = ti > tl
    low_incl = ti >= tl
    eye = (ti == tl).astype(F32)
    first_head = lax.broadcasted_iota(jnp.int32, (1, pw), 1) < hd

    def same_block(b):
        return (ti // b) == (tl // b)

    def to_batch(t2, c0, nc):
        return jnp.concatenate([t2[c0 * L:(c0 + nc) * L, q * pw:(q + 1) * pw].reshape(nc, L, pw)
                                for q in range(npair)], axis=0)

    def bd(x):
        zero = jnp.zeros_like(x)
        return jnp.concatenate([jnp.where(first_head, x, zero), jnp.where(first_head, zero, x)], axis=-2)

    def fold(x):
        return x[:, :hd] + x[:, hd:]

    def c16(x):
        return x.astype(BF16)

    lw_hi, lw_lo = _split(lw)
    cl = _mm(tril_ref[...], lw_hi, _NN) + _mm(tril_ref[...], lw_lo, _NN)
    cl3 = cl.reshape(nch, L, gw)
    cl_end = cl3[:, L - 1:L, :]
    e_end = jnp.exp(cl_end - cl3).reshape(tt, gw)
    e_neg = jnp.exp(-cl)
    bb = kk * a
    at_t = -kk * jnp.exp(cl - lw)
    bh_t = bb * e_neg
    kh_t = k * e_neg
    rt_t = r * jnp.exp(cl)
    be_t = bb * e_end
    ke_t = k * e_end
    p_end = jnp.exp(cl_end)

    def chunk_operators(c0, nc):
        at = c16(to_batch(at_t, c0, nc))
        bh = c16(to_batch(bh_t, c0, nc))
        kh = c16(to_batch(kh_t, c0, nc))
        rt = to_batch(rt_t, c0, nc)
        be = to_batch(be_t, c0, nc)
        ke = to_batch(ke_t, c0, nc)
        vb = bd(c16(to_batch(v, c0, nc)))
        pe = jnp.concatenate([p_end[c0:c0 + nc, :, q * pw:(q + 1) * pw] for q in range(npair)], axis=0)
        lhs = jnp.concatenate([at, c16(rt)], axis=1)
        g_b = _bmm(lhs, bd(bh), True)
        g_k = _bmm(lhs, bd(kh), True)
        a_ab = jnp.where(low_strict, g_b[:, :L], 0.0)
        b_rb = jnp.where(low_incl, g_b[:, L:], 0.0)
        a_ak = jnp.where(low_strict, g_k[:, :L], 0.0)
        b_rk = jnp.where(low_incl, g_k[:, L:], 0.0)
        tinv = eye + jnp.where(same_block(2), a_ab, 0.0)
        b = 2
        while b < L:
            e = jnp.where(same_block(2 * b) & ~same_block(b), a_ab, 0.0)
            xe = _bmm(c16(tinv), bd(c16(e)), False)
            tinv = tinv + _bmm(c16(xe), bd(c16(tinv)), False)
            b *= 2
        av = _bmm(c16(a_ak), vb, False)
        wu = _bmm(c16(tinv), jnp.concatenate([bd(at), bd(c16(av))], axis=2), False)
        rhs = jnp.concatenate([bd(c16(wu[:, :, :pw])), bd(c16(wu[:, :, pw:]))], axis=2)
        lhs_b = jnp.concatenate([c16(b_rb), c16(jnp.swapaxes(bd(be), 1, 2))], axis=1)
        lhs_k = jnp.concatenate([c16(b_rk), c16(jnp.swapaxes(bd(ke), 1, 2))], axis=1)
        o_wu = _bmm(lhs_b, rhs, False)
        o_v = _bmm(lhs_k, vb, False)
        rq = rt + o_wu[:, :L, :pw]
        y0 = o_wu[:, :L, pw:] + o_v[:, :L]
        m2 = eye * pe + fold(o_wu[:, L:, :pw])
        c2 = fold(o_wu[:, L:, pw:] + o_v[:, L:])
        return jnp.concatenate([rq, m2], axis=1), jnp.concatenate([y0, c2], axis=1)

    def state_walk(zs, ops, c0, nc):
        rqm, y0c = ops
        for c in range(nc):
            for q in range(npair):
                b = q * nc + c
                yz = _dot(rqm[b], bd(zs[q]), P_STATE) + y0c[b]
                yr_s[(c0 + c) * L:(c0 + c + 1) * L, q * pw:(q + 1) * pw] = yz[:L]
                zs[q] = yz[L:]
        return zs

    zs = state_walk([st_ref[q] for q in range(npair)], chunk_operators(0, nch), 0, nch)
    for q in range(npair):
        st_ref[q] = zs[q]

    y = yr_s[...]
    inv_n = 1.0 / hd
    mean = head_sum(y) * inv_n
    yc = y - mean
    var = head_sum(yc * yc) * inv_n
    yn = yc * lax.rsqrt(var + RWKV_GN_EPS) * lng_ref[...] + lnb_ref[...]
    bonus = head_sum(r * k * rk_ref[...]) * v
    y_ref[...] = (yn + bonus) * g


def _rwkv(pd, v_first, prm):
    t = pd.shape[0]
    tt = _tile(t, 512)
    nch = tt // CHUNK
    has_vres = v_first is not None
    gw = GROUP_W
    row_blk = lambda n: pl.BlockSpec((tt, n), lambda i: (i, 0))
    vec = _full((1, gw))
    in_specs = [row_blk(N_D),
                pl.BlockSpec((SUBLANES, N_D), lambda i: (jnp.maximum(i * (tt // SUBLANES) - 1, 0), 0))]
    args = [pd, pd]
    if has_vres:
        in_specs.append(row_blk(gw))
        args.append(v_first)
    tpos = jnp.arange(tt)
    tril = ((tpos[:, None] // CHUNK == tpos[None, :] // CHUNK) & (tpos[:, None] >= tpos[None, :])).astype(BF16)
    in_specs.append(_full((tt, tt)))
    args.append(tril)
    in_specs += [_full((1, N_D)), vec, _full((LORA_W, gw)), vec, _full((LORA_A, gw)), _full((LORA_G, gw)),
                 vec, vec, vec, vec, vec]
    args += [prm["mu"], prm["w0"], prm["w2"], prm["a0"], prm["a2"], prm["g2"], prm["k_k"], prm["k_a"],
             prm["r_k"], prm["ln_g"], prm["ln_b"]]
    if has_vres:
        in_specs += [vec, _full((gw, LORA_V)), _full((LORA_V, gw))]
        args += [prm["v0"], prm["v1"], prm["v2"]]
        out_shape = jax.ShapeDtypeStruct((t, gw), F32)
        out_specs = row_blk(gw)
    else:
        out_shape = [jax.ShapeDtypeStruct((t, gw), F32)] * 2
        out_specs = [row_blk(gw)] * 2
    npair, pw = RWKV_HEADS // 2, 2 * RWKV_HEAD
    scratch = [pltpu.VMEM((npair, RWKV_HEAD, pw), F32), pltpu.VMEM((tt, gw), F32)]
    return pl.pallas_call(
        functools.partial(_rwkv_kernel, tt=tt, has_vres=has_vres),
        out_shape=out_shape,
        grid=(t // tt,),
        in_specs=in_specs,
        out_specs=out_specs,
        scratch_shapes=scratch,
        compiler_params=_cparams(("arbitrary",)),
        name="rwkv7",
    )(*args)


def _route_record(lg):
    lane = lax.broadcasted_iota(jnp.int32, (lg.shape[0], LANES), 1)
    lanef = lane.astype(F32)
    gl = jnp.where(lane < N_GROUPS, lg[:, 0:LANES], -jnp.inf)
    gmax = jnp.max(gl, axis=-1, keepdims=True)
    gp = 1.0 / jnp.sum(jnp.exp(gl - gmax), axis=-1, keepdims=True)
    gi = jnp.min(jnp.where(gl == gmax, lanef, float(LANES)), axis=-1, keepdims=True)
    lo = gi * EXP_PER_GROUP
    in_grp = (lanef >= lo) & (lanef < lo + EXP_PER_GROUP)
    el = jnp.where(in_grp, lg[:, LANES:2 * LANES], -jnp.inf)
    m1 = jnp.max(el, axis=-1, keepdims=True)
    i1 = jnp.min(jnp.where(el == m1, lanef, float(LANES)), axis=-1, keepdims=True)
    el2 = jnp.where(lanef == i1, -jnp.inf, el)
    m2 = jnp.max(el2, axis=-1, keepdims=True)
    i2 = jnp.min(jnp.where(el2 == m2, lanef, float(LANES)), axis=-1, keepdims=True)
    e2 = jnp.exp(m2 - m1)
    w_top = gp / (1.0 + e2)
    return jnp.where(lanef == i1 - lo, w_top, jnp.where(lanef == i2 - lo, w_top * e2,
                     jnp.where(lane == ROUTE_GROUP_LANE, gi, 0.0)))


def _outproj_kernel(ya_ref, yb_ref, yc_ref, yd_ref, x_ref, wo_ref, gt_ref, g_ref, sc_ref, sh_ref, wr_ref, br_ref,
                    x1_ref, h2_ref, gate_ref, *, tm):
    sub = tm // OUT_SUB
    for blk in range(OUT_SUB):
        rows = slice(blk * sub, (blk + 1) * sub)
        ycat = jnp.concatenate([ya_ref[rows, :], yb_ref[rows, :], yc_ref[rows, :], yd_ref[rows, :]], axis=1)
        x1 = x_ref[rows, :] + gt_ref[...] * _dot_w(ycat, wo_ref[...], P_OUT)
        x1_ref[rows, :] = x1
        h2 = _modulated_norm(x1, g_ref[...], sc_ref[...], sh_ref[...])
        h2_ref[rows, :] = h2.astype(h2_ref.dtype)
        gate_ref[rows, :] = _route_record(_dot_w(h2, wr_ref[...], P_ROUTER) + br_ref[...])


def _out_proj(ya, yb, yc, yd, x, wo_parts, gt, g, sc, sh, w_router, b_router):
    t = x.shape[0]
    tm = _tile(t, 1024)
    row_blk = lambda n: pl.BlockSpec((tm, n), lambda i: (i, 0))
    vec = _full((1, D_MODEL))
    return pl.pallas_call(
        functools.partial(_outproj_kernel, tm=tm),
        out_shape=[jax.ShapeDtypeStruct((t, D_MODEL), F32), jax.ShapeDtypeStruct((t, D_MODEL), BF16),
                   jax.ShapeDtypeStruct((t, LANES), F32)],
        grid=(t // tm,),
        in_specs=[row_blk(GROUP_W)] * 4 + [row_blk(D_MODEL), _full(wo_parts.shape), vec, vec, vec, vec,
                                           _full(w_router.shape), _full((1, 2 * LANES))],
        out_specs=[row_blk(D_MODEL), row_blk(D_MODEL), row_blk(LANES)],
        compiler_params=_cparams(("parallel",)),
        name="out_proj_router",
    )(ya, yb, yc, yd, x, wo_parts, gt, g, sc, sh, w_router, b_router)


def _moe_kernel(h_ref, route_ref, tril_ref, wg_ref, wu_ref, wd_ref, x1_ref, gt_ref, fg_ref, o_ref,
                acc_ref, col_ref, row_ref, cnt_ref, *, final, tm, cap, cap_extra):
    g = pl.program_id(1)
    lane = lax.broadcasted_iota(jnp.int32, (tm, LANES), 1)

    @pl.when(g == 0)
    def _():
        acc_ref[...] = jnp.zeros_like(acc_ref)
        gi = route_ref[:, ROUTE_GROUP_LANE:ROUTE_GROUP_LANE + 1]
        onehot = lane.astype(F32) == gi
        cum = _mm(tril_ref[...], jnp.where(onehot, 1.0, 0.0).astype(BF16), _NN)
        rank = jnp.sum(jnp.where(onehot, cum, 0.0), axis=-1, keepdims=True) - 1.0
        cnt_ref[0:1, :] = cum[tm - 1:tm, :]
        cnt_ref[1:2, :] = cum[tm // 2 - 1:tm // 2, :]
        rec = jnp.where(lane == 0, rank, jnp.where(lane == 1, gi, 0.0))
        col_ref[...] = rec
        row_ref[...] = rec.T[0:SUBLANES, :]

    gf = g.astype(F32)
    this_group = lane[0:1, :] == g
    count = jnp.sum(jnp.where(this_group, cnt_ref[0:1, :], 0.0)).astype(jnp.int32)
    count_half = jnp.sum(jnp.where(this_group, cnt_ref[1:2, :], 0.0)).astype(jnp.int32)
    sel_col = jnp.where(col_ref[:, 1:2] == gf, col_ref[:, 0:1], -1.0)
    sel_row = jnp.where(row_ref[1:2, :] == gf, row_ref[0:1, :], -1.0)
    route16 = route_ref[...].astype(BF16)

    def block(base, rows, t0=0):
        base = base.astype(F32)
        nt = tm - t0
        slot_r = lax.broadcasted_iota(jnp.int32, (rows, nt), 0).astype(F32) + base
        gather = jnp.where(slot_r == sel_row[:, t0:], 1.0, 0.0).astype(BF16)
        h = _mm(gather, h_ref[t0:, :], _NN).astype(BF16)
        slot_gate = _mm(gather, route16[t0:, :], _NN)
        hg = _mm(h, wg_ref[...], _NN)
        hu = _mm(h, wu_ref[...], _NN)
        hu = jnp.concatenate([hu[:, j * D_EXPERT:(j + 1) * D_EXPERT] * slot_gate[:, j:j + 1]
                              for j in range(EXP_PER_GROUP)], axis=1)
        act = hg * _sigmoid(hg) * hu
        ys = _mm(act.astype(BF16), wd_ref[...], _NN)
        slot_c = lax.broadcasted_iota(jnp.int32, (nt, rows), 1).astype(F32) + base
        scatter = jnp.where(slot_c == sel_col[t0:, :], 1.0, 0.0).astype(BF16)
        acc_ref[t0:, :] += _mm(scatter, ys.astype(BF16), _NN)

    @pl.when(count > 0)
    def _():
        block(jnp.int32(0), cap)

    n_extra = (jnp.maximum(count - cap, 0) + (cap_extra - 1)) // cap_extra
    late = count_half <= cap

    def extra_late(r, carry):
        block(cap + r * cap_extra, cap_extra, t0=tm // 2)
        return carry

    def extra_any(r, carry):
        block(cap + r * cap_extra, cap_extra)
        return carry

    lax.fori_loop(0, jnp.where(late, n_extra, 0), extra_late, 0)
    lax.fori_loop(0, jnp.where(late, 0, n_extra), extra_any, 0)

    @pl.when(g == pl.num_programs(1) - 1)
    def _():
        out = x1_ref[...] + gt_ref[...] * acc_ref[...]
        if final:
            out = out * lax.rsqrt(jnp.mean(out * out, axis=-1, keepdims=True) + NORM_EPS) * fg_ref[...]
        o_ref[...] = out


def _moe(h2, route, wg, wu, wd, x1, gt, final_g, final):
    t = h2.shape[0]
    tm = _tile(t, MOE_TILE)
    cap = tm // N_GROUPS
    cap_extra = cap // 2
    nw = EXP_PER_GROUP * D_EXPERT
    vec = _full((1, D_MODEL))
    tpos = jnp.arange(tm)
    tril = (tpos[:, None] >= tpos[None, :]).astype(BF16)
    return pl.pallas_call(
        functools.partial(_moe_kernel, final=final, tm=tm, cap=cap, cap_extra=cap_extra),
        out_shape=jax.ShapeDtypeStruct((t, D_MODEL), F32),
        grid=(t // tm, N_GROUPS),
        in_specs=[
            pl.BlockSpec((tm, D_MODEL), lambda i, g: (i, 0)),
            pl.BlockSpec((tm, LANES), lambda i, g: (i, 0)),
            _full((tm, tm)),
            pl.BlockSpec((D_MODEL, nw), lambda i, g: (0, g)),
            pl.BlockSpec((D_MODEL, nw), lambda i, g: (0, g)),
            pl.BlockSpec((nw, D_MODEL), lambda i, g: (g, 0)),
            pl.BlockSpec((tm, D_MODEL), lambda i, g: (i, 0)),
            vec, vec,
        ],
        out_specs=pl.BlockSpec((tm, D_MODEL), lambda i, g: (i, 0)),
        scratch_shapes=[pltpu.VMEM((tm, D_MODEL), F32), pltpu.VMEM((tm, LANES), F32),
                        pltpu.VMEM((SUBLANES, tm), F32), pltpu.VMEM((2, LANES), F32)],
        compiler_params=_cparams(("parallel", "arbitrary")),
        name="moe_experts",
    )(h2, route, tril, wg, wu, wd, x1, gt, final_g)


def kernel(x, c, w_ada, b_ada, norm1_g, norm2_g, w_in, w_out, pool_w, pool_scale, attn_sinks, conv_w, rwkv_mu,
           rwkv_w0, rwkv_w2, rwkv_a0, rwkv_a2, rwkv_g2, rwkv_k_k, rwkv_k_a, rwkv_r_k, rwkv_ln_g, rwkv_ln_b,
           rwkv_v0, rwkv_v1, rwkv_v2, moe_w_grp, moe_b_grp, moe_w_exp, moe_b_exp, moe_w_gate, moe_w_up,
           moe_w_down, final_g):
    bsz, t, d = x.shape
    assert bsz == 1 and d == D_MODEL
    depth = w_ada.shape[0]
    xs = x.reshape(t, d)
    mod = _ada_mod(c, w_ada, b_ada)
    v_first = None
    row = lambda a: a.reshape(1, -1)
    for l in range(depth):
        sh1, sc1, gt1, sh2, sc2, gt2 = [mod[l, :, j * d:(j + 1) * d] for j in range(6)]
        wp = jax.scipy.linalg.block_diag(*[pool_w[l, gidx] for gidx in range(len(POOL_WINDOWS))])
        ya, pb, yc, pd = _in_proj(xs, row(norm1_g[l]), sc1, sh1, _split_w(w_in[l], P_IN), _split_w(wp, P_POOL),
                                  row(pool_scale[l]), conv_w[l])
        yb = _attention(pb, row(attn_sinks[l]))
        prm = dict(mu=row(rwkv_mu[l]), w0=row(rwkv_w0[l]), w2=rwkv_w2[l], a0=row(rwkv_a0[l]), a2=rwkv_a2[l],
                   g2=rwkv_g2[l], k_k=row(rwkv_k_k[l]), k_a=row(rwkv_k_a[l]), r_k=row(rwkv_r_k[l]),
                   ln_g=row(rwkv_ln_g[l]), ln_b=row(rwkv_ln_b[l]))
        if l == 0:
            yd, v_first = _rwkv(pd, None, prm)
        else:
            prm.update(v0=row(rwkv_v0[l - 1]), v1=rwkv_v1[l - 1], v2=rwkv_v2[l - 1])
            yd = _rwkv(pd, v_first, prm)
        lane_pad = lambda a: jnp.pad(a, ((0, 0), (0, LANES - a.shape[1])))
        w_router = jnp.concatenate([lane_pad(moe_w_grp[l]), lane_pad(moe_w_exp[l])], axis=1)
        b_router = jnp.concatenate([lane_pad(row(moe_b_grp[l])), lane_pad(row(moe_b_exp[l]))], axis=1)
        x1, h2, gates = _out_proj(ya, yb, yc, yd, xs, _split_w(w_out[l], P_OUT), gt1, row(norm2_g[l]), sc2, sh2,
                                  _split_w(w_router, P_ROUTER), b_router)
        wg = jnp.transpose(moe_w_gate[l], (1, 0, 2)).reshape(d, N_EXPERTS * D_EXPERT).astype(BF16)
        wu = jnp.transpose(moe_w_up[l], (1, 0, 2)).reshape(d, N_EXPERTS * D_EXPERT).astype(BF16)
        wd = moe_w_down[l].astype(BF16).reshape(N_EXPERTS * D_EXPERT, d)
        xs = _moe(h2, gates, wg, wu, wd, x1, gt2, row(final_g), final=(l == depth - 1))
    return xs.reshape(bsz, t, d)
```

```python
import functools

import jax
import jax.numpy as jnp
from jax import lax
from jax.experimental import pallas as pl
from jax.experimental.pallas import tpu as pltpu

F32 = jnp.float32
BF16 = jnp.bfloat16

D_MODEL = 1024
GROUP_W = 256
NORM_EPS = 1e-6
POOL_WINDOWS = (2, 4, 8, 16)
POOL_CG = 64
HEAD_DIM = 64
ATT_HEADS = 4
ATT_KV_HEADS = 2
ATT_BLOCK = 128
CONV_W = 3
RWKV_HEAD = 64
RWKV_HEADS = 4
LORA_W = 32
LORA_A = 32
LORA_G = 64
LORA_V = 32
RWKV_GN_EPS = 64e-5
N_A = GROUP_W
N_B = (ATT_HEADS + 2 * ATT_KV_HEADS) * HEAD_DIM
N_C = 3 * GROUP_W
N_D = 3 * GROUP_W + LORA_W + LORA_A + LORA_G
N_GROUPS = 4
EXP_PER_GROUP = 8
N_EXPERTS = 32
D_EXPERT = 128

LANES = 128
SUBLANES = 8
VMEM_LIMIT = 56 * 1024 * 1024

CHUNK = 64
POOL_HALO = 16
CONV_HALO = 8
OUT_SUB = 2
ROUTE_GROUP_LANE = 8
MOE_TILE = 512

P_IN = 1
P_OUT = 1
P_POOL = 1
P_ATT = 1
P_LORA = 1
P_STATE = 1
P_ROUTER = 3


def _split(a):
    hi = a.astype(BF16)
    lo = (a - hi.astype(F32)).astype(BF16)
    return hi, lo


def _mm(a, b, dims):
    return lax.dot_general(a, b, (dims, ((), ())), preferred_element_type=F32)


_NN = ((1,), (0,))
_NT = ((1,), (1,))


def _dot(a, b, passes=1, dims=_NN):
    if passes == 1:
        return _mm(a.astype(BF16), b.astype(BF16), dims)
    a0, a1 = _split(a)
    b0, b1 = _split(b)
    return _mm(a0, b0, dims) + (_mm(a0, b1, dims) + _mm(a1, b0, dims))


def _bmm(a, b, nt):
    dims = (((2,), (2 if nt else 1,)), ((0,), (0,)))
    return lax.dot_general(a, b, dims, preferred_element_type=F32)


def _bdot(a, b, passes=1, nt=False):
    if passes == 1:
        return _bmm(a.astype(BF16), b.astype(BF16), nt)
    a0, a1 = _split(a)
    b0, b1 = _split(b)
    return _bmm(a0, b0, nt) + (_bmm(a0, b1, nt) + _bmm(a1, b0, nt))


def _lhs_w(a, passes):
    if passes == 1:
        return a.astype(BF16)
    a0, a1 = _split(a)
    return jnp.concatenate([a0, a0, a1], axis=1)


def _dot_w(a, w_cat, passes):
    return _mm(_lhs_w(a, passes), w_cat, _NN)


def _split_w(w, passes):
    hi = w.astype(BF16)
    if passes == 1:
        return hi
    lo = (w - hi.astype(F32)).astype(BF16)
    return jnp.concatenate([hi, lo, hi], axis=0)


def _sigmoid(x):
    return 1.0 / (1.0 + jnp.exp(-x))


def _cparams(sem):
    return pltpu.CompilerParams(dimension_semantics=sem, vmem_limit_bytes=VMEM_LIMIT)


def _full(shape):
    return pl.BlockSpec(shape, lambda *_: (0,) * len(shape))


def _tile(n, pref):
    t = min(n, pref)
    assert n % t == 0, (n, t)
    return t


def _mod_kernel(c_ref, w_ref, b_ref, o_ref):
    c = c_ref[...]
    cond = c * _sigmoid(c)
    o_ref[...] = _dot(cond, w_ref[...], 3) + b_ref[...]


def _ada_mod(c, w_ada, b_ada):
    depth = w_ada.shape[0]
    c8 = jnp.broadcast_to(c, (SUBLANES, D_MODEL))
    out = pl.pallas_call(
        _mod_kernel,
        out_shape=jax.ShapeDtypeStruct((depth, SUBLANES, 6 * D_MODEL), F32),
        grid=(depth, 6),
        in_specs=[
            pl.BlockSpec((SUBLANES, D_MODEL), lambda l, j: (0, 0)),
            pl.BlockSpec((None, D_MODEL, D_MODEL), lambda l, j: (l, 0, j)),
            pl.BlockSpec((None, 1, D_MODEL), lambda l, j: (l, 0, j)),
        ],
        out_specs=pl.BlockSpec((None, SUBLANES, D_MODEL), lambda l, j: (l, 0, j)),
        compiler_params=_cparams(("arbitrary", "arbitrary")),
        name="ada_mod",
    )(c8, w_ada, b_ada.reshape(depth, 1, 6 * D_MODEL))
    return out[:, 0:1, :]


def _modulated_norm(x, g, sc, sh):
    y = x * lax.rsqrt(jnp.mean(x * x, axis=-1, keepdims=True) + NORM_EPS) * g
    return y * (1.0 + sc) + sh


def _shift_rows(ext, n):
    return pltpu.roll(ext, n, axis=0)


def _pool_mixer(u, halo, wp, scale, t0):
    tt = u.shape[0]
    ext = jnp.concatenate([halo, u], axis=0)
    s2 = ext + _shift_rows(ext, 1)
    s4 = s2 + _shift_rows(s2, 2)
    s8 = s4 + _shift_rows(s4, 4)
    s16 = s8 + _shift_rows(s8, 8)
    grp = lax.broadcasted_iota(jnp.int32, (tt, GROUP_W), 1) // POOL_CG
    tpos = lax.broadcasted_iota(jnp.int32, (tt, GROUP_W), 0) + (t0 + 1)
    wsum = jnp.where(grp == 0, s2[POOL_HALO:], jnp.where(grp == 1, s4[POOL_HALO:],
                     jnp.where(grp == 2, s8[POOL_HALO:], s16[POOL_HALO:])))
    win = jnp.where(grp == 0, 2, jnp.where(grp == 1, 4, jnp.where(grp == 2, 8, 16)))
    cnt = jnp.minimum(tpos, win).astype(F32)
    return _dot_w(wsum / cnt - u, wp, P_POOL) * scale


def _conv_mixer(bg, z, z_halo, cw):
    zext = jnp.concatenate([z_halo, z], axis=0)
    y = (cw[2:3, :] * z + cw[1:2, :] * _shift_rows(zext, 1)[CONV_HALO:]
         + cw[0:1, :] * _shift_rows(zext, 2)[CONV_HALO:])
    return bg * y


def _inproj_kernel(x_ref, g_ref, sc_ref, sh_ref, w_ref, wp_ref, scale_ref, cw_ref,
                   ya_ref, pb_ref, yc_ref, pd_ref, u_tail, z_tail, *, passes, tm):
    i = pl.program_id(0)

    @pl.when(i == 0)
    def _():
        u_tail[...] = jnp.zeros_like(u_tail)
        z_tail[...] = jnp.zeros_like(z_tail)

    h = _modulated_norm(x_ref[...], g_ref[...], sc_ref[...], sh_ref[...])
    lhs = _lhs_w(h, passes)
    proj = lambda col, width: _mm(lhs, w_ref[:, col:col + width], _NN)
    u = proj(0, N_A)
    pc = proj(N_A + N_B, N_C)
    ya_ref[...] = _pool_mixer(u, u_tail[...], wp_ref[...], scale_ref[...], i * tm)
    u_tail[...] = u[tm - POOL_HALO:, :]
    z = pc[:, GROUP_W:2 * GROUP_W] * pc[:, 2 * GROUP_W:3 * GROUP_W]
    yc_ref[...] = _conv_mixer(pc[:, 0:GROUP_W], z, z_tail[...], cw_ref[...])
    z_tail[...] = z[tm - CONV_HALO:, :]
    pb_ref[...] = proj(N_A, N_B)
    pd_ref[...] = proj(N_A + N_B + N_C, N_D)


def _in_proj(x, g, sc, sh, w_parts, wp_parts, pool_scale, conv_w):
    t = x.shape[0]
    tm = _tile(t, 1024)
    vec = _full((1, D_MODEL))
    widths = (GROUP_W, N_B, GROUP_W, N_D)
    return pl.pallas_call(
        functools.partial(_inproj_kernel, passes=P_IN, tm=tm),
        out_shape=[jax.ShapeDtypeStruct((t, n), F32) for n in widths],
        grid=(t // tm,),
        in_specs=[pl.BlockSpec((tm, D_MODEL), lambda i: (i, 0)), vec, vec, vec, _full(w_parts.shape),
                  _full(wp_parts.shape), _full((1, GROUP_W)), _full((CONV_W, GROUP_W))],
        out_specs=[pl.BlockSpec((tm, n), lambda i: (i, 0)) for n in widths],
        scratch_shapes=[pltpu.VMEM((POOL_HALO, GROUP_W), F32), pltpu.VMEM((CONV_HALO, GROUP_W), F32)],
        compiler_params=_cparams(("arbitrary",)),
        name="in_proj_pool_conv",
    )(x, g, sc, sh, w_parts, wp_parts, pool_scale, conv_w)


def _attn_kernel(cur_ref, prev_ref, sink_ref, o_ref, *, tq):
    i = pl.program_id(0)
    nq = ATT_HEADS * HEAD_DIM
    nkv = ATT_KV_HEADS * HEAD_DIM
    blk = ATT_BLOCK
    hd = HEAD_DIM
    rep = ATT_HEADS // ATT_KV_HEADS
    nblk = tq // blk
    ri = lax.broadcasted_iota(jnp.int32, (rep * blk, 2 * blk), 0)
    ki = lax.broadcasted_iota(jnp.int32, (rep * blk, 2 * blk), 1)
    dist = (ri % blk) + blk - ki
    in_win = (dist >= 0) & (dist < blk)
    distf = dist.astype(F32)
    head_col = lax.broadcasted_iota(jnp.int32, (rep * blk, 1), 0) // blk
    bias, sink = [], []
    for g in range(ATT_KV_HEADS):
        slope_g = jnp.zeros((rep * blk, 1), F32)
        sink_g = jnp.zeros((rep * blk, 1), F32)
        for j in range(rep):
            h = g * rep + j
            slope_g = jnp.where(head_col == j, 2.0 ** (-8.0 * (h + 1) / ATT_HEADS), slope_g)
            sink_g = jnp.where(head_col == j, sink_ref[:, h:h + 1], sink_g)
        bias.append(slope_g * distf)
        sink.append(sink_g)
    bias = jnp.stack(bias)[None]
    sink = jnp.stack(sink)[None]

    q = cur_ref[:, 0:nq] * (hd ** -0.5)
    kv = jnp.concatenate([prev_ref[...], cur_ref[:, nq:nq + 2 * nkv]], axis=0)
    qs, ks, vs = [], [], []
    for b in range(nblk):
        for g in range(ATT_KV_HEADS):
            qs.append(jnp.concatenate(
                [q[b * blk:(b + 1) * blk, (g * rep + j) * hd:(g * rep + j + 1) * hd] for j in range(rep)], axis=0))
            ks.append(kv[b * blk:(b + 2) * blk, g * hd:(g + 1) * hd])
            vs.append(kv[b * blk:(b + 2) * blk, nkv + g * hd:nkv + (g + 1) * hd])
    s = _bdot(jnp.stack(qs), jnp.stack(ks), P_ATT, nt=True)
    s = s.reshape(nblk, ATT_KV_HEADS, rep * blk, 2 * blk)
    s = jnp.where(in_win, s - bias, -jnp.inf)
    first_ok = ki >= blk * (1 - (i > 0).astype(jnp.int32))
    s = jnp.concatenate([jnp.where(first_ok, s[0:1], -jnp.inf), s[1:]], axis=0)
    m = jnp.maximum(jnp.max(s, axis=-1, keepdims=True), sink)
    p = jnp.exp(s - m)
    den = jnp.sum(p, axis=-1, keepdims=True) + jnp.exp(sink - m)
    o = _bdot(p.reshape(nblk * ATT_KV_HEADS, rep * blk, 2 * blk), jnp.stack(vs), P_ATT)
    o = o.reshape(nblk, ATT_KV_HEADS, rep * blk, hd) / den
    for b in range(nblk):
        o_ref[b * blk:(b + 1) * blk, :] = jnp.concatenate(
            [o[b, g, j * blk:(j + 1) * blk, :] for g in range(ATT_KV_HEADS) for j in range(rep)], axis=-1)


def _attention(pb, sinks):
    t = pb.shape[0]
    tq = _tile(t, 2048)
    nb = tq // ATT_BLOCK
    return pl.pallas_call(
        functools.partial(_attn_kernel, tq=tq),
        out_shape=jax.ShapeDtypeStruct((t, GROUP_W), F32),
        grid=(t // tq,),
        in_specs=[
            pl.BlockSpec((tq, N_B), lambda i: (i, 0)),
            pl.BlockSpec((ATT_BLOCK, 2 * ATT_KV_HEADS * HEAD_DIM), lambda i: (jnp.maximum(i * nb - 1, 0), 1)),
            _full((1, ATT_HEADS)),
        ],
        out_specs=pl.BlockSpec((tq, GROUP_W), lambda i: (i, 0)),
        compiler_params=_cparams(("parallel",)),
        name="swa_attention",
    )(pb, pb, sinks)


def _rwkv_kernel(*refs, tt, has_vres):
    if has_vres:
        (pd_ref, prev_ref, vf_ref, tril_ref, mu_ref, w0_ref, w2_ref, a0_ref, a2_ref, g2_ref, kk_ref, ka_ref,
         rk_ref, lng_ref, lnb_ref, v0_ref, v1_ref, v2_ref, y_ref, st_ref, yr_s) = refs
    else:
        (pd_ref, prev_ref, tril_ref, mu_ref, w0_ref, w2_ref, a0_ref, a2_ref, g2_ref, kk_ref, ka_ref,
         rk_ref, lng_ref, lnb_ref, y_ref, vout_ref, st_ref, yr_s) = refs
    i = pl.program_id(0)
    nh, hd, gw = RWKV_HEADS, RWKV_HEAD, GROUP_W
    @pl.when(i == 0)
    def _():
        st_ref[...] = jnp.zeros_like(st_ref)

    hr = lax.broadcasted_iota(jnp.int32, (2 * gw, gw), 0) % gw // hd
    hc = lax.broadcasted_iota(jnp.int32, (2 * gw, gw), 1) // hd
    hsum2 = jnp.where(hr == hc, 1.0, 0.0).astype(BF16)

    def head_sum(t2):
        return _mm(jnp.concatenate(_split(t2), axis=1), hsum2, _NN)

    p = pd_ref[...]
    row = lax.broadcasted_iota(jnp.int32, (tt, 1), 0)
    prev_row = prev_ref[SUBLANES - 1:SUBLANES, :] * (i > 0).astype(F32)
    shifted = jnp.where(row == 0, prev_row, pltpu.roll(p, 1, axis=0))
    z = p + (shifted - p) * mu_ref[...]
    r = z[:, 0:gw]
    k = z[:, gw:2 * gw]
    v = z[:, 2 * gw:3 * gw]
    o = 3 * gw
    wd = z[:, o:o + LORA_W]
    ad = z[:, o + LORA_W:o + LORA_W + LORA_A]
    gd = z[:, o + LORA_W + LORA_A:o + LORA_W + LORA_A + LORA_G]
    wpre = -(w0_ref[...] + _dot(jnp.tanh(wd), w2_ref[...], P_LORA))
    softplus = jnp.maximum(wpre, 0.0) + jnp.log(1.0 + jnp.exp(-jnp.abs(wpre)))
    lw = -jnp.exp(-softplus - 0.5)
    a = _sigmoid(a0_ref[...] + _dot(ad, a2_ref[...], P_LORA))
    g = _dot(_sigmoid(gd), g2_ref[...], P_LORA)
    if has_vres:
        mix = _sigmoid(v0_ref[...] + _dot(_dot(v, v1_ref[...], P_LORA), v2_ref[...], P_LORA))
        v = v + (vf_ref[...] - v) * mix
    else:
        vout_ref[...] = v
    kk = k * kk_ref[...]
    kk = kk / jnp.maximum(jnp.sqrt(head_sum(kk * kk)), 1e-12)
    k = k * (1.0 + (a - 1.0) * ka_ref[...])
    nch = tt // CHUNK
    L = CHUNK
    npair = nh // 2
    pw = 2 * hd
    ti = lax.broadcasted_iota(jnp.int32, (L, pw), 0)
    tl = lax.broadcasted_iota(jnp.int32, (L, pw), 1) % hd
    low_strict = ti > tl
    low_incl = ti >= tl
    eye = (ti == tl).astype(F32)
    first_head = lax.broadcasted_iota(jnp.int32, (1, pw), 1) < hd

    def same_block(b):
        return (ti // b) == (tl // b)

    def to_batch(t2, c0, nc):
        return jnp.concatenate([t2[c0 * L:(c0 + nc) * L, q * pw:(q + 1) * pw].reshape(nc, L, pw)
                                for q in range(npair)], axis=0)

    def bd(x):
        zero = jnp.zeros_like(x)
        return jnp.concatenate([jnp.where(first_head, x, zero), jnp.where(first_head, zero, x)], axis=-2)

    def fold(x):
        return x[:, :hd] + x[:, hd:]

    def c16(x):
        return x.astype(BF16)

    lw_hi, lw_lo = _split(lw)
    cl = _mm(tril_ref[...], lw_hi, _NN) + _mm(tril_ref[...], lw_lo, _NN)
    cl3 = cl.reshape(nch, L, gw)
    cl_end = cl3[:, L - 1:L, :]
    e_end = jnp.exp(cl_end - cl3).reshape(tt, gw)
    e_neg = jnp.exp(-cl)
    bb = kk * a
    at_t = -kk * jnp.exp(cl - lw)
    bh_t = bb * e_neg
    kh_t = k * e_neg
    rt_t = r * jnp.exp(cl)
    be_t = bb * e_end
    ke_t = k * e_end
    p_end = jnp.exp(cl_end)

    def chunk_operators(c0, nc):
        at = c16(to_batch(at_t, c0, nc))
        bh = c16(to_batch(bh_t, c0, nc))
        kh = c16(to_batch(kh_t, c0, nc))
        rt = to_batch(rt_t, c0, nc)
        be = to_batch(be_t, c0, nc)
        ke = to_batch(ke_t, c0, nc)
        vb = bd(c16(to_batch(v, c0, nc)))
        pe = jnp.concatenate([p_end[c0:c0 + nc, :, q * pw:(q + 1) * pw] for q in range(npair)], axis=0)
        lhs = jnp.concatenate([at, c16(rt)], axis=1)
        g_b = _bmm(lhs, bd(bh), True)
        g_k = _bmm(lhs, bd(kh), True)
        a_ab = jnp.where(low_strict, g_b[:, :L], 0.0)
        b_rb = jnp.where(low_incl, g_b[:, L:], 0.0)
        a_ak = jnp.where(low_strict, g_k[:, :L], 0.0)
        b_rk = jnp.where(low_incl, g_k[:, L:], 0.0)
        tinv = eye + jnp.where(same_block(2), a_ab, 0.0)
        b = 2
        while b < L:
            e = jnp.where(same_block(2 * b) & ~same_block(b), a_ab, 0.0)
            xe = _bmm(c16(tinv), bd(c16(e)), False)
            tinv = tinv + _bmm(c16(xe), bd(c16(tinv)), False)
            b *= 2
        av = _bmm(c16(a_ak), vb, False)
        wu = _bmm(c16(tinv), jnp.concatenate([bd(at), bd(c16(av))], axis=2), False)
        rhs = jnp.concatenate([bd(c16(wu[:, :, :pw])), bd(c16(wu[:, :, pw:]))], axis=2)
        lhs_b = jnp.concatenate([c16(b_rb), c16(jnp.swapaxes(bd(be), 1, 2))], axis=1)
        lhs_k = jnp.concatenate([c16(b_rk), c16(jnp.swapaxes(bd(ke), 1, 2))], axis=1)
        o_wu = _bmm(lhs_b, rhs, False)
        o_v = _bmm(lhs_k, vb, False)
        rq = rt + o_wu[:, :L, :pw]
        y0 = o_wu[:, :L, pw:] + o_v[:, :L]
        m2 = eye * pe + fold(o_wu[:, L:, :pw])
        c2 = fold(o_wu[:, L:, pw:] + o_v[:, L:])
        return jnp.concatenate([rq, m2], axis=1), jnp.concatenate([y0, c2], axis=1)

    def state_walk(zs, ops, c0, nc):
        rqm, y0c = ops
        for c in range(nc):
            for q in range(npair):
                b = q * nc + c
                yz = _dot(rqm[b], bd(zs[q]), P_STATE) + y0c[b]
                yr_s[(c0 + c) * L:(c0 + c + 1) * L, q * pw:(q + 1) * pw] = yz[:L]
                zs[q] = yz[L:]
        return zs

    zs = state_walk([st_ref[q] for q in range(npair)], chunk_operators(0, nch), 0, nch)
    for q in range(npair):
        st_ref[q] = zs[q]

    y = yr_s[...]
    inv_n = 1.0 / hd
    mean = head_sum(y) * inv_n
    yc = y - mean
    var = head_sum(yc * yc) * inv_n
    yn = yc * lax.rsqrt(var + RWKV_GN_EPS) * lng_ref[...] + lnb_ref[...]
    bonus = head_sum(r * k * rk_ref[...]) * v
    y_ref[...] = (yn + bonus) * g


def _rwkv(pd, v_first, prm):
    t = pd.shape[0]
    tt = _tile(t, 512)
    nch = tt // CHUNK
    has_vres = v_first is not None
    gw = GROUP_W
    row_blk = lambda n: pl.BlockSpec((tt, n), lambda i: (i, 0))
    vec = _full((1, gw))
    in_specs = [row_blk(N_D),
                pl.BlockSpec((SUBLANES, N_D), lambda i: (jnp.maximum(i * (tt // SUBLANES) - 1, 0), 0))]
    args = [pd, pd]
    if has_vres:
        in_specs.append(row_blk(gw))
        args.append(v_first)
    tpos = jnp.arange(tt)
    tril = ((tpos[:, None] // CHUNK == tpos[None, :] // CHUNK) & (tpos[:, None] >= tpos[None, :])).astype(BF16)
    in_specs.append(_full((tt, tt)))
    args.append(tril)
    in_specs += [_full((1, N_D)), vec, _full((LORA_W, gw)), vec, _full((LORA_A, gw)), _full((LORA_G, gw)),
                 vec, vec, vec, vec, vec]
    args += [prm["mu"], prm["w0"], prm["w2"], prm["a0"], prm["a2"], prm["g2"], prm["k_k"], prm["k_a"],
             prm["r_k"], prm["ln_g"], prm["ln_b"]]
    if has_vres:
        in_specs += [vec, _full((gw, LORA_V)), _full((LORA_V, gw))]
        args += [prm["v0"], prm["v1"], prm["v2"]]
        out_shape = jax.ShapeDtypeStruct((t, gw), F32)
        out_specs = row_blk(gw)
    else:
        out_shape = [jax.ShapeDtypeStruct((t, gw), F32)] * 2
        out_specs = [row_blk(gw)] * 2
    npair, pw = RWKV_HEADS // 2, 2 * RWKV_HEAD
    scratch = [pltpu.VMEM((npair, RWKV_HEAD, pw), F32), pltpu.VMEM((tt, gw), F32)]
    return pl.pallas_call(
        functools.partial(_rwkv_kernel, tt=tt, has_vres=has_vres),
        out_shape=out_shape,
        grid=(t // tt,),
        in_specs=in_specs,
        out_specs=out_specs,
        scratch_shapes=scratch,
        compiler_params=_cparams(("arbitrary",)),
        name="rwkv7",
    )(*args)


def _route_record(lg):
    lane = lax.broadcasted_iota(jnp.int32, (lg.shape[0], LANES), 1)
    lanef = lane.astype(F32)
    gl = jnp.where(lane < N_GROUPS, lg[:, 0:LANES], -jnp.inf)
    gmax = jnp.max(gl, axis=-1, keepdims=True)
    gp = 1.0 / jnp.sum(jnp.exp(gl - gmax), axis=-1, keepdims=True)
    gi = jnp.min(jnp.where(gl == gmax, lanef, float(LANES)), axis=-1, keepdims=True)
    lo = gi * EXP_PER_GROUP
    in_grp = (lanef >= lo) & (lanef < lo + EXP_PER_GROUP)
    el = jnp.where(in_grp, lg[:, LANES:2 * LANES], -jnp.inf)
    m1 = jnp.max(el, axis=-1, keepdims=True)
    i1 = jnp.min(jnp.where(el == m1, lanef, float(LANES)), axis=-1, keepdims=True)
    el2 = jnp.where(lanef == i1, -jnp.inf, el)
    m2 = jnp.max(el2, axis=-1, keepdims=True)
    i2 = jnp.min(jnp.where(el2 == m2, lanef, float(LANES)), axis=-1, keepdims=True)
    e2 = jnp.exp(m2 - m1)
    w_top = gp / (1.0 + e2)
    return jnp.where(lanef == i1 - lo, w_top, jnp.where(lanef == i2 - lo, w_top * e2,
                     jnp.where(lane == ROUTE_GROUP_LANE, gi, 0.0)))


def _outproj_kernel(ya_ref, yb_ref, yc_ref, yd_ref, x_ref, wo_ref, gt_ref, g_ref, sc_ref, sh_ref, wr_ref, br_ref,
                    x1_ref, h2_ref, gate_ref, *, tm):
    sub = tm // OUT_SUB
    for blk in range(OUT_SUB):
        rows = slice(blk * sub, (blk + 1) * sub)
        ycat = jnp.concatenate([ya_ref[rows, :], yb_ref[rows, :], yc_ref[rows, :], yd_ref[rows, :]], axis=1)
        x1 = x_ref[rows, :] + gt_ref[...] * _dot_w(ycat, wo_ref[...], P_OUT)
        x1_ref[rows, :] = x1
        h2 = _modulated_norm(x1, g_ref[...], sc_ref[...], sh_ref[...])
        h2_ref[rows, :] = h2.astype(h2_ref.dtype)
        gate_ref[rows, :] = _route_record(_dot_w(h2, wr_ref[...], P_ROUTER) + br_ref[...])


def _out_proj(ya, yb, yc, yd, x, wo_parts, gt, g, sc, sh, w_router, b_router):
    t = x.shape[0]
    tm = _tile(t, 1024)
    row_blk = lambda n: pl.BlockSpec((tm, n), lambda i: (i, 0))
    vec = _full((1, D_MODEL))
    return pl.pallas_call(
        functools.partial(_outproj_kernel, tm=tm),
        out_shape=[jax.ShapeDtypeStruct((t, D_MODEL), F32), jax.ShapeDtypeStruct((t, D_MODEL), BF16),
                   jax.ShapeDtypeStruct((t, LANES), F32)],
        grid=(t // tm,),
        in_specs=[row_blk(GROUP_W)] * 4 + [row_blk(D_MODEL), _full(wo_parts.shape), vec, vec, vec, vec,
                                           _full(w_router.shape), _full((1, 2 * LANES))],
        out_specs=[row_blk(D_MODEL), row_blk(D_MODEL), row_blk(LANES)],
        compiler_params=_cparams(("parallel",)),
        name="out_proj_router",
    )(ya, yb, yc, yd, x, wo_parts, gt, g, sc, sh, w_router, b_router)


def _moe_kernel(h_ref, route_ref, tril_ref, wg_ref, wu_ref, wd_ref, x1_ref, gt_ref, fg_ref, o_ref,
                acc_ref, col_ref, row_ref, cnt_ref, *, final, tm, cap, cap_extra):
    g = pl.program_id(1)
    lane = lax.broadcasted_iota(jnp.int32, (tm, LANES), 1)

    @pl.when(g == 0)
    def _():
        acc_ref[...] = jnp.zeros_like(acc_ref)
        gi = route_ref[:, ROUTE_GROUP_LANE:ROUTE_GROUP_LANE + 1]
        onehot = lane.astype(F32) == gi
        cum = _mm(tril_ref[...], jnp.where(onehot, 1.0, 0.0).astype(BF16), _NN)
        rank = jnp.sum(jnp.where(onehot, cum, 0.0), axis=-1, keepdims=True) - 1.0
        cnt_ref[0:1, :] = cum[tm - 1:tm, :]
        cnt_ref[1:2, :] = cum[tm // 2 - 1:tm // 2, :]
        rec = jnp.where(lane == 0, rank, jnp.where(lane == 1, gi, 0.0))
        col_ref[...] = rec
        row_ref[...] = rec.T[0:SUBLANES, :]

    gf = g.astype(F32)
    route = route_ref[...]
    slot_gate = jnp.where(route[:, ROUTE_GROUP_LANE:ROUTE_GROUP_LANE + 1] == gf, route, 0.0)
    h = h_ref[...]
    hg = _mm(h, wg_ref[...], _NN)
    hu = _mm(h, wu_ref[...], _NN)
    hu = jnp.concatenate([hu[:, j * D_EXPERT:(j + 1) * D_EXPERT] * slot_gate[:, j:j + 1]
                          for j in range(EXP_PER_GROUP)], axis=1)
    act = hg * _sigmoid(hg) * hu
    acc_ref[...] += _mm(act.astype(BF16), wd_ref[...], _NN)

    @pl.when(g == pl.num_programs(1) - 1)
    def _():
        out = x1_ref[...] + gt_ref[...] * acc_ref[...]
        if final:
            out = out * lax.rsqrt(jnp.mean(out * out, axis=-1, keepdims=True) + NORM_EPS) * fg_ref[...]
        o_ref[...] = out


def _moe(h2, route, wg, wu, wd, x1, gt, final_g, final):
    t = h2.shape[0]
    tm = _tile(t, MOE_TILE)
    cap = tm // N_GROUPS
    cap_extra = cap // 2
    nw = EXP_PER_GROUP * D_EXPERT
    vec = _full((1, D_MODEL))
    tpos = jnp.arange(tm)
    tril = (tpos[:, None] >= tpos[None, :]).astype(BF16)
    return pl.pallas_call(
        functools.partial(_moe_kernel, final=final, tm=tm, cap=cap, cap_extra=cap_extra),
        out_shape=jax.ShapeDtypeStruct((t, D_MODEL), F32),
        grid=(t // tm, N_GROUPS),
        in_specs=[
            pl.BlockSpec((tm, D_MODEL), lambda i, g: (i, 0)),
            pl.BlockSpec((tm, LANES), lambda i, g: (i, 0)),
            _full((tm, tm)),
            pl.BlockSpec((D_MODEL, nw), lambda i, g: (0, g)),
            pl.BlockSpec((D_MODEL, nw), lambda i, g: (0, g)),
            pl.BlockSpec((nw, D_MODEL), lambda i, g: (g, 0)),
            pl.BlockSpec((tm, D_MODEL), lambda i, g: (i, 0)),
            vec, vec,
        ],
        out_specs=pl.BlockSpec((tm, D_MODEL), lambda i, g: (i, 0)),
        scratch_shapes=[pltpu.VMEM((tm, D_MODEL), F32), pltpu.VMEM((tm, LANES), F32),
                        pltpu.VMEM((SUBLANES, tm), F32), pltpu.VMEM((2, LANES), F32)],
        compiler_params=_cparams(("parallel", "arbitrary")),
        name="moe_experts",
    )(h2, route, tril, wg, wu, wd, x1, gt, final_g)


def kernel(x, c, w_ada, b_ada, norm1_g, norm2_g, w_in, w_out, pool_w, pool_scale, attn_sinks, conv_w, rwkv_mu,
           rwkv_w0, rwkv_w2, rwkv_a0, rwkv_a2, rwkv_g2, rwkv_k_k, rwkv_k_a, rwkv_r_k, rwkv_ln_g, rwkv_ln_b,
           rwkv_v0, rwkv_v1, rwkv_v2, moe_w_grp, moe_b_grp, moe_w_exp, moe_b_exp, moe_w_gate, moe_w_up,
           moe_w_down, final_g):
    bsz, t, d = x.shape
    assert bsz == 1 and d == D_MODEL
    depth = w_ada.shape[0]
    xs = x.reshape(t, d)
    mod = _ada_mod(c, w_ada, b_ada)
    v_first = None
    row = lambda a: a.reshape(1, -1)
    for l in range(depth):
        sh1, sc1, gt1, sh2, sc2, gt2 = [mod[l, :, j * d:(j + 1) * d] for j in range(6)]
        wp = jax.scipy.linalg.block_diag(*[pool_w[l, gidx] for gidx in range(len(POOL_WINDOWS))])
        ya, pb, yc, pd = _in_proj(xs, row(norm1_g[l]), sc1, sh1, _split_w(w_in[l], P_IN), _split_w(wp, P_POOL),
                                  row(pool_scale[l]), conv_w[l])
        yb = _attention(pb, row(attn_sinks[l]))
        prm = dict(mu=row(rwkv_mu[l]), w0=row(rwkv_w0[l]), w2=rwkv_w2[l], a0=row(rwkv_a0[l]), a2=rwkv_a2[l],
                   g2=rwkv_g2[l], k_k=row(rwkv_k_k[l]), k_a=row(rwkv_k_a[l]), r_k=row(rwkv_r_k[l]),
                   ln_g=row(rwkv_ln_g[l]), ln_b=row(rwkv_ln_b[l]))
        if l == 0:
            yd, v_first = _rwkv(pd, None, prm)
        else:
            prm.update(v0=row(rwkv_v0[l - 1]), v1=rwkv_v1[l - 1], v2=rwkv_v2[l - 1])
            yd = _rwkv(pd, v_first, prm)
        lane_pad = lambda a: jnp.pad(a, ((0, 0), (0, LANES - a.shape[1])))
        w_router = jnp.concatenate([lane_pad(moe_w_grp[l]), lane_pad(moe_w_exp[l])], axis=1)
        b_router = jnp.concatenate([lane_pad(row(moe_b_grp[l])), lane_pad(row(moe_b_exp[l]))], axis=1)
        x1, h2, gates = _out_proj(ya, yb, yc, yd, xs, _split_w(w_out[l], P_OUT), gt1, row(norm2_g[l]), sc2, sh2,
                                  _split_w(w_router, P_ROUTER), b_router)
        wg = jnp.transpose(moe_w_gate[l], (1, 0, 2)).reshape(d, N_EXPERTS * D_EXPERT).astype(BF16)
        wu = jnp.transpose(moe_w_up[l], (1, 0, 2)).reshape(d, N_EXPERTS * D_EXPERT).astype(BF16)
        wd = moe_w_down[l].astype(BF16).reshape(N_EXPERTS * D_EXPERT, d)
        xs = _moe(h2, gates, wg, wu, wd, x1, gt2, row(final_g), final=(l == depth - 1))
    return xs.reshape(bsz, t, d)
```

```python
import functools

import jax
import jax.numpy as jnp
from jax import lax
from jax.experimental import pallas as pl
from jax.experimental.pallas import tpu as pltpu

F32 = jnp.float32
BF16 = jnp.bfloat16

D_MODEL = 1024
GROUP_W = 256
NORM_EPS = 1e-6
POOL_WINDOWS = (2, 4, 8, 16)
POOL_CG = 64
HEAD_DIM = 64
ATT_HEADS = 4
ATT_KV_HEADS = 2
ATT_BLOCK = 128
CONV_W = 3
RWKV_HEAD = 64
RWKV_HEADS = 4
LORA_W = 32
LORA_A = 32
LORA_G = 64
LORA_V = 32
RWKV_GN_EPS = 64e-5
N_A = GROUP_W
N_B = (ATT_HEADS + 2 * ATT_KV_HEADS) * HEAD_DIM
N_C = 3 * GROUP_W
N_D = 3 * GROUP_W + LORA_W + LORA_A + LORA_G
N_GROUPS = 4
EXP_PER_GROUP = 8
N_EXPERTS = 32
D_EXPERT = 128

LANES = 128
SUBLANES = 8
VMEM_LIMIT = 56 * 1024 * 1024

CHUNK = 64
POOL_HALO = 16
CONV_HALO = 8
OUT_SUB = 2
ROUTE_GROUP_LANE = 8
MOE_TILE = 1024

P_IN = 1
P_OUT = 1
P_POOL = 1
P_ATT = 1
P_LORA = 1
P_STATE = 1
P_ROUTER = 3


def _split(a):
    hi = a.astype(BF16)
    lo = (a - hi.astype(F32)).astype(BF16)
    return hi, lo


def _mm(a, b, dims):
    return lax.dot_general(a, b, (dims, ((), ())), preferred_element_type=F32)


_NN = ((1,), (0,))
_NT = ((1,), (1,))


def _dot(a, b, passes=1, dims=_NN):
    if passes == 1:
        return _mm(a.astype(BF16), b.astype(BF16), dims)
    a0, a1 = _split(a)
    b0, b1 = _split(b)
    return _mm(a0, b0, dims) + (_mm(a0, b1, dims) + _mm(a1, b0, dims))


def _bmm(a, b, nt):
    dims = (((2,), (2 if nt else 1,)), ((0,), (0,)))
    return lax.dot_general(a, b, dims, preferred_element_type=F32)


def _bdot(a, b, passes=1, nt=False):
    if passes == 1:
        return _bmm(a.astype(BF16), b.astype(BF16), nt)
    a0, a1 = _split(a)
    b0, b1 = _split(b)
    return _bmm(a0, b0, nt) + (_bmm(a0, b1, nt) + _bmm(a1, b0, nt))


def _lhs_w(a, passes):
    if passes == 1:
        return a.astype(BF16)
    a0, a1 = _split(a)
    return jnp.concatenate([a0, a0, a1], axis=1)


def _dot_w(a, w_cat, passes):
    return _mm(_lhs_w(a, passes), w_cat, _NN)


def _split_w(w, passes):
    hi = w.astype(BF16)
    if passes == 1:
        return hi
    lo = (w - hi.astype(F32)).astype(BF16)
    return jnp.concatenate([hi, lo, hi], axis=0)


def _sigmoid(x):
    return 1.0 / (1.0 + jnp.exp(-x))


def _cparams(sem):
    return pltpu.CompilerParams(dimension_semantics=sem, vmem_limit_bytes=VMEM_LIMIT)


def _full(shape):
    return pl.BlockSpec(shape, lambda *_: (0,) * len(shape))


def _tile(n, pref):
    t = min(n, pref)
    assert n % t == 0, (n, t)
    return t


def _mod_kernel(c_ref, w_ref, b_ref, o_ref):
    c = c_ref[...]
    cond = c * _sigmoid(c)
    o_ref[...] = _dot(cond, w_ref[...], 3) + b_ref[...]


def _ada_mod(c, w_ada, b_ada):
    depth = w_ada.shape[0]
    c8 = jnp.broadcast_to(c, (SUBLANES, D_MODEL))
    out = pl.pallas_call(
        _mod_kernel,
        out_shape=jax.ShapeDtypeStruct((depth, SUBLANES, 6 * D_MODEL), F32),
        grid=(depth, 6),
        in_specs=[
            pl.BlockSpec((SUBLANES, D_MODEL), lambda l, j: (0, 0)),
            pl.BlockSpec((None, D_MODEL, D_MODEL), lambda l, j: (l, 0, j)),
            pl.BlockSpec((None, 1, D_MODEL), lambda l, j: (l, 0, j)),
        ],
        out_specs=pl.BlockSpec((None, SUBLANES, D_MODEL), lambda l, j: (l, 0, j)),
        compiler_params=_cparams(("arbitrary", "arbitrary")),
        name="ada_mod",
    )(c8, w_ada, b_ada.reshape(depth, 1, 6 * D_MODEL))
    return out[:, 0:1, :]


def _modulated_norm(x, g, sc, sh):
    y = x * lax.rsqrt(jnp.mean(x * x, axis=-1, keepdims=True) + NORM_EPS) * g
    return y * (1.0 + sc) + sh


def _shift_rows(ext, n):
    return pltpu.roll(ext, n, axis=0)


def _pool_mixer(u, halo, wp, scale, t0):
    tt = u.shape[0]
    ext = jnp.concatenate([halo, u], axis=0)
    s2 = ext + _shift_rows(ext, 1)
    s4 = s2 + _shift_rows(s2, 2)
    s8 = s4 + _shift_rows(s4, 4)
    s16 = s8 + _shift_rows(s8, 8)
    grp = lax.broadcasted_iota(jnp.int32, (tt, GROUP_W), 1) // POOL_CG
    tpos = lax.broadcasted_iota(jnp.int32, (tt, GROUP_W), 0) + (t0 + 1)
    wsum = jnp.where(grp == 0, s2[POOL_HALO:], jnp.where(grp == 1, s4[POOL_HALO:],
                     jnp.where(grp == 2, s8[POOL_HALO:], s16[POOL_HALO:])))
    win = jnp.where(grp == 0, 2, jnp.where(grp == 1, 4, jnp.where(grp == 2, 8, 16)))
    cnt = jnp.minimum(tpos, win).astype(F32)
    return _dot_w(wsum / cnt - u, wp, P_POOL) * scale


def _conv_mixer(bg, z, z_halo, cw):
    zext = jnp.concatenate([z_halo, z], axis=0)
    y = (cw[2:3, :] * z + cw[1:2, :] * _shift_rows(zext, 1)[CONV_HALO:]
         + cw[0:1, :] * _shift_rows(zext, 2)[CONV_HALO:])
    return bg * y


def _inproj_kernel(x_ref, g_ref, sc_ref, sh_ref, w_ref, wp_ref, scale_ref, cw_ref,
                   ya_ref, pb_ref, yc_ref, pd_ref, u_tail, z_tail, *, passes, tm):
    i = pl.program_id(0)

    @pl.when(i == 0)
    def _():
        u_tail[...] = jnp.zeros_like(u_tail)
        z_tail[...] = jnp.zeros_like(z_tail)

    h = _modulated_norm(x_ref[...], g_ref[...], sc_ref[...], sh_ref[...])
    lhs = _lhs_w(h, passes)
    proj = lambda col, width: _mm(lhs, w_ref[:, col:col + width], _NN)
    u = proj(0, N_A)
    pc = proj(N_A + N_B, N_C)
    ya_ref[...] = _pool_mixer(u, u_tail[...], wp_ref[...], scale_ref[...], i * tm).astype(ya_ref.dtype)
    u_tail[...] = u[tm - POOL_HALO:, :]
    z = pc[:, GROUP_W:2 * GROUP_W] * pc[:, 2 * GROUP_W:3 * GROUP_W]
    yc_ref[...] = _conv_mixer(pc[:, 0:GROUP_W], z, z_tail[...], cw_ref[...]).astype(yc_ref.dtype)
    z_tail[...] = z[tm - CONV_HALO:, :]
    pb_ref[...] = proj(N_A, N_B).astype(pb_ref.dtype)
    pd_ref[...] = proj(N_A + N_B + N_C, N_D)


def _in_proj(x, g, sc, sh, w_parts, wp_parts, pool_scale, conv_w):
    t = x.shape[0]
    tm = _tile(t, 1024)
    vec = _full((1, D_MODEL))
    widths = (GROUP_W, N_B, GROUP_W, N_D)
    return pl.pallas_call(
        functools.partial(_inproj_kernel, passes=P_IN, tm=tm),
        out_shape=[jax.ShapeDtypeStruct((t, n), dt) for n, dt in zip(widths, (BF16, BF16, BF16, F32))],
        grid=(t // tm,),
        in_specs=[pl.BlockSpec((tm, D_MODEL), lambda i: (i, 0)), vec, vec, vec, _full(w_parts.shape),
                  _full(wp_parts.shape), _full((1, GROUP_W)), _full((CONV_W, GROUP_W))],
        out_specs=[pl.BlockSpec((tm, n), lambda i: (i, 0)) for n in widths],
        scratch_shapes=[pltpu.VMEM((POOL_HALO, GROUP_W), F32), pltpu.VMEM((CONV_HALO, GROUP_W), F32)],
        compiler_params=_cparams(("arbitrary",)),
        name="in_proj_pool_conv",
    )(x, g, sc, sh, w_parts, wp_parts, pool_scale, conv_w)


def _attn_kernel(cur_ref, prev_ref, sink_ref, o_ref, *, tq):
    i = pl.program_id(0)
    nq = ATT_HEADS * HEAD_DIM
    nkv = ATT_KV_HEADS * HEAD_DIM
    blk = ATT_BLOCK
    hd = HEAD_DIM
    rep = ATT_HEADS // ATT_KV_HEADS
    nblk = tq // blk
    ri = lax.broadcasted_iota(jnp.int32, (rep * blk, 2 * blk), 0)
    ki = lax.broadcasted_iota(jnp.int32, (rep * blk, 2 * blk), 1)
    dist = (ri % blk) + blk - ki
    in_win = (dist >= 0) & (dist < blk)
    distf = dist.astype(F32)
    head_col = lax.broadcasted_iota(jnp.int32, (rep * blk, 1), 0) // blk
    bias, sink = [], []
    for g in range(ATT_KV_HEADS):
        slope_g = jnp.zeros((rep * blk, 1), F32)
        sink_g = jnp.zeros((rep * blk, 1), F32)
        for j in range(rep):
            h = g * rep + j
            slope_g = jnp.where(head_col == j, 2.0 ** (-8.0 * (h + 1) / ATT_HEADS), slope_g)
            sink_g = jnp.where(head_col == j, sink_ref[:, h:h + 1], sink_g)
        bias.append(slope_g * distf)
        sink.append(sink_g)
    bias = jnp.stack(bias)[None]
    sink = jnp.stack(sink)[None]

    q = cur_ref[:, 0:nq].astype(F32) * (hd ** -0.5)
    kv = jnp.concatenate([prev_ref[...], cur_ref[:, nq:nq + 2 * nkv]], axis=0).astype(F32)
    qs, ks, vs = [], [], []
    for b in range(nblk):
        for g in range(ATT_KV_HEADS):
            qs.append(jnp.concatenate(
                [q[b * blk:(b + 1) * blk, (g * rep + j) * hd:(g * rep + j + 1) * hd] for j in range(rep)], axis=0))
            ks.append(kv[b * blk:(b + 2) * blk, g * hd:(g + 1) * hd])
            vs.append(kv[b * blk:(b + 2) * blk, nkv + g * hd:nkv + (g + 1) * hd])
    s = _bdot(jnp.stack(qs), jnp.stack(ks), P_ATT, nt=True)
    s = s.reshape(nblk, ATT_KV_HEADS, rep * blk, 2 * blk)
    s = jnp.where(in_win, s - bias, -jnp.inf)
    first_ok = ki >= blk * (1 - (i > 0).astype(jnp.int32))
    s = jnp.concatenate([jnp.where(first_ok, s[0:1], -jnp.inf), s[1:]], axis=0)
    m = jnp.maximum(jnp.max(s, axis=-1, keepdims=True), sink)
    p = jnp.exp(s - m)
    den = jnp.sum(p, axis=-1, keepdims=True) + jnp.exp(sink - m)
    o = _bdot(p.reshape(nblk * ATT_KV_HEADS, rep * blk, 2 * blk), jnp.stack(vs), P_ATT)
    o = o.reshape(nblk, ATT_KV_HEADS, rep * blk, hd) / den
    for b in range(nblk):
        o_ref[b * blk:(b + 1) * blk, :] = jnp.concatenate(
            [o[b, g, j * blk:(j + 1) * blk, :] for g in range(ATT_KV_HEADS) for j in range(rep)],
            axis=-1).astype(o_ref.dtype)


def _attention(pb, sinks):
    t = pb.shape[0]
    tq = _tile(t, 2048)
    nb = tq // ATT_BLOCK
    return pl.pallas_call(
        functools.partial(_attn_kernel, tq=tq),
        out_shape=jax.ShapeDtypeStruct((t, GROUP_W), BF16),
        grid=(t // tq,),
        in_specs=[
            pl.BlockSpec((tq, N_B), lambda i: (i, 0)),
            pl.BlockSpec((ATT_BLOCK, 2 * ATT_KV_HEADS * HEAD_DIM), lambda i: (jnp.maximum(i * nb - 1, 0), 1)),
            _full((1, ATT_HEADS)),
        ],
        out_specs=pl.BlockSpec((tq, GROUP_W), lambda i: (i, 0)),
        compiler_params=_cparams(("parallel",)),
        name="swa_attention",
    )(pb, pb, sinks)


def _rwkv_kernel(*refs, tt, has_vres):
    if has_vres:
        (pd_ref, prev_ref, vf_ref, tril_ref, mu_ref, w0_ref, w2_ref, a0_ref, a2_ref, g2_ref, kk_ref, ka_ref,
         rk_ref, lng_ref, lnb_ref, v0_ref, v1_ref, v2_ref, y_ref, st_ref, yr_s) = refs
    else:
        (pd_ref, prev_ref, tril_ref, mu_ref, w0_ref, w2_ref, a0_ref, a2_ref, g2_ref, kk_ref, ka_ref,
         rk_ref, lng_ref, lnb_ref, y_ref, vout_ref, st_ref, yr_s) = refs
    i = pl.program_id(0)
    nh, hd, gw = RWKV_HEADS, RWKV_HEAD, GROUP_W
    @pl.when(i == 0)
    def _():
        st_ref[...] = jnp.zeros_like(st_ref)

    hr = lax.broadcasted_iota(jnp.int32, (2 * gw, gw), 0) % gw // hd
    hc = lax.broadcasted_iota(jnp.int32, (2 * gw, gw), 1) // hd
    hsum2 = jnp.where(hr == hc, 1.0, 0.0).astype(BF16)

    def head_sum(t2):
        return _mm(jnp.concatenate(_split(t2), axis=1), hsum2, _NN)

    p = pd_ref[...]
    row = lax.broadcasted_iota(jnp.int32, (tt, 1), 0)
    prev_row = prev_ref[SUBLANES - 1:SUBLANES, :] * (i > 0).astype(F32)
    shifted = jnp.where(row == 0, prev_row, pltpu.roll(p, 1, axis=0))
    z = p + (shifted - p) * mu_ref[...]
    r = z[:, 0:gw]
    k = z[:, gw:2 * gw]
    v = z[:, 2 * gw:3 * gw]
    o = 3 * gw
    wd = z[:, o:o + LORA_W]
    ad = z[:, o + LORA_W:o + LORA_W + LORA_A]
    gd = z[:, o + LORA_W + LORA_A:o + LORA_W + LORA_A + LORA_G]
    wpre = -(w0_ref[...] + _dot(jnp.tanh(wd), w2_ref[...], P_LORA))
    softplus = jnp.maximum(wpre, 0.0) + jnp.log(1.0 + jnp.exp(-jnp.abs(wpre)))
    lw = -jnp.exp(-softplus - 0.5)
    a = _sigmoid(a0_ref[...] + _dot(ad, a2_ref[...], P_LORA))
    g = _dot(_sigmoid(gd), g2_ref[...], P_LORA)
    if has_vres:
        mix = _sigmoid(v0_ref[...] + _dot(_dot(v, v1_ref[...], P_LORA), v2_ref[...], P_LORA))
        v = v + (vf_ref[...] - v) * mix
    else:
        vout_ref[...] = v
    kk = k * kk_ref[...]
    kk = kk / jnp.maximum(jnp.sqrt(head_sum(kk * kk)), 1e-12)
    k = k * (1.0 + (a - 1.0) * ka_ref[...])
    nch = tt // CHUNK
    L = CHUNK
    npair = nh // 2
    pw = 2 * hd
    ti = lax.broadcasted_iota(jnp.int32, (L, pw), 0)
    tl = lax.broadcasted_iota(jnp.int32, (L, pw), 1) % hd
    low_strict = ti > tl
    low_incl = ti >= tl
    eye = (ti == tl).astype(F32)
    first_head = lax.broadcasted_iota(jnp.int32, (1, pw), 1) < hd

    def same_block(b):
        return (ti // b) == (tl // b)

    def to_batch(t2, c0, nc):
        return jnp.concatenate([t2[c0 * L:(c0 + nc) * L, q * pw:(q + 1) * pw].reshape(nc, L, pw)
                                for q in range(npair)], axis=0)

    def bd(x):
        zero = jnp.zeros_like(x)
        return jnp.concatenate([jnp.where(first_head, x, zero), jnp.where(first_head, zero, x)], axis=-2)

    def fold(x):
        return x[:, :hd] + x[:, hd:]

    def c16(x):
        return x.astype(BF16)

    lw_hi, lw_lo = _split(lw)
    cl = _mm(tril_ref[...], lw_hi, _NN) + _mm(tril_ref[...], lw_lo, _NN)
    cl3 = cl.reshape(nch, L, gw)
    cl_end = cl3[:, L - 1:L, :]
    e_end = jnp.exp(cl_end - cl3).reshape(tt, gw)
    e_neg = jnp.exp(-cl)
    bb = kk * a
    at_t = -kk * jnp.exp(cl - lw)
    bh_t = bb * e_neg
    kh_t = k * e_neg
    rt_t = r * jnp.exp(cl)
    be_t = bb * e_end
    ke_t = k * e_end
    p_end = jnp.exp(cl_end)

    def chunk_operators(c0, nc):
        at = c16(to_batch(at_t, c0, nc))
        bh = c16(to_batch(bh_t, c0, nc))
        kh = c16(to_batch(kh_t, c0, nc))
        rt = to_batch(rt_t, c0, nc)
        be = to_batch(be_t, c0, nc)
        ke = to_batch(ke_t, c0, nc)
        vb = bd(c16(to_batch(v, c0, nc)))
        pe = jnp.concatenate([p_end[c0:c0 + nc, :, q * pw:(q + 1) * pw] for q in range(npair)], axis=0)
        lhs = jnp.concatenate([at, c16(rt)], axis=1)
        g_b = _bmm(lhs, bd(bh), True)
        g_k = _bmm(lhs, bd(kh), True)
        a_ab = jnp.where(low_strict, g_b[:, :L], 0.0)
        b_rb = jnp.where(low_incl, g_b[:, L:], 0.0)
        a_ak = jnp.where(low_strict, g_k[:, :L], 0.0)
        b_rk = jnp.where(low_incl, g_k[:, L:], 0.0)
        tinv = eye + jnp.where(same_block(2), a_ab, 0.0)
        b = 2
        while b < L:
            e = jnp.where(same_block(2 * b) & ~same_block(b), a_ab, 0.0)
            xe = _bmm(c16(tinv), bd(c16(e)), False)
            tinv = tinv + _bmm(c16(xe), bd(c16(tinv)), False)
            b *= 2
        av = _bmm(c16(a_ak), vb, False)
        wu = _bmm(c16(tinv), jnp.concatenate([bd(at), bd(c16(av))], axis=2), False)
        rhs = jnp.concatenate([bd(c16(wu[:, :, :pw])), bd(c16(wu[:, :, pw:]))], axis=2)
        lhs_b = jnp.concatenate([c16(b_rb), c16(jnp.swapaxes(bd(be), 1, 2))], axis=1)
        lhs_k = jnp.concatenate([c16(b_rk), c16(jnp.swapaxes(bd(ke), 1, 2))], axis=1)
        o_wu = _bmm(lhs_b, rhs, False)
        o_v = _bmm(lhs_k, vb, False)
        rq = rt + o_wu[:, :L, :pw]
        y0 = o_wu[:, :L, pw:] + o_v[:, :L]
        m2 = eye * pe + fold(o_wu[:, L:, :pw])
        c2 = fold(o_wu[:, L:, pw:] + o_v[:, L:])
        return jnp.concatenate([rq, m2], axis=1), jnp.concatenate([y0, c2], axis=1)

    def state_walk(zs, ops, c0, nc):
        rqm, y0c = ops
        for c in range(nc):
            for q in range(npair):
                b = q * nc + c
                yz = _dot(rqm[b], bd(zs[q]), P_STATE) + y0c[b]
                yr_s[(c0 + c) * L:(c0 + c + 1) * L, q * pw:(q + 1) * pw] = yz[:L]
                zs[q] = yz[L:]
        return zs

    zs = state_walk([st_ref[q] for q in range(npair)], chunk_operators(0, nch), 0, nch)
    for q in range(npair):
        st_ref[q] = zs[q]

    y = yr_s[...]
    inv_n = 1.0 / hd
    mean = head_sum(y) * inv_n
    yc = y - mean
    var = head_sum(yc * yc) * inv_n
    yn = yc * lax.rsqrt(var + RWKV_GN_EPS) * lng_ref[...] + lnb_ref[...]
    bonus = head_sum(r * k * rk_ref[...]) * v
    y_ref[...] = ((yn + bonus) * g).astype(y_ref.dtype)


def _rwkv(pd, v_first, prm):
    t = pd.shape[0]
    tt = _tile(t, 512)
    nch = tt // CHUNK
    has_vres = v_first is not None
    gw = GROUP_W
    row_blk = lambda n: pl.BlockSpec((tt, n), lambda i: (i, 0))
    vec = _full((1, gw))
    in_specs = [row_blk(N_D),
                pl.BlockSpec((SUBLANES, N_D), lambda i: (jnp.maximum(i * (tt // SUBLANES) - 1, 0), 0))]
    args = [pd, pd]
    if has_vres:
        in_specs.append(row_blk(gw))
        args.append(v_first)
    tpos = jnp.arange(tt)
    tril = ((tpos[:, None] // CHUNK == tpos[None, :] // CHUNK) & (tpos[:, None] >= tpos[None, :])).astype(BF16)
    in_specs.append(_full((tt, tt)))
    args.append(tril)
    in_specs += [_full((1, N_D)), vec, _full((LORA_W, gw)), vec, _full((LORA_A, gw)), _full((LORA_G, gw)),
                 vec, vec, vec, vec, vec]
    args += [prm["mu"], prm["w0"], prm["w2"], prm["a0"], prm["a2"], prm["g2"], prm["k_k"], prm["k_a"],
             prm["r_k"], prm["ln_g"], prm["ln_b"]]
    if has_vres:
        in_specs += [vec, _full((gw, LORA_V)), _full((LORA_V, gw))]
        args += [prm["v0"], prm["v1"], prm["v2"]]
        out_shape = jax.ShapeDtypeStruct((t, gw), BF16)
        out_specs = row_blk(gw)
    else:
        out_shape = [jax.ShapeDtypeStruct((t, gw), BF16), jax.ShapeDtypeStruct((t, gw), F32)]
        out_specs = [row_blk(gw)] * 2
    npair, pw = RWKV_HEADS // 2, 2 * RWKV_HEAD
    scratch = [pltpu.VMEM((npair, RWKV_HEAD, pw), F32), pltpu.VMEM((tt, gw), F32)]
    return pl.pallas_call(
        functools.partial(_rwkv_kernel, tt=tt, has_vres=has_vres),
        out_shape=out_shape,
        grid=(t // tt,),
        in_specs=in_specs,
        out_specs=out_specs,
        scratch_shapes=scratch,
        compiler_params=_cparams(("arbitrary",)),
        name="rwkv7",
    )(*args)


def _route_record(lg):
    lane = lax.broadcasted_iota(jnp.int32, (lg.shape[0], LANES), 1)
    lanef = lane.astype(F32)
    gl = jnp.where(lane < N_GROUPS, lg[:, 0:LANES], -jnp.inf)
    gmax = jnp.max(gl, axis=-1, keepdims=True)
    gp = 1.0 / jnp.sum(jnp.exp(gl - gmax), axis=-1, keepdims=True)
    gi = jnp.min(jnp.where(gl == gmax, lanef, float(LANES)), axis=-1, keepdims=True)
    lo = gi * EXP_PER_GROUP
    in_grp = (lanef >= lo) & (lanef < lo + EXP_PER_GROUP)
    el = jnp.where(in_grp, lg[:, LANES:2 * LANES], -jnp.inf)
    m1 = jnp.max(el, axis=-1, keepdims=True)
    i1 = jnp.min(jnp.where(el == m1, lanef, float(LANES)), axis=-1, keepdims=True)
    el2 = jnp.where(lanef == i1, -jnp.inf, el)
    m2 = jnp.max(el2, axis=-1, keepdims=True)
    i2 = jnp.min(jnp.where(el2 == m2, lanef, float(LANES)), axis=-1, keepdims=True)
    e2 = jnp.exp(m2 - m1)
    w_top = gp / (1.0 + e2)
    return jnp.where(lanef == i1 - lo, w_top, jnp.where(lanef == i2 - lo, w_top * e2,
                     jnp.where(lane == ROUTE_GROUP_LANE, gi, 0.0)))


def _outproj_kernel(ya_ref, yb_ref, yc_ref, yd_ref, x_ref, wo_ref, gt_ref, g_ref, sc_ref, sh_ref, wr_ref, br_ref,
                    x1_ref, h2_ref, gate_ref, *, tm):
    sub = tm // OUT_SUB
    for blk in range(OUT_SUB):
        rows = slice(blk * sub, (blk + 1) * sub)
        ycat = jnp.concatenate([ya_ref[rows, :], yb_ref[rows, :], yc_ref[rows, :], yd_ref[rows, :]], axis=1)
        x1 = x_ref[rows, :] + gt_ref[...] * _dot_w(ycat, wo_ref[...], P_OUT)
        x1_ref[rows, :] = x1
        h2 = _modulated_norm(x1, g_ref[...], sc_ref[...], sh_ref[...])
        h2_ref[rows, :] = h2.astype(h2_ref.dtype)
        gate_ref[rows, :] = _route_record(_dot_w(h2, wr_ref[...], P_ROUTER) + br_ref[...])


def _out_proj(ya, yb, yc, yd, x, wo_parts, gt, g, sc, sh, w_router, b_router):
    t = x.shape[0]
    tm = _tile(t, 1024)
    row_blk = lambda n: pl.BlockSpec((tm, n), lambda i: (i, 0))
    vec = _full((1, D_MODEL))
    return pl.pallas_call(
        functools.partial(_outproj_kernel, tm=tm),
        out_shape=[jax.ShapeDtypeStruct((t, D_MODEL), F32), jax.ShapeDtypeStruct((t, D_MODEL), BF16),
                   jax.ShapeDtypeStruct((t, LANES), F32)],
        grid=(t // tm,),
        in_specs=[row_blk(GROUP_W)] * 4 + [row_blk(D_MODEL), _full(wo_parts.shape), vec, vec, vec, vec,
                                           _full(w_router.shape), _full((1, 2 * LANES))],
        out_specs=[row_blk(D_MODEL), row_blk(D_MODEL), row_blk(LANES)],
        compiler_params=_cparams(("parallel",)),
        name="out_proj_router",
    )(ya, yb, yc, yd, x, wo_parts, gt, g, sc, sh, w_router, b_router)


def _moe_kernel(h_ref, route_ref, tril_ref, wg_ref, wu_ref, wd_ref, x1_ref, gt_ref, fg_ref, o_ref,
                acc_ref, col_ref, row_ref, cnt_ref, *, final, tm, cap, cap_extra):
    g = pl.program_id(1)
    lane = lax.broadcasted_iota(jnp.int32, (tm, LANES), 1)

    @pl.when(g == 0)
    def _():
        acc_ref[...] = jnp.zeros_like(acc_ref)
        gi = route_ref[:, ROUTE_GROUP_LANE:ROUTE_GROUP_LANE + 1]
        onehot = lane.astype(F32) == gi
        cum = _mm(tril_ref[...], jnp.where(onehot, 1.0, 0.0).astype(BF16), _NN)
        rank = jnp.sum(jnp.where(onehot, cum, 0.0), axis=-1, keepdims=True) - 1.0
        cnt_ref[0:1, :] = cum[tm - 1:tm, :]
        cnt_ref[1:2, :] = cum[tm // 2 - 1:tm // 2, :]
        rec = jnp.where(lane == 0, rank, jnp.where(lane == 1, gi, 0.0))
        col_ref[...] = rec
        row_ref[...] = rec.T[0:SUBLANES, :]

    gf = g.astype(F32)
    this_group = lane[0:1, :] == g
    count = jnp.sum(jnp.where(this_group, cnt_ref[0:1, :], 0.0)).astype(jnp.int32)
    count_half = jnp.sum(jnp.where(this_group, cnt_ref[1:2, :], 0.0)).astype(jnp.int32)
    sel_col = jnp.where(col_ref[:, 1:2] == gf, col_ref[:, 0:1], -1.0)
    sel_row = jnp.where(row_ref[1:2, :] == gf, row_ref[0:1, :], -1.0)
    route16 = route_ref[...].astype(BF16)

    def block(base, rows, t0=0):
        base = base.astype(F32)
        nt = tm - t0
        slot_r = lax.broadcasted_iota(jnp.int32, (rows, nt), 0).astype(F32) + base
        gather = jnp.where(slot_r == sel_row[:, t0:], 1.0, 0.0).astype(BF16)
        h = _mm(gather, h_ref[t0:, :], _NN).astype(BF16)
        slot_gate = _mm(gather, route16[t0:, :], _NN)
        hg = _mm(h, wg_ref[...], _NN)
        hu = _mm(h, wu_ref[...], _NN)
        hu = jnp.concatenate([hu[:, j * D_EXPERT:(j + 1) * D_EXPERT] * slot_gate[:, j:j + 1]
                              for j in range(EXP_PER_GROUP)], axis=1)
        act = hg * _sigmoid(hg) * hu
        ys = _mm(act.astype(BF16), wd_ref[...], _NN)
        slot_c = lax.broadcasted_iota(jnp.int32, (nt, rows), 1).astype(F32) + base
        scatter = jnp.where(slot_c == sel_col[t0:, :], 1.0, 0.0).astype(BF16)
        acc_ref[t0:, :] += _mm(scatter, ys.astype(BF16), _NN)

    @pl.when(count > 0)
    def _():
        block(jnp.int32(0), cap)

    n_extra = (jnp.maximum(count - cap, 0) + (cap_extra - 1)) // cap_extra
    late = count_half <= cap

    def extra_late(r, carry):
        block(cap + r * cap_extra, cap_extra, t0=tm // 2)
        return carry

    def extra_any(r, carry):
        block(cap + r * cap_extra, cap_extra)
        return carry

    lax.fori_loop(0, jnp.where(late, n_extra, 0), extra_late, 0)
    lax.fori_loop(0, jnp.where(late, 0, n_extra), extra_any, 0)

    @pl.when(g == pl.num_programs(1) - 1)
    def _():
        out = x1_ref[...] + gt_ref[...] * acc_ref[...]
        if final:
            out = out * lax.rsqrt(jnp.mean(out * out, axis=-1, keepdims=True) + NORM_EPS) * fg_ref[...]
        o_ref[...] = out


def _moe(h2, route, wg, wu, wd, x1, gt, final_g, final):
    t = h2.shape[0]
    tm = _tile(t, MOE_TILE)
    cap = tm // N_GROUPS
    cap_extra = cap // 2
    nw = EXP_PER_GROUP * D_EXPERT
    vec = _full((1, D_MODEL))
    tpos = jnp.arange(tm)
    tril = (tpos[:, None] >= tpos[None, :]).astype(BF16)
    return pl.pallas_call(
        functools.partial(_moe_kernel, final=final, tm=tm, cap=cap, cap_extra=cap_extra),
        out_shape=jax.ShapeDtypeStruct((t, D_MODEL), F32),
        grid=(t // tm, N_GROUPS),
        in_specs=[
            pl.BlockSpec((tm, D_MODEL), lambda i, g: (i, 0)),
            pl.BlockSpec((tm, LANES), lambda i, g: (i, 0)),
            _full((tm, tm)),
            pl.BlockSpec((D_MODEL, nw), lambda i, g: (0, g)),
            pl.BlockSpec((D_MODEL, nw), lambda i, g: (0, g)),
            pl.BlockSpec((nw, D_MODEL), lambda i, g: (g, 0)),
            pl.BlockSpec((tm, D_MODEL), lambda i, g: (i, 0)),
            vec, vec,
        ],
        out_specs=pl.BlockSpec((tm, D_MODEL), lambda i, g: (i, 0)),
        scratch_shapes=[pltpu.VMEM((tm, D_MODEL), F32), pltpu.VMEM((tm, LANES), F32),
                        pltpu.VMEM((SUBLANES, tm), F32), pltpu.VMEM((2, LANES), F32)],
        compiler_params=_cparams(("parallel", "arbitrary")),
        name="moe_experts",
    )(h2, route, tril, wg, wu, wd, x1, gt, final_g)


def kernel(x, c, w_ada, b_ada, norm1_g, norm2_g, w_in, w_out, pool_w, pool_scale, attn_sinks, conv_w, rwkv_mu,
           rwkv_w0, rwkv_w2, rwkv_a0, rwkv_a2, rwkv_g2, rwkv_k_k, rwkv_k_a, rwkv_r_k, rwkv_ln_g, rwkv_ln_b,
           rwkv_v0, rwkv_v1, rwkv_v2, moe_w_grp, moe_b_grp, moe_w_exp, moe_b_exp, moe_w_gate, moe_w_up,
           moe_w_down, final_g):
    bsz, t, d = x.shape
    assert bsz == 1 and d == D_MODEL
    depth = w_ada.shape[0]
    xs = x.reshape(t, d)
    mod = _ada_mod(c, w_ada, b_ada)
    v_first = None
    row = lambda a: a.reshape(1, -1)
    for l in range(depth):
        sh1, sc1, gt1, sh2, sc2, gt2 = [mod[l, :, j * d:(j + 1) * d] for j in range(6)]
        wp = jax.scipy.linalg.block_diag(*[pool_w[l, gidx] for gidx in range(len(POOL_WINDOWS))])
        ya, pb, yc, pd = _in_proj(xs, row(norm1_g[l]), sc1, sh1, _split_w(w_in[l], P_IN), _split_w(wp, P_POOL),
                                  row(pool_scale[l]), conv_w[l])
        yb = _attention(pb, row(attn_sinks[l]))
        prm = dict(mu=row(rwkv_mu[l]), w0=row(rwkv_w0[l]), w2=rwkv_w2[l], a0=row(rwkv_a0[l]), a2=rwkv_a2[l],
                   g2=rwkv_g2[l], k_k=row(rwkv_k_k[l]), k_a=row(rwkv_k_a[l]), r_k=row(rwkv_r_k[l]),
                   ln_g=row(rwkv_ln_g[l]), ln_b=row(rwkv_ln_b[l]))
        if l == 0:
            yd, v_first = _rwkv(pd, None, prm)
        else:
            prm.update(v0=row(rwkv_v0[l - 1]), v1=rwkv_v1[l - 1], v2=rwkv_v2[l - 1])
            yd = _rwkv(pd, v_first, prm)
        lane_pad = lambda a: jnp.pad(a, ((0, 0), (0, LANES - a.shape[1])))
        w_router = jnp.concatenate([lane_pad(moe_w_grp[l]), lane_pad(moe_w_exp[l])], axis=1)
        b_router = jnp.concatenate([lane_pad(row(moe_b_grp[l])), lane_pad(row(moe_b_exp[l]))], axis=1)
        x1, h2, gates = _out_proj(ya, yb, yc, yd, xs, _split_w(w_out[l], P_OUT), gt1, row(norm2_g[l]), sc2, sh2,
                                  _split_w(w_router, P_ROUTER), b_router)
        wg = jnp.transpose(moe_w_gate[l], (1, 0, 2)).reshape(d, N_EXPERTS * D_EXPERT).astype(BF16)
        wu = jnp.transpose(moe_w_up[l], (1, 0, 2)).reshape(d, N_EXPERTS * D_EXPERT).astype(BF16)
        wd = moe_w_down[l].astype(BF16).reshape(N_EXPERTS * D_EXPERT, d)
        xs = _moe(h2, gates, wg, wu, wd, x1, gt2, row(final_g), final=(l == depth - 1))
    return xs.reshape(bsz, t, d)
```

```python
import functools

import jax
import jax.numpy as jnp
from jax import lax
from jax.experimental import pallas as pl
from jax.experimental.pallas import tpu as pltpu

F32 = jnp.float32
BF16 = jnp.bfloat16

D_MODEL = 1024
GROUP_W = 256
NORM_EPS = 1e-6
POOL_WINDOWS = (2, 4, 8, 16)
POOL_CG = 64
HEAD_DIM = 64
ATT_HEADS = 4
ATT_KV_HEADS = 2
ATT_BLOCK = 128
CONV_W = 3
RWKV_HEAD = 64
RWKV_HEADS = 4
LORA_W = 32
LORA_A = 32
LORA_G = 64
LORA_V = 32
RWKV_GN_EPS = 64e-5
N_A = GROUP_W
N_B = (ATT_HEADS + 2 * ATT_KV_HEADS) * HEAD_DIM
N_C = 3 * GROUP_W
N_D = 3 * GROUP_W + LORA_W + LORA_A + LORA_G
N_GROUPS = 4
EXP_PER_GROUP = 8
N_EXPERTS = 32
D_EXPERT = 128

LANES = 128
SUBLANES = 8
VMEM_LIMIT = 56 * 1024 * 1024

CHUNK = 64
POOL_HALO = 16
CONV_HALO = 8
OUT_SUB = 2
ROUTE_GROUP_LANE = 8
MOE_TILE = 1024

P_IN = 1
P_OUT = 1
P_POOL = 1
P_ATT = 1
P_LORA = 1
P_STATE = 1
P_ROUTER = 3


def _split(a):
    hi = a.astype(BF16)
    lo = (a - hi.astype(F32)).astype(BF16)
    return hi, lo


def _mm(a, b, dims):
    return lax.dot_general(a, b, (dims, ((), ())), preferred_element_type=F32)


_NN = ((1,), (0,))
_NT = ((1,), (1,))


def _dot(a, b, passes=1, dims=_NN):
    if passes == 1:
        return _mm(a.astype(BF16), b.astype(BF16), dims)
    a0, a1 = _split(a)
    b0, b1 = _split(b)
    return _mm(a0, b0, dims) + (_mm(a0, b1, dims) + _mm(a1, b0, dims))


def _bmm(a, b, nt):
    dims = (((2,), (2 if nt else 1,)), ((0,), (0,)))
    return lax.dot_general(a, b, dims, preferred_element_type=F32)


def _bdot(a, b, passes=1, nt=False):
    if passes == 1:
        return _bmm(a.astype(BF16), b.astype(BF16), nt)
    a0, a1 = _split(a)
    b0, b1 = _split(b)
    return _bmm(a0, b0, nt) + (_bmm(a0, b1, nt) + _bmm(a1, b0, nt))


def _lhs_w(a, passes):
    if passes == 1:
        return a.astype(BF16)
    a0, a1 = _split(a)
    return jnp.concatenate([a0, a0, a1], axis=1)


def _dot_w(a, w_cat, passes):
    return _mm(_lhs_w(a, passes), w_cat, _NN)


def _split_w(w, passes):
    hi = w.astype(BF16)
    if passes == 1:
        return hi
    lo = (w - hi.astype(F32)).astype(BF16)
    return jnp.concatenate([hi, lo, hi], axis=0)


def _sigmoid(x):
    return 1.0 / (1.0 + jnp.exp(-x))


def _cparams(sem):
    return pltpu.CompilerParams(dimension_semantics=sem, vmem_limit_bytes=VMEM_LIMIT)


def _full(shape):
    return pl.BlockSpec(shape, lambda *_: (0,) * len(shape))


def _tile(n, pref):
    t = min(n, pref)
    assert n % t == 0, (n, t)
    return t


def _mod_kernel(c_ref, w_ref, b_ref, o_ref):
    c = c_ref[...]
    cond = c * _sigmoid(c)
    o_ref[...] = _dot(cond, w_ref[...], 3) + b_ref[...]


def _ada_mod(c, w_ada, b_ada):
    depth = w_ada.shape[0]
    c8 = jnp.broadcast_to(c, (SUBLANES, D_MODEL))
    out = pl.pallas_call(
        _mod_kernel,
        out_shape=jax.ShapeDtypeStruct((depth, SUBLANES, 6 * D_MODEL), F32),
        grid=(depth, 6),
        in_specs=[
            pl.BlockSpec((SUBLANES, D_MODEL), lambda l, j: (0, 0)),
            pl.BlockSpec((None, D_MODEL, D_MODEL), lambda l, j: (l, 0, j)),
            pl.BlockSpec((None, 1, D_MODEL), lambda l, j: (l, 0, j)),
        ],
        out_specs=pl.BlockSpec((None, SUBLANES, D_MODEL), lambda l, j: (l, 0, j)),
        compiler_params=_cparams(("arbitrary", "arbitrary")),
        name="ada_mod",
    )(c8, w_ada, b_ada.reshape(depth, 1, 6 * D_MODEL))
    return out[:, 0:1, :]


def _modulated_norm(x, g, sc, sh):
    y = x * lax.rsqrt(jnp.mean(x * x, axis=-1, keepdims=True) + NORM_EPS) * g
    return y * (1.0 + sc) + sh


def _shift_rows(ext, n):
    return pltpu.roll(ext, n, axis=0)


def _pool_mixer(u, halo, wp, scale, t0):
    tt = u.shape[0]
    ext = jnp.concatenate([halo, u], axis=0)
    s2 = ext + _shift_rows(ext, 1)
    s4 = s2 + _shift_rows(s2, 2)
    s8 = s4 + _shift_rows(s4, 4)
    s16 = s8 + _shift_rows(s8, 8)
    grp = lax.broadcasted_iota(jnp.int32, (tt, GROUP_W), 1) // POOL_CG
    tpos = lax.broadcasted_iota(jnp.int32, (tt, GROUP_W), 0) + (t0 + 1)
    wsum = jnp.where(grp == 0, s2[POOL_HALO:], jnp.where(grp == 1, s4[POOL_HALO:],
                     jnp.where(grp == 2, s8[POOL_HALO:], s16[POOL_HALO:])))
    win = jnp.where(grp == 0, 2, jnp.where(grp == 1, 4, jnp.where(grp == 2, 8, 16)))
    cnt = jnp.minimum(tpos, win).astype(F32)
    return _dot_w(wsum / cnt - u, wp, P_POOL) * scale


def _conv_mixer(bg, z, z_halo, cw):
    zext = jnp.concatenate([z_halo, z], axis=0)
    y = (cw[2:3, :] * z + cw[1:2, :] * _shift_rows(zext, 1)[CONV_HALO:]
         + cw[0:1, :] * _shift_rows(zext, 2)[CONV_HALO:])
    return bg * y


def _inproj_kernel(x_ref, g_ref, sc_ref, sh_ref, w_ref, wp_ref, scale_ref, cw_ref,
                   ya_ref, pb_ref, yc_ref, pd_ref, u_tail, z_tail, *, passes, tm):
    i = pl.program_id(0)

    @pl.when(i == 0)
    def _():
        u_tail[...] = jnp.zeros_like(u_tail)
        z_tail[...] = jnp.zeros_like(z_tail)

    h = _modulated_norm(x_ref[...], g_ref[...], sc_ref[...], sh_ref[...])
    lhs = _lhs_w(h, passes)
    proj = lambda col, width: _mm(lhs, w_ref[:, col:col + width], _NN)
    u = proj(0, N_A)
    pc = proj(N_A + N_B, N_C)
    ya_ref[...] = _pool_mixer(u, u_tail[...], wp_ref[...], scale_ref[...], i * tm).astype(ya_ref.dtype)
    u_tail[...] = u[tm - POOL_HALO:, :]
    z = pc[:, GROUP_W:2 * GROUP_W] * pc[:, 2 * GROUP_W:3 * GROUP_W]
    yc_ref[...] = _conv_mixer(pc[:, 0:GROUP_W], z, z_tail[...], cw_ref[...]).astype(yc_ref.dtype)
    z_tail[...] = z[tm - CONV_HALO:, :]
    pb_ref[...] = proj(N_A, N_B).astype(pb_ref.dtype)
    pd_ref[...] = proj(N_A + N_B + N_C, N_D)


def _in_proj(x, g, sc, sh, w_parts, wp_parts, pool_scale, conv_w):
    t = x.shape[0]
    tm = _tile(t, 1024)
    vec = _full((1, D_MODEL))
    widths = (GROUP_W, N_B, GROUP_W, N_D)
    return pl.pallas_call(
        functools.partial(_inproj_kernel, passes=P_IN, tm=tm),
        out_shape=[jax.ShapeDtypeStruct((t, n), dt) for n, dt in zip(widths, (BF16, BF16, BF16, F32))],
        grid=(t // tm,),
        in_specs=[pl.BlockSpec((tm, D_MODEL), lambda i: (i, 0)), vec, vec, vec, _full(w_parts.shape),
                  _full(wp_parts.shape), _full((1, GROUP_W)), _full((CONV_W, GROUP_W))],
        out_specs=[pl.BlockSpec((tm, n), lambda i: (i, 0)) for n in widths],
        scratch_shapes=[pltpu.VMEM((POOL_HALO, GROUP_W), F32), pltpu.VMEM((CONV_HALO, GROUP_W), F32)],
        compiler_params=_cparams(("arbitrary",)),
        name="in_proj_pool_conv",
    )(x, g, sc, sh, w_parts, wp_parts, pool_scale, conv_w)


def _attn_kernel(cur_ref, prev_ref, sink_ref, o_ref, *, tq):
    i = pl.program_id(0)
    nq = ATT_HEADS * HEAD_DIM
    nkv = ATT_KV_HEADS * HEAD_DIM
    blk = ATT_BLOCK
    hd = HEAD_DIM
    rep = ATT_HEADS // ATT_KV_HEADS
    nblk = tq // blk
    ri = lax.broadcasted_iota(jnp.int32, (rep * blk, 2 * blk), 0)
    ki = lax.broadcasted_iota(jnp.int32, (rep * blk, 2 * blk), 1)
    dist = (ri % blk) + blk - ki
    in_win = (dist >= 0) & (dist < blk)
    distf = dist.astype(F32)
    head_col = lax.broadcasted_iota(jnp.int32, (rep * blk, 1), 0) // blk
    bias, sink = [], []
    for g in range(ATT_KV_HEADS):
        slope_g = jnp.zeros((rep * blk, 1), F32)
        sink_g = jnp.zeros((rep * blk, 1), F32)
        for j in range(rep):
            h = g * rep + j
            slope_g = jnp.where(head_col == j, 2.0 ** (-8.0 * (h + 1) / ATT_HEADS), slope_g)
            sink_g = jnp.where(head_col == j, sink_ref[:, h:h + 1], sink_g)
        bias.append(slope_g * distf)
        sink.append(sink_g)
    bias = jnp.stack(bias)[None]
    sink = jnp.stack(sink)[None]

    q = cur_ref[:, 0:nq] * (hd ** -0.5)
    kv = jnp.concatenate([prev_ref[...], cur_ref[:, nq:nq + 2 * nkv]], axis=0)
    qs, ks, vs = [], [], []
    for b in range(nblk):
        for g in range(ATT_KV_HEADS):
            qs.append(jnp.concatenate(
                [q[b * blk:(b + 1) * blk, (g * rep + j) * hd:(g * rep + j + 1) * hd] for j in range(rep)], axis=0))
            ks.append(kv[b * blk:(b + 2) * blk, g * hd:(g + 1) * hd])
            vs.append(kv[b * blk:(b + 2) * blk, nkv + g * hd:nkv + (g + 1) * hd])
    s = _bdot(jnp.stack(qs), jnp.stack(ks), P_ATT, nt=True)
    s = s.reshape(nblk, ATT_KV_HEADS, rep * blk, 2 * blk)
    s = jnp.where(in_win, s - bias, -jnp.inf)
    first_ok = ki >= blk * (1 - (i > 0).astype(jnp.int32))
    s = jnp.concatenate([jnp.where(first_ok, s[0:1], -jnp.inf), s[1:]], axis=0)
    m = jnp.maximum(jnp.max(s, axis=-1, keepdims=True), sink)
    p = jnp.exp(s - m)
    den = jnp.sum(p, axis=-1, keepdims=True) + jnp.exp(sink - m)
    o = _bdot(p.reshape(nblk * ATT_KV_HEADS, rep * blk, 2 * blk), jnp.stack(vs), P_ATT)
    o = o.reshape(nblk, ATT_KV_HEADS, rep * blk, hd) / den
    for b in range(nblk):
        o_ref[b * blk:(b + 1) * blk, :] = jnp.concatenate(
            [o[b, g, j * blk:(j + 1) * blk, :] for g in range(ATT_KV_HEADS) for j in range(rep)],
            axis=-1).astype(o_ref.dtype)


def _attention(pb, sinks):
    t = pb.shape[0]
    tq = _tile(t, 2048)
    nb = tq // ATT_BLOCK
    return pl.pallas_call(
        functools.partial(_attn_kernel, tq=tq),
        out_shape=jax.ShapeDtypeStruct((t, GROUP_W), BF16),
        grid=(t // tq,),
        in_specs=[
            pl.BlockSpec((tq, N_B), lambda i: (i, 0)),
            pl.BlockSpec((ATT_BLOCK, 2 * ATT_KV_HEADS * HEAD_DIM), lambda i: (jnp.maximum(i * nb - 1, 0), 1)),
            _full((1, ATT_HEADS)),
        ],
        out_specs=pl.BlockSpec((tq, GROUP_W), lambda i: (i, 0)),
        compiler_params=_cparams(("parallel",)),
        name="swa_attention",
    )(pb, pb, sinks)


def _rwkv_kernel(*refs, tt, has_vres):
    if has_vres:
        (pd_ref, prev_ref, vf_ref, tril_ref, mu_ref, w0_ref, w2_ref, a0_ref, a2_ref, g2_ref, kk_ref, ka_ref,
         rk_ref, lng_ref, lnb_ref, v0_ref, v1_ref, v2_ref, y_ref, st_ref, yr_s) = refs
    else:
        (pd_ref, prev_ref, tril_ref, mu_ref, w0_ref, w2_ref, a0_ref, a2_ref, g2_ref, kk_ref, ka_ref,
         rk_ref, lng_ref, lnb_ref, y_ref, vout_ref, st_ref, yr_s) = refs
    i = pl.program_id(0)
    nh, hd, gw = RWKV_HEADS, RWKV_HEAD, GROUP_W
    @pl.when(i == 0)
    def _():
        st_ref[...] = jnp.zeros_like(st_ref)

    hr = lax.broadcasted_iota(jnp.int32, (2 * gw, gw), 0) % gw // hd
    hc = lax.broadcasted_iota(jnp.int32, (2 * gw, gw), 1) // hd
    hsum2 = jnp.where(hr == hc, 1.0, 0.0).astype(BF16)

    def head_sum(t2):
        return _mm(jnp.concatenate(_split(t2), axis=1), hsum2, _NN)

    p = pd_ref[...]
    row = lax.broadcasted_iota(jnp.int32, (tt, 1), 0)
    prev_row = prev_ref[SUBLANES - 1:SUBLANES, :] * (i > 0).astype(F32)
    shifted = jnp.where(row == 0, prev_row, pltpu.roll(p, 1, axis=0))
    z = p + (shifted - p) * mu_ref[...]
    r = z[:, 0:gw]
    k = z[:, gw:2 * gw]
    v = z[:, 2 * gw:3 * gw]
    o = 3 * gw
    wd = z[:, o:o + LORA_W]
    ad = z[:, o + LORA_W:o + LORA_W + LORA_A]
    gd = z[:, o + LORA_W + LORA_A:o + LORA_W + LORA_A + LORA_G]
    wpre = -(w0_ref[...] + _dot(jnp.tanh(wd), w2_ref[...], P_LORA))
    softplus = jnp.maximum(wpre, 0.0) + jnp.log(1.0 + jnp.exp(-jnp.abs(wpre)))
    lw = -jnp.exp(-softplus - 0.5)
    a = _sigmoid(a0_ref[...] + _dot(ad, a2_ref[...], P_LORA))
    g = _dot(_sigmoid(gd), g2_ref[...], P_LORA)
    if has_vres:
        mix = _sigmoid(v0_ref[...] + _dot(_dot(v, v1_ref[...], P_LORA), v2_ref[...], P_LORA))
        v = v + (vf_ref[...] - v) * mix
    else:
        vout_ref[...] = v
    kk = k * kk_ref[...]
    kk = kk / jnp.maximum(jnp.sqrt(head_sum(kk * kk)), 1e-12)
    k = k * (1.0 + (a - 1.0) * ka_ref[...])
    nch = tt // CHUNK
    L = CHUNK
    npair = nh // 2
    pw = 2 * hd
    ti = lax.broadcasted_iota(jnp.int32, (L, pw), 0)
    tl = lax.broadcasted_iota(jnp.int32, (L, pw), 1) % hd
    low_strict = ti > tl
    low_incl = ti >= tl
    eye = (ti == tl).astype(F32)
    first_head = lax.broadcasted_iota(jnp.int32, (1, pw), 1) < hd

    def same_block(b):
        return (ti // b) == (tl // b)

    def to_batch(t2, c0, nc):
        return jnp.concatenate([t2[c0 * L:(c0 + nc) * L, q * pw:(q + 1) * pw].reshape(nc, L, pw)
                                for q in range(npair)], axis=0)

    def bd(x):
        zero = jnp.zeros_like(x)
        return jnp.concatenate([jnp.where(first_head, x, zero), jnp.where(first_head, zero, x)], axis=-2)

    def fold(x):
        return x[:, :hd] + x[:, hd:]

    def c16(x):
        return x.astype(BF16)

    lw_hi, lw_lo = _split(lw)
    cl = _mm(tril_ref[...], lw_hi, _NN) + _mm(tril_ref[...], lw_lo, _NN)
    cl3 = cl.reshape(nch, L, gw)
    cl_end = cl3[:, L - 1:L, :]
    e_end = jnp.exp(cl_end - cl3).reshape(tt, gw)
    e_neg = jnp.exp(-cl)
    bb = kk * a
    at_t = -kk * jnp.exp(cl - lw)
    bh_t = bb * e_neg
    kh_t = k * e_neg
    rt_t = r * jnp.exp(cl)
    be_t = bb * e_end
    ke_t = k * e_end
    p_end = jnp.exp(cl_end)

    def chunk_operators(c0, nc):
        at = c16(to_batch(at_t, c0, nc))
        bh = c16(to_batch(bh_t, c0, nc))
        kh = c16(to_batch(kh_t, c0, nc))
        rt = to_batch(rt_t, c0, nc)
        be = to_batch(be_t, c0, nc)
        ke = to_batch(ke_t, c0, nc)
        vb = bd(c16(to_batch(v, c0, nc)))
        pe = jnp.concatenate([p_end[c0:c0 + nc, :, q * pw:(q + 1) * pw] for q in range(npair)], axis=0)
        lhs = jnp.concatenate([at, c16(rt)], axis=1)
        g_b = _bmm(lhs, bd(bh), True)
        g_k = _bmm(lhs, bd(kh), True)
        a_ab = jnp.where(low_strict, g_b[:, :L], 0.0)
        b_rb = jnp.where(low_incl, g_b[:, L:], 0.0)
        a_ak = jnp.where(low_strict, g_k[:, :L], 0.0)
        b_rk = jnp.where(low_incl, g_k[:, L:], 0.0)
        tinv = eye + jnp.where(same_block(2), a_ab, 0.0)
        b = 2
        while b < L:
            e = jnp.where(same_block(2 * b) & ~same_block(b), a_ab, 0.0)
            xe = _bmm(c16(tinv), bd(c16(e)), False)
            tinv = tinv + _bmm(c16(xe), bd(c16(tinv)), False)
            b *= 2
        av = _bmm(c16(a_ak), vb, False)
        wu = _bmm(c16(tinv), jnp.concatenate([bd(at), bd(c16(av))], axis=2), False)
        rhs = jnp.concatenate([bd(c16(wu[:, :, :pw])), bd(c16(wu[:, :, pw:]))], axis=2)
        lhs_b = jnp.concatenate([c16(b_rb), c16(jnp.swapaxes(bd(be), 1, 2))], axis=1)
        lhs_k = jnp.concatenate([c16(b_rk), c16(jnp.swapaxes(bd(ke), 1, 2))], axis=1)
        o_wu = _bmm(lhs_b, rhs, False)
        o_v = _bmm(lhs_k, vb, False)
        rq = rt + o_wu[:, :L, :pw]
        y0 = o_wu[:, :L, pw:] + o_v[:, :L]
        m2 = eye * pe + fold(o_wu[:, L:, :pw])
        c2 = fold(o_wu[:, L:, pw:] + o_v[:, L:])
        return jnp.concatenate([rq, m2], axis=1), jnp.concatenate([y0, c2], axis=1)

    def state_walk(zs, ops, c0, nc):
        rqm, y0c = ops
        for c in range(nc):
            for q in range(npair):
                b = q * nc + c
                yz = _dot(rqm[b], bd(zs[q]), P_STATE) + y0c[b]
                yr_s[(c0 + c) * L:(c0 + c + 1) * L, q * pw:(q + 1) * pw] = yz[:L]
                zs[q] = yz[L:]
        return zs

    zs = state_walk([st_ref[q] for q in range(npair)], chunk_operators(0, nch), 0, nch)
    for q in range(npair):
        st_ref[q] = zs[q]

    y = yr_s[...]
    inv_n = 1.0 / hd
    mean = head_sum(y) * inv_n
    yc = y - mean
    var = head_sum(yc * yc) * inv_n
    yn = yc * lax.rsqrt(var + RWKV_GN_EPS) * lng_ref[...] + lnb_ref[...]
    bonus = head_sum(r * k * rk_ref[...]) * v
    y_ref[...] = ((yn + bonus) * g).astype(y_ref.dtype)


def _rwkv(pd, v_first, prm):
    t = pd.shape[0]
    tt = _tile(t, 512)
    nch = tt // CHUNK
    has_vres = v_first is not None
    gw = GROUP_W
    row_blk = lambda n: pl.BlockSpec((tt, n), lambda i: (i, 0))
    vec = _full((1, gw))
    in_specs = [row_blk(N_D),
                pl.BlockSpec((SUBLANES, N_D), lambda i: (jnp.maximum(i * (tt // SUBLANES) - 1, 0), 0))]
    args = [pd, pd]
    if has_vres:
        in_specs.append(row_blk(gw))
        args.append(v_first)
    tpos = jnp.arange(tt)
    tril = ((tpos[:, None] // CHUNK == tpos[None, :] // CHUNK) & (tpos[:, None] >= tpos[None, :])).astype(BF16)
    in_specs.append(_full((tt, tt)))
    args.append(tril)
    in_specs += [_full((1, N_D)), vec, _full((LORA_W, gw)), vec, _full((LORA_A, gw)), _full((LORA_G, gw)),
                 vec, vec, vec, vec, vec]
    args += [prm["mu"], prm["w0"], prm["w2"], prm["a0"], prm["a2"], prm["g2"], prm["k_k"], prm["k_a"],
             prm["r_k"], prm["ln_g"], prm["ln_b"]]
    if has_vres:
        in_specs += [vec, _full((gw, LORA_V)), _full((LORA_V, gw))]
        args += [prm["v0"], prm["v1"], prm["v2"]]
        out_shape = jax.ShapeDtypeStruct((t, gw), BF16)
        out_specs = row_blk(gw)
    else:
        out_shape = [jax.ShapeDtypeStruct((t, gw), BF16), jax.ShapeDtypeStruct((t, gw), F32)]
        out_specs = [row_blk(gw)] * 2
    npair, pw = RWKV_HEADS // 2, 2 * RWKV_HEAD
    scratch = [pltpu.VMEM((npair, RWKV_HEAD, pw), F32), pltpu.VMEM((tt, gw), F32)]
    return pl.pallas_call(
        functools.partial(_rwkv_kernel, tt=tt, has_vres=has_vres),
        out_shape=out_shape,
        grid=(t // tt,),
        in_specs=in_specs,
        out_specs=out_specs,
        scratch_shapes=scratch,
        compiler_params=_cparams(("arbitrary",)),
        name="rwkv7",
    )(*args)


def _route_record(lg):
    lane = lax.broadcasted_iota(jnp.int32, (lg.shape[0], LANES), 1)
    lanef = lane.astype(F32)
    gl = jnp.where(lane < N_GROUPS, lg[:, 0:LANES], -jnp.inf)
    gmax = jnp.max(gl, axis=-1, keepdims=True)
    gp = 1.0 / jnp.sum(jnp.exp(gl - gmax), axis=-1, keepdims=True)
    gi = jnp.min(jnp.where(gl == gmax, lanef, float(LANES)), axis=-1, keepdims=True)
    lo = gi * EXP_PER_GROUP
    in_grp = (lanef >= lo) & (lanef < lo + EXP_PER_GROUP)
    el = jnp.where(in_grp, lg[:, LANES:2 * LANES], -jnp.inf)
    m1 = jnp.max(el, axis=-1, keepdims=True)
    i1 = jnp.min(jnp.where(el == m1, lanef, float(LANES)), axis=-1, keepdims=True)
    el2 = jnp.where(lanef == i1, -jnp.inf, el)
    m2 = jnp.max(el2, axis=-1, keepdims=True)
    i2 = jnp.min(jnp.where(el2 == m2, lanef, float(LANES)), axis=-1, keepdims=True)
    e2 = jnp.exp(m2 - m1)
    w_top = gp / (1.0 + e2)
    return jnp.where(lanef == i1 - lo, w_top, jnp.where(lanef == i2 - lo, w_top * e2,
                     jnp.where(lane == ROUTE_GROUP_LANE, gi, 0.0)))


def _outproj_kernel(ya_ref, yb_ref, yc_ref, yd_ref, x_ref, wo_ref, gt_ref, g_ref, sc_ref, sh_ref, wr_ref, br_ref,
                    x1_ref, h2_ref, gate_ref, *, tm):
    sub = tm // OUT_SUB
    for blk in range(OUT_SUB):
        rows = slice(blk * sub, (blk + 1) * sub)
        ycat = jnp.concatenate([ya_ref[rows, :], yb_ref[rows, :], yc_ref[rows, :], yd_ref[rows, :]], axis=1)
        x1 = x_ref[rows, :] + gt_ref[...] * _dot_w(ycat, wo_ref[...], P_OUT)
        x1_ref[rows, :] = x1
        h2 = _modulated_norm(x1, g_ref[...], sc_ref[...], sh_ref[...])
        h2_ref[rows, :] = h2.astype(h2_ref.dtype)
        gate_ref[rows, :] = _route_record(_dot_w(h2, wr_ref[...], P_ROUTER) + br_ref[...])


def _out_proj(ya, yb, yc, yd, x, wo_parts, gt, g, sc, sh, w_router, b_router):
    t = x.shape[0]
    tm = _tile(t, 1024)
    row_blk = lambda n: pl.BlockSpec((tm, n), lambda i: (i, 0))
    vec = _full((1, D_MODEL))
    return pl.pallas_call(
        functools.partial(_outproj_kernel, tm=tm),
        out_shape=[jax.ShapeDtypeStruct((t, D_MODEL), F32), jax.ShapeDtypeStruct((t, D_MODEL), BF16),
                   jax.ShapeDtypeStruct((t, LANES), F32)],
        grid=(t // tm,),
        in_specs=[row_blk(GROUP_W)] * 4 + [row_blk(D_MODEL), _full(wo_parts.shape), vec, vec, vec, vec,
                                           _full(w_router.shape), _full((1, 2 * LANES))],
        out_specs=[row_blk(D_MODEL), row_blk(D_MODEL), row_blk(LANES)],
        compiler_params=_cparams(("parallel",)),
        name="out_proj_router",
    )(ya, yb, yc, yd, x, wo_parts, gt, g, sc, sh, w_router, b_router)


def _moe_kernel(h_ref, route_ref, tril_ref, wg_ref, wu_ref, wd_ref, x1_ref, gt_ref, fg_ref, o_ref,
                acc_ref, col_ref, row_ref, cnt_ref, *, final, tm, cap, cap_extra):
    g = pl.program_id(1)
    lane = lax.broadcasted_iota(jnp.int32, (tm, LANES), 1)

    @pl.when(g == 0)
    def _():
        acc_ref[...] = jnp.zeros_like(acc_ref)
        gi = route_ref[:, ROUTE_GROUP_LANE:ROUTE_GROUP_LANE + 1]
        onehot = lane.astype(F32) == gi
        cum = _mm(tril_ref[...], jnp.where(onehot, 1.0, 0.0).astype(BF16), _NN)
        rank = jnp.sum(jnp.where(onehot, cum, 0.0), axis=-1, keepdims=True) - 1.0
        cnt_ref[0:1, :] = cum[tm - 1:tm, :]
        cnt_ref[1:2, :] = cum[tm // 2 - 1:tm // 2, :]
        rec = jnp.where(lane == 0, rank, jnp.where(lane == 1, gi, 0.0))
        col_ref[...] = rec
        row_ref[...] = rec.T[0:SUBLANES, :]

    gf = g.astype(F32)
    this_group = lane[0:1, :] == g
    count = jnp.sum(jnp.where(this_group, cnt_ref[0:1, :], 0.0)).astype(jnp.int32)
    count_half = jnp.sum(jnp.where(this_group, cnt_ref[1:2, :], 0.0)).astype(jnp.int32)
    sel_col = jnp.where(col_ref[:, 1:2] == gf, col_ref[:, 0:1], -1.0)
    sel_row = jnp.where(row_ref[1:2, :] == gf, row_ref[0:1, :], -1.0)
    route16 = route_ref[...].astype(BF16)

    def block(base, rows, t0=0):
        base = base.astype(F32)
        nt = tm - t0
        slot_r = lax.broadcasted_iota(jnp.int32, (rows, nt), 0).astype(F32) + base
        gather = jnp.where(slot_r == sel_row[:, t0:], 1.0, 0.0).astype(BF16)
        h = _mm(gather, h_ref[t0:, :], _NN).astype(BF16)
        slot_gate = _mm(gather, route16[t0:, :], _NN)
        hg = _mm(h, wg_ref[...], _NN)
        hu = _mm(h, wu_ref[...], _NN)
        hu = jnp.concatenate([hu[:, j * D_EXPERT:(j + 1) * D_EXPERT] * slot_gate[:, j:j + 1]
                              for j in range(EXP_PER_GROUP)], axis=1)
        act = hg * _sigmoid(hg) * hu
        ys = _mm(act.astype(BF16), wd_ref[...], _NN)
        slot_c = lax.broadcasted_iota(jnp.int32, (nt, rows), 1).astype(F32) + base
        scatter = jnp.where(slot_c == sel_col[t0:, :], 1.0, 0.0).astype(BF16)
        acc_ref[t0:, :] += _mm(scatter, ys.astype(BF16), _NN)

    @pl.when(count > 0)
    def _():
        block(jnp.int32(0), cap)

    n_extra = (jnp.maximum(count - cap, 0) + (cap_extra - 1)) // cap_extra
    late = count_half <= cap

    def extra_late(r, carry):
        block(cap + r * cap_extra, cap_extra, t0=tm // 2)
        return carry

    def extra_any(r, carry):
        block(cap + r * cap_extra, cap_extra)
        return carry

    lax.fori_loop(0, jnp.where(late, n_extra, 0), extra_late, 0)
    lax.fori_loop(0, jnp.where(late, 0, n_extra), extra_any, 0)

    @pl.when(g == pl.num_programs(1) - 1)
    def _():
        out = x1_ref[...] + gt_ref[...] * acc_ref[...]
        if final:
            out = out * lax.rsqrt(jnp.mean(out * out, axis=-1, keepdims=True) + NORM_EPS) * fg_ref[...]
        o_ref[...] = out


def _moe(h2, route, wg, wu, wd, x1, gt, final_g, final):
    t = h2.shape[0]
    tm = _tile(t, MOE_TILE)
    cap = tm // N_GROUPS
    cap_extra = cap // 2
    nw = EXP_PER_GROUP * D_EXPERT
    vec = _full((1, D_MODEL))
    tpos = jnp.arange(tm)
    tril = (tpos[:, None] >= tpos[None, :]).astype(BF16)
    return pl.pallas_call(
        functools.partial(_moe_kernel, final=final, tm=tm, cap=cap, cap_extra=cap_extra),
        out_shape=jax.ShapeDtypeStruct((t, D_MODEL), F32),
        grid=(t // tm, N_GROUPS),
        in_specs=[
            pl.BlockSpec((tm, D_MODEL), lambda i, g: (i, 0)),
            pl.BlockSpec((tm, LANES), lambda i, g: (i, 0)),
            _full((tm, tm)),
            pl.BlockSpec((D_MODEL, nw), lambda i, g: (0, g)),
            pl.BlockSpec((D_MODEL, nw), lambda i, g: (0, g)),
            pl.BlockSpec((nw, D_MODEL), lambda i, g: (g, 0)),
            pl.BlockSpec((tm, D_MODEL), lambda i, g: (i, 0)),
            vec, vec,
        ],
        out_specs=pl.BlockSpec((tm, D_MODEL), lambda i, g: (i, 0)),
        scratch_shapes=[pltpu.VMEM((tm, D_MODEL), F32), pltpu.VMEM((tm, LANES), F32),
                        pltpu.VMEM((SUBLANES, tm), F32), pltpu.VMEM((2, LANES), F32)],
        compiler_params=_cparams(("parallel", "arbitrary")),
        name="moe_experts",
    )(h2, route, tril, wg, wu, wd, x1, gt, final_g)


def kernel(x, c, w_ada, b_ada, norm1_g, norm2_g, w_in, w_out, pool_w, pool_scale, attn_sinks, conv_w, rwkv_mu,
           rwkv_w0, rwkv_w2, rwkv_a0, rwkv_a2, rwkv_g2, rwkv_k_k, rwkv_k_a, rwkv_r_k, rwkv_ln_g, rwkv_ln_b,
           rwkv_v0, rwkv_v1, rwkv_v2, moe_w_grp, moe_b_grp, moe_w_exp, moe_b_exp, moe_w_gate, moe_w_up,
           moe_w_down, final_g):
    bsz, t, d = x.shape
    assert bsz == 1 and d == D_MODEL
    depth = w_ada.shape[0]
    xs = x.reshape(t, d)
    mod = _ada_mod(c, w_ada, b_ada)
    v_first = None
    row = lambda a: a.reshape(1, -1)
    for l in range(depth):
        sh1, sc1, gt1, sh2, sc2, gt2 = [mod[l, :, j * d:(j + 1) * d] for j in range(6)]
        wp = jax.scipy.linalg.block_diag(*[pool_w[l, gidx] for gidx in range(len(POOL_WINDOWS))])
        ya, pb, yc, pd = _in_proj(xs, row(norm1_g[l]), sc1, sh1, _split_w(w_in[l], P_IN), _split_w(wp, P_POOL),
                                  row(pool_scale[l]), conv_w[l])
        yb = _attention(pb, row(attn_sinks[l]))
        prm = dict(mu=row(rwkv_mu[l]), w0=row(rwkv_w0[l]), w2=rwkv_w2[l], a0=row(rwkv_a0[l]), a2=rwkv_a2[l],
                   g2=rwkv_g2[l], k_k=row(rwkv_k_k[l]), k_a=row(rwkv_k_a[l]), r_k=row(rwkv_r_k[l]),
                   ln_g=row(rwkv_ln_g[l]), ln_b=row(rwkv_ln_b[l]))
        if l == 0:
            yd, v_first = _rwkv(pd, None, prm)
        else:
            prm.update(v0=row(rwkv_v0[l - 1]), v1=rwkv_v1[l - 1], v2=rwkv_v2[l - 1])
            yd = _rwkv(pd, v_first, prm)
        lane_pad = lambda a: jnp.pad(a, ((0, 0), (0, LANES - a.shape[1])))
        w_router = jnp.concatenate([lane_pad(moe_w_grp[l]), lane_pad(moe_w_exp[l])], axis=1)
        b_router = jnp.concatenate([lane_pad(row(moe_b_grp[l])), lane_pad(row(moe_b_exp[l]))], axis=1)
        x1, h2, gates = _out_proj(ya, yb, yc, yd, xs, _split_w(w_out[l], P_OUT), gt1, row(norm2_g[l]), sc2, sh2,
                                  _split_w(w_router, P_ROUTER), b_router)
        wg = jnp.transpose(moe_w_gate[l], (1, 0, 2)).reshape(d, N_EXPERTS * D_EXPERT).astype(BF16)
        wu = jnp.transpose(moe_w_up[l], (1, 0, 2)).reshape(d, N_EXPERTS * D_EXPERT).astype(BF16)
        wd = moe_w_down[l].astype(BF16).reshape(N_EXPERTS * D_EXPERT, d)
        xs = _moe(h2, gates, wg, wu, wd, x1, gt2, row(final_g), final=(l == depth - 1))
    return xs.reshape(bsz, t, d)
```

```python
import functools

import jax
import jax.numpy as jnp
from jax import lax
from jax.experimental import pallas as pl
from jax.experimental.pallas import tpu as pltpu

F32 = jnp.float32
BF16 = jnp.bfloat16

D_MODEL = 1024
GROUP_W = 256
NORM_EPS = 1e-6
POOL_WINDOWS = (2, 4, 8, 16)
POOL_CG = 64
HEAD_DIM = 64
ATT_HEADS = 4
ATT_KV_HEADS = 2
ATT_BLOCK = 128
CONV_W = 3
RWKV_HEAD = 64
RWKV_HEADS = 4
LORA_W = 32
LORA_A = 32
LORA_G = 64
LORA_V = 32
RWKV_GN_EPS = 64e-5
N_A = GROUP_W
N_B = (ATT_HEADS + 2 * ATT_KV_HEADS) * HEAD_DIM
N_C = 3 * GROUP_W
N_D = 3 * GROUP_W + LORA_W + LORA_A + LORA_G
N_GROUPS = 4
EXP_PER_GROUP = 8
N_EXPERTS = 32
D_EXPERT = 128

LANES = 128
SUBLANES = 8
VMEM_LIMIT = 56 * 1024 * 1024

CHUNK = 64
POOL_HALO = 16
CONV_HALO = 8
OUT_SUB = 2
ROUTE_GROUP_LANE = 8
MOE_TILE = 1024

P_IN = 1
P_OUT = 1
P_POOL = 1
P_ATT = 1
P_LORA = 1
P_STATE = 1
P_ROUTER = 3


def _split(a):
    hi = a.astype(BF16)
    lo = (a - hi.astype(F32)).astype(BF16)
    return hi, lo


def _mm(a, b, dims):
    return lax.dot_general(a, b, (dims, ((), ())), preferred_element_type=F32)


_NN = ((1,), (0,))
_NT = ((1,), (1,))


def _dot(a, b, passes=1, dims=_NN):
    if passes == 1:
        return _mm(a.astype(BF16), b.astype(BF16), dims)
    a0, a1 = _split(a)
    b0, b1 = _split(b)
    return _mm(a0, b0, dims) + (_mm(a0, b1, dims) + _mm(a1, b0, dims))


def _bmm(a, b, nt):
    dims = (((2,), (2 if nt else 1,)), ((0,), (0,)))
    return lax.dot_general(a, b, dims, preferred_element_type=F32)


def _bdot(a, b, passes=1, nt=False):
    if passes == 1:
        return _bmm(a.astype(BF16), b.astype(BF16), nt)
    a0, a1 = _split(a)
    b0, b1 = _split(b)
    return _bmm(a0, b0, nt) + (_bmm(a0, b1, nt) + _bmm(a1, b0, nt))


def _lhs_w(a, passes):
    if passes == 1:
        return a.astype(BF16)
    a0, a1 = _split(a)
    return jnp.concatenate([a0, a0, a1], axis=1)


def _dot_w(a, w_cat, passes):
    return _mm(_lhs_w(a, passes), w_cat, _NN)


def _split_w(w, passes):
    hi = w.astype(BF16)
    if passes == 1:
        return hi
    lo = (w - hi.astype(F32)).astype(BF16)
    return jnp.concatenate([hi, lo, hi], axis=0)


def _sigmoid(x):
    return 1.0 / (1.0 + jnp.exp(-x))


def _cparams(sem):
    return pltpu.CompilerParams(dimension_semantics=sem, vmem_limit_bytes=VMEM_LIMIT)


def _full(shape):
    return pl.BlockSpec(shape, lambda *_: (0,) * len(shape))


def _tile(n, pref):
    t = min(n, pref)
    assert n % t == 0, (n, t)
    return t


def _mod_kernel(c_ref, w_ref, b_ref, o_ref):
    c = c_ref[...]
    cond = c * _sigmoid(c)
    o_ref[...] = _dot(cond, w_ref[...], 3) + b_ref[...]


def _ada_mod(c, w_ada, b_ada):
    depth = w_ada.shape[0]
    c8 = jnp.broadcast_to(c, (SUBLANES, D_MODEL))
    out = pl.pallas_call(
        _mod_kernel,
        out_shape=jax.ShapeDtypeStruct((depth, SUBLANES, 6 * D_MODEL), F32),
        grid=(depth, 6),
        in_specs=[
            pl.BlockSpec((SUBLANES, D_MODEL), lambda l, j: (0, 0)),
            pl.BlockSpec((None, D_MODEL, D_MODEL), lambda l, j: (l, 0, j)),
            pl.BlockSpec((None, 1, D_MODEL), lambda l, j: (l, 0, j)),
        ],
        out_specs=pl.BlockSpec((None, SUBLANES, D_MODEL), lambda l, j: (l, 0, j)),
        compiler_params=_cparams(("arbitrary", "arbitrary")),
        name="ada_mod",
    )(c8, w_ada, b_ada.reshape(depth, 1, 6 * D_MODEL))
    return out[:, 0:1, :]


def _modulated_norm(x, g, sc, sh):
    y = x * lax.rsqrt(jnp.mean(x * x, axis=-1, keepdims=True) + NORM_EPS) * g
    return y * (1.0 + sc) + sh


def _shift_rows(ext, n):
    return pltpu.roll(ext, n, axis=0)


def _pool_mixer(u, halo, wp, scale, t0):
    tt = u.shape[0]
    ext = jnp.concatenate([halo, u], axis=0)
    s2 = ext + _shift_rows(ext, 1)
    s4 = s2 + _shift_rows(s2, 2)
    s8 = s4 + _shift_rows(s4, 4)
    s16 = s8 + _shift_rows(s8, 8)
    grp = lax.broadcasted_iota(jnp.int32, (tt, GROUP_W), 1) // POOL_CG
    tpos = lax.broadcasted_iota(jnp.int32, (tt, GROUP_W), 0) + (t0 + 1)
    wsum = jnp.where(grp == 0, s2[POOL_HALO:], jnp.where(grp == 1, s4[POOL_HALO:],
                     jnp.where(grp == 2, s8[POOL_HALO:], s16[POOL_HALO:])))
    win = jnp.where(grp == 0, 2, jnp.where(grp == 1, 4, jnp.where(grp == 2, 8, 16)))
    cnt = jnp.minimum(tpos, win).astype(F32)
    return _dot_w(wsum / cnt - u, wp, P_POOL) * scale


def _conv_mixer(bg, z, z_halo, cw):
    zext = jnp.concatenate([z_halo, z], axis=0)
    y = (cw[2:3, :] * z + cw[1:2, :] * _shift_rows(zext, 1)[CONV_HALO:]
         + cw[0:1, :] * _shift_rows(zext, 2)[CONV_HALO:])
    return bg * y


def _inproj_kernel(x_ref, g_ref, sc_ref, sh_ref, w_ref, wp_ref, scale_ref, cw_ref,
                   ya_ref, pb_ref, yc_ref, pd_ref, u_tail, z_tail, *, passes, tm):
    i = pl.program_id(0)

    @pl.when(i == 0)
    def _():
        u_tail[...] = jnp.zeros_like(u_tail)
        z_tail[...] = jnp.zeros_like(z_tail)

    h = _modulated_norm(x_ref[...], g_ref[...], sc_ref[...], sh_ref[...])
    lhs = _lhs_w(h, passes)
    proj = lambda col, width: _mm(lhs, w_ref[:, col:col + width], _NN)
    u = proj(0, N_A)
    pc = proj(N_A + N_B, N_C)
    ya_ref[...] = _pool_mixer(u, u_tail[...], wp_ref[...], scale_ref[...], i * tm).astype(ya_ref.dtype)
    u_tail[...] = u[tm - POOL_HALO:, :]
    z = pc[:, GROUP_W:2 * GROUP_W] * pc[:, 2 * GROUP_W:3 * GROUP_W]
    yc_ref[...] = _conv_mixer(pc[:, 0:GROUP_W], z, z_tail[...], cw_ref[...]).astype(yc_ref.dtype)
    z_tail[...] = z[tm - CONV_HALO:, :]
    pb_ref[...] = proj(N_A, N_B).astype(pb_ref.dtype)
    pd_ref[...] = proj(N_A + N_B + N_C, N_D)


def _in_proj(x, g, sc, sh, w_parts, wp_parts, pool_scale, conv_w):
    t = x.shape[0]
    tm = _tile(t, 1024)
    vec = _full((1, D_MODEL))
    widths = (GROUP_W, N_B, GROUP_W, N_D)
    return pl.pallas_call(
        functools.partial(_inproj_kernel, passes=P_IN, tm=tm),
        out_shape=[jax.ShapeDtypeStruct((t, n), dt) for n, dt in zip(widths, (BF16, BF16, BF16, F32))],
        grid=(t // tm,),
        in_specs=[pl.BlockSpec((tm, D_MODEL), lambda i: (i, 0)), vec, vec, vec, _full(w_parts.shape),
                  _full(wp_parts.shape), _full((1, GROUP_W)), _full((CONV_W, GROUP_W))],
        out_specs=[pl.BlockSpec((tm, n), lambda i: (i, 0)) for n in widths],
        scratch_shapes=[pltpu.VMEM((POOL_HALO, GROUP_W), F32), pltpu.VMEM((CONV_HALO, GROUP_W), F32)],
        compiler_params=_cparams(("arbitrary",)),
        name="in_proj_pool_conv",
    )(x, g, sc, sh, w_parts, wp_parts, pool_scale, conv_w)


def _attn_kernel(cur_ref, prev_ref, sink_ref, o_ref, *, tq):
    i = pl.program_id(0)
    nq = ATT_HEADS * HEAD_DIM
    nkv = ATT_KV_HEADS * HEAD_DIM
    blk = ATT_BLOCK
    hd = HEAD_DIM
    rep = ATT_HEADS // ATT_KV_HEADS
    nblk = tq // blk
    ri = lax.broadcasted_iota(jnp.int32, (rep * blk, 2 * blk), 0)
    ki = lax.broadcasted_iota(jnp.int32, (rep * blk, 2 * blk), 1)
    dist = (ri % blk) + blk - ki
    in_win = (dist >= 0) & (dist < blk)
    distf = dist.astype(F32)
    head_col = lax.broadcasted_iota(jnp.int32, (rep * blk, 1), 0) // blk
    bias, sink = [], []
    for g in range(ATT_KV_HEADS):
        slope_g = jnp.zeros((rep * blk, 1), F32)
        sink_g = jnp.zeros((rep * blk, 1), F32)
        for j in range(rep):
            h = g * rep + j
            slope_g = jnp.where(head_col == j, 2.0 ** (-8.0 * (h + 1) / ATT_HEADS), slope_g)
            sink_g = jnp.where(head_col == j, sink_ref[:, h:h + 1], sink_g)
        bias.append(slope_g * distf)
        sink.append(sink_g)
    bias = jnp.stack(bias)[None]
    sink = jnp.stack(sink)[None]

    q = cur_ref[:, 0:nq] * (hd ** -0.5)
    kv = jnp.concatenate([prev_ref[...], cur_ref[:, nq:nq + 2 * nkv]], axis=0)
    qs, ks, vs = [], [], []
    for b in range(nblk):
        for g in range(ATT_KV_HEADS):
            qs.append(jnp.concatenate(
                [q[b * blk:(b + 1) * blk, (g * rep + j) * hd:(g * rep + j + 1) * hd] for j in range(rep)], axis=0))
            ks.append(kv[b * blk:(b + 2) * blk, g * hd:(g + 1) * hd])
            vs.append(kv[b * blk:(b + 2) * blk, nkv + g * hd:nkv + (g + 1) * hd])
    s = _bdot(jnp.stack(qs), jnp.stack(ks), P_ATT, nt=True)
    s = s.reshape(nblk, ATT_KV_HEADS, rep * blk, 2 * blk)
    s = jnp.where(in_win, s - bias, -jnp.inf)
    first_ok = ki >= blk * (1 - (i > 0).astype(jnp.int32))
    s = jnp.concatenate([jnp.where(first_ok, s[0:1], -jnp.inf), s[1:]], axis=0)
    m = jnp.maximum(jnp.max(s, axis=-1, keepdims=True), sink)
    p = jnp.exp(s - m)
    den = jnp.sum(p, axis=-1, keepdims=True) + jnp.exp(sink - m)
    o = _bdot(p.reshape(nblk * ATT_KV_HEADS, rep * blk, 2 * blk), jnp.stack(vs), P_ATT)
    o = o.reshape(nblk, ATT_KV_HEADS, rep * blk, hd) / den
    for b in range(nblk):
        o_ref[b * blk:(b + 1) * blk, :] = jnp.concatenate(
            [o[b, g, j * blk:(j + 1) * blk, :] for g in range(ATT_KV_HEADS) for j in range(rep)],
            axis=-1).astype(o_ref.dtype)


def _attention(pb, sinks):
    t = pb.shape[0]
    tq = _tile(t, 2048)
    nb = tq // ATT_BLOCK
    return pl.pallas_call(
        functools.partial(_attn_kernel, tq=tq),
        out_shape=jax.ShapeDtypeStruct((t, GROUP_W), BF16),
        grid=(t // tq,),
        in_specs=[
            pl.BlockSpec((tq, N_B), lambda i: (i, 0)),
            pl.BlockSpec((ATT_BLOCK, 2 * ATT_KV_HEADS * HEAD_DIM), lambda i: (jnp.maximum(i * nb - 1, 0), 1)),
            _full((1, ATT_HEADS)),
        ],
        out_specs=pl.BlockSpec((tq, GROUP_W), lambda i: (i, 0)),
        compiler_params=_cparams(("parallel",)),
        name="swa_attention",
    )(pb, pb, sinks)


def _rwkv_kernel(*refs, tt, has_vres):
    if has_vres:
        (pd_ref, prev_ref, vf_ref, tril_ref, mu_ref, w0_ref, w2_ref, a0_ref, a2_ref, g2_ref, kk_ref, ka_ref,
         rk_ref, lng_ref, lnb_ref, v0_ref, v1_ref, v2_ref, y_ref, st_ref, yr_s) = refs
    else:
        (pd_ref, prev_ref, tril_ref, mu_ref, w0_ref, w2_ref, a0_ref, a2_ref, g2_ref, kk_ref, ka_ref,
         rk_ref, lng_ref, lnb_ref, y_ref, vout_ref, st_ref, yr_s) = refs
    i = pl.program_id(0)
    nh, hd, gw = RWKV_HEADS, RWKV_HEAD, GROUP_W
    @pl.when(i == 0)
    def _():
        st_ref[...] = jnp.zeros_like(st_ref)

    hr = lax.broadcasted_iota(jnp.int32, (2 * gw, gw), 0) % gw // hd
    hc = lax.broadcasted_iota(jnp.int32, (2 * gw, gw), 1) // hd
    hsum2 = jnp.where(hr == hc, 1.0, 0.0).astype(BF16)

    def head_sum(t2):
        return _mm(jnp.concatenate(_split(t2), axis=1), hsum2, _NN)

    p = pd_ref[...]
    row = lax.broadcasted_iota(jnp.int32, (tt, 1), 0)
    prev_row = prev_ref[SUBLANES - 1:SUBLANES, :] * (i > 0).astype(F32)
    shifted = jnp.where(row == 0, prev_row, pltpu.roll(p, 1, axis=0))
    z = p + (shifted - p) * mu_ref[...]
    r = z[:, 0:gw]
    k = z[:, gw:2 * gw]
    v = z[:, 2 * gw:3 * gw]
    o = 3 * gw
    wd = z[:, o:o + LORA_W]
    ad = z[:, o + LORA_W:o + LORA_W + LORA_A]
    gd = z[:, o + LORA_W + LORA_A:o + LORA_W + LORA_A + LORA_G]
    wpre = -(w0_ref[...] + _dot(jnp.tanh(wd), w2_ref[...], P_LORA))
    softplus = jnp.maximum(wpre, 0.0) + jnp.log(1.0 + jnp.exp(-jnp.abs(wpre)))
    lw = -jnp.exp(-softplus - 0.5)
    a = _sigmoid(a0_ref[...] + _dot(ad, a2_ref[...], P_LORA))
    g = _dot(_sigmoid(gd), g2_ref[...], P_LORA)
    if has_vres:
        mix = _sigmoid(v0_ref[...] + _dot(_dot(v, v1_ref[...], P_LORA), v2_ref[...], P_LORA))
        v = v + (vf_ref[...] - v) * mix
    else:
        vout_ref[...] = v
    kk = k * kk_ref[...]
    kk = kk / jnp.maximum(jnp.sqrt(head_sum(kk * kk)), 1e-12)
    k = k * (1.0 + (a - 1.0) * ka_ref[...])
    nch = tt // CHUNK
    L = CHUNK
    npair = nh // 2
    pw = 2 * hd
    ti = lax.broadcasted_iota(jnp.int32, (L, pw), 0)
    tl = lax.broadcasted_iota(jnp.int32, (L, pw), 1) % hd
    low_strict = ti > tl
    low_incl = ti >= tl
    eye = (ti == tl).astype(F32)
    first_head = lax.broadcasted_iota(jnp.int32, (1, pw), 1) < hd

    def same_block(b):
        return (ti // b) == (tl // b)

    def to_batch(t2, c0, nc):
        return jnp.concatenate([t2[c0 * L:(c0 + nc) * L, q * pw:(q + 1) * pw].reshape(nc, L, pw)
                                for q in range(npair)], axis=0)

    def bd(x):
        zero = jnp.zeros_like(x)
        return jnp.concatenate([jnp.where(first_head, x, zero), jnp.where(first_head, zero, x)], axis=-2)

    def fold(x):
        return x[:, :hd] + x[:, hd:]

    def c16(x):
        return x.astype(BF16)

    lw_hi, lw_lo = _split(lw)
    cl = _mm(tril_ref[...], lw_hi, _NN) + _mm(tril_ref[...], lw_lo, _NN)
    cl3 = cl.reshape(nch, L, gw)
    cl_end = cl3[:, L - 1:L, :]
    e_end = jnp.exp(cl_end - cl3).reshape(tt, gw)
    e_neg = jnp.exp(-cl)
    bb = kk * a
    at_t = -kk * jnp.exp(cl - lw)
    bh_t = bb * e_neg
    kh_t = k * e_neg
    rt_t = r * jnp.exp(cl)
    be_t = bb * e_end
    ke_t = k * e_end
    p_end = jnp.exp(cl_end)

    def chunk_operators(c0, nc):
        at = c16(to_batch(at_t, c0, nc))
        bh = c16(to_batch(bh_t, c0, nc))
        kh = c16(to_batch(kh_t, c0, nc))
        rt = to_batch(rt_t, c0, nc)
        be = to_batch(be_t, c0, nc)
        ke = to_batch(ke_t, c0, nc)
        vb = bd(c16(to_batch(v, c0, nc)))
        pe = jnp.concatenate([p_end[c0:c0 + nc, :, q * pw:(q + 1) * pw] for q in range(npair)], axis=0)
        lhs = jnp.concatenate([at, c16(rt)], axis=1)
        g_b = _bmm(lhs, bd(bh), True)
        g_k = _bmm(lhs, bd(kh), True)
        a_ab = jnp.where(low_strict, g_b[:, :L], 0.0)
        b_rb = jnp.where(low_incl, g_b[:, L:], 0.0)
        a_ak = jnp.where(low_strict, g_k[:, :L], 0.0)
        b_rk = jnp.where(low_incl, g_k[:, L:], 0.0)
        tinv = eye + jnp.where(same_block(2), a_ab, 0.0)
        b = 2
        while b < L:
            e = jnp.where(same_block(2 * b) & ~same_block(b), a_ab, 0.0)
            xe = _bmm(c16(tinv), bd(c16(e)), False)
            tinv = tinv + _bmm(c16(xe), bd(c16(tinv)), False)
            b *= 2
        av = _bmm(c16(a_ak), vb, False)
        wu = _bmm(c16(tinv), jnp.concatenate([bd(at), bd(c16(av))], axis=2), False)
        rhs = jnp.concatenate([bd(c16(wu[:, :, :pw])), bd(c16(wu[:, :, pw:]))], axis=2)
        lhs_b = jnp.concatenate([c16(b_rb), c16(jnp.swapaxes(bd(be), 1, 2))], axis=1)
        lhs_k = jnp.concatenate([c16(b_rk), c16(jnp.swapaxes(bd(ke), 1, 2))], axis=1)
        o_wu = _bmm(lhs_b, rhs, False)
        o_v = _bmm(lhs_k, vb, False)
        rq = rt + o_wu[:, :L, :pw]
        y0 = o_wu[:, :L, pw:] + o_v[:, :L]
        m2 = eye * pe + fold(o_wu[:, L:, :pw])
        c2 = fold(o_wu[:, L:, pw:] + o_v[:, L:])
        return jnp.concatenate([rq, m2], axis=1), jnp.concatenate([y0, c2], axis=1)

    def state_walk(zs, ops, c0, nc):
        rqm, y0c = ops
        for c in range(nc):
            for q in range(npair):
                b = q * nc + c
                yz = _dot(rqm[b], bd(zs[q]), P_STATE) + y0c[b]
                yr_s[(c0 + c) * L:(c0 + c + 1) * L, q * pw:(q + 1) * pw] = yz[:L]
                zs[q] = yz[L:]
        return zs

    zs = state_walk([st_ref[q] for q in range(npair)], chunk_operators(0, nch), 0, nch)
    for q in range(npair):
        st_ref[q] = zs[q]

    y = yr_s[...]
    inv_n = 1.0 / hd
    mean = head_sum(y) * inv_n
    yc = y - mean
    var = head_sum(yc * yc) * inv_n
    yn = yc * lax.rsqrt(var + RWKV_GN_EPS) * lng_ref[...] + lnb_ref[...]
    bonus = head_sum(r * k * rk_ref[...]) * v
    y_ref[...] = ((yn + bonus) * g).astype(y_ref.dtype)


def _rwkv(pd, v_first, prm):
    t = pd.shape[0]
    tt = _tile(t, 512)
    nch = tt // CHUNK
    has_vres = v_first is not None
    gw = GROUP_W
    row_blk = lambda n: pl.BlockSpec((tt, n), lambda i: (i, 0))
    vec = _full((1, gw))
    in_specs = [row_blk(N_D),
                pl.BlockSpec((SUBLANES, N_D), lambda i: (jnp.maximum(i * (tt // SUBLANES) - 1, 0), 0))]
    args = [pd, pd]
    if has_vres:
        in_specs.append(row_blk(gw))
        args.append(v_first)
    tpos = jnp.arange(tt)
    tril = ((tpos[:, None] // CHUNK == tpos[None, :] // CHUNK) & (tpos[:, None] >= tpos[None, :])).astype(BF16)
    in_specs.append(_full((tt, tt)))
    args.append(tril)
    in_specs += [_full((1, N_D)), vec, _full((LORA_W, gw)), vec, _full((LORA_A, gw)), _full((LORA_G, gw)),
                 vec, vec, vec, vec, vec]
    args += [prm["mu"], prm["w0"], prm["w2"], prm["a0"], prm["a2"], prm["g2"], prm["k_k"], prm["k_a"],
             prm["r_k"], prm["ln_g"], prm["ln_b"]]
    if has_vres:
        in_specs += [vec, _full((gw, LORA_V)), _full((LORA_V, gw))]
        args += [prm["v0"], prm["v1"], prm["v2"]]
        out_shape = jax.ShapeDtypeStruct((t, gw), BF16)
        out_specs = row_blk(gw)
    else:
        out_shape = [jax.ShapeDtypeStruct((t, gw), BF16), jax.ShapeDtypeStruct((t, gw), F32)]
        out_specs = [row_blk(gw)] * 2
    npair, pw = RWKV_HEADS // 2, 2 * RWKV_HEAD
    scratch = [pltpu.VMEM((npair, RWKV_HEAD, pw), F32), pltpu.VMEM((tt, gw), F32)]
    return pl.pallas_call(
        functools.partial(_rwkv_kernel, tt=tt, has_vres=has_vres),
        out_shape=out_shape,
        grid=(t // tt,),
        in_specs=in_specs,
        out_specs=out_specs,
        scratch_shapes=scratch,
        compiler_params=_cparams(("arbitrary",)),
        name="rwkv7",
    )(*args)


def _route_record(lg):
    lane = lax.broadcasted_iota(jnp.int32, (lg.shape[0], LANES), 1)
    lanef = lane.astype(F32)
    gl = jnp.where(lane < N_GROUPS, lg[:, 0:LANES], -jnp.inf)
    gmax = jnp.max(gl, axis=-1, keepdims=True)
    gp = 1.0 / jnp.sum(jnp.exp(gl - gmax), axis=-1, keepdims=True)
    gi = jnp.min(jnp.where(gl == gmax, lanef, float(LANES)), axis=-1, keepdims=True)
    lo = gi * EXP_PER_GROUP
    in_grp = (lanef >= lo) & (lanef < lo + EXP_PER_GROUP)
    el = jnp.where(in_grp, lg[:, LANES:2 * LANES], -jnp.inf)
    m1 = jnp.max(el, axis=-1, keepdims=True)
    i1 = jnp.min(jnp.where(el == m1, lanef, float(LANES)), axis=-1, keepdims=True)
    el2 = jnp.where(lanef == i1, -jnp.inf, el)
    m2 = jnp.max(el2, axis=-1, keepdims=True)
    i2 = jnp.min(jnp.where(el2 == m2, lanef, float(LANES)), axis=-1, keepdims=True)
    e2 = jnp.exp(m2 - m1)
    w_top = gp / (1.0 + e2)
    return jnp.where(lanef == i1 - lo, w_top, jnp.where(lanef == i2 - lo, w_top * e2,
                     jnp.where(lane == ROUTE_GROUP_LANE, gi, 0.0)))


def _outproj_kernel(ya_ref, yb_ref, yc_ref, yd_ref, x_ref, wo_ref, gt_ref, g_ref, sc_ref, sh_ref, wr_ref, br_ref,
                    x1_ref, h2_ref, gate_ref, *, tm):
    sub = tm // OUT_SUB
    for blk in range(OUT_SUB):
        rows = slice(blk * sub, (blk + 1) * sub)
        ycat = jnp.concatenate([ya_ref[rows, :], yb_ref[rows, :], yc_ref[rows, :], yd_ref[rows, :]], axis=1)
        x1 = x_ref[rows, :] + gt_ref[...] * _dot_w(ycat, wo_ref[...], P_OUT)
        x1_ref[rows, :] = x1
        h2 = _modulated_norm(x1, g_ref[...], sc_ref[...], sh_ref[...])
        h2_ref[rows, :] = h2.astype(h2_ref.dtype)
        gate_ref[rows, :] = _route_record(_dot_w(h2, wr_ref[...], P_ROUTER) + br_ref[...]).astype(gate_ref.dtype)


def _out_proj(ya, yb, yc, yd, x, wo_parts, gt, g, sc, sh, w_router, b_router):
    t = x.shape[0]
    tm = _tile(t, 1024)
    row_blk = lambda n: pl.BlockSpec((tm, n), lambda i: (i, 0))
    vec = _full((1, D_MODEL))
    return pl.pallas_call(
        functools.partial(_outproj_kernel, tm=tm),
        out_shape=[jax.ShapeDtypeStruct((t, D_MODEL), F32), jax.ShapeDtypeStruct((t, D_MODEL), BF16),
                   jax.ShapeDtypeStruct((t, LANES), BF16)],
        grid=(t // tm,),
        in_specs=[row_blk(GROUP_W)] * 4 + [row_blk(D_MODEL), _full(wo_parts.shape), vec, vec, vec, vec,
                                           _full(w_router.shape), _full((1, 2 * LANES))],
        out_specs=[row_blk(D_MODEL), row_blk(D_MODEL), row_blk(LANES)],
        compiler_params=_cparams(("parallel",)),
        name="out_proj_router",
    )(ya, yb, yc, yd, x, wo_parts, gt, g, sc, sh, w_router, b_router)


def _moe_kernel(h_ref, route_ref, tril_ref, wg_ref, wu_ref, wd_ref, x1_ref, gt_ref, fg_ref, o_ref,
                acc_ref, col_ref, row_ref, cnt_ref, *, final, tm, cap, cap_extra):
    g = pl.program_id(1)
    lane = lax.broadcasted_iota(jnp.int32, (tm, LANES), 1)

    @pl.when(g == 0)
    def _():
        acc_ref[...] = jnp.zeros_like(acc_ref)
        gi = route_ref[:, ROUTE_GROUP_LANE:ROUTE_GROUP_LANE + 1].astype(F32)
        onehot = lane.astype(F32) == gi
        cum = _mm(tril_ref[...], jnp.where(onehot, 1.0, 0.0).astype(BF16), _NN)
        rank = jnp.sum(jnp.where(onehot, cum, 0.0), axis=-1, keepdims=True) - 1.0
        cnt_ref[0:1, :] = cum[tm - 1:tm, :]
        cnt_ref[1:2, :] = cum[tm // 2 - 1:tm // 2, :]
        rec = jnp.where(lane == 0, rank, jnp.where(lane == 1, gi, 0.0))
        col_ref[...] = rec
        row_ref[...] = rec.T[0:SUBLANES, :]

    gf = g.astype(F32)
    this_group = lane[0:1, :] == g
    count = jnp.sum(jnp.where(this_group, cnt_ref[0:1, :], 0.0)).astype(jnp.int32)
    count_half = jnp.sum(jnp.where(this_group, cnt_ref[1:2, :], 0.0)).astype(jnp.int32)
    sel_col = jnp.where(col_ref[:, 1:2] == gf, col_ref[:, 0:1], -1.0)
    sel_row = jnp.where(row_ref[1:2, :] == gf, row_ref[0:1, :], -1.0)
    route16 = route_ref[...]

    def block(base, rows, t0=0):
        base = base.astype(F32)
        nt = tm - t0
        slot_r = lax.broadcasted_iota(jnp.int32, (rows, nt), 0).astype(F32) + base
        gather = jnp.where(slot_r == sel_row[:, t0:], 1.0, 0.0).astype(BF16)
        h = _mm(gather, h_ref[t0:, :], _NN).astype(BF16)
        slot_gate = _mm(gather, route16[t0:, :], _NN)
        hg = _mm(h, wg_ref[...], _NN)
        hu = _mm(h, wu_ref[...], _NN)
        hu = jnp.concatenate([hu[:, j * D_EXPERT:(j + 1) * D_EXPERT] * slot_gate[:, j:j + 1]
                              for j in range(EXP_PER_GROUP)], axis=1)
        act = hg * _sigmoid(hg) * hu
        ys = _mm(act.astype(BF16), wd_ref[...], _NN)
        slot_c = lax.broadcasted_iota(jnp.int32, (nt, rows), 1).astype(F32) + base
        scatter = jnp.where(slot_c == sel_col[t0:, :], 1.0, 0.0).astype(BF16)
        acc_ref[t0:, :] += _mm(scatter, ys.astype(BF16), _NN)

    @pl.when(count > 0)
    def _():
        block(jnp.int32(0), cap)

    n_extra = (jnp.maximum(count - cap, 0) + (cap_extra - 1)) // cap_extra
    late = count_half <= cap

    def extra_late(r, carry):
        block(cap + r * cap_extra, cap_extra, t0=tm // 2)
        return carry

    def extra_any(r, carry):
        block(cap + r * cap_extra, cap_extra)
        return carry

    lax.fori_loop(0, jnp.where(late, n_extra, 0), extra_late, 0)
    lax.fori_loop(0, jnp.where(late, 0, n_extra), extra_any, 0)

    @pl.when(g == pl.num_programs(1) - 1)
    def _():
        out = x1_ref[...] + gt_ref[...] * acc_ref[...]
        if final:
            out = out * lax.rsqrt(jnp.mean(out * out, axis=-1, keepdims=True) + NORM_EPS) * fg_ref[...]
        o_ref[...] = out


def _moe(h2, route, wg, wu, wd, x1, gt, final_g, final):
    t = h2.shape[0]
    tm = _tile(t, MOE_TILE)
    cap = tm // N_GROUPS
    cap_extra = cap // 2
    nw = EXP_PER_GROUP * D_EXPERT
    vec = _full((1, D_MODEL))
    tpos = jnp.arange(tm)
    tril = (tpos[:, None] >= tpos[None, :]).astype(BF16)
    return pl.pallas_call(
        functools.partial(_moe_kernel, final=final, tm=tm, cap=cap, cap_extra=cap_extra),
        out_shape=jax.ShapeDtypeStruct((t, D_MODEL), F32),
        grid=(t // tm, N_GROUPS),
        in_specs=[
            pl.BlockSpec((tm, D_MODEL), lambda i, g: (i, 0)),
            pl.BlockSpec((tm, LANES), lambda i, g: (i, 0)),
            _full((tm, tm)),
            pl.BlockSpec((D_MODEL, nw), lambda i, g: (0, g)),
            pl.BlockSpec((D_MODEL, nw), lambda i, g: (0, g)),
            pl.BlockSpec((nw, D_MODEL), lambda i, g: (g, 0)),
            pl.BlockSpec((tm, D_MODEL), lambda i, g: (i, 0)),
            vec, vec,
        ],
        out_specs=pl.BlockSpec((tm, D_MODEL), lambda i, g: (i, 0)),
        scratch_shapes=[pltpu.VMEM((tm, D_MODEL), F32), pltpu.VMEM((tm, LANES), F32),
                        pltpu.VMEM((SUBLANES, tm), F32), pltpu.VMEM((2, LANES), F32)],
        compiler_params=_cparams(("parallel", "arbitrary")),
        name="moe_experts",
    )(h2, route, tril, wg, wu, wd, x1, gt, final_g)


def kernel(x, c, w_ada, b_ada, norm1_g, norm2_g, w_in, w_out, pool_w, pool_scale, attn_sinks, conv_w, rwkv_mu,
           rwkv_w0, rwkv_w2, rwkv_a0, rwkv_a2, rwkv_g2, rwkv_k_k, rwkv_k_a, rwkv_r_k, rwkv_ln_g, rwkv_ln_b,
           rwkv_v0, rwkv_v1, rwkv_v2, moe_w_grp, moe_b_grp, moe_w_exp, moe_b_exp, moe_w_gate, moe_w_up,
           moe_w_down, final_g):
    bsz, t, d = x.shape
    assert bsz == 1 and d == D_MODEL
    depth = w_ada.shape[0]
    xs = x.reshape(t, d)
    mod = _ada_mod(c, w_ada, b_ada)
    v_first = None
    row = lambda a: a.reshape(1, -1)
    for l in range(depth):
        sh1, sc1, gt1, sh2, sc2, gt2 = [mod[l, :, j * d:(j + 1) * d] for j in range(6)]
        wp = jax.scipy.linalg.block_diag(*[pool_w[l, gidx] for gidx in range(len(POOL_WINDOWS))])
        ya, pb, yc, pd = _in_proj(xs, row(norm1_g[l]), sc1, sh1, _split_w(w_in[l], P_IN), _split_w(wp, P_POOL),
                                  row(pool_scale[l]), conv_w[l])
        yb = _attention(pb, row(attn_sinks[l]))
        prm = dict(mu=row(rwkv_mu[l]), w0=row(rwkv_w0[l]), w2=rwkv_w2[l], a0=row(rwkv_a0[l]), a2=rwkv_a2[l],
                   g2=rwkv_g2[l], k_k=row(rwkv_k_k[l]), k_a=row(rwkv_k_a[l]), r_k=row(rwkv_r_k[l]),
                   ln_g=row(rwkv_ln_g[l]), ln_b=row(rwkv_ln_b[l]))
        if l == 0:
            yd, v_first = _rwkv(pd, None, prm)
        else:
            prm.update(v0=row(rwkv_v0[l - 1]), v1=rwkv_v1[l - 1], v2=rwkv_v2[l - 1])
            yd = _rwkv(pd, v_first, prm)
        lane_pad = lambda a: jnp.pad(a, ((0, 0), (0, LANES - a.shape[1])))
        w_router = jnp.concatenate([lane_pad(moe_w_grp[l]), lane_pad(moe_w_exp[l])], axis=1)
        b_router = jnp.concatenate([lane_pad(row(moe_b_grp[l])), lane_pad(row(moe_b_exp[l]))], axis=1)
        x1, h2, gates = _out_proj(ya, yb, yc, yd, xs, _split_w(w_out[l], P_OUT), gt1, row(norm2_g[l]), sc2, sh2,
                                  _split_w(w_router, P_ROUTER), b_router)
        wg = jnp.transpose(moe_w_gate[l], (1, 0, 2)).reshape(d, N_EXPERTS * D_EXPERT).astype(BF16)
        wu = jnp.transpose(moe_w_up[l], (1, 0, 2)).reshape(d, N_EXPERTS * D_EXPERT).astype(BF16)
        wd = moe_w_down[l].astype(BF16).reshape(N_EXPERTS * D_EXPERT, d)
        xs = _moe(h2, gates, wg, wu, wd, x1, gt2, row(final_g), final=(l == depth - 1))
    return xs.reshape(bsz, t, d)
```
